```python
import jax
import jax.numpy as jnp
from jax import lax
import numpy as np

D_MODEL = 1024
BATCH = 8
SEQ = 2048
DEPTH = 4

GRID_W = 64
CTX_LEN = 256
CONV_WIDTH = 512
CONV_TAPS = 3
HG_WIDTH = 512
HG_HEADS = 4
HG_HEAD_DIM = HG_WIDTH // HG_HEADS
MIX_WIDTH = CONV_WIDTH + HG_WIDTH
PROJ_WIDTH = 3 * CONV_WIDTH + 5 * HG_WIDTH
PROJ_SPLITS = (CONV_WIDTH, 2 * CONV_WIDTH, 3 * CONV_WIDTH, 3 * CONV_WIDTH + HG_WIDTH, 3 * CONV_WIDTH + 2 * HG_WIDTH, 3 * CONV_WIDTH + 3 * HG_WIDTH, 3 * CONV_WIDTH + 4 * HG_WIDTH)
CTX_STATE_LO = 3 * CONV_WIDTH + HG_WIDTH
CTX_STATE_HI = 3 * CONV_WIDTH + 4 * HG_WIDTH
CHUNK = 16
N_GROUPS = 4
EXPERTS_PER_GROUP = 8
N_EXPERTS = N_GROUPS * EXPERTS_PER_GROUP
TOP_K = 2
D_EXPERT = 512
ROUTE_BLOCK = 128
N_MOD = 6
NORM_EPS = 1e-6

kernel_name = 'hybrid_conv_hgrn2_hmoe_dit'


def rms_norm(x, w):
    xf = x.astype(jnp.float32)
    y = xf * lax.rsqrt(jnp.mean(xf * xf, axis=-1, keepdims=True) + NORM_EPS)
    return (y * w.astype(jnp.float32)).astype(x.dtype)


def modulate(x, shift, scale):
    return x * (1 + scale) + shift


def heads(a):
    return a.reshape(a.shape[0], a.shape[1], HG_HEADS, HG_HEAD_DIM)


def conv3_seq(u, w):
    up = jnp.pad(u, ((0, 0), (1, 1), (0, 0)))
    return w[0] * up[:, :-2] + w[1] * up[:, 1:-1] + w[2] * up[:, 2:]


def conv3_grid(u, w, rows):
    bsz, t_len, ch = u.shape
    half = ch // 2
    g = u.reshape(bsz, rows, GRID_W, ch)
    gh = jnp.pad(g[..., :half], ((0, 0), (0, 0), (1, 1), (0, 0)))
    wh = w[:, :half]
    yh = wh[0] * gh[:, :, :-2] + wh[1] * gh[:, :, 1:-1] + wh[2] * gh[:, :, 2:]
    gv = jnp.pad(g[..., half:], ((0, 0), (1, 1), (0, 0), (0, 0)))
    wv = w[:, half:]
    yv = wv[0] * gv[:, :-2] + wv[1] * gv[:, 1:-1] + wv[2] * gv[:, 2:]
    return jnp.concatenate([yh, yv], axis=-1).reshape(bsz, t_len, ch)


def hgrn2_lower_bounds(lb_raw):
    p = jax.nn.softmax(lb_raw.astype(jnp.float32), axis=1)
    cs = jnp.cumsum(p, axis=1)
    return cs - cs[:, :1]


def hgrn2_forget(zf, lb):
    zf = zf.astype(jnp.float32)
    logf = jnp.logaddexp(jnp.log(lb), jnp.log1p(-lb) + jax.nn.log_sigmoid(zf))
    return heads(logf), heads(-jnp.expm1(logf))


def gla_chunked(q, k, v, logf, s0):
    bsz, t_len, n_h, d_k = q.shape
    d_v = v.shape[-1]
    n_c = t_len // CHUNK
    q, k, logf = [a.astype(jnp.float32).reshape(bsz, n_c, CHUNK, n_h, d_k) for a in (q, k, logf)]
    v = v.astype(jnp.float32).reshape(bsz, n_c, CHUNK, n_h, d_v)
    b = jnp.cumsum(logf, axis=2)
    order = jnp.tril(jnp.ones((CHUNK, CHUNK), dtype=bool))[None, None, :, :, None, None]
    diff = b[:, :, :, None] - b[:, :, None, :]
    decay = jnp.exp(jnp.where(order, diff, -jnp.inf))
    scores = jnp.einsum('bnthk,bnshk,bntshk->bnhts', q, k, decay)
    o_intra = jnp.einsum('bnhts,bnshv->bnthv', scores, v)
    b_last = b[:, :, -1:]
    q_in = q * jnp.exp(b)
    k_out = k * jnp.exp(b_last - b)
    d_chunk = jnp.exp(b_last[:, :, 0])

    def step(state, inp):
        q_c, k_c, v_c, d_c = inp
        o_c = jnp.einsum('bthk,bhkv->bthv', q_c, state)
        state = d_c[..., None] * state + jnp.einsum('bthk,bthv->bhkv', k_c, v_c)
        return state, o_c

    xs = tuple(jnp.moveaxis(a, 1, 0) for a in (q_in, k_out, v, d_chunk))
    s_fin, o_inter = lax.scan(step, s0.astype(jnp.float32), xs)
    o = o_intra + jnp.moveaxis(o_inter, 0, 1)
    return o.reshape(bsz, t_len, n_h, d_v), s_fin


def gla_final_state(k, logf, v):
    b = jnp.cumsum(logf, axis=1)
    return jnp.einsum('bthk,bthv->bhkv', k * jnp.exp(b[:, -1:] - b), v.astype(jnp.float32))


def hgrn2_mixer(q_raw, zf_f, zf_b, i_raw, g_raw, lb_f, lb_b, norm_w, s0_f, s0_b):
    q = heads(jax.nn.silu(q_raw.astype(jnp.float32)))
    v = heads(i_raw.astype(jnp.float32))
    logf_f, k_f = hgrn2_forget(zf_f, lb_f)
    logf_b, k_b = hgrn2_forget(zf_b, lb_b)
    o_f, s_f = gla_chunked(q, k_f, v, logf_f, s0_f)
    rev = lambda a: jnp.flip(a, axis=1)
    o_b, s_b = gla_chunked(rev(q), rev(k_b), rev(v), rev(logf_b), s0_b)
    o = rms_norm(o_f + rev(o_b), norm_w) * jax.nn.silu(heads(g_raw.astype(jnp.float32)))
    return o.reshape(q_raw.shape[0], q_raw.shape[1], HG_WIDTH).astype(q_raw.dtype), s_f, s_b


def hgrn2_context_states(zf_f, zf_b, i_raw, lb_f, lb_b):
    v = heads(i_raw.astype(jnp.float32))
    logf_f, k_f = hgrn2_forget(zf_f, lb_f)
    logf_b, k_b = hgrn2_forget(zf_b, lb_b)
    rev = lambda a: jnp.flip(a, axis=1)
    return gla_final_state(k_f, logf_f, v), gla_final_state(rev(k_b), rev(logf_b), rev(v))


def token_mixer(h, w_in_l, conv_w_l, conv_norm_w_l, lb_f, lb_b, hg_norm_w_l, s0_f, s0_b, conv_fn):
    p = h @ w_in_l
    bg, cg, hv, q, zf_f, zf_b, iv, g = jnp.split(p, PROJ_SPLITS, axis=-1)
    y_conv = rms_norm(bg * conv_fn(cg * hv, conv_w_l), conv_norm_w_l)
    y_rec, s_f, s_b = hgrn2_mixer(q, zf_f, zf_b, iv, g, lb_f, lb_b, hg_norm_w_l, s0_f, s0_b)
    return jnp.concatenate([y_conv, y_rec], axis=-1), s_f, s_b


def hier_moe(t, w_rg, b_rg, w_re, b_re, w_gu, w_down):
    n_tok, d = t.shape
    tf = t.astype(jnp.float32)
    g_logits = tf @ w_rg.astype(jnp.float32) + b_rg.astype(jnp.float32)
    g_sel = jnp.argmax(g_logits, axis=-1)
    p_group = jnp.take_along_axis(jax.nn.softmax(g_logits, axis=-1), g_sel[:, None], axis=-1)
    e_logits = (tf @ w_re.astype(jnp.float32) + b_re.astype(jnp.float32)).reshape(n_tok, N_GROUPS, EXPERTS_PER_GROUP)
    e_logits = jnp.take_along_axis(e_logits, g_sel[:, None, None], axis=1)[:, 0]
    top_p, top_i = lax.top_k(jax.nn.softmax(e_logits, axis=-1), TOP_K)
    gate = p_group * top_p / jnp.sum(top_p, axis=-1, keepdims=True)
    expert_id = g_sel[:, None].astype(jnp.int32) * EXPERTS_PER_GROUP + top_i.astype(jnp.int32)
    n_assign = n_tok * TOP_K
    flat_e = expert_id.reshape(-1)
    flat_tok = jnp.repeat(jnp.arange(n_tok, dtype=jnp.int32), TOP_K)
    flat_gate = gate.reshape(-1)
    order = jnp.argsort(flat_e)
    sorted_e = flat_e[order]
    counts = jnp.bincount(flat_e, length=N_EXPERTS)
    starts = jnp.cumsum(counts) - counts
    padded = (counts + ROUTE_BLOCK - 1) // ROUTE_BLOCK * ROUTE_BLOCK
    pad_ends = jnp.cumsum(padded)
    pad_starts = pad_ends - padded
    dest = pad_starts[sorted_e] + jnp.arange(n_assign, dtype=jnp.int32) - starts[sorted_e]
    n_blocks = -(-(n_assign + N_EXPERTS * (ROUTE_BLOCK - 1)) // ROUTE_BLOCK)
    n_rows = n_blocks * ROUTE_BLOCK
    row_tok = jnp.full((n_rows,), n_tok, dtype=jnp.int32).at[dest].set(flat_tok[order])
    row_gate = jnp.zeros((n_rows,), jnp.float32).at[dest].set(flat_gate[order])
    block_e = jnp.minimum(jnp.searchsorted(pad_ends, jnp.arange(n_blocks) * ROUTE_BLOCK, side='right'), N_EXPERTS - 1)
    t_pad = jnp.concatenate([t, jnp.zeros((1, d), t.dtype)], axis=0)
    xb = t_pad[row_tok].reshape(n_blocks, ROUTE_BLOCK, d)

    def expert_block(args):
        x_blk, e = args
        a, u = jnp.split(x_blk @ w_gu[e], 2, axis=-1)
        return (jax.nn.silu(a) * u) @ w_down[e]

    yb = lax.map(expert_block, (xb, block_e)).reshape(n_rows, d)
    out = jnp.zeros((n_tok + 1, d), t.dtype).at[row_tok].add((yb * row_gate[:, None]).astype(t.dtype))
    return out[:n_tok]


def setup_inputs(seed: int = 0) -> dict:
    key = jax.random.key(seed)
    ks = jax.random.split(key, 20)
    f32 = jnp.float32
    nrm = lambda k, shape, s: jax.random.normal(k, shape, f32) * s
    d = D_MODEL
    return {
        'x': nrm(ks[0], (BATCH, SEQ, d), 1.0),
        'c': nrm(ks[1], (BATCH, d), 1.0),
        'ctx': nrm(ks[2], (BATCH, CTX_LEN, d), 1.0),
        'c_ctx': nrm(ks[3], (d,), 1.0),
        'norm_w': 1.0 + nrm(ks[4], (DEPTH, 2, d), 0.02),
        'w_ada': nrm(ks[5], (DEPTH, d, N_MOD * d), 0.5 * d ** -0.5),
        'b_ada': nrm(ks[6], (DEPTH, N_MOD * d), 0.02),
        'w_in': nrm(ks[7], (DEPTH, d, PROJ_WIDTH), d ** -0.5),
        'conv_w': nrm(ks[8], (DEPTH, CONV_TAPS, CONV_WIDTH), CONV_TAPS ** -0.5),
        'conv_norm_w': 1.0 + nrm(ks[9], (DEPTH, CONV_WIDTH), 0.02),
        'hg_lb': nrm(ks[10], (2, DEPTH, HG_WIDTH), 1.0),
        'hg_norm_w': 1.0 + nrm(ks[11], (DEPTH, HG_HEAD_DIM), 0.02),
        'w_out': nrm(ks[12], (DEPTH, MIX_WIDTH, d), MIX_WIDTH ** -0.5),
        'w_rg': nrm(ks[13], (DEPTH, d, N_GROUPS), d ** -0.5),
        'b_rg': nrm(ks[14], (DEPTH, N_GROUPS), 0.01),
        'w_re': nrm(ks[15], (DEPTH, d, N_EXPERTS), d ** -0.5),
        'b_re': nrm(ks[16], (DEPTH, N_EXPERTS), 0.01),
        'w_e_gu': nrm(ks[17], (DEPTH, N_EXPERTS, d, 2 * D_EXPERT), d ** -0.5),
        'w_e_down': nrm(ks[18], (DEPTH, N_EXPERTS, D_EXPERT, d), D_EXPERT ** -0.5),
        'final_norm_w': 1.0 + nrm(ks[19], (d,), 0.02),
    }


def reference(x, c, ctx, c_ctx, norm_w, w_ada, b_ada, w_in, conv_w, conv_norm_w, hg_lb, hg_norm_w, w_out, w_rg, b_rg, w_re, b_re, w_e_gu, w_e_down, final_norm_w):
    bsz, t_len, d = x.shape
    rows = t_len // GRID_W
    lb = hgrn2_lower_bounds(hg_lb)
    sc = jax.nn.silu(c)
    scc = jax.nn.silu(c_ctx)
    latent_conv = lambda u, w: conv3_grid(u, w, rows)
    zero_state = jnp.zeros((bsz, HG_HEADS, HG_HEAD_DIM, HG_HEAD_DIM), jnp.float32)
    for l in range(DEPTH):
        last = l == DEPTH - 1
        mod = (sc @ w_ada[l] + b_ada[l]).reshape(bsz, N_MOD, 1, d)
        modc = (scc @ w_ada[l] + b_ada[l]).reshape(N_MOD, d)
        h = modulate(rms_norm(x, norm_w[l, 0]), mod[:, 0], mod[:, 1])
        hc = modulate(rms_norm(ctx, norm_w[l, 0]), modc[0], modc[1])
        if last:
            zf_f, zf_b, iv = jnp.split(hc @ w_in[l][:, CTX_STATE_LO:CTX_STATE_HI], 3, axis=-1)
            s_f, s_b = hgrn2_context_states(zf_f, zf_b, iv, lb[0, l], lb[1, l])
        else:
            yc, s_f, s_b = token_mixer(hc, w_in[l], conv_w[l], conv_norm_w[l], lb[0, l], lb[1, l], hg_norm_w[l], zero_state, zero_state, conv3_seq)
            ctx = ctx + modc[2] * (yc @ w_out[l])
        y, _, _ = token_mixer(h, w_in[l], conv_w[l], conv_norm_w[l], lb[0, l], lb[1, l], hg_norm_w[l], s_f, s_b, latent_conv)
        x = x + mod[:, 2] * (y @ w_out[l])
        h2 = modulate(rms_norm(x, norm_w[l, 1]), mod[:, 3], mod[:, 4])
        moe_w = (w_rg[l], b_rg[l], w_re[l], b_re[l], w_e_gu[l], w_e_down[l])
        if last:
            f = hier_moe(h2.reshape(bsz * t_len, d), *moe_w).reshape(bsz, t_len, d)
        else:
            hc2 = modulate(rms_norm(ctx, norm_w[l, 1]), modc[3], modc[4])
            n_lat = bsz * t_len
            f_all = hier_moe(jnp.concatenate([h2.reshape(n_lat, d), hc2.reshape(-1, d)], axis=0), *moe_w)
            f = f_all[:n_lat].reshape(bsz, t_len, d)
            ctx = ctx + modc[5] * f_all[n_lat:].reshape(ctx.shape)
        x = x + mod[:, 5] * f
    return rms_norm(x, final_norm_w)
```

```python
import functools

import jax
import jax.numpy as jnp
from jax import lax
from jax.experimental import pallas as pl
from jax.experimental.pallas import tpu as pltpu

F32 = jnp.float32
BF16 = jnp.bfloat16

D = 1024
DEPTH = 4
GRID_W = 64
CW = 512
HW = 512
NH = 4
HD = HW // NH
PW = 3 * CW + 5 * HW
PPW = PW - CW
N_GROUPS = 4
EPG = 8
NE = N_GROUPS * EPG
DE = 512
N_MOD = 6
EPS = 1e-6

TM = 256
CH = 128
LEAF = 16
MOE_P = 256
DCH = 2048
LANES = 128

VMEM_LIMIT = 56 * 1024 * 1024


def _cparams(n_axes):
    return pltpu.CompilerParams(dimension_semantics=("arbitrary",) * n_axes,
                                vmem_limit_bytes=VMEM_LIMIT)


def _dot(a, b):
    return jnp.dot(a, b, preferred_element_type=F32)


def _dot_nt(a, b):
    return lax.dot_general(a, b, (((1,), (1,)), ((), ())), preferred_element_type=F32)


def _dot_tn(a, b):
    return lax.dot_general(a, b, (((0,), (0,)), ((), ())), preferred_element_type=F32)


def _split2(x):
    hi = x.astype(BF16)
    lo = (x - hi.astype(F32)).astype(BF16)
    return hi, lo


def _dot_hp(a, b):
    a_hi, a_lo = _split2(a)
    b_hi, b_lo = _split2(b)
    return _dot(a_hi, b_hi) + _dot(a_hi, b_lo) + _dot(a_lo, b_hi)


def _sigmoid(x):
    return 1.0 / (1.0 + jnp.exp(-x))


def _rms(x, w):
    return x * lax.rsqrt(jnp.mean(x * x, axis=-1, keepdims=True) + EPS) * w


def _ada_kernel(cc_ref, w_ref, b_ref, o_ref):
    s = cc_ref[...]
    s = s * _sigmoid(s)
    o_ref[0] = _dot_hp(s, w_ref[0]) + b_ref[0]


def _ada(cc, w_ada, b_ada):
    tn = 1536
    n = N_MOD * D
    return pl.pallas_call(
        _ada_kernel,
        grid=(DEPTH, n // tn),
        in_specs=[pl.BlockSpec((16, D), lambda l, j: (0, 0)),
                  pl.BlockSpec((1, D, tn), lambda l, j: (l, 0, j)),
                  pl.BlockSpec((1, 1, tn), lambda l, j: (l, 0, j))],
        out_specs=pl.BlockSpec((1, 16, tn), lambda l, j: (l, 0, j)),
        out_shape=jax.ShapeDtypeStruct((DEPTH, 16, n), F32),
        compiler_params=_cparams(2),
        name="ada",
    )(cc, w_ada, b_ada.reshape(DEPTH, 1, n))


def _inproj_kernel(x_ref, nw_ref, mod_ref, w_ref, o_ref):
    m = mod_ref[0, 0]
    h = _rms(x_ref[0], nw_ref[...]) * (1.0 + m[1:2]) + m[0:1]
    hb = h.astype(BF16)
    blk = lambda k: _dot(hb, w_ref[:, k * 512:(k + 1) * 512])
    o_ref[0, :, 0:512] = blk(0)
    o_ref[0, :, 512:1024] = blk(1) * blk(2)
    for k in range(3, 8):
        o_ref[0, :, (k - 1) * 512:k * 512] = blk(k)


def _inproj(xall, nw, modall, w_in_b):
    bsz, t_all, _ = xall.shape
    return pl.pallas_call(
        _inproj_kernel,
        grid=(bsz, t_all // TM),
        in_specs=[pl.BlockSpec((1, TM, D), lambda b, j: (b, j, 0)),
                  pl.BlockSpec((1, D), lambda b, j: (0, 0)),
                  pl.BlockSpec((1, 1, N_MOD, D), lambda b, j: (b, jnp.minimum(j, 1), 0, 0)),
                  pl.BlockSpec((D, PW), lambda b, j: (0, 0))],
        out_specs=pl.BlockSpec((1, TM, PPW), lambda b, j: (b, j, 0)),
        out_shape=jax.ShapeDtypeStruct((bsz, t_all, PPW), F32),
        compiler_params=_cparams(2),
        name="inproj",
    )(xall, nw, modall, w_in_b)


def _gla_direction(q_raw, zf, v, loglb, l1mlb, s_ref, reverse):
    row = lax.broadcasted_iota(jnp.int32, (CH, CH), 0)
    col = lax.broadcasted_iota(jnp.int32, (CH, CH), 1)
    ls = jnp.minimum(zf, 0.0) - jnp.log1p(jnp.exp(-jnp.abs(zf)))
    c = l1mlb + ls
    logf = jnp.maximum(loglb, c) + jnp.log1p(jnp.exp(-jnp.abs(loglb - c)))
    kk = jnp.exp(c - zf)
    q = q_raw * _sigmoid(q_raw)
    tri = jnp.where((col >= row) if reverse else (col <= row), 1.0, 0.0).astype(BF16)
    hi = logf.astype(BF16)
    r1 = logf - hi.astype(F32)
    mid = r1.astype(BF16)
    lo = (r1 - mid.astype(F32)).astype(BF16)
    b = _dot(tri, hi) + _dot(tri, mid) + _dot(tri, lo)

    def block_ref(n, idx):
        r = b.reshape(CH // n, n, HW)[:, idx:idx + 1, :]
        return jnp.broadcast_to(r, (CH // n, n, HW)).reshape(CH, HW)

    pieces = []
    n = CH
    while n > LEAF:
        h = n // 2
        a = jnp.exp(-jnp.abs(b - block_ref(n, h if reverse else h - 1)))
        same = (row ^ col) < n
        r_up = (row & h) != 0
        c_up = (col & h) != 0
        mask = same & ((~r_up & c_up) if reverse else (r_up & ~c_up))
        pieces.append(((q * a).astype(BF16), (kk * a).astype(BF16), mask))
        n = h
    dl = b - block_ref(LEAF, LEAF // 2)
    mask = ((row ^ col) < LEAF) & ((col >= row) if reverse else (col <= row))
    pieces.append(((q * jnp.exp(dl)).astype(BF16), (kk * jnp.exp(-dl)).astype(BF16), mask))

    edge = 0 if reverse else CH - 1
    b_edge = b[edge:edge + 1, :]
    q_in = (q * jnp.exp(b)).astype(BF16)
    k_out = (kk * jnp.exp(b_edge - b)).astype(BF16)
    d_chunk = jnp.exp(b_edge)
    vb = v.astype(BF16)
    outs = []
    for hd in range(NH):
        hs = slice(hd * HD, (hd + 1) * HD)
        sc = jnp.zeros((CH, CH), F32)
        for qa, ka, m in pieces:
            sc = sc + jnp.where(m, _dot_nt(qa[:, hs], ka[:, hs]), 0.0)
        st = s_ref[hd]
        o = _dot(sc.astype(BF16), vb[:, hs]) + _dot_nt(q_in[:, hs], st.astype(BF16))
        s_ref[hd] = st * d_chunk[:, hs] + _dot_tn(vb[:, hs], k_out[:, hs])
        outs.append(o)
    return jnp.concatenate(outs, axis=-1)


def _gla_kernel(qf_ref, zf_ref, vf_ref, qb_ref, zb_ref, vb_ref, lb_ref, of_ref, ob_ref, sf_ref, sb_ref):
    @pl.when(pl.program_id(1) == 0)
    def _():
        sf_ref[...] = jnp.zeros_like(sf_ref)
        sb_ref[...] = jnp.zeros_like(sb_ref)

    of_ref[0] = _gla_direction(qf_ref[0], zf_ref[0], vf_ref[0], lb_ref[0:1], lb_ref[1:2], sf_ref, False)
    ob_ref[0] = _gla_direction(qb_ref[0], zb_ref[0], vb_ref[0], lb_ref[2:3], lb_ref[3:4], sb_ref, True)


def _gla(pp, lbp, n_ctx_chunks):
    bsz, t_all, _ = pp.shape
    nc = t_all // CH

    def cb(j):
        return jnp.where(j < n_ctx_chunks, n_ctx_chunks - 1 - j, nc - 1 - (j - n_ctx_chunks))

    blk = (1, CH, HW)
    return pl.pallas_call(
        _gla_kernel,
        grid=(bsz, nc),
        in_specs=[pl.BlockSpec(blk, lambda b, j: (b, j, 2)),
                  pl.BlockSpec(blk, lambda b, j: (b, j, 3)),
                  pl.BlockSpec(blk, lambda b, j: (b, j, 5)),
                  pl.BlockSpec(blk, lambda b, j: (b, cb(j), 2)),
                  pl.BlockSpec(blk, lambda b, j: (b, cb(j), 4)),
                  pl.BlockSpec(blk, lambda b, j: (b, cb(j), 5)),
                  pl.BlockSpec((4, HW), lambda b, j: (0, 0))],
        out_specs=[pl.BlockSpec(blk, lambda b, j: (b, j, 0)),
                   pl.BlockSpec(blk, lambda b, j: (b, cb(j), 0))],
        out_shape=[jax.ShapeDtypeStruct((bsz, t_all, HW), F32)] * 2,
        scratch_shapes=[pltpu.VMEM((NH, HD, HD), F32), pltpu.VMEM((NH, HD, HD), F32)],
        compiler_params=_cparams(2),
        name="gla",
    )(pp, pp, pp, pp, pp, pp, lbp)


def _shift_rows(u, k):
    return pltpu.roll(u, k % u.shape[0], axis=0)


def _mixout_kernel(x_ref, bg_ref, u_ref, g_ref, up_ref, un_ref, of_ref, ob_ref, mod_ref, cw_ref, cnw_ref,
                   hnw_ref, wo_ref, nw2_ref, wr_ref, br_ref, xm_ref, h2_ref, slab_ref, cnt_ref):
    j = pl.program_id(1)
    nt = pl.num_programs(1)
    first = jnp.logical_and(pl.program_id(0) == 0, j == 0)

    @pl.when(first)
    def _():
        cnt_ref[...] = jnp.zeros_like(cnt_ref)

    is_ctx = j == 0
    m = mod_ref[0, 0]
    u = u_ref[0]
    cw = cw_ref[...]
    t = lax.broadcasted_iota(jnp.int32, (TM, 1), 0)
    col_in_row = t & (GRID_W - 1)
    keep_l = jnp.where(is_ctx, jnp.where(t == 0, 0.0, 1.0), jnp.where(col_in_row == 0, 0.0, 1.0))
    keep_r = jnp.where(is_ctx, jnp.where(t == TM - 1, 0.0, 1.0), jnp.where(col_in_row == GRID_W - 1, 0.0, 1.0))
    left = jnp.where(keep_l > 0.5, _shift_rows(u, 1), 0.0)
    right = jnp.where(keep_r > 0.5, _shift_rows(u, -1), 0.0)
    y_seq = cw[0:1] * left + cw[1:2] * u + cw[2:3] * right
    hc = CW // 2
    uv = u[:, hc:]
    up = jnp.concatenate([jnp.where(j == 1, 0.0, up_ref[0]), uv[:TM - GRID_W]], axis=0)
    dn = jnp.concatenate([uv[GRID_W:], jnp.where(j == nt - 1, 0.0, un_ref[0])], axis=0)
    y_col = cw[0:1, hc:] * up + cw[1:2, hc:] * uv + cw[2:3, hc:] * dn
    conv = jnp.concatenate([y_seq[:, :hc], jnp.where(is_ctx, y_seq[:, hc:], y_col)], axis=-1)
    y_conv = _rms(bg_ref[0] * conv, cnw_ref[...])

    o = of_ref[0] + ob_ref[0]
    g = g_ref[0]
    hnw = hnw_ref[...]
    recs = []
    for hd in range(NH):
        hs = slice(hd * HD, (hd + 1) * HD)
        recs.append(_rms(o[:, hs], hnw[:, hs]))
    y_rec = jnp.concatenate(recs, axis=-1) * (g * _sigmoid(g))

    y = jnp.concatenate([y_conv, y_rec], axis=-1).astype(BF16)
    xm = x_ref[0] + m[2:3] * _dot(y, wo_ref[...])
    xm_ref[0] = xm
    h2 = _rms(xm, nw2_ref[...]) * (1.0 + m[4:5]) + m[3:4]
    h2_ref[0] = h2

    logit = _dot_hp(h2, wr_ref[...]) + br_ref[...]
    lane = lax.broadcasted_iota(jnp.int32, (TM, LANES), 1).astype(F32)
    ninf = -jnp.inf
    big = 1e9
    gl = jnp.where((lane >= NE) & (lane < NE + N_GROUPS), logit, ninf)
    gmax = jnp.max(gl, axis=-1, keepdims=True)
    gsel = jnp.min(jnp.where(gl == gmax, lane, big), axis=-1, keepdims=True) - NE
    pg = 1.0 / jnp.sum(jnp.exp(gl - gmax), axis=-1, keepdims=True)
    el = jnp.where((lane >= gsel * EPG) & (lane < gsel * EPG + EPG), logit, ninf)
    e1 = jnp.max(el, axis=-1, keepdims=True)
    i1 = jnp.min(jnp.where(el == e1, lane, big), axis=-1, keepdims=True)
    el2 = jnp.where(lane == i1, ninf, el)
    e2 = jnp.max(el2, axis=-1, keepdims=True)
    i2 = jnp.min(jnp.where(el2 == e2, lane, big), axis=-1, keepdims=True)
    r = jnp.exp(e2 - e1)
    g1 = pg / (1.0 + r)
    g2 = pg * r / (1.0 + r)
    oh1 = jnp.where(lane == i1, 1.0, 0.0)
    oh2 = jnp.where(lane == i2, 1.0, 0.0)
    cnt = oh1 + oh2
    rr = lax.broadcasted_iota(jnp.int32, (TM, TM), 0)
    cc = lax.broadcasted_iota(jnp.int32, (TM, TM), 1)
    before = jnp.where(cc < rr, 1.0, 0.0).astype(BF16)
    prior = _dot(before, cnt.astype(BF16)) + cnt_ref[...]
    rank1 = jnp.sum(oh1 * prior, axis=-1, keepdims=True)
    rank2 = jnp.sum(oh2 * prior, axis=-1, keepdims=True)
    cnt_ref[...] += jnp.sum(cnt, axis=0, keepdims=True)
    slab = jnp.where(lane == 0, i1, 0.0)
    for k, val in enumerate((i2, rank1, rank2, g1, g2), start=1):
        slab = jnp.where(lane == k, val, slab)
    slab_ref[0] = slab


def _mixout(xall, pp, o_f, o_b, modall, cw, cnw, hnw, w_out_b, nw2, w_r, b_r):
    bsz, t_all, _ = xall.shape
    nt = t_all // TM
    n64 = t_all // GRID_W
    per = TM // GRID_W
    tile = lambda w, c: pl.BlockSpec((1, TM, w), lambda b, j, c=c: (b, j, c))
    full = lambda shape: pl.BlockSpec(shape, lambda b, j: (0,) * len(shape))
    return pl.pallas_call(
        _mixout_kernel,
        grid=(bsz, nt),
        in_specs=[tile(D, 0),
                  tile(CW, 0), tile(CW, 1), tile(HW, 6),
                  pl.BlockSpec((1, GRID_W, CW // 2), lambda b, j: (b, jnp.maximum(j * per - 1, 0), 3)),
                  pl.BlockSpec((1, GRID_W, CW // 2), lambda b, j: (b, jnp.minimum(j * per + per, n64 - 1), 3)),
                  tile(HW, 0), tile(HW, 0),
                  pl.BlockSpec((1, 1, N_MOD, D), lambda b, j: (b, jnp.minimum(j, 1), 0, 0)),
                  full((3, CW)), full((1, CW)), full((1, HW)), full((D, D)), full((1, D)),
                  full((D, LANES)), full((1, LANES))],
        out_specs=[tile(D, 0), tile(D, 0), tile(LANES, 0), pl.BlockSpec((1, LANES), lambda b, j: (0, 0))],
        out_shape=[jax.ShapeDtypeStruct((bsz, t_all, D), F32),
                   jax.ShapeDtypeStruct((bsz, t_all, D), F32),
                   jax.ShapeDtypeStruct((bsz, t_all, LANES), F32),
                   jax.ShapeDtypeStruct((1, LANES), F32)],
        compiler_params=_cparams(2),
        name="mixout",
    )(xall, pp, pp, pp, pp, pp, o_f, o_b, modall, cw, cnw, hnw, w_out_b, nw2, w_r, b_r)


def _row_copy(src, s, dst, d, sem):
    return pltpu.make_async_copy(src.at[pl.ds(s, 1)], dst.at[pl.ds(d, 1)], sem)


def _dispatch_kernel(dest_ref, h2_ref, xs_in_ref, xs_ref, sem):
    del xs_in_ref
    base = pl.program_id(0) * DCH

    def issue(i, carry):
        tok = lax.shift_right_logical(base + i, 1)
        _row_copy(h2_ref, tok, xs_ref, dest_ref[0, 0, i], sem).start()
        return carry

    lax.fori_loop(0, DCH, issue, 0)

    def drain(i, carry):
        _row_copy(h2_ref, 0, xs_ref, 0, sem).wait()
        return carry

    lax.fori_loop(0, DCH, drain, 0)


def _dispatch(dest, h2, n_rows):
    n_assign = dest.shape[0]
    return pl.pallas_call(
        _dispatch_kernel,
        grid=(n_assign // DCH,),
        in_specs=[pl.BlockSpec((1, 1, DCH), lambda i: (i, 0, 0), memory_space=pltpu.SMEM),
                  pl.BlockSpec(memory_space=pl.ANY),
                  pl.BlockSpec(memory_space=pl.ANY)],
        out_specs=pl.BlockSpec(memory_space=pl.ANY),
        out_shape=jax.ShapeDtypeStruct((n_rows, D), F32),
        scratch_shapes=[pltpu.SemaphoreType.DMA(())],
        input_output_aliases={2: 0},
        compiler_params=_cparams(1),
        name="dispatch",
    )(dest.reshape(n_assign // DCH, 1, DCH), h2, jnp.zeros((n_rows, D), F32))


def _expert_kernel(be_ref, nu_ref, xs_ref, wgu_ref, wd_ref, ys_ref):
    del be_ref
    i = pl.program_id(0)

    @pl.when(i < nu_ref[0])
    def _():
        hmid = _dot(xs_ref[...].astype(BF16), wgu_ref[0])
        a = hmid[:, :DE]
        act = (a * _sigmoid(a) * hmid[:, DE:]).astype(BF16)
        ys_ref[...] = _dot(act, wd_ref[0])

    @pl.when(i >= nu_ref[0])
    def _():
        ys_ref[...] = jnp.zeros_like(ys_ref)


def _experts(block_e, n_used, xs, w_gu_b, w_down_b):
    n_rows = xs.shape[0]
    grid_spec = pltpu.PrefetchScalarGridSpec(
        num_scalar_prefetch=2,
        grid=(n_rows // MOE_P,),
        in_specs=[pl.BlockSpec((MOE_P, D), lambda i, be, nu: (i, 0)),
                  pl.BlockSpec((1, D, 2 * DE), lambda i, be, nu: (be[i], 0, 0)),
                  pl.BlockSpec((1, DE, D), lambda i, be, nu: (be[i], 0, 0))],
        out_specs=pl.BlockSpec((MOE_P, D), lambda i, be, nu: (i, 0)),
    )
    return pl.pallas_call(
        _expert_kernel,
        grid_spec=grid_spec,
        out_shape=jax.ShapeDtypeStruct((n_rows, D), F32),
        compiler_params=_cparams(1),
        name="experts",
    )(block_e, n_used, xs, w_gu_b, w_down_b)


def _combine_kernel(dest_ref, xm_ref, slab_ref, mod_ref, fnw_ref, ys_ref, o_ref, buf, sem, *, final):
    def issue(r, carry):
        _row_copy(ys_ref, dest_ref[0, 0, 2 * r], buf.at[0], r, sem).start()
        _row_copy(ys_ref, dest_ref[0, 0, 2 * r + 1], buf.at[1], r, sem).start()
        return carry

    lax.fori_loop(0, TM, issue, 0)

    def drain(r, carry):
        _row_copy(ys_ref, 0, buf.at[0], 0, sem).wait()
        return carry

    lax.fori_loop(0, 2 * TM, drain, 0)
    slab = slab_ref[0]
    f = slab[:, 4:5] * buf[0] + slab[:, 5:6] * buf[1]
    x = xm_ref[0] + mod_ref[0, 0][5:6] * f
    o_ref[0] = _rms(x, fnw_ref[...]) if final else x


def _combine(dest, xmid, slab, modall, fnw, ys, final):
    bsz, t_all, _ = xmid.shape
    nt = t_all // TM
    out_t = t_all - TM if final else t_all
    out_map = (lambda b, j: (b, jnp.maximum(j - 1, 0), 0)) if final else (lambda b, j: (b, j, 0))
    return pl.pallas_call(
        functools.partial(_combine_kernel, final=final),
        grid=(bsz, nt),
        in_specs=[pl.BlockSpec((1, 1, 2 * TM), lambda b, j: (b * nt + j, 0, 0), memory_space=pltpu.SMEM),
                  pl.BlockSpec((1, TM, D), lambda b, j: (b, j, 0)),
                  pl.BlockSpec((1, TM, LANES), lambda b, j: (b, j, 0)),
                  pl.BlockSpec((1, 1, N_MOD, D), lambda b, j: (b, jnp.minimum(j, 1), 0, 0)),
                  pl.BlockSpec((1, D), lambda b, j: (0, 0)),
                  pl.BlockSpec(memory_space=pl.ANY)],
        out_specs=pl.BlockSpec((1, TM, D), out_map),
        out_shape=jax.ShapeDtypeStruct((bsz, out_t, D), F32),
        scratch_shapes=[pltpu.VMEM((2, TM, D), F32), pltpu.SemaphoreType.DMA(())],
        compiler_params=_cparams(2),
        name="combine_final" if final else "combine",
    )(dest.reshape(bsz * nt, 1, 2 * TM), xmid, slab, modall, fnw, ys)


def _lower_bounds(hg_lb):
    p = jax.nn.softmax(hg_lb.astype(F32), axis=1)
    cs = jnp.cumsum(p, axis=1)
    return cs - cs[:, :1]


def kernel(x, c, ctx, c_ctx, norm_w, w_ada, b_ada, w_in, conv_w, conv_norm_w, hg_lb, hg_norm_w, w_out, w_rg, b_rg,
           w_re, b_re, w_e_gu, w_e_down, final_norm_w):
    bsz, seq, _ = x.shape
    n_ctx = ctx.shape[1]
    assert n_ctx == TM and seq % TM == 0 and (bsz * (n_ctx + seq) * 2) % DCH == 0
    t_all = n_ctx + seq
    n_tok = bsz * t_all
    n_assign = 2 * n_tok
    n_blocks = -(-(n_assign + NE * (MOE_P - 1)) // MOE_P)
    n_rows = n_blocks * MOE_P

    xall = jnp.concatenate([ctx, x], axis=1)
    cc = jnp.zeros((16, D), F32).at[:bsz].set(c).at[bsz].set(c_ctx)
    mod = _ada(cc, w_ada, b_ada).reshape(DEPTH, 16, N_MOD, D)
    lb = _lower_bounds(hg_lb)
    lbp = jnp.stack([jnp.log(lb[0]), jnp.log1p(-lb[0]), jnp.log(lb[1]), jnp.log1p(-lb[1])], axis=1)
    w_r = jnp.concatenate([w_re, w_rg, jnp.zeros((DEPTH, D, LANES - NE - N_GROUPS), F32)], axis=-1)
    b_r = jnp.concatenate([b_re, b_rg, jnp.zeros((DEPTH, LANES - NE - N_GROUPS), F32)], axis=-1)

    out = None
    for l in range(DEPTH):
        final = l == DEPTH - 1
        modall = jnp.stack([jnp.broadcast_to(mod[l, bsz], (bsz, N_MOD, D)), mod[l, :bsz]], axis=1)
        pp = _inproj(xall, norm_w[l, 0].reshape(1, D), modall, w_in[l].astype(BF16))
        o_f, o_b = _gla(pp, lbp[l], n_ctx // CH)
        xmid, h2, slab, counts = _mixout(
            xall, pp, o_f, o_b, modall, conv_w[l], conv_norm_w[l].reshape(1, CW),
            jnp.tile(hg_norm_w[l], NH).reshape(1, HW), w_out[l].astype(BF16), norm_w[l, 1].reshape(1, D),
            w_r[l], b_r[l].reshape(1, LANES))
        cnt = counts[0, :NE].astype(jnp.int32)
        padded = (cnt + MOE_P - 1) // MOE_P * MOE_P
        pad_ends = jnp.cumsum(padded)
        pad_starts = pad_ends - padded
        route = slab.reshape(n_tok, LANES)
        eid = route[:, 0:2].astype(jnp.int32)
        rank = route[:, 2:4].astype(jnp.int32)
        dest = (pad_starts[eid] + rank).reshape(n_assign)
        block_e = jnp.minimum(jnp.searchsorted(pad_ends, jnp.arange(n_blocks, dtype=jnp.int32) * MOE_P, side='right'),
                              NE - 1).astype(jnp.int32)
        n_used = (pad_ends[-1:] // MOE_P).astype(jnp.int32)
        xs = _dispatch(dest, h2.reshape(n_tok, D), n_rows)
        ys = _experts(block_e, n_used, xs, w_e_gu[l].astype(BF16), w_e_down[l].astype(BF16))
        res = _combine(dest, xmid, slab, modall, final_norm_w.reshape(1, D), ys, final)
        if final:
            out = res
        else:
            xall = res
    return out
```

```python
import functools

import jax
import jax.numpy as jnp
from jax import lax
from jax.experimental import pallas as pl
from jax.experimental.pallas import tpu as pltpu

F32 = jnp.float32
BF16 = jnp.bfloat16

D = 1024
DEPTH = 4
GRID_W = 64
CW = 512
HW = 512
NH = 4
HD = HW // NH
PW = 3 * CW + 5 * HW
PPW = PW - CW
N_GROUPS = 4
EPG = 8
NE = N_GROUPS * EPG
DE = 512
N_MOD = 6
EPS = 1e-6

TM = 256
CH = 128
LEAF = 16
MOE_P = 256
LANES = 128

VMEM_LIMIT = 56 * 1024 * 1024


def _cparams(n_axes):
    return pltpu.CompilerParams(dimension_semantics=("arbitrary",) * n_axes,
                                vmem_limit_bytes=VMEM_LIMIT)


def _dot(a, b):
    return jnp.dot(a, b, preferred_element_type=F32)


def _dot_nt(a, b):
    return lax.dot_general(a, b, (((1,), (1,)), ((), ())), preferred_element_type=F32)


def _dot_tn(a, b):
    return lax.dot_general(a, b, (((0,), (0,)), ((), ())), preferred_element_type=F32)


def _split2(x):
    hi = x.astype(BF16)
    lo = (x - hi.astype(F32)).astype(BF16)
    return hi, lo


def _dot_hp(a, b):
    a_hi, a_lo = _split2(a)
    b_hi, b_lo = _split2(b)
    return _dot(a_hi, b_hi) + _dot(a_hi, b_lo) + _dot(a_lo, b_hi)


def _sigmoid(x):
    return 1.0 / (1.0 + jnp.exp(-x))


def _rms(x, w):
    return x * lax.rsqrt(jnp.mean(x * x, axis=-1, keepdims=True) + EPS) * w


def _ada_kernel(cc_ref, w_ref, b_ref, o_ref):
    s = cc_ref[...]
    s = s * _sigmoid(s)
    o_ref[0] = _dot_hp(s, w_ref[0]) + b_ref[0]


def _ada(cc, w_ada, b_ada):
    tn = 1536
    n = N_MOD * D
    return pl.pallas_call(
        _ada_kernel,
        grid=(DEPTH, n // tn),
        in_specs=[pl.BlockSpec((16, D), lambda l, j: (0, 0)),
                  pl.BlockSpec((1, D, tn), lambda l, j: (l, 0, j)),
                  pl.BlockSpec((1, 1, tn), lambda l, j: (l, 0, j))],
        out_specs=pl.BlockSpec((1, 16, tn), lambda l, j: (l, 0, j)),
        out_shape=jax.ShapeDtypeStruct((DEPTH, 16, n), F32),
        compiler_params=_cparams(2),
        name="ada",
    )(cc, w_ada, b_ada.reshape(DEPTH, 1, n))


def _inproj_kernel(x_ref, nw_ref, mod_ref, w_ref, o_ref, w_b):
    @pl.when(jnp.logical_and(pl.program_id(0) == 0, pl.program_id(1) == 0))
    def _():
        def cast_rows(r, carry):
            rows = pl.ds(pl.multiple_of(r * 128, 128), 128)
            w_b[rows, :] = w_ref[rows, :].astype(BF16)
            return carry

        lax.fori_loop(0, D // 128, cast_rows, 0)

    m = mod_ref[0, 0]
    h = _rms(x_ref[0], nw_ref[...]) * (1.0 + m[1:2]) + m[0:1]
    hb = h.astype(BF16)
    blk = lambda k: _dot(hb, w_b[:, k * 512:(k + 1) * 512])
    o_ref[0, :, 0:512] = blk(0)
    o_ref[0, :, 512:1024] = blk(1) * blk(2)
    for k in range(3, 8):
        o_ref[0, :, (k - 1) * 512:k * 512] = blk(k)


def _inproj(xall, nw, modall, w_in):
    bsz, t_all, _ = xall.shape
    return pl.pallas_call(
        _inproj_kernel,
        grid=(bsz, t_all // TM),
        in_specs=[pl.BlockSpec((1, TM, D), lambda b, j: (b, j, 0)),
                  pl.BlockSpec((1, D), lambda b, j: (0, 0)),
                  pl.BlockSpec((1, 1, N_MOD, D), lambda b, j: (b, jnp.minimum(j, 1), 0, 0)),
                  pl.BlockSpec((D, PW), lambda b, j: (0, 0), pipeline_mode=pl.Buffered(1))],
        out_specs=pl.BlockSpec((1, TM, PPW), lambda b, j: (b, j, 0)),
        out_shape=jax.ShapeDtypeStruct((bsz, t_all, PPW), F32),
        scratch_shapes=[pltpu.VMEM((D, PW), BF16)],
        compiler_params=_cparams(2),
        name="inproj",
    )(xall, nw, modall, w_in)


def _gla_direction(q_raw, zf, v, loglb, l1mlb, s_ref, reverse):
    row = lax.broadcasted_iota(jnp.int32, (CH, CH), 0)
    col = lax.broadcasted_iota(jnp.int32, (CH, CH), 1)
    ls = jnp.minimum(zf, 0.0) - jnp.log1p(jnp.exp(-jnp.abs(zf)))
    c = l1mlb + ls
    logf = jnp.maximum(loglb, c) + jnp.log1p(jnp.exp(-jnp.abs(loglb - c)))
    kk = jnp.exp(c - zf)
    q = q_raw * _sigmoid(q_raw)
    tri = jnp.where((col >= row) if reverse else (col <= row), 1.0, 0.0).astype(BF16)
    hi = logf.astype(BF16)
    r1 = logf - hi.astype(F32)
    mid = r1.astype(BF16)
    lo = (r1 - mid.astype(F32)).astype(BF16)
    b = _dot(tri, hi) + _dot(tri, mid) + _dot(tri, lo)

    def block_ref(n, idx):
        r = b.reshape(CH // n, n, HW)[:, idx:idx + 1, :]
        return jnp.broadcast_to(r, (CH // n, n, HW)).reshape(CH, HW)

    pieces = []
    n = CH
    while n > LEAF:
        h = n // 2
        a = jnp.exp(-jnp.abs(b - block_ref(n, h if reverse else h - 1)))
        same = (row ^ col) < n
        r_up = (row & h) != 0
        c_up = (col & h) != 0
        mask = same & ((~r_up & c_up) if reverse else (r_up & ~c_up))
        pieces.append(((q * a).astype(BF16), (kk * a).astype(BF16), mask))
        n = h
    dl = b - block_ref(LEAF, LEAF // 2)
    mask = ((row ^ col) < LEAF) & ((col >= row) if reverse else (col <= row))
    pieces.append(((q * jnp.exp(dl)).astype(BF16), (kk * jnp.exp(-dl)).astype(BF16), mask))

    edge = 0 if reverse else CH - 1
    b_edge = b[edge:edge + 1, :]
    q_in = (q * jnp.exp(b)).astype(BF16)
    k_out = (kk * jnp.exp(b_edge - b)).astype(BF16)
    d_chunk = jnp.exp(b_edge)
    vb = v.astype(BF16)
    outs = []
    for hd in range(NH):
        hs = slice(hd * HD, (hd + 1) * HD)
        sc = jnp.zeros((CH, CH), F32)
        for qa, ka, m in pieces:
            sc = sc + jnp.where(m, _dot_nt(qa[:, hs], ka[:, hs]), 0.0)
        st = s_ref[hd]
        o = _dot(sc.astype(BF16), vb[:, hs]) + _dot_nt(q_in[:, hs], st.astype(BF16))
        s_ref[hd] = st * d_chunk[:, hs] + _dot_tn(vb[:, hs], k_out[:, hs])
        outs.append(o)
    return jnp.concatenate(outs, axis=-1)


def _gla_kernel(qf_ref, zf_ref, vf_ref, qb_ref, zb_ref, vb_ref, lb_ref, of_ref, ob_ref, sf_ref, sb_ref):
    @pl.when(pl.program_id(1) == 0)
    def _():
        sf_ref[...] = jnp.zeros_like(sf_ref)
        sb_ref[...] = jnp.zeros_like(sb_ref)

    of_ref[0] = _gla_direction(qf_ref[0], zf_ref[0], vf_ref[0], lb_ref[0:1], lb_ref[1:2], sf_ref, False)
    ob_ref[0] = _gla_direction(qb_ref[0], zb_ref[0], vb_ref[0], lb_ref[2:3], lb_ref[3:4], sb_ref, True)


def _gla(pp, lbp, n_ctx_chunks):
    bsz, t_all, _ = pp.shape
    nc = t_all // CH

    def cb(j):
        return jnp.where(j < n_ctx_chunks, n_ctx_chunks - 1 - j, nc - 1 - (j - n_ctx_chunks))

    blk = (1, CH, HW)
    return pl.pallas_call(
        _gla_kernel,
        grid=(bsz, nc),
        in_specs=[pl.BlockSpec(blk, lambda b, j: (b, j, 2)),
                  pl.BlockSpec(blk, lambda b, j: (b, j, 3)),
                  pl.BlockSpec(blk, lambda b, j: (b, j, 5)),
                  pl.BlockSpec(blk, lambda b, j: (b, cb(j), 2)),
                  pl.BlockSpec(blk, lambda b, j: (b, cb(j), 4)),
                  pl.BlockSpec(blk, lambda b, j: (b, cb(j), 5)),
                  pl.BlockSpec((4, HW), lambda b, j: (0, 0))],
        out_specs=[pl.BlockSpec(blk, lambda b, j: (b, j, 0)),
                   pl.BlockSpec(blk, lambda b, j: (b, cb(j), 0))],
        out_shape=[jax.ShapeDtypeStruct((bsz, t_all, HW), F32)] * 2,
        scratch_shapes=[pltpu.VMEM((NH, HD, HD), F32), pltpu.VMEM((NH, HD, HD), F32)],
        compiler_params=_cparams(2),
        name="gla",
    )(pp, pp, pp, pp, pp, pp, lbp)


def _shift_rows(u, k):
    return pltpu.roll(u, k % u.shape[0], axis=0)


def _mixout_kernel(x_ref, bg_ref, u_ref, g_ref, up_ref, un_ref, of_ref, ob_ref, mod_ref, cw_ref, cnw_ref,
                   hnw_ref, wo_ref, nw2_ref, wr_ref, br_ref, xm_ref, h2_ref, slab_ref, runs_ref, cnt_ref, wo_b):
    j = pl.program_id(1)
    nt = pl.num_programs(1)
    first = jnp.logical_and(pl.program_id(0) == 0, j == 0)

    @pl.when(first)
    def _():
        cnt_ref[...] = jnp.zeros_like(cnt_ref)
        wo_b[...] = wo_ref[...].astype(BF16)

    is_ctx = j == 0
    m = mod_ref[0, 0]
    u = u_ref[0]
    cw = cw_ref[...]
    t = lax.broadcasted_iota(jnp.int32, (TM, 1), 0)
    col_in_row = t & (GRID_W - 1)
    keep_l = jnp.where(is_ctx, jnp.where(t == 0, 0.0, 1.0), jnp.where(col_in_row == 0, 0.0, 1.0))
    keep_r = jnp.where(is_ctx, jnp.where(t == TM - 1, 0.0, 1.0), jnp.where(col_in_row == GRID_W - 1, 0.0, 1.0))
    left = jnp.where(keep_l > 0.5, _shift_rows(u, 1), 0.0)
    right = jnp.where(keep_r > 0.5, _shift_rows(u, -1), 0.0)
    y_seq = cw[0:1] * left + cw[1:2] * u + cw[2:3] * right
    hc = CW // 2
    uv = u[:, hc:]
    up = jnp.concatenate([jnp.where(j == 1, 0.0, up_ref[0]), uv[:TM - GRID_W]], axis=0)
    dn = jnp.concatenate([uv[GRID_W:], jnp.where(j == nt - 1, 0.0, un_ref[0])], axis=0)
    y_col = cw[0:1, hc:] * up + cw[1:2, hc:] * uv + cw[2:3, hc:] * dn
    conv = jnp.concatenate([y_seq[:, :hc], jnp.where(is_ctx, y_seq[:, hc:], y_col)], axis=-1)
    y_conv = _rms(bg_ref[0] * conv, cnw_ref[...])

    o = of_ref[0] + ob_ref[0]
    g = g_ref[0]
    hnw = hnw_ref[...]
    recs = []
    for hd in range(NH):
        hs = slice(hd * HD, (hd + 1) * HD)
        recs.append(_rms(o[:, hs], hnw[:, hs]))
    y_rec = jnp.concatenate(recs, axis=-1) * (g * _sigmoid(g))

    y = jnp.concatenate([y_conv, y_rec], axis=-1).astype(BF16)
    xm = x_ref[0] + m[2:3] * _dot(y, wo_b[...])
    xm_ref[0] = xm
    h2 = _rms(xm, nw2_ref[...]) * (1.0 + m[4:5]) + m[3:4]
    h2_ref[0] = h2.astype(BF16)

    logit = _dot_hp(h2, wr_ref[...]) + br_ref[...]
    lane = lax.broadcasted_iota(jnp.int32, (TM, LANES), 1).astype(F32)
    ninf = -jnp.inf
    big = 1e9
    gl = jnp.where((lane >= NE) & (lane < NE + N_GROUPS), logit, ninf)
    gmax = jnp.max(gl, axis=-1, keepdims=True)
    gsel = jnp.min(jnp.where(gl == gmax, lane, big), axis=-1, keepdims=True) - NE
    pg = 1.0 / jnp.sum(jnp.exp(gl - gmax), axis=-1, keepdims=True)
    el = jnp.where((lane >= gsel * EPG) & (lane < gsel * EPG + EPG), logit, ninf)
    e1 = jnp.max(el, axis=-1, keepdims=True)
    i1 = jnp.min(jnp.where(el == e1, lane, big), axis=-1, keepdims=True)
    el2 = jnp.where(lane == i1, ninf, el)
    e2 = jnp.max(el2, axis=-1, keepdims=True)
    i2 = jnp.min(jnp.where(el2 == e2, lane, big), axis=-1, keepdims=True)
    r = jnp.exp(e2 - e1)
    g1 = pg / (1.0 + r)
    g2 = pg * r / (1.0 + r)
    oh1 = jnp.where(lane == i1, 1.0, 0.0)
    oh2 = jnp.where(lane == i2, 1.0, 0.0)
    cnt = oh1 + oh2
    rr = lax.broadcasted_iota(jnp.int32, (TM, TM), 0)
    cc = lax.broadcasted_iota(jnp.int32, (TM, TM), 1)
    before = jnp.where(cc < rr, 1.0, 0.0).astype(BF16)
    prior = _dot(before, cnt.astype(BF16))
    c_tile = jnp.sum(cnt, axis=0, keepdims=True)
    li = lax.broadcasted_iota(jnp.int32, (LANES, LANES), 0)
    lj = lax.broadcasted_iota(jnp.int32, (LANES, LANES), 1)
    lower_e = jnp.where(li < lj, 1.0, 0.0).astype(BF16)
    c_hi, c_lo = _split2(jnp.broadcast_to(c_tile, (8, LANES)))
    off = (_dot(c_hi, lower_e) + _dot(c_lo, lower_e))[0:1]
    pos = prior + off
    lpos1 = jnp.sum(oh1 * pos, axis=-1, keepdims=True)
    lpos2 = jnp.sum(oh2 * pos, axis=-1, keepdims=True)
    sub = lax.broadcasted_iota(jnp.int32, (8, LANES), 0)
    runs_ref[0] = jnp.where(sub == 0, c_tile, jnp.where(sub == 1, cnt_ref[...], jnp.where(sub == 2, off, 0.0)))
    cnt_ref[...] += c_tile
    slab = jnp.where(lane == 0, i1, 0.0)
    for k, val in enumerate((i2, lpos1, lpos2, g1, g2), start=1):
        slab = jnp.where(lane == k, val, slab)
    slab_ref[0] = slab


def _mixout(xall, pp, o_f, o_b, modall, cw, cnw, hnw, w_out, nw2, w_r, b_r):
    bsz, t_all, _ = xall.shape
    nt = t_all // TM
    n64 = t_all // GRID_W
    per = TM // GRID_W
    tile = lambda w, c: pl.BlockSpec((1, TM, w), lambda b, j, c=c: (b, j, c))
    full = lambda shape: pl.BlockSpec(shape, lambda b, j: (0,) * len(shape))
    once = lambda shape: pl.BlockSpec(shape, lambda b, j: (0,) * len(shape), pipeline_mode=pl.Buffered(1))
    return pl.pallas_call(
        _mixout_kernel,
        grid=(bsz, nt),
        in_specs=[tile(D, 0),
                  tile(CW, 0), tile(CW, 1), tile(HW, 6),
                  pl.BlockSpec((1, GRID_W, CW // 2), lambda b, j: (b, jnp.maximum(j * per - 1, 0), 3)),
                  pl.BlockSpec((1, GRID_W, CW // 2), lambda b, j: (b, jnp.minimum(j * per + per, n64 - 1), 3)),
                  tile(HW, 0), tile(HW, 0),
                  pl.BlockSpec((1, 1, N_MOD, D), lambda b, j: (b, jnp.minimum(j, 1), 0, 0)),
                  full((3, CW)), full((1, CW)), full((1, HW)), once((D, D)), full((1, D)),
                  full((D, LANES)), full((1, LANES))],
        out_specs=[tile(D, 0), tile(D, 0), tile(LANES, 0),
                   pl.BlockSpec((1, 8, LANES), lambda b, j: (b * nt + j, 0, 0)),
                   pl.BlockSpec((1, LANES), lambda b, j: (0, 0))],
        out_shape=[jax.ShapeDtypeStruct((bsz, t_all, D), F32),
                   jax.ShapeDtypeStruct((bsz, t_all, D), BF16),
                   jax.ShapeDtypeStruct((bsz, t_all, LANES), F32),
                   jax.ShapeDtypeStruct((bsz * nt, 8, LANES), F32),
                   jax.ShapeDtypeStruct((1, LANES), F32)],
        scratch_shapes=[pltpu.VMEM((D, D), BF16)],
        compiler_params=_cparams(2),
        name="mixout",
    )(xall, pp, pp, pp, pp, pp, o_f, o_b, modall, cw, cnw, hnw, w_out, nw2, w_r, b_r)


RUN_BITS = TM.bit_length()
ROW_TILE = (8, LANES)


def _run_copies(tab_ref, local_ref, sorted_ref, sem, *, to_sorted, wait):
    def per_expert(e, carry):
        ln = tab_ref[0, 0, e]
        lo = tab_ref[0, 0, NE + e]
        go = tab_ref[0, 0, 2 * NE + e]
        for bit in range(RUN_BITS - 1, -1, -1):
            size = 1 << bit
            done = lax.shift_left(lax.shift_right_logical(ln, bit + 1), bit + 1)

            @pl.when((ln & size) != 0)
            def _():
                a = local_ref.at[pl.ds(lo + done, size)]
                b = sorted_ref.at[pl.ds(go + done, size)]
                cp = pltpu.make_async_copy(a, b, sem) if to_sorted else pltpu.make_async_copy(b, a, sem)
                if wait:
                    cp.wait()
                else:
                    cp.start()
        return carry

    lax.fori_loop(0, NE, per_expert, 0)


def _local_positions(slab):
    col = lax.broadcasted_iota(jnp.int32, (TM, 2 * TM), 1).astype(F32)
    return col == slab[:, 2:3], col == slab[:, 3:4]


def _dispatch_kernel(tab_ref, h2_ref, slab_ref, xs_in_ref, xs_ref, buf, sem):
    del xs_in_ref
    p1, p2 = _local_positions(slab_ref[...])
    perm_t = jnp.where(p1 | p2, 1.0, 0.0).astype(BF16)
    buf[...] = _dot_tn(perm_t, h2_ref[...]).reshape(2 * TM, *ROW_TILE)
    _run_copies(tab_ref, buf, xs_ref, sem, to_sorted=True, wait=False)
    _run_copies(tab_ref, buf, xs_ref, sem, to_sorted=True, wait=True)


def _dispatch(tab, h2, slab, n_rows):
    n_tiles = tab.shape[0]
    return pl.pallas_call(
        _dispatch_kernel,
        grid=(n_tiles,),
        in_specs=[pl.BlockSpec((1, 1, LANES), lambda i: (i, 0, 0), memory_space=pltpu.SMEM),
                  pl.BlockSpec((TM, D), lambda i: (i, 0)),
                  pl.BlockSpec((TM, LANES), lambda i: (i, 0)),
                  pl.BlockSpec(memory_space=pl.ANY)],
        out_specs=pl.BlockSpec(memory_space=pl.ANY),
        out_shape=jax.ShapeDtypeStruct((n_rows, *ROW_TILE), F32),
        scratch_shapes=[pltpu.VMEM((2 * TM, *ROW_TILE), F32), pltpu.SemaphoreType.DMA(())],
        input_output_aliases={3: 0},
        compiler_params=_cparams(1),
        name="dispatch",
    )(tab, h2, slab, jnp.zeros((n_rows, *ROW_TILE), F32))


def _expert_kernel(be_ref, nu_ref, xs_ref, wgu_ref, wd_ref, ys_ref, wgu_b, wd_b):
    i = pl.program_id(0)

    @pl.when(jnp.logical_or(i == 0, be_ref[i] != be_ref[jnp.maximum(i - 1, 0)]))
    def _():
        wgu_b[...] = wgu_ref[0].astype(BF16)
        wd_b[...] = wd_ref[0].astype(BF16)

    @pl.when(i < nu_ref[0])
    def _():
        hmid = _dot(xs_ref[...].reshape(MOE_P, D).astype(BF16), wgu_b[...])
        a = hmid[:, :DE]
        act = (a * _sigmoid(a) * hmid[:, DE:]).astype(BF16)
        ys_ref[...] = _dot(act, wd_b[...]).reshape(MOE_P, *ROW_TILE)

    @pl.when(i >= nu_ref[0])
    def _():
        ys_ref[...] = jnp.zeros_like(ys_ref)


def _experts(block_e, n_used, xs, w_gu, w_down):
    n_rows = xs.shape[0]
    grid_spec = pltpu.PrefetchScalarGridSpec(
        num_scalar_prefetch=2,
        grid=(n_rows // MOE_P,),
        in_specs=[pl.BlockSpec((MOE_P, *ROW_TILE), lambda i, be, nu: (i, 0, 0)),
                  pl.BlockSpec((1, D, 2 * DE), lambda i, be, nu: (be[i], 0, 0)),
                  pl.BlockSpec((1, DE, D), lambda i, be, nu: (be[i], 0, 0))],
        out_specs=pl.BlockSpec((MOE_P, *ROW_TILE), lambda i, be, nu: (i, 0, 0)),
        scratch_shapes=[pltpu.VMEM((D, 2 * DE), BF16), pltpu.VMEM((DE, D), BF16)],
    )
    return pl.pallas_call(
        _expert_kernel,
        grid_spec=grid_spec,
        out_shape=jax.ShapeDtypeStruct((n_rows, *ROW_TILE), F32),
        compiler_params=_cparams(1),
        name="experts",
    )(block_e, n_used, xs, w_gu, w_down)


def _combine_kernel(tab_ref, xm_ref, slab_ref, mod_ref, fnw_ref, ys_ref, o_ref, buf, sem, *, final):
    _run_copies(tab_ref, buf, ys_ref, sem, to_sorted=False, wait=False)
    _run_copies(tab_ref, buf, ys_ref, sem, to_sorted=False, wait=True)
    slab = slab_ref[0]
    p1, p2 = _local_positions(slab)
    yb = buf[...].reshape(2 * TM, D).astype(BF16)
    y1 = _dot(jnp.where(p1, 1.0, 0.0).astype(BF16), yb)
    y2 = _dot(jnp.where(p2, 1.0, 0.0).astype(BF16), yb)
    f = slab[:, 4:5] * y1 + slab[:, 5:6] * y2
    x = xm_ref[0] + mod_ref[0, 0][5:6] * f
    o_ref[0] = _rms(x, fnw_ref[...]) if final else x


def _combine(tab, xmid, slab, modall, fnw, ys, final):
    bsz, t_all, _ = xmid.shape
    nt = t_all // TM
    out_t = t_all - TM if final else t_all
    out_map = (lambda b, j: (b, jnp.maximum(j - 1, 0), 0)) if final else (lambda b, j: (b, j, 0))
    return pl.pallas_call(
        functools.partial(_combine_kernel, final=final),
        grid=(bsz, nt),
        in_specs=[pl.BlockSpec((1, 1, LANES), lambda b, j: (b * nt + j, 0, 0), memory_space=pltpu.SMEM),
                  pl.BlockSpec((1, TM, D), lambda b, j: (b, j, 0)),
                  pl.BlockSpec((1, TM, LANES), lambda b, j: (b, j, 0)),
                  pl.BlockSpec((1, 1, N_MOD, D), lambda b, j: (b, jnp.minimum(j, 1), 0, 0)),
                  pl.BlockSpec((1, D), lambda b, j: (0, 0)),
                  pl.BlockSpec(memory_space=pl.ANY)],
        out_specs=pl.BlockSpec((1, TM, D), out_map),
        out_shape=jax.ShapeDtypeStruct((bsz, out_t, D), F32),
        scratch_shapes=[pltpu.VMEM((2 * TM, *ROW_TILE), F32), pltpu.SemaphoreType.DMA(())],
        compiler_params=_cparams(2),
        name="combine_final" if final else "combine",
    )(tab, xmid, slab, modall, fnw, ys)


def _lower_bounds(hg_lb):
    p = jax.nn.softmax(hg_lb.astype(F32), axis=1)
    cs = jnp.cumsum(p, axis=1)
    return cs - cs[:, :1]


def kernel(x, c, ctx, c_ctx, norm_w, w_ada, b_ada, w_in, conv_w, conv_norm_w, hg_lb, hg_norm_w, w_out, w_rg, b_rg,
           w_re, b_re, w_e_gu, w_e_down, final_norm_w):
    bsz, seq, _ = x.shape
    n_ctx = ctx.shape[1]
    assert n_ctx == TM and seq % TM == 0
    t_all = n_ctx + seq
    n_tok = bsz * t_all
    n_assign = 2 * n_tok
    n_blocks = -(-(n_assign + NE * (MOE_P - 1)) // MOE_P)
    n_rows = n_blocks * MOE_P

    xall = jnp.concatenate([ctx, x], axis=1)
    cc = jnp.zeros((16, D), F32).at[:bsz].set(c).at[bsz].set(c_ctx)
    mod = _ada(cc, w_ada, b_ada).reshape(DEPTH, 16, N_MOD, D)
    lb = _lower_bounds(hg_lb)
    lbp = jnp.stack([jnp.log(lb[0]), jnp.log1p(-lb[0]), jnp.log(lb[1]), jnp.log1p(-lb[1])], axis=1)
    w_r = jnp.concatenate([w_re, w_rg, jnp.zeros((DEPTH, D, LANES - NE - N_GROUPS), F32)], axis=-1)
    b_r = jnp.concatenate([b_re, b_rg, jnp.zeros((DEPTH, LANES - NE - N_GROUPS), F32)], axis=-1)

    out = None
    for l in range(DEPTH):
        final = l == DEPTH - 1
        modall = jnp.stack([jnp.broadcast_to(mod[l, bsz], (bsz, N_MOD, D)), mod[l, :bsz]], axis=1)
        pp = _inproj(xall, norm_w[l, 0].reshape(1, D), modall, w_in[l])
        o_f, o_b = _gla(pp, lbp[l], n_ctx // CH)
        xmid, h2, slab, runs, counts = _mixout(
            xall, pp, o_f, o_b, modall, conv_w[l], conv_norm_w[l].reshape(1, CW),
            jnp.tile(hg_norm_w[l], NH).reshape(1, HW), w_out[l], norm_w[l, 1].reshape(1, D),
            w_r[l], b_r[l].reshape(1, LANES))
        cnt = counts[0, :NE].astype(jnp.int32)
        padded = (cnt + MOE_P - 1) // MOE_P * MOE_P
        pad_ends = jnp.cumsum(padded)
        pad_starts = pad_ends - padded
        run_len = runs[:, 0, :NE].astype(jnp.int32)
        run_global = pad_starts[None, :] + runs[:, 1, :NE].astype(jnp.int32)
        run_local = runs[:, 2, :NE].astype(jnp.int32)
        tab = jnp.concatenate([run_len, run_local, run_global, jnp.zeros_like(run_len)], axis=1)[:, None, :]
        block_start = jnp.arange(n_blocks, dtype=jnp.int32) * MOE_P
        block_e = jnp.minimum(jnp.sum((pad_ends[None, :] <= block_start[:, None]).astype(jnp.int32), axis=1), NE - 1)
        n_used = (pad_ends[-1:] // MOE_P).astype(jnp.int32)
        xs = _dispatch(tab, h2.reshape(n_tok, D), slab.reshape(n_tok, LANES), n_rows)
        ys = _experts(block_e, n_used, xs, w_e_gu[l], w_e_down[l])
        res = _combine(tab, xmid, slab, modall, final_norm_w.reshape(1, D), ys, final)
        if final:
            out = res
        else:
            xall = res
    return out
```

```python
import functools

import jax
import jax.numpy as jnp
import numpy as np
from jax import lax
from jax.experimental import pallas as pl
from jax.experimental.pallas import tpu as pltpu

F32 = jnp.float32
BF16 = jnp.bfloat16

D = 1024
DEPTH = 4
GRID_W = 64
CW = 512
HW = 512
NH = 4
HD = HW // NH
PW = 3 * CW + 5 * HW
PPW = PW - CW
N_GROUPS = 4
EPG = 8
NE = N_GROUPS * EPG
DE = 512
N_MOD = 6
EPS = 1e-6
LOG2E = 1.4426950408889634

TM = 256
CH = 128
LEAF = 16
MOE_P = 256
LANES = 128

VMEM_LIMIT = 56 * 1024 * 1024


def _cparams(n_axes, flags=None):
    return pltpu.CompilerParams(dimension_semantics=("arbitrary",) * n_axes,
                                vmem_limit_bytes=VMEM_LIMIT, flags=flags)


def _dot(a, b):
    return jnp.dot(a, b, preferred_element_type=F32)


def _dot_nt(a, b):
    return lax.dot_general(a, b, (((1,), (1,)), ((), ())), preferred_element_type=F32)


def _dot_tn(a, b):
    return lax.dot_general(a, b, (((0,), (0,)), ((), ())), preferred_element_type=F32)


def _split2(x):
    hi = x.astype(BF16)
    lo = (x - hi.astype(F32)).astype(BF16)
    return hi, lo


def _dot_hp(a, b):
    a_hi, a_lo = _split2(a)
    b_hi, b_lo = _split2(b)
    return _dot(a_hi, b_hi) + _dot(a_hi, b_lo) + _dot(a_lo, b_hi)


def _sigmoid(x):
    return 1.0 / (1.0 + jnp.exp(-x))


def _rms(x, w):
    return x * lax.rsqrt(jnp.mean(x * x, axis=-1, keepdims=True) + EPS) * w


def _ada_kernel(cc_ref, w_ref, b_ref, o_ref):
    s = cc_ref[...]
    s = s * _sigmoid(s)
    o_ref[0] = _dot_hp(s, w_ref[0]) + b_ref[0]


def _ada(cc, w_ada, b_ada):
    tn = 1536
    n = N_MOD * D
    return pl.pallas_call(
        _ada_kernel,
        grid=(DEPTH, n // tn),
        in_specs=[pl.BlockSpec((16, D), lambda l, j: (0, 0)),
                  pl.BlockSpec((1, D, tn), lambda l, j: (l, 0, j)),
                  pl.BlockSpec((1, 1, tn), lambda l, j: (l, 0, j))],
        out_specs=pl.BlockSpec((1, 16, tn), lambda l, j: (l, 0, j)),
        out_shape=jax.ShapeDtypeStruct((DEPTH, 16, n), F32),
        compiler_params=_cparams(2),
        name="ada",
    )(cc, w_ada, b_ada.reshape(DEPTH, 1, n))


def _inproj_kernel(x_ref, nw_ref, mod_ref, w_ref, o_ref, w_b):
    @pl.when(jnp.logical_and(pl.program_id(0) == 0, pl.program_id(1) == 0))
    def _():
        def cast_rows(r, carry):
            rows = pl.ds(pl.multiple_of(r * 128, 128), 128)
            w_b[rows, :] = w_ref[0, rows, :].astype(BF16)
            return carry

        lax.fori_loop(0, D // 128, cast_rows, 0)

    m = mod_ref[0, 0]
    h = _rms(x_ref[0], nw_ref[...]) * (1.0 + m[1:2]) + m[0:1]
    hb = h.astype(BF16)
    blk = lambda k: _dot(hb, w_b[:, k * 512:(k + 1) * 512])
    o_ref[0, :, 0:512] = blk(0)
    o_ref[0, :, 512:1024] = blk(1) * blk(2)
    for k in range(3, 8):
        o_ref[0, :, (k - 1) * 512:k * 512] = blk(k)


def _inproj(xall, nw, modall, w_in, l):
    bsz, t_all, _ = xall.shape
    return pl.pallas_call(
        _inproj_kernel,
        grid=(bsz, t_all // TM),
        in_specs=[pl.BlockSpec((1, TM, D), lambda b, j: (b, j, 0)),
                  pl.BlockSpec((1, D), lambda b, j: (0, 0)),
                  pl.BlockSpec((1, 1, N_MOD, D), lambda b, j: (b, jnp.minimum(j, 1), 0, 0)),
                  pl.BlockSpec((1, D, PW), lambda b, j: (l, 0, 0), pipeline_mode=pl.Buffered(1))],
        out_specs=pl.BlockSpec((1, TM, PPW), lambda b, j: (b, j, 0)),
        out_shape=jax.ShapeDtypeStruct((bsz, t_all, PPW), F32),
        scratch_shapes=[pltpu.VMEM((D, PW), BF16)],
        compiler_params=_cparams(2),
        name="inproj",
    )(xall, nw, modall, w_in)


def _silu(x):
    h = 0.5 * x
    return h + h * jnp.tanh(h)


def _neg_abs(x):
    bits = lax.bitcast_convert_type(x, jnp.uint32) | jnp.uint32(0x80000000)
    return lax.bitcast_convert_type(bits, F32)


def _gla_tables():
    t = np.arange(CH)[:, None]
    s = np.arange(CH)[None, :]
    x = t ^ s
    lvl = np.full((CH, CH), -1, np.int32)
    n, k = CH, 0
    while n > LEAF:
        lvl[(x < n) & (x >= n // 2)] = k
        n, k = n // 2, k + 1
    lvl[x < LEAF] = k
    fwd = np.where(s <= t, lvl, -1)
    return (np.stack([s <= t, s >= t]).astype(np.float32), np.stack([fwd, fwd.T]).astype(np.int32))


def _gla_direction(q_raw, zf, v, loglb, l1mlb, tri, lvl, s_ref, reverse):
    ls = jnp.minimum(zf, 0.0) - jnp.log(1.0 + jnp.exp(_neg_abs(zf)))
    c = l1mlb + ls
    logf = jnp.maximum(loglb, c) + jnp.log(1.0 + jnp.exp(_neg_abs(loglb - c)))
    kk = jnp.exp(c - zf)
    q = _silu(q_raw)
    lf2 = logf * LOG2E
    hi = lf2.astype(BF16)
    lo = (lf2 - hi.astype(F32)).astype(BF16)
    b = _dot(tri, hi) + _dot(tri, lo)

    def block_ref(n, idx):
        r = b.reshape(CH // n, n, HW)[:, idx:idx + 1, :]
        return jnp.broadcast_to(r, (CH // n, n, HW)).reshape(CH, HW)

    qh = q.astype(BF16)
    kh = kk.astype(BF16)
    pow2 = lambda e: jnp.exp2(e.astype(BF16))
    pieces = []
    n = CH
    while n > LEAF:
        h = n // 2
        a = pow2(_neg_abs(b - block_ref(n, h if reverse else h - 1)))
        pieces.append((qh * a, kh * a))
        n = h
    dl = b - block_ref(LEAF, LEAF // 2)
    pieces.append((qh * pow2(dl), kh * pow2(-dl)))

    edge = 0 if reverse else CH - 1
    b_edge = b[edge:edge + 1, :]
    q_in = qh * pow2(b)
    k_out = kh * pow2(b_edge - b)
    d_chunk = jnp.exp2(b_edge)
    vb = v.astype(BF16)
    outs = []
    for hd in range(NH):
        hs = slice(hd * HD, (hd + 1) * HD)
        sc = jnp.zeros((CH, CH), F32)
        for k, (qa, ka) in enumerate(pieces):
            sc = jnp.where(lvl == k, _dot_nt(qa[:, hs], ka[:, hs]), sc)
        st = s_ref[hd]
        o = _dot(sc.astype(BF16), vb[:, hs]) + _dot_nt(q_in[:, hs], st.astype(BF16))
        s_ref[hd] = st * d_chunk[:, hs] + _dot_tn(vb[:, hs], k_out[:, hs])
        outs.append(o)
    return jnp.concatenate(outs, axis=-1)


def _gla_kernel(qf_ref, zf_ref, vf_ref, qb_ref, zb_ref, vb_ref, lb_ref, tri_ref, lvl_ref, of_ref, ob_ref,
                sf_ref, sb_ref):
    @pl.when(pl.program_id(1) == 0)
    def _():
        sf_ref[...] = jnp.zeros_like(sf_ref)
        sb_ref[...] = jnp.zeros_like(sb_ref)

    of_ref[0] = _gla_direction(qf_ref[0], zf_ref[0], vf_ref[0], lb_ref[0:1], lb_ref[1:2], tri_ref[0], lvl_ref[0],
                               sf_ref, False)
    ob_ref[0] = _gla_direction(qb_ref[0], zb_ref[0], vb_ref[0], lb_ref[2:3], lb_ref[3:4], tri_ref[1], lvl_ref[1],
                               sb_ref, True)


def _gla(pp, lbp, n_ctx_chunks):
    bsz, t_all, _ = pp.shape
    nc = t_all // CH

    def cb(j):
        return jnp.where(j < n_ctx_chunks, n_ctx_chunks - 1 - j, nc - 1 - (j - n_ctx_chunks))

    blk = (1, CH, HW)
    tri, lvl = _gla_tables()
    return pl.pallas_call(
        _gla_kernel,
        grid=(bsz, nc),
        in_specs=[pl.BlockSpec(blk, lambda b, j: (b, j, 2)),
                  pl.BlockSpec(blk, lambda b, j: (b, j, 3)),
                  pl.BlockSpec(blk, lambda b, j: (b, j, 5)),
                  pl.BlockSpec(blk, lambda b, j: (b, cb(j), 2)),
                  pl.BlockSpec(blk, lambda b, j: (b, cb(j), 4)),
                  pl.BlockSpec(blk, lambda b, j: (b, cb(j), 5)),
                  pl.BlockSpec((4, HW), lambda b, j: (0, 0)),
                  pl.BlockSpec((2, CH, CH), lambda b, j: (0, 0, 0)),
                  pl.BlockSpec((2, CH, CH), lambda b, j: (0, 0, 0))],
        out_specs=[pl.BlockSpec(blk, lambda b, j: (b, j, 0)),
                   pl.BlockSpec(blk, lambda b, j: (b, cb(j), 0))],
        out_shape=[jax.ShapeDtypeStruct((bsz, t_all, HW), F32)] * 2,
        scratch_shapes=[pltpu.VMEM((NH, HD, HD), F32), pltpu.VMEM((NH, HD, HD), F32)],
        compiler_params=_cparams(2),
        name="gla",
    )(pp, pp, pp, pp, pp, pp, lbp, jnp.asarray(tri, BF16), jnp.asarray(lvl))


def _shift_rows(u, k):
    return pltpu.roll(u, k % u.shape[0], axis=0)


def _mixout_kernel(x_ref, bg_ref, u_ref, g_ref, up_ref, un_ref, of_ref, ob_ref, mod_ref, cw_ref, cnw_ref,
                   hnw_ref, wo_ref, nw2_ref, wr_ref, br_ref, xm_ref, h2_ref, slab_ref, runs_ref, cnt_ref, wo_b):
    j = pl.program_id(1)
    nt = pl.num_programs(1)
    first = jnp.logical_and(pl.program_id(0) == 0, j == 0)

    @pl.when(first)
    def _():
        cnt_ref[...] = jnp.zeros_like(cnt_ref)
        wo_b[...] = wo_ref[0].astype(BF16)

    is_ctx = j == 0
    m = mod_ref[0, 0]
    u = u_ref[0]
    cw = cw_ref[...]
    t = lax.broadcasted_iota(jnp.int32, (TM, 1), 0)
    col_in_row = t & (GRID_W - 1)
    keep_l = jnp.where(is_ctx, jnp.where(t == 0, 0.0, 1.0), jnp.where(col_in_row == 0, 0.0, 1.0))
    keep_r = jnp.where(is_ctx, jnp.where(t == TM - 1, 0.0, 1.0), jnp.where(col_in_row == GRID_W - 1, 0.0, 1.0))
    left = jnp.where(keep_l > 0.5, _shift_rows(u, 1), 0.0)
    right = jnp.where(keep_r > 0.5, _shift_rows(u, -1), 0.0)
    y_seq = cw[0:1] * left + cw[1:2] * u + cw[2:3] * right
    hc = CW // 2
    uv = u[:, hc:]
    up = jnp.concatenate([jnp.where(j == 1, 0.0, up_ref[0]), uv[:TM - GRID_W]], axis=0)
    dn = jnp.concatenate([uv[GRID_W:], jnp.where(j == nt - 1, 0.0, un_ref[0])], axis=0)
    y_col = cw[0:1, hc:] * up + cw[1:2, hc:] * uv + cw[2:3, hc:] * dn
    conv = jnp.concatenate([y_seq[:, :hc], jnp.where(is_ctx, y_seq[:, hc:], y_col)], axis=-1)
    y_conv = _rms(bg_ref[0] * conv, cnw_ref[...])

    o = of_ref[0] + ob_ref[0]
    g = g_ref[0]
    hnw = hnw_ref[...]
    recs = []
    for hd in range(NH):
        hs = slice(hd * HD, (hd + 1) * HD)
        recs.append(_rms(o[:, hs], hnw[:, hs]))
    y_rec = jnp.concatenate(recs, axis=-1) * (g * _sigmoid(g))

    y = jnp.concatenate([y_conv, y_rec], axis=-1).astype(BF16)
    xm = x_ref[0] + m[2:3] * _dot(y, wo_b[...])
    xm_ref[0] = xm
    h2 = _rms(xm, nw2_ref[...]) * (1.0 + m[4:5]) + m[3:4]
    h2_ref[0] = h2.astype(BF16)

    logit = _dot_hp(h2, wr_ref[...]) + br_ref[...]
    lane = lax.broadcasted_iota(jnp.int32, (TM, LANES), 1).astype(F32)
    ninf = -jnp.inf
    big = 1e9
    gl = jnp.where((lane >= NE) & (lane < NE + N_GROUPS), logit, ninf)
    gmax = jnp.max(gl, axis=-1, keepdims=True)
    gsel = jnp.min(jnp.where(gl == gmax, lane, big), axis=-1, keepdims=True) - NE
    pg = 1.0 / jnp.sum(jnp.exp(gl - gmax), axis=-1, keepdims=True)
    el = jnp.where((lane >= gsel * EPG) & (lane < gsel * EPG + EPG), logit, ninf)
    e1 = jnp.max(el, axis=-1, keepdims=True)
    i1 = jnp.min(jnp.where(el == e1, lane, big), axis=-1, keepdims=True)
    el2 = jnp.where(lane == i1, ninf, el)
    e2 = jnp.max(el2, axis=-1, keepdims=True)
    i2 = jnp.min(jnp.where(el2 == e2, lane, big), axis=-1, keepdims=True)
    r = jnp.exp(e2 - e1)
    g1 = pg / (1.0 + r)
    g2 = pg * r / (1.0 + r)
    oh1 = jnp.where(lane == i1, 1.0, 0.0)
    oh2 = jnp.where(lane == i2, 1.0, 0.0)
    cnt = oh1 + oh2
    rr = lax.broadcasted_iota(jnp.int32, (TM, TM), 0)
    cc = lax.broadcasted_iota(jnp.int32, (TM, TM), 1)
    before = jnp.where(cc < rr, 1.0, 0.0).astype(BF16)
    prior = _dot(before, cnt.astype(BF16))
    c_tile = jnp.sum(cnt, axis=0, keepdims=True)
    li = lax.broadcasted_iota(jnp.int32, (LANES, LANES), 0)
    lj = lax.broadcasted_iota(jnp.int32, (LANES, LANES), 1)
    lower_e = jnp.where(li < lj, 1.0, 0.0).astype(BF16)
    c_hi, c_lo = _split2(jnp.broadcast_to(c_tile, (8, LANES)))
    off = (_dot(c_hi, lower_e) + _dot(c_lo, lower_e))[0:1]
    pos = prior + off
    lpos1 = jnp.sum(oh1 * pos, axis=-1, keepdims=True)
    lpos2 = jnp.sum(oh2 * pos, axis=-1, keepdims=True)
    sub = lax.broadcasted_iota(jnp.int32, (8, LANES), 0)
    runs_ref[0] = jnp.where(sub == 0, c_tile, jnp.where(sub == 1, cnt_ref[...], jnp.where(sub == 2, off, 0.0)))
    cnt_ref[...] += c_tile
    slab = jnp.where(lane == 0, i1, 0.0)
    for k, val in enumerate((i2, lpos1, lpos2, g1, g2), start=1):
        slab = jnp.where(lane == k, val, slab)
    slab_ref[0] = slab


def _mixout(xall, pp, o_f, o_b, modall, cw, cnw, hnw, w_out, nw2, w_r, b_r, l):
    bsz, t_all, _ = xall.shape
    nt = t_all // TM
    n64 = t_all // GRID_W
    per = TM // GRID_W
    tile = lambda w, c: pl.BlockSpec((1, TM, w), lambda b, j, c=c: (b, j, c))
    full = lambda shape: pl.BlockSpec(shape, lambda b, j: (0,) * len(shape))
    layer = lambda shape: pl.BlockSpec((1, *shape), lambda b, j: (l,) + (0,) * len(shape), pipeline_mode=pl.Buffered(1))
    return pl.pallas_call(
        _mixout_kernel,
        grid=(bsz, nt),
        in_specs=[tile(D, 0),
                  tile(CW, 0), tile(CW, 1), tile(HW, 6),
                  pl.BlockSpec((1, GRID_W, CW // 2), lambda b, j: (b, jnp.maximum(j * per - 1, 0), 3)),
                  pl.BlockSpec((1, GRID_W, CW // 2), lambda b, j: (b, jnp.minimum(j * per + per, n64 - 1), 3)),
                  tile(HW, 0), tile(HW, 0),
                  pl.BlockSpec((1, 1, N_MOD, D), lambda b, j: (b, jnp.minimum(j, 1), 0, 0)),
                  full((3, CW)), full((1, CW)), full((1, HW)), layer((D, D)), full((1, D)),
                  full((D, LANES)), full((1, LANES))],
        out_specs=[tile(D, 0), tile(D, 0), tile(LANES, 0),
                   pl.BlockSpec((1, 8, LANES), lambda b, j: (b * nt + j, 0, 0)),
                   pl.BlockSpec((1, LANES), lambda b, j: (0, 0))],
        out_shape=[jax.ShapeDtypeStruct((bsz, t_all, D), F32),
                   jax.ShapeDtypeStruct((bsz, t_all, D), BF16),
                   jax.ShapeDtypeStruct((bsz, t_all, LANES), F32),
                   jax.ShapeDtypeStruct((bsz * nt, 8, LANES), F32),
                   jax.ShapeDtypeStruct((1, LANES), F32)],
        scratch_shapes=[pltpu.VMEM((D, D), BF16)],
        compiler_params=_cparams(2),
        name="mixout",
    )(xall, pp, pp, pp, pp, pp, o_f, o_b, modall, cw, cnw, hnw, w_out, nw2, w_r, b_r)


RUN_BITS = TM.bit_length()
ROW_TILE = (8, LANES)


def _run_copies(tab_ref, local_ref, sorted_ref, sem, *, to_sorted, wait):
    def per_expert(e, carry):
        ln = tab_ref[0, 0, e]
        lo = tab_ref[0, 0, NE + e]
        go = tab_ref[0, 0, 2 * NE + e]
        for bit in range(RUN_BITS - 1, -1, -1):
            size = 1 << bit
            done = lax.shift_left(lax.shift_right_logical(ln, bit + 1), bit + 1)

            @pl.when((ln & size) != 0)
            def _():
                a = local_ref.at[pl.ds(lo + done, size)]
                b = sorted_ref.at[pl.ds(go + done, size)]
                cp = pltpu.make_async_copy(a, b, sem) if to_sorted else pltpu.make_async_copy(b, a, sem)
                if wait:
                    cp.wait()
                else:
                    cp.start()
        return carry

    lax.fori_loop(0, NE, per_expert, 0)


def _local_positions(slab):
    col = lax.broadcasted_iota(jnp.int32, (TM, 2 * TM), 1).astype(F32)
    return col == slab[:, 2:3], col == slab[:, 3:4]


def _dispatch_kernel(tab_ref, ptab_ref, h2_ref, slab_ref, xs_ref, buf, zbuf, sem):
    @pl.when(pl.program_id(0) == 0)
    def _():
        zbuf[...] = jnp.zeros_like(zbuf)
        _run_copies(ptab_ref, zbuf, xs_ref, sem, to_sorted=True, wait=False)
        _run_copies(ptab_ref, zbuf, xs_ref, sem, to_sorted=True, wait=True)

        def zero_block(i, carry):
            cp = pltpu.make_async_copy(zbuf, xs_ref.at[pl.ds(i * MOE_P, MOE_P)], sem)
            cp.start()
            cp.wait()
            return carry

        lax.fori_loop(ptab_ref[0, 0, 3 * NE], xs_ref.shape[0] // MOE_P, zero_block, 0)

    p1, p2 = _local_positions(slab_ref[...])
    perm_t = jnp.where(p1 | p2, 1.0, 0.0).astype(BF16)
    buf[...] = _dot_tn(perm_t, h2_ref[...]).reshape(2 * TM, *ROW_TILE)
    _run_copies(tab_ref, buf, xs_ref, sem, to_sorted=True, wait=False)
    _run_copies(tab_ref, buf, xs_ref, sem, to_sorted=True, wait=True)


def _dispatch(tab, pad_tab, h2, slab, n_rows):
    n_tiles = tab.shape[0]
    return pl.pallas_call(
        _dispatch_kernel,
        grid=(n_tiles,),
        in_specs=[pl.BlockSpec((1, 1, LANES), lambda i: (i, 0, 0), memory_space=pltpu.SMEM),
                  pl.BlockSpec((1, 1, LANES), lambda i: (0, 0, 0), memory_space=pltpu.SMEM),
                  pl.BlockSpec((TM, D), lambda i: (i, 0)),
                  pl.BlockSpec((TM, LANES), lambda i: (i, 0))],
        out_specs=pl.BlockSpec(memory_space=pl.ANY),
        out_shape=jax.ShapeDtypeStruct((n_rows, *ROW_TILE), F32),
        scratch_shapes=[pltpu.VMEM((2 * TM, *ROW_TILE), F32), pltpu.VMEM((MOE_P, *ROW_TILE), F32),
                        pltpu.SemaphoreType.DMA(())],
        compiler_params=_cparams(1),
        name="dispatch",
    )(tab, pad_tab, h2, slab)


def _expert_kernel(be_ref, nu_ref, xs_ref, wgu_ref, wd_ref, ys_ref, wgu_b, wd_b):
    i = pl.program_id(0)

    @pl.when(jnp.logical_or(i == 0, be_ref[i] != be_ref[jnp.maximum(i - 1, 0)]))
    def _():
        wgu_b[...] = wgu_ref[0, 0].astype(BF16)
        wd_b[...] = wd_ref[0, 0].astype(BF16)

    @pl.when(i < nu_ref[0])
    def _():
        hmid = _dot(xs_ref[...].reshape(MOE_P, D).astype(BF16), wgu_b[...])
        a = hmid[:, :DE]
        act = (a * _sigmoid(a) * hmid[:, DE:]).astype(BF16)
        ys_ref[...] = _dot(act, wd_b[...]).reshape(MOE_P, *ROW_TILE)

    @pl.when(i >= nu_ref[0])
    def _():
        ys_ref[...] = jnp.zeros_like(ys_ref)


def _experts(block_e, n_used, xs, w_gu, w_down, l):
    n_rows = xs.shape[0]
    grid_spec = pltpu.PrefetchScalarGridSpec(
        num_scalar_prefetch=2,
        grid=(n_rows // MOE_P,),
        in_specs=[pl.BlockSpec((MOE_P, *ROW_TILE), lambda i, be, nu: (jnp.minimum(i, nu[0] - 1), 0, 0)),
                  pl.BlockSpec((1, 1, D, 2 * DE), lambda i, be, nu: (l, be[i], 0, 0)),
                  pl.BlockSpec((1, 1, DE, D), lambda i, be, nu: (l, be[i], 0, 0))],
        out_specs=pl.BlockSpec((MOE_P, *ROW_TILE), lambda i, be, nu: (i, 0, 0)),
        scratch_shapes=[pltpu.VMEM((D, 2 * DE), BF16), pltpu.VMEM((DE, D), BF16)],
    )
    return pl.pallas_call(
        _expert_kernel,
        grid_spec=grid_spec,
        out_shape=jax.ShapeDtypeStruct((n_rows, *ROW_TILE), F32),
        compiler_params=_cparams(1),
        name="experts",
    )(block_e, n_used, xs, w_gu, w_down)


def _combine_kernel(tab_ref, xm_ref, slab_ref, mod_ref, fnw_ref, ys_ref, o_ref, buf, sem, *, final):
    _run_copies(tab_ref, buf, ys_ref, sem, to_sorted=False, wait=False)
    _run_copies(tab_ref, buf, ys_ref, sem, to_sorted=False, wait=True)
    slab = slab_ref[0]
    p1, p2 = _local_positions(slab)
    yb = buf[...].reshape(2 * TM, D).astype(BF16)
    y1 = _dot(jnp.where(p1, 1.0, 0.0).astype(BF16), yb)
    y2 = _dot(jnp.where(p2, 1.0, 0.0).astype(BF16), yb)
    f = slab[:, 4:5] * y1 + slab[:, 5:6] * y2
    x = xm_ref[0] + mod_ref[0, 0][5:6] * f
    o_ref[0] = _rms(x, fnw_ref[...]) if final else x


def _combine(tab, xmid, slab, modall, fnw, ys, final):
    bsz, t_all, _ = xmid.shape
    nt = t_all // TM
    out_t = t_all - TM if final else t_all
    out_map = (lambda b, j: (b, jnp.maximum(j - 1, 0), 0)) if final else (lambda b, j: (b, j, 0))
    return pl.pallas_call(
        functools.partial(_combine_kernel, final=final),
        grid=(bsz, nt),
        in_specs=[pl.BlockSpec((1, 1, LANES), lambda b, j: (b * nt + j, 0, 0), memory_space=pltpu.SMEM),
                  pl.BlockSpec((1, TM, D), lambda b, j: (b, j, 0)),
                  pl.BlockSpec((1, TM, LANES), lambda b, j: (b, j, 0)),
                  pl.BlockSpec((1, 1, N_MOD, D), lambda b, j: (b, jnp.minimum(j, 1), 0, 0)),
                  pl.BlockSpec((1, D), lambda b, j: (0, 0)),
                  pl.BlockSpec(memory_space=pl.ANY)],
        out_specs=pl.BlockSpec((1, TM, D), out_map),
        out_shape=jax.ShapeDtypeStruct((bsz, out_t, D), F32),
        scratch_shapes=[pltpu.VMEM((2 * TM, *ROW_TILE), F32), pltpu.SemaphoreType.DMA(())],
        compiler_params=_cparams(2),
        name="combine_final" if final else "combine",
    )(tab, xmid, slab, modall, fnw, ys)


def _lower_bounds(hg_lb):
    p = jax.nn.softmax(hg_lb.astype(F32), axis=1)
    cs = jnp.cumsum(p, axis=1)
    return cs - cs[:, :1]


def kernel(x, c, ctx, c_ctx, norm_w, w_ada, b_ada, w_in, conv_w, conv_norm_w, hg_lb, hg_norm_w, w_out, w_rg, b_rg,
           w_re, b_re, w_e_gu, w_e_down, final_norm_w):
    bsz, seq, _ = x.shape
    n_ctx = ctx.shape[1]
    assert n_ctx == TM and seq % TM == 0
    t_all = n_ctx + seq
    n_tok = bsz * t_all
    n_assign = 2 * n_tok
    n_blocks = -(-(n_assign + NE * (MOE_P - 1)) // MOE_P)
    n_rows = n_blocks * MOE_P

    xall = jnp.concatenate([ctx, x], axis=1)
    cc = jnp.zeros((16, D), F32).at[:bsz].set(c).at[bsz].set(c_ctx)
    mod = _ada(cc, w_ada, b_ada).reshape(DEPTH, 16, N_MOD, D)
    lb = _lower_bounds(hg_lb)
    lbp = jnp.stack([jnp.log(lb[0]), jnp.log1p(-lb[0]), jnp.log(lb[1]), jnp.log1p(-lb[1])], axis=1)
    w_r = jnp.concatenate([w_re, w_rg, jnp.zeros((DEPTH, D, LANES - NE - N_GROUPS), F32)], axis=-1)
    b_r = jnp.concatenate([b_re, b_rg, jnp.zeros((DEPTH, LANES - NE - N_GROUPS), F32)], axis=-1)

    out = None
    for l in range(DEPTH):
        final = l == DEPTH - 1
        modall = jnp.stack([jnp.broadcast_to(mod[l, bsz], (bsz, N_MOD, D)), mod[l, :bsz]], axis=1)
        pp = _inproj(xall, norm_w[l, 0].reshape(1, D), modall, w_in, l)
        o_f, o_b = _gla(pp, lbp[l], n_ctx // CH)
        xmid, h2, slab, runs, counts = _mixout(
            xall, pp, o_f, o_b, modall, conv_w[l], conv_norm_w[l].reshape(1, CW),
            jnp.tile(hg_norm_w[l], NH).reshape(1, HW), w_out, norm_w[l, 1].reshape(1, D),
            w_r[l], b_r[l].reshape(1, LANES), l)
        cnt = counts[0, :NE].astype(jnp.int32)
        padded = (cnt + MOE_P - 1) // MOE_P * MOE_P
        pad_ends = jnp.cumsum(padded)
        pad_starts = pad_ends - padded
        run_len = runs[:, 0, :NE].astype(jnp.int32)
        run_global = pad_starts[None, :] + runs[:, 1, :NE].astype(jnp.int32)
        run_local = runs[:, 2, :NE].astype(jnp.int32)
        tab = jnp.concatenate([run_len, run_local, run_global, jnp.zeros_like(run_len)], axis=1)[:, None, :]
        block_start = jnp.arange(n_blocks, dtype=jnp.int32) * MOE_P
        block_e = jnp.minimum(jnp.sum((pad_ends[None, :] <= block_start[:, None]).astype(jnp.int32), axis=1), NE - 1)
        n_used = (pad_ends[-1:] // MOE_P).astype(jnp.int32)
        zero = jnp.zeros_like(cnt)
        pad_tab = jnp.concatenate([padded - cnt, zero, pad_starts + cnt, zero + n_used]).reshape(1, 1, LANES)
        xs = _dispatch(tab, pad_tab, h2.reshape(n_tok, D), slab.reshape(n_tok, LANES), n_rows)
        ys = _experts(block_e, n_used, xs, w_e_gu, w_e_down, l)
        res = _combine(tab, xmid, slab, modall, final_norm_w.reshape(1, D), ys, final)
        if final:
            out = res
        else:
            xall = res
    return out
```

```python
import functools

import jax
import jax.numpy as jnp
import numpy as np
from jax import lax
from jax.experimental import pallas as pl
from jax.experimental.pallas import tpu as pltpu

F32 = jnp.float32
BF16 = jnp.bfloat16

D = 1024
DEPTH = 4
GRID_W = 64
CW = 512
HW = 512
NH = 4
HD = HW // NH
PW = 3 * CW + 5 * HW
PPW = PW - CW
N_GROUPS = 4
EPG = 8
NE = N_GROUPS * EPG
DE = 512
N_MOD = 6
EPS = 1e-6
LOG2E = 1.4426950408889634

TM = 256
CH = 128
LEAF = 16
MOE_P = 256
LANES = 128

VMEM_LIMIT = 56 * 1024 * 1024


def _cparams(n_axes, flags=None):
    return pltpu.CompilerParams(dimension_semantics=("arbitrary",) * n_axes,
                                vmem_limit_bytes=VMEM_LIMIT, flags=flags)


def _dot(a, b):
    return jnp.dot(a, b, preferred_element_type=F32)


def _dot_nt(a, b):
    return lax.dot_general(a, b, (((1,), (1,)), ((), ())), preferred_element_type=F32)


def _dot_tn(a, b):
    return lax.dot_general(a, b, (((0,), (0,)), ((), ())), preferred_element_type=F32)


def _split2(x):
    hi = x.astype(BF16)
    lo = (x - hi.astype(F32)).astype(BF16)
    return hi, lo


def _dot_hp(a, b):
    a_hi, a_lo = _split2(a)
    b_hi, b_lo = _split2(b)
    return _dot(a_hi, b_hi) + _dot(a_hi, b_lo) + _dot(a_lo, b_hi)


def _sigmoid(x):
    return 1.0 / (1.0 + jnp.exp(-x))


def _rms(x, w):
    return x * lax.rsqrt(jnp.mean(x * x, axis=-1, keepdims=True) + EPS) * w


def _ada_kernel(cc_ref, w_ref, b_ref, o_ref):
    s = cc_ref[...]
    s = s * _sigmoid(s)
    o_ref[0] = _dot_hp(s, w_ref[0]) + b_ref[0]


def _ada(cc, w_ada, b_ada):
    tn = 1536
    n = N_MOD * D
    return pl.pallas_call(
        _ada_kernel,
        grid=(DEPTH, n // tn),
        in_specs=[pl.BlockSpec((16, D), lambda l, j: (0, 0)),
                  pl.BlockSpec((1, D, tn), lambda l, j: (l, 0, j)),
                  pl.BlockSpec((1, 1, tn), lambda l, j: (l, 0, j))],
        out_specs=pl.BlockSpec((1, 16, tn), lambda l, j: (l, 0, j)),
        out_shape=jax.ShapeDtypeStruct((DEPTH, 16, n), F32),
        compiler_params=_cparams(2),
        name="ada",
    )(cc, w_ada, b_ada.reshape(DEPTH, 1, n))


def _inproj_kernel(x_ref, nw_ref, mod_ref, w_ref, o_ref, w_b):
    @pl.when(jnp.logical_and(pl.program_id(0) == 0, pl.program_id(1) == 0))
    def _():
        def cast_rows(r, carry):
            rows = pl.ds(pl.multiple_of(r * 128, 128), 128)
            w_b[rows, :] = w_ref[0, rows, :].astype(BF16)
            return carry

        lax.fori_loop(0, D // 128, cast_rows, 0)

    m = mod_ref[0, 0]
    h = _rms(x_ref[0], nw_ref[...]) * (1.0 + m[1:2]) + m[0:1]
    hb = h.astype(BF16)
    blk = lambda k: _dot(hb, w_b[:, k * 512:(k + 1) * 512])
    o_ref[0, :, 0:512] = blk(0)
    o_ref[0, :, 512:1024] = blk(1) * blk(2)
    for k in range(3, 8):
        o_ref[0, :, (k - 1) * 512:k * 512] = blk(k)


def _inproj(xall, nw, modall, w_in, l):
    bsz, t_all, _ = xall.shape
    return pl.pallas_call(
        _inproj_kernel,
        grid=(bsz, t_all // TM),
        in_specs=[pl.BlockSpec((1, TM, D), lambda b, j: (b, j, 0)),
                  pl.BlockSpec((1, D), lambda b, j: (0, 0)),
                  pl.BlockSpec((1, 1, N_MOD, D), lambda b, j: (b, jnp.minimum(j, 1), 0, 0)),
                  pl.BlockSpec((1, D, PW), lambda b, j: (l, 0, 0), pipeline_mode=pl.Buffered(1))],
        out_specs=pl.BlockSpec((1, TM, PPW), lambda b, j: (b, j, 0)),
        out_shape=jax.ShapeDtypeStruct((bsz, t_all, PPW), F32),
        scratch_shapes=[pltpu.VMEM((D, PW), BF16)],
        compiler_params=_cparams(2),
        name="inproj",
    )(xall, nw, modall, w_in)


def _silu(x):
    h = 0.5 * x
    return h + h * jnp.tanh(h)


def _neg_abs(x):
    bits = lax.bitcast_convert_type(x, jnp.uint32) | jnp.uint32(0x80000000)
    return lax.bitcast_convert_type(bits, F32)


def _gla_tables():
    t = np.arange(CH)[:, None]
    s = np.arange(CH)[None, :]
    x = t ^ s
    lvl = np.full((CH, CH), -1, np.int32)
    n, k = CH, 0
    while n > LEAF:
        lvl[(x < n) & (x >= n // 2)] = k
        n, k = n // 2, k + 1
    lvl[x < LEAF] = k
    fwd = np.where(s <= t, lvl, -1)
    return (np.stack([s <= t, s >= t]).astype(np.float32), np.stack([fwd, fwd.T]).astype(np.int32))


def _gla_direction(q_raw, zf, v, loglb, l1mlb, tri, lvl, s_ref, reverse):
    ls = jnp.minimum(zf, 0.0) - jnp.log(1.0 + jnp.exp(_neg_abs(zf)))
    c = l1mlb + ls
    logf = jnp.maximum(loglb, c) + jnp.log(1.0 + jnp.exp(_neg_abs(loglb - c)))
    kk = jnp.exp(c - zf)
    q = _silu(q_raw)
    lf2 = logf * LOG2E
    hi = lf2.astype(BF16)
    lo = (lf2 - hi.astype(F32)).astype(BF16)
    b = _dot(tri, hi) + _dot(tri, lo)

    def block_ref(n, idx):
        r = b.reshape(CH // n, n, HW)[:, idx:idx + 1, :]
        return jnp.broadcast_to(r, (CH // n, n, HW)).reshape(CH, HW)

    qh = q.astype(BF16)
    kh = kk.astype(BF16)
    pow2 = lambda e: jnp.exp2(e.astype(BF16))
    pieces = []
    n = CH
    while n > LEAF:
        h = n // 2
        a = pow2(_neg_abs(b - block_ref(n, h if reverse else h - 1)))
        pieces.append((qh * a, kh * a))
        n = h
    dl = b - block_ref(LEAF, LEAF // 2)
    pieces.append((qh * pow2(dl), kh * pow2(-dl)))

    edge = 0 if reverse else CH - 1
    b_edge = b[edge:edge + 1, :]
    q_in = qh * pow2(b)
    k_out = kh * pow2(b_edge - b)
    d_chunk = jnp.exp2(b_edge)
    vb = v.astype(BF16)
    outs = []
    for hd in range(NH):
        hs = slice(hd * HD, (hd + 1) * HD)
        sc = jnp.zeros((CH, CH), F32)
        for k, (qa, ka) in enumerate(pieces):
            sc = jnp.where(lvl == k, _dot_nt(qa[:, hs], ka[:, hs]), sc)
        st = s_ref[hd]
        o = _dot(sc.astype(BF16), vb[:, hs]) + _dot_nt(q_in[:, hs], st.astype(BF16))
        s_ref[hd] = st * d_chunk[:, hs] + _dot_tn(vb[:, hs], k_out[:, hs])
        outs.append(o)
    return jnp.concatenate(outs, axis=-1)


def _gla_kernel(qf_ref, zf_ref, vf_ref, qb_ref, zb_ref, vb_ref, lb_ref, tri_ref, lvl_ref, of_ref, ob_ref,
                sf_ref, sb_ref):
    @pl.when(pl.program_id(1) == 0)
    def _():
        sf_ref[...] = jnp.zeros_like(sf_ref)
        sb_ref[...] = jnp.zeros_like(sb_ref)

    of_ref[0] = _gla_direction(qf_ref[0], zf_ref[0], vf_ref[0], lb_ref[0:1], lb_ref[1:2], tri_ref[0], lvl_ref[0],
                               sf_ref, False)
    ob_ref[0] = _gla_direction(qb_ref[0], zb_ref[0], vb_ref[0], lb_ref[2:3], lb_ref[3:4], tri_ref[1], lvl_ref[1],
                               sb_ref, True)


def _gla(pp, lbp, n_ctx_chunks):
    bsz, t_all, _ = pp.shape
    nc = t_all // CH

    def cb(j):
        return jnp.where(j < n_ctx_chunks, n_ctx_chunks - 1 - j, nc - 1 - (j - n_ctx_chunks))

    blk = (1, CH, HW)
    tri, lvl = _gla_tables()
    return pl.pallas_call(
        _gla_kernel,
        grid=(bsz, nc),
        in_specs=[pl.BlockSpec(blk, lambda b, j: (b, j, 2)),
                  pl.BlockSpec(blk, lambda b, j: (b, j, 3)),
                  pl.BlockSpec(blk, lambda b, j: (b, j, 5)),
                  pl.BlockSpec(blk, lambda b, j: (b, cb(j), 2)),
                  pl.BlockSpec(blk, lambda b, j: (b, cb(j), 4)),
                  pl.BlockSpec(blk, lambda b, j: (b, cb(j), 5)),
                  pl.BlockSpec((4, HW), lambda b, j: (0, 0)),
                  pl.BlockSpec((2, CH, CH), lambda b, j: (0, 0, 0)),
                  pl.BlockSpec((2, CH, CH), lambda b, j: (0, 0, 0))],
        out_specs=[pl.BlockSpec(blk, lambda b, j: (b, j, 0)),
                   pl.BlockSpec(blk, lambda b, j: (b, cb(j), 0))],
        out_shape=[jax.ShapeDtypeStruct((bsz, t_all, HW), F32)] * 2,
        scratch_shapes=[pltpu.VMEM((NH, HD, HD), F32), pltpu.VMEM((NH, HD, HD), F32)],
        compiler_params=_cparams(2),
        name="gla",
    )(pp, pp, pp, pp, pp, pp, lbp, jnp.asarray(tri, BF16), jnp.asarray(lvl))


def _shift_rows(u, k):
    return pltpu.roll(u, k % u.shape[0], axis=0)


def _mixout_kernel(x_ref, bg_ref, u_ref, g_ref, up_ref, un_ref, of_ref, ob_ref, mod_ref, cw_ref, cnw_ref,
                   hnw_ref, wo_ref, nw2_ref, wr_ref, br_ref, xm_ref, h2_ref, slab_ref, runs_ref, cnt_ref, wo_b):
    j = pl.program_id(1)
    nt = pl.num_programs(1)
    first = jnp.logical_and(pl.program_id(0) == 0, j == 0)

    @pl.when(first)
    def _():
        cnt_ref[...] = jnp.zeros_like(cnt_ref)
        wo_b[...] = wo_ref[0].astype(BF16)

    is_ctx = j == 0
    m = mod_ref[0, 0]
    u = u_ref[0]
    cw = cw_ref[...]
    t = lax.broadcasted_iota(jnp.int32, (TM, 1), 0)
    col_in_row = t & (GRID_W - 1)
    keep_l = jnp.where(is_ctx, jnp.where(t == 0, 0.0, 1.0), jnp.where(col_in_row == 0, 0.0, 1.0))
    keep_r = jnp.where(is_ctx, jnp.where(t == TM - 1, 0.0, 1.0), jnp.where(col_in_row == GRID_W - 1, 0.0, 1.0))
    left = jnp.where(keep_l > 0.5, _shift_rows(u, 1), 0.0)
    right = jnp.where(keep_r > 0.5, _shift_rows(u, -1), 0.0)
    y_seq = cw[0:1] * left + cw[1:2] * u + cw[2:3] * right
    hc = CW // 2
    uv = u[:, hc:]
    up = jnp.concatenate([jnp.where(j == 1, 0.0, up_ref[0]), uv[:TM - GRID_W]], axis=0)
    dn = jnp.concatenate([uv[GRID_W:], jnp.where(j == nt - 1, 0.0, un_ref[0])], axis=0)
    y_col = cw[0:1, hc:] * up + cw[1:2, hc:] * uv + cw[2:3, hc:] * dn
    conv = jnp.concatenate([y_seq[:, :hc], jnp.where(is_ctx, y_seq[:, hc:], y_col)], axis=-1)
    y_conv = _rms(bg_ref[0] * conv, cnw_ref[...])

    o = of_ref[0] + ob_ref[0]
    g = g_ref[0]
    hnw = hnw_ref[...]
    recs = []
    for hd in range(NH):
        hs = slice(hd * HD, (hd + 1) * HD)
        recs.append(_rms(o[:, hs], hnw[:, hs]))
    y_rec = jnp.concatenate(recs, axis=-1) * (g * _sigmoid(g))

    y = jnp.concatenate([y_conv, y_rec], axis=-1).astype(BF16)
    xm = x_ref[0] + m[2:3] * _dot(y, wo_b[...])
    xm_ref[0] = xm
    h2 = _rms(xm, nw2_ref[...]) * (1.0 + m[4:5]) + m[3:4]
    h2_ref[0] = h2.astype(BF16)

    logit = _dot_hp(h2, wr_ref[...]) + br_ref[...]
    lane = lax.broadcasted_iota(jnp.int32, (TM, LANES), 1).astype(F32)
    ninf = -jnp.inf
    big = 1e9
    gl = jnp.where((lane >= NE) & (lane < NE + N_GROUPS), logit, ninf)
    gmax = jnp.max(gl, axis=-1, keepdims=True)
    gsel = jnp.min(jnp.where(gl == gmax, lane, big), axis=-1, keepdims=True) - NE
    pg = 1.0 / jnp.sum(jnp.exp(gl - gmax), axis=-1, keepdims=True)
    el = jnp.where((lane >= gsel * EPG) & (lane < gsel * EPG + EPG), logit, ninf)
    e1 = jnp.max(el, axis=-1, keepdims=True)
    i1 = jnp.min(jnp.where(el == e1, lane, big), axis=-1, keepdims=True)
    el2 = jnp.where(lane == i1, ninf, el)
    e2 = jnp.max(el2, axis=-1, keepdims=True)
    i2 = jnp.min(jnp.where(el2 == e2, lane, big), axis=-1, keepdims=True)
    r = jnp.exp(e2 - e1)
    g1 = pg / (1.0 + r)
    g2 = pg * r / (1.0 + r)
    oh1 = jnp.where(lane == i1, 1.0, 0.0)
    oh2 = jnp.where(lane == i2, 1.0, 0.0)
    cnt = oh1 + oh2
    rr = lax.broadcasted_iota(jnp.int32, (TM, TM), 0)
    cc = lax.broadcasted_iota(jnp.int32, (TM, TM), 1)
    before = jnp.where(cc < rr, 1.0, 0.0).astype(BF16)
    prior = _dot(before, cnt.astype(BF16))
    c_tile = jnp.sum(cnt, axis=0, keepdims=True)
    li = lax.broadcasted_iota(jnp.int32, (LANES, LANES), 0)
    lj = lax.broadcasted_iota(jnp.int32, (LANES, LANES), 1)
    lower_e = jnp.where(li < lj, 1.0, 0.0).astype(BF16)
    c_hi, c_lo = _split2(jnp.broadcast_to(c_tile, (8, LANES)))
    off = (_dot(c_hi, lower_e) + _dot(c_lo, lower_e))[0:1]
    pos = prior + off
    lpos1 = jnp.sum(oh1 * pos, axis=-1, keepdims=True)
    lpos2 = jnp.sum(oh2 * pos, axis=-1, keepdims=True)
    sub = lax.broadcasted_iota(jnp.int32, (8, LANES), 0)
    runs_ref[0] = jnp.where(sub == 0, c_tile, jnp.where(sub == 1, cnt_ref[...], jnp.where(sub == 2, off, 0.0)))
    cnt_ref[...] += c_tile
    slab = jnp.where(lane == 0, i1, 0.0)
    for k, val in enumerate((i2, lpos1, lpos2, g1, g2), start=1):
        slab = jnp.where(lane == k, val, slab)
    slab_ref[0] = slab


def _mixout(xall, pp, o_f, o_b, modall, cw, cnw, hnw, w_out, nw2, w_r, b_r, l):
    bsz, t_all, _ = xall.shape
    nt = t_all // TM
    n64 = t_all // GRID_W
    per = TM // GRID_W
    tile = lambda w, c: pl.BlockSpec((1, TM, w), lambda b, j, c=c: (b, j, c))
    full = lambda shape: pl.BlockSpec(shape, lambda b, j: (0,) * len(shape))
    layer = lambda shape: pl.BlockSpec((1, *shape), lambda b, j: (l,) + (0,) * len(shape), pipeline_mode=pl.Buffered(1))
    return pl.pallas_call(
        _mixout_kernel,
        grid=(bsz, nt),
        in_specs=[tile(D, 0),
                  tile(CW, 0), tile(CW, 1), tile(HW, 6),
                  pl.BlockSpec((1, GRID_W, CW // 2), lambda b, j: (b, jnp.maximum(j * per - 1, 0), 3)),
                  pl.BlockSpec((1, GRID_W, CW // 2), lambda b, j: (b, jnp.minimum(j * per + per, n64 - 1), 3)),
                  tile(HW, 0), tile(HW, 0),
                  pl.BlockSpec((1, 1, N_MOD, D), lambda b, j: (b, jnp.minimum(j, 1), 0, 0)),
                  full((3, CW)), full((1, CW)), full((1, HW)), layer((D, D)), full((1, D)),
                  full((D, LANES)), full((1, LANES))],
        out_specs=[tile(D, 0), tile(D, 0), tile(LANES, 0),
                   pl.BlockSpec((1, 8, LANES), lambda b, j: (b * nt + j, 0, 0)),
                   pl.BlockSpec((1, LANES), lambda b, j: (0, 0))],
        out_shape=[jax.ShapeDtypeStruct((bsz, t_all, D), F32),
                   jax.ShapeDtypeStruct((bsz, t_all, D), BF16),
                   jax.ShapeDtypeStruct((bsz, t_all, LANES), F32),
                   jax.ShapeDtypeStruct((bsz * nt, 8, LANES), F32),
                   jax.ShapeDtypeStruct((1, LANES), F32)],
        scratch_shapes=[pltpu.VMEM((D, D), BF16)],
        compiler_params=_cparams(2),
        name="mixout",
    )(xall, pp, pp, pp, pp, pp, o_f, o_b, modall, cw, cnw, hnw, w_out, nw2, w_r, b_r)


RUN_BITS = TM.bit_length()
LONG_RUN_BIT = 6
ROW_TILE = (8, LANES)


def _run_copies(tab_ref, local_ref, sorted_ref, sem, *, to_sorted, wait):
    def per_expert(e, carry):
        ln = tab_ref[0, 0, e]
        lo = tab_ref[0, 0, NE + e]
        go = tab_ref[0, 0, 2 * NE + e]
        def copy_bits(bits):
            for bit in bits:
                size = 1 << bit
                done = lax.shift_left(lax.shift_right_logical(ln, bit + 1), bit + 1)

                @pl.when((ln & size) != 0)
                def _():
                    a = local_ref.at[pl.ds(lo + done, size)]
                    b = sorted_ref.at[pl.ds(go + done, size)]
                    cp = pltpu.make_async_copy(a, b, sem) if to_sorted else pltpu.make_async_copy(b, a, sem)
                    if wait:
                        cp.wait()
                    else:
                        cp.start()

        @pl.when(ln >= (1 << LONG_RUN_BIT))
        def _():
            copy_bits(range(RUN_BITS - 1, LONG_RUN_BIT - 1, -1))

        copy_bits(range(LONG_RUN_BIT - 1, -1, -1))
        return carry

    lax.fori_loop(0, NE, per_expert, 0)


def _wait_tile(local_ref, sorted_ref, sem, *, to_sorted):
    whole = sorted_ref.at[pl.ds(0, 2 * TM)]
    (pltpu.make_async_copy(local_ref, whole, sem) if to_sorted else pltpu.make_async_copy(whole, local_ref, sem)).wait()


def _local_positions(slab):
    col = lax.broadcasted_iota(jnp.int32, (TM, 2 * TM), 1).astype(F32)
    return col == slab[:, 2:3], col == slab[:, 3:4]


def _dispatch_kernel(tab_ref, ptab_ref, h2_ref, slab_ref, xs_ref, buf, zbuf, sem):
    @pl.when(pl.program_id(0) == 0)
    def _():
        zbuf[...] = jnp.zeros_like(zbuf)
        _run_copies(ptab_ref, zbuf, xs_ref, sem, to_sorted=True, wait=False)
        _run_copies(ptab_ref, zbuf, xs_ref, sem, to_sorted=True, wait=True)

        def zero_block(i, carry):
            cp = pltpu.make_async_copy(zbuf, xs_ref.at[pl.ds(i * MOE_P, MOE_P)], sem)
            cp.start()
            cp.wait()
            return carry

        lax.fori_loop(ptab_ref[0, 0, 3 * NE], xs_ref.shape[0] // MOE_P, zero_block, 0)

    i = pl.program_id(0)
    slot = lax.rem(i, 2)
    p1, p2 = _local_positions(slab_ref[...])
    perm_t = jnp.where(p1 | p2, 1.0, 0.0).astype(BF16)
    buf[slot] = _dot_tn(perm_t, h2_ref[...]).reshape(2 * TM, *ROW_TILE)

    @pl.when(i > 0)
    def _():
        _wait_tile(buf.at[1 - slot], xs_ref, sem, to_sorted=True)

    _run_copies(tab_ref, buf.at[slot], xs_ref, sem, to_sorted=True, wait=False)

    @pl.when(i == pl.num_programs(0) - 1)
    def _():
        _wait_tile(buf.at[slot], xs_ref, sem, to_sorted=True)


def _dispatch(tab, pad_tab, h2, slab, n_rows):
    n_tiles = tab.shape[0]
    return pl.pallas_call(
        _dispatch_kernel,
        grid=(n_tiles,),
        in_specs=[pl.BlockSpec((1, 1, LANES), lambda i: (i, 0, 0), memory_space=pltpu.SMEM),
                  pl.BlockSpec((1, 1, LANES), lambda i: (0, 0, 0), memory_space=pltpu.SMEM),
                  pl.BlockSpec((TM, D), lambda i: (i, 0)),
                  pl.BlockSpec((TM, LANES), lambda i: (i, 0))],
        out_specs=pl.BlockSpec(memory_space=pl.ANY),
        out_shape=jax.ShapeDtypeStruct((n_rows, *ROW_TILE), F32),
        scratch_shapes=[pltpu.VMEM((2, 2 * TM, *ROW_TILE), F32), pltpu.VMEM((MOE_P, *ROW_TILE), F32),
                        pltpu.SemaphoreType.DMA(())],
        compiler_params=_cparams(1),
        name="dispatch",
    )(tab, pad_tab, h2, slab)


def _expert_kernel(be_ref, nu_ref, xs_ref, wgu_ref, wd_ref, ys_ref, wgu_b, wd_b):
    i = pl.program_id(0)

    @pl.when(jnp.logical_or(i == 0, be_ref[i] != be_ref[jnp.maximum(i - 1, 0)]))
    def _():
        wgu_b[...] = wgu_ref[0, 0].astype(BF16)
        wd_b[...] = wd_ref[0, 0].astype(BF16)

    @pl.when(i < nu_ref[0])
    def _():
        hmid = _dot(xs_ref[...].reshape(MOE_P, D).astype(BF16), wgu_b[...])
        a = hmid[:, :DE]
        act = (a * _sigmoid(a) * hmid[:, DE:]).astype(BF16)
        ys_ref[...] = _dot(act, wd_b[...]).reshape(MOE_P, *ROW_TILE)

    @pl.when(i >= nu_ref[0])
    def _():
        ys_ref[...] = jnp.zeros_like(ys_ref)


def _experts(block_e, n_used, xs, w_gu, w_down, l):
    n_rows = xs.shape[0]
    grid_spec = pltpu.PrefetchScalarGridSpec(
        num_scalar_prefetch=2,
        grid=(n_rows // MOE_P,),
        in_specs=[pl.BlockSpec((MOE_P, *ROW_TILE), lambda i, be, nu: (jnp.minimum(i, nu[0] - 1), 0, 0)),
                  pl.BlockSpec((1, 1, D, 2 * DE), lambda i, be, nu: (l, be[i], 0, 0)),
                  pl.BlockSpec((1, 1, DE, D), lambda i, be, nu: (l, be[i], 0, 0))],
        out_specs=pl.BlockSpec((MOE_P, *ROW_TILE), lambda i, be, nu: (i, 0, 0)),
        scratch_shapes=[pltpu.VMEM((D, 2 * DE), BF16), pltpu.VMEM((DE, D), BF16)],
    )
    return pl.pallas_call(
        _expert_kernel,
        grid_spec=grid_spec,
        out_shape=jax.ShapeDtypeStruct((n_rows, *ROW_TILE), F32),
        compiler_params=_cparams(1),
        name="experts",
    )(block_e, n_used, xs, w_gu, w_down)


def _combine_kernel(tab_ref, tab_next_ref, xm_ref, slab_ref, mod_ref, fnw_ref, ys_ref, o_ref, buf, sem, *, final):
    i = pl.program_id(0) * pl.num_programs(1) + pl.program_id(1)
    n = pl.num_programs(0) * pl.num_programs(1)
    slot = lax.rem(i, 2)

    @pl.when(i == 0)
    def _():
        _run_copies(tab_ref, buf.at[0], ys_ref, sem.at[0], to_sorted=False, wait=False)

    @pl.when(i + 1 < n)
    def _():
        _run_copies(tab_next_ref, buf.at[1 - slot], ys_ref, sem.at[1 - slot], to_sorted=False, wait=False)

    _wait_tile(buf.at[slot], ys_ref, sem.at[slot], to_sorted=False)
    slab = slab_ref[0]
    p1, p2 = _local_positions(slab)
    yb = buf[slot].reshape(2 * TM, D).astype(BF16)
    y1 = _dot(jnp.where(p1, 1.0, 0.0).astype(BF16), yb)
    y2 = _dot(jnp.where(p2, 1.0, 0.0).astype(BF16), yb)
    f = slab[:, 4:5] * y1 + slab[:, 5:6] * y2
    x = xm_ref[0] + mod_ref[0, 0][5:6] * f
    o_ref[0] = _rms(x, fnw_ref[...]) if final else x


def _combine(tab, xmid, slab, modall, fnw, ys, final):
    bsz, t_all, _ = xmid.shape
    nt = t_all // TM
    out_t = t_all - TM if final else t_all
    out_map = (lambda b, j: (b, jnp.maximum(j - 1, 0), 0)) if final else (lambda b, j: (b, j, 0))
    return pl.pallas_call(
        functools.partial(_combine_kernel, final=final),
        grid=(bsz, nt),
        in_specs=[pl.BlockSpec((1, 1, LANES), lambda b, j: (b * nt + j, 0, 0), memory_space=pltpu.SMEM),
                  pl.BlockSpec((1, 1, LANES), lambda b, j: (jnp.minimum(b * nt + j + 1, bsz * nt - 1), 0, 0),
                               memory_space=pltpu.SMEM),
                  pl.BlockSpec((1, TM, D), lambda b, j: (b, j, 0)),
                  pl.BlockSpec((1, TM, LANES), lambda b, j: (b, j, 0)),
                  pl.BlockSpec((1, 1, N_MOD, D), lambda b, j: (b, jnp.minimum(j, 1), 0, 0)),
                  pl.BlockSpec((1, D), lambda b, j: (0, 0)),
                  pl.BlockSpec(memory_space=pl.ANY)],
        out_specs=pl.BlockSpec((1, TM, D), out_map),
        out_shape=jax.ShapeDtypeStruct((bsz, out_t, D), F32),
        scratch_shapes=[pltpu.VMEM((2, 2 * TM, *ROW_TILE), F32), pltpu.SemaphoreType.DMA((2,))],
        compiler_params=_cparams(2),
        name="combine_final" if final else "combine",
    )(tab, tab, xmid, slab, modall, fnw, ys)


def _lower_bounds(hg_lb):
    p = jax.nn.softmax(hg_lb.astype(F32), axis=1)
    cs = jnp.cumsum(p, axis=1)
    return cs - cs[:, :1]


def kernel(x, c, ctx, c_ctx, norm_w, w_ada, b_ada, w_in, conv_w, conv_norm_w, hg_lb, hg_norm_w, w_out, w_rg, b_rg,
           w_re, b_re, w_e_gu, w_e_down, final_norm_w):
    bsz, seq, _ = x.shape
    n_ctx = ctx.shape[1]
    assert n_ctx == TM and seq % TM == 0
    t_all = n_ctx + seq
    n_tok = bsz * t_all
    n_assign = 2 * n_tok
    n_blocks = -(-(n_assign + NE * (MOE_P - 1)) // MOE_P)
    n_rows = n_blocks * MOE_P

    xall = jnp.concatenate([ctx, x], axis=1)
    cc = jnp.zeros((16, D), F32).at[:bsz].set(c).at[bsz].set(c_ctx)
    mod = _ada(cc, w_ada, b_ada).reshape(DEPTH, 16, N_MOD, D)
    lb = _lower_bounds(hg_lb)
    lbp = jnp.stack([jnp.log(lb[0]), jnp.log1p(-lb[0]), jnp.log(lb[1]), jnp.log1p(-lb[1])], axis=1)
    w_r = jnp.concatenate([w_re, w_rg, jnp.zeros((DEPTH, D, LANES - NE - N_GROUPS), F32)], axis=-1)
    b_r = jnp.concatenate([b_re, b_rg, jnp.zeros((DEPTH, LANES - NE - N_GROUPS), F32)], axis=-1)

    out = None
    for l in range(DEPTH):
        final = l == DEPTH - 1
        modall = jnp.stack([jnp.broadcast_to(mod[l, bsz], (bsz, N_MOD, D)), mod[l, :bsz]], axis=1)
        pp = _inproj(xall, norm_w[l, 0].reshape(1, D), modall, w_in, l)
        o_f, o_b = _gla(pp, lbp[l], n_ctx // CH)
        xmid, h2, slab, runs, counts = _mixout(
            xall, pp, o_f, o_b, modall, conv_w[l], conv_norm_w[l].reshape(1, CW),
            jnp.tile(hg_norm_w[l], NH).reshape(1, HW), w_out, norm_w[l, 1].reshape(1, D),
            w_r[l], b_r[l].reshape(1, LANES), l)
        cnt = counts[0, :NE].astype(jnp.int32)
        padded = (cnt + MOE_P - 1) // MOE_P * MOE_P
        pad_ends = jnp.cumsum(padded)
        pad_starts = pad_ends - padded
        run_len = runs[:, 0, :NE].astype(jnp.int32)
        run_global = pad_starts[None, :] + runs[:, 1, :NE].astype(jnp.int32)
        run_local = runs[:, 2, :NE].astype(jnp.int32)
        tab = jnp.concatenate([run_len, run_local, run_global, jnp.zeros_like(run_len)], axis=1)[:, None, :]
        block_start = jnp.arange(n_blocks, dtype=jnp.int32) * MOE_P
        block_e = jnp.minimum(jnp.sum((pad_ends[None, :] <= block_start[:, None]).astype(jnp.int32), axis=1), NE - 1)
        n_used = (pad_ends[-1:] // MOE_P).astype(jnp.int32)
        zero = jnp.zeros_like(cnt)
        pad_tab = jnp.concatenate([padded - cnt, zero, pad_starts + cnt, zero + n_used]).reshape(1, 1, LANES)
        xs = _dispatch(tab, pad_tab, h2.reshape(n_tok, D), slab.reshape(n_tok, LANES), n_rows)
        ys = _experts(block_e, n_used, xs, w_e_gu, w_e_down, l)
        res = _combine(tab, xmid, slab, modall, final_norm_w.reshape(1, D), ys, final)
        if final:
            out = res
        else:
            xall = res
    return out
```

```python
import functools

import jax
import jax.numpy as jnp
import numpy as np
from jax import lax
from jax.experimental import pallas as pl
from jax.experimental.pallas import tpu as pltpu

F32 = jnp.float32
BF16 = jnp.bfloat16

D = 1024
DEPTH = 4
GRID_W = 64
CW = 512
HW = 512
NH = 4
HD = HW // NH
PW = 3 * CW + 5 * HW
PPW = PW - CW
N_GROUPS = 4
EPG = 8
NE = N_GROUPS * EPG
DE = 512
N_MOD = 6
EPS = 1e-6
LOG2E = 1.4426950408889634

TM = 256
CH = 128
LEAF = 16
MOE_P = 256
LANES = 128

VMEM_LIMIT = 56 * 1024 * 1024


def _cparams(n_axes, flags=None):
    return pltpu.CompilerParams(dimension_semantics=("arbitrary",) * n_axes,
                                vmem_limit_bytes=VMEM_LIMIT, flags=flags)


def _dot(a, b):
    return jnp.dot(a, b, preferred_element_type=F32)


def _dot_nt(a, b):
    return lax.dot_general(a, b, (((1,), (1,)), ((), ())), preferred_element_type=F32)


def _dot_tn(a, b):
    return lax.dot_general(a, b, (((0,), (0,)), ((), ())), preferred_element_type=F32)


def _split2(x):
    hi = x.astype(BF16)
    lo = (x - hi.astype(F32)).astype(BF16)
    return hi, lo


def _dot_hp(a, b):
    a_hi, a_lo = _split2(a)
    b_hi, b_lo = _split2(b)
    return _dot(a_hi, b_hi) + _dot(a_hi, b_lo) + _dot(a_lo, b_hi)


def _sigmoid(x):
    return 1.0 / (1.0 + jnp.exp(-x))


def _rms(x, w):
    return x * lax.rsqrt(jnp.mean(x * x, axis=-1, keepdims=True) + EPS) * w


def _ada_kernel(cc_ref, w_ref, b_ref, o_ref):
    s = cc_ref[...]
    s = s * _sigmoid(s)
    o_ref[0] = _dot_hp(s, w_ref[0]) + b_ref[0]


def _ada(cc, w_ada, b_ada):
    tn = 1536
    n = N_MOD * D
    return pl.pallas_call(
        _ada_kernel,
        grid=(DEPTH, n // tn),
        in_specs=[pl.BlockSpec((16, D), lambda l, j: (0, 0)),
                  pl.BlockSpec((1, D, tn), lambda l, j: (l, 0, j)),
                  pl.BlockSpec((1, 1, tn), lambda l, j: (l, 0, j))],
        out_specs=pl.BlockSpec((1, 16, tn), lambda l, j: (l, 0, j)),
        out_shape=jax.ShapeDtypeStruct((DEPTH, 16, n), F32),
        compiler_params=_cparams(2),
        name="ada",
    )(cc, w_ada, b_ada.reshape(DEPTH, 1, n))


def _silu(x):
    h = 0.5 * x
    return h + h * jnp.tanh(h)


def _neg_abs(x):
    bits = lax.bitcast_convert_type(x, jnp.uint32) | jnp.uint32(0x80000000)
    return lax.bitcast_convert_type(bits, F32)


def _forget_gate(z2, log2lb, log2_1mlb):
    ls = jnp.minimum(z2, 0.0) - jnp.log2(1.0 + jnp.exp2(_neg_abs(z2)))
    c = log2_1mlb + ls
    lf2 = jnp.maximum(log2lb, c) + jnp.log2(1.0 + jnp.exp2(_neg_abs(log2lb - c)))
    return lf2, jnp.exp2(c - z2)


def _inproj_kernel(x_ref, nw_ref, mod_ref, w_ref, lb_ref, conv_ref, g_ref, lf_ref, qkv_ref, w_b):
    @pl.when(jnp.logical_and(pl.program_id(0) == 0, pl.program_id(1) == 0))
    def _():
        def cast_rows(r, carry):
            rows = pl.ds(pl.multiple_of(r * 128, 128), 128)
            lo, hi = 3 * CW + HW, 3 * CW + 3 * HW
            w_b[rows, :lo] = w_ref[0, rows, :lo].astype(BF16)
            w_b[rows, lo:hi] = (w_ref[0, rows, lo:hi] * LOG2E).astype(BF16)
            w_b[rows, hi:] = w_ref[0, rows, hi:].astype(BF16)
            return carry

        lax.fori_loop(0, D // 128, cast_rows, 0)

    m = mod_ref[0, 0]
    h = _rms(x_ref[0], nw_ref[...]) * (1.0 + m[1:2]) + m[0:1]
    hb = h.astype(BF16)
    blk = {k: _dot(hb, w_b[:, k * 512:(k + 1) * 512]) for k in (4, 5, 3, 1, 2, 0, 6, 7)}
    for d in range(2):
        lf2, kk = _forget_gate(blk[4 + d], lb_ref[2 * d:2 * d + 1], lb_ref[2 * d + 1:2 * d + 2])
        lf_ref[0, :, d * HW:(d + 1) * HW] = lf2
        qkv_ref[0, :, (1 + d) * HW:(2 + d) * HW] = kk.astype(BF16)
    qkv_ref[0, :, 0:HW] = _silu(blk[3]).astype(BF16)
    conv_ref[0, :, CW:2 * CW] = blk[1] * blk[2]
    conv_ref[0, :, 0:CW] = blk[0]
    qkv_ref[0, :, 3 * HW:4 * HW] = blk[6].astype(BF16)
    g_ref[0] = blk[7]


def _inproj(xall, nw, modall, w_in, lbp, l):
    bsz, t_all, _ = xall.shape
    tile = lambda w: pl.BlockSpec((1, TM, w), lambda b, j: (b, j, 0))
    return pl.pallas_call(
        _inproj_kernel,
        grid=(bsz, t_all // TM),
        in_specs=[pl.BlockSpec((1, TM, D), lambda b, j: (b, j, 0)),
                  pl.BlockSpec((1, D), lambda b, j: (0, 0)),
                  pl.BlockSpec((1, 1, N_MOD, D), lambda b, j: (b, jnp.minimum(j, 1), 0, 0)),
                  pl.BlockSpec((1, D, PW), lambda b, j: (l, 0, 0), pipeline_mode=pl.Buffered(1)),
                  pl.BlockSpec((4, HW), lambda b, j: (0, 0))],
        out_specs=[tile(2 * CW), tile(HW), tile(2 * HW), tile(4 * HW)],
        out_shape=[jax.ShapeDtypeStruct((bsz, t_all, 2 * CW), F32),
                   jax.ShapeDtypeStruct((bsz, t_all, HW), F32),
                   jax.ShapeDtypeStruct((bsz, t_all, 2 * HW), F32),
                   jax.ShapeDtypeStruct((bsz, t_all, 4 * HW), BF16)],
        scratch_shapes=[pltpu.VMEM((D, PW), BF16)],
        compiler_params=_cparams(2),
        name="inproj",
    )(xall, nw, modall, w_in, lbp)


def _gla_tables():
    t = np.arange(CH)[:, None]
    s = np.arange(CH)[None, :]
    x = t ^ s
    lvl = np.full((CH, CH), -1, np.int32)
    n, k = CH, 0
    while n > LEAF:
        lvl[(x < n) & (x >= n // 2)] = k
        n, k = n // 2, k + 1
    lvl[x < LEAF] = k
    fwd = np.where(s <= t, lvl, -1)
    return (np.stack([s <= t, s >= t]).astype(np.float32), np.stack([fwd, fwd.T]).astype(np.int32))


def _gla_factors(lf2, qh, kh, tri, reverse):
    hi = lf2.astype(BF16)
    lo = (lf2 - hi.astype(F32)).astype(BF16)
    b = _dot(tri, hi) + _dot(tri, lo)

    def block_ref(n, idx):
        r = b.reshape(CH // n, n, HW)[:, idx:idx + 1, :]
        return jnp.broadcast_to(r, (CH // n, n, HW)).reshape(CH, HW)

    pow2 = lambda e: jnp.exp2(e.astype(BF16))
    pieces = []
    n = CH
    while n > LEAF:
        h = n // 2
        a = pow2(_neg_abs(b - block_ref(n, h if reverse else h - 1)))
        pieces.append((qh * a, kh * a))
        n = h
    dl = b - block_ref(LEAF, LEAF // 2)
    pieces.append((qh * pow2(dl), kh * pow2(-dl)))

    edge = 0 if reverse else CH - 1
    b_edge = b[edge:edge + 1, :]
    q_in = qh * pow2(b)
    k_out = kh * pow2(b_edge - b)
    d_chunk = jnp.exp2(b_edge)
    return pieces, q_in, k_out, d_chunk


def _gla_heads(factors, v_refs, lvls, s_refs, o_refs):
    dirs = range(len(factors))
    for hd in range(NH):
        hs = slice(hd * HD, (hd + 1) * HD)
        dots = [[_dot_nt(qa[:, hs], ka[:, hs]) for qa, ka in factors[d][0]] for d in dirs]
        scs = []
        for d in dirs:
            sc = jnp.zeros((CH, CH), F32)
            for k, dk in enumerate(dots[d]):
                sc = jnp.where(lvls[d] == k, dk, sc)
            scs.append(sc.astype(BF16))
        for d in dirs:
            _, q_in, k_out, d_chunk = factors[d]
            vb = v_refs[d][0, :, hs]
            st = s_refs[d][hd]
            o_refs[d][0, :, hs] = _dot(scs[d], vb) + _dot_nt(q_in[:, hs], st.astype(BF16))
            s_refs[d][hd] = st * d_chunk[:, hs] + _dot_tn(vb, k_out[:, hs])


def _gla_kernel(lff_ref, qf_ref, kf_ref, vf_ref, lfb_ref, qb_ref, kb_ref, vb_ref, tri_ref, lvl_ref, of_ref, ob_ref,
                sf_ref, sb_ref):
    @pl.when(pl.program_id(1) == 0)
    def _():
        sf_ref[...] = jnp.zeros_like(sf_ref)
        sb_ref[...] = jnp.zeros_like(sb_ref)

    factors = (_gla_factors(lff_ref[0], qf_ref[0], kf_ref[0], tri_ref[0], False),
               _gla_factors(lfb_ref[0], qb_ref[0], kb_ref[0], tri_ref[1], True))
    _gla_heads(factors, (vf_ref, vb_ref), (lvl_ref[0], lvl_ref[1]), (sf_ref, sb_ref), (of_ref, ob_ref))


def _gla(lf, qkv, n_ctx_chunks):
    bsz, t_all, _ = lf.shape
    nc = t_all // CH

    def cb(j):
        return jnp.where(j < n_ctx_chunks, n_ctx_chunks - 1 - j, nc - 1 - (j - n_ctx_chunks))

    blk = (1, CH, HW)
    fwd = lambda c: pl.BlockSpec(blk, lambda b, j: (b, j, c))
    bwd = lambda c: pl.BlockSpec(blk, lambda b, j: (b, cb(j), c))
    tri, lvl = _gla_tables()
    return pl.pallas_call(
        _gla_kernel,
        grid=(bsz, nc),
        in_specs=[fwd(0), fwd(0), fwd(1), fwd(3),
                  bwd(1), bwd(0), bwd(2), bwd(3),
                  pl.BlockSpec((2, CH, CH), lambda b, j: (0, 0, 0)),
                  pl.BlockSpec((2, CH, CH), lambda b, j: (0, 0, 0))],
        out_specs=[fwd(0), bwd(0)],
        out_shape=[jax.ShapeDtypeStruct((bsz, t_all, HW), F32)] * 2,
        scratch_shapes=[pltpu.VMEM((NH, HD, HD), F32), pltpu.VMEM((NH, HD, HD), F32)],
        compiler_params=_cparams(2),
        name="gla",
    )(lf, qkv, qkv, qkv, lf, qkv, qkv, qkv, jnp.asarray(tri, BF16), jnp.asarray(lvl))


def _shift_rows(u, k):
    return pltpu.roll(u, k % u.shape[0], axis=0)


def _mixout_kernel(x_ref, bg_ref, u_ref, g_ref, up_ref, un_ref, of_ref, ob_ref, mod_ref, cw_ref, cnw_ref,
                   hnw_ref, wo_ref, nw2_ref, wr_ref, br_ref, xm_ref, h2_ref, slab_ref, runs_ref, cnt_ref, wo_b, wr_b):
    j = pl.program_id(1)
    nt = pl.num_programs(1)
    first = jnp.logical_and(pl.program_id(0) == 0, j == 0)

    @pl.when(first)
    def _():
        cnt_ref[...] = jnp.zeros_like(cnt_ref)
        wo_b[...] = wo_ref[0].astype(BF16)
        wr_b[0], wr_b[1] = _split2(wr_ref[...])

    is_ctx = j == 0
    m = mod_ref[0, 0]
    u = u_ref[0]
    cw = cw_ref[...]
    t = lax.broadcasted_iota(jnp.int32, (TM, 1), 0)
    col_in_row = t & (GRID_W - 1)
    keep_l = jnp.where(is_ctx, jnp.where(t == 0, 0.0, 1.0), jnp.where(col_in_row == 0, 0.0, 1.0))
    keep_r = jnp.where(is_ctx, jnp.where(t == TM - 1, 0.0, 1.0), jnp.where(col_in_row == GRID_W - 1, 0.0, 1.0))
    left = jnp.where(keep_l > 0.5, _shift_rows(u, 1), 0.0)
    right = jnp.where(keep_r > 0.5, _shift_rows(u, -1), 0.0)
    y_seq = cw[0:1] * left + cw[1:2] * u + cw[2:3] * right
    hc = CW // 2
    uv = u[:, hc:]
    up = jnp.concatenate([jnp.where(j == 1, 0.0, up_ref[0]), uv[:TM - GRID_W]], axis=0)
    dn = jnp.concatenate([uv[GRID_W:], jnp.where(j == nt - 1, 0.0, un_ref[0])], axis=0)
    y_col = cw[0:1, hc:] * up + cw[1:2, hc:] * uv + cw[2:3, hc:] * dn
    conv = jnp.concatenate([y_seq[:, :hc], jnp.where(is_ctx, y_seq[:, hc:], y_col)], axis=-1)
    y_conv = _rms(bg_ref[0] * conv, cnw_ref[...])

    o = of_ref[0] + ob_ref[0]
    g = g_ref[0]
    hnw = hnw_ref[...]
    recs = []
    for hd in range(NH):
        hs = slice(hd * HD, (hd + 1) * HD)
        recs.append(_rms(o[:, hs], hnw[:, hs]))
    y_rec = jnp.concatenate(recs, axis=-1) * (g * _sigmoid(g))

    y = jnp.concatenate([y_conv, y_rec], axis=-1).astype(BF16)
    xm = x_ref[0] + m[2:3] * _dot(y, wo_b[...])
    xm_ref[0] = xm
    h2 = _rms(xm, nw2_ref[...]) * (1.0 + m[4:5]) + m[3:4]
    h2_ref[0] = h2.astype(BF16)

    h_hi, h_lo = _split2(h2)
    logit = _dot(h_hi, wr_b[0]) + _dot(h_hi, wr_b[1]) + _dot(h_lo, wr_b[0]) + br_ref[...]
    lane = lax.broadcasted_iota(jnp.int32, (TM, LANES), 1).astype(F32)
    ninf = -jnp.inf
    big = 1e9
    gl = jnp.where((lane >= NE) & (lane < NE + N_GROUPS), logit, ninf)
    gmax = jnp.max(gl, axis=-1, keepdims=True)
    gsel = jnp.min(jnp.where(gl == gmax, lane, big), axis=-1, keepdims=True) - NE
    pg = 1.0 / jnp.sum(jnp.exp(gl - gmax), axis=-1, keepdims=True)
    el = jnp.where((lane >= gsel * EPG) & (lane < gsel * EPG + EPG), logit, ninf)
    e1 = jnp.max(el, axis=-1, keepdims=True)
    i1 = jnp.min(jnp.where(el == e1, lane, big), axis=-1, keepdims=True)
    el2 = jnp.where(lane == i1, ninf, el)
    e2 = jnp.max(el2, axis=-1, keepdims=True)
    i2 = jnp.min(jnp.where(el2 == e2, lane, big), axis=-1, keepdims=True)
    r = jnp.exp(e2 - e1)
    g1 = pg / (1.0 + r)
    g2 = pg * r / (1.0 + r)
    oh1 = jnp.where(lane == i1, 1.0, 0.0)
    oh2 = jnp.where(lane == i2, 1.0, 0.0)
    cnt = oh1 + oh2
    rr = lax.broadcasted_iota(jnp.int32, (TM, TM), 0)
    cc = lax.broadcasted_iota(jnp.int32, (TM, TM), 1)
    before = jnp.where(cc < rr, 1.0, 0.0).astype(BF16)
    prior = _dot(before, cnt.astype(BF16))
    c_tile = jnp.sum(cnt, axis=0, keepdims=True)
    li = lax.broadcasted_iota(jnp.int32, (LANES, LANES), 0)
    lj = lax.broadcasted_iota(jnp.int32, (LANES, LANES), 1)
    lower_e = jnp.where(li < lj, 1.0, 0.0).astype(BF16)
    c_hi, c_lo = _split2(jnp.broadcast_to(c_tile, (8, LANES)))
    off = (_dot(c_hi, lower_e) + _dot(c_lo, lower_e))[0:1]
    pos = prior + off
    lpos1 = jnp.sum(oh1 * pos, axis=-1, keepdims=True)
    lpos2 = jnp.sum(oh2 * pos, axis=-1, keepdims=True)
    sub = lax.broadcasted_iota(jnp.int32, (8, LANES), 0)
    runs_ref[0] = jnp.where(sub == 0, c_tile, jnp.where(sub == 1, cnt_ref[...], jnp.where(sub == 2, off, 0.0)))
    cnt_ref[...] += c_tile
    slab = jnp.where(lane == 0, i1, 0.0)
    for k, val in enumerate((i2, lpos1, lpos2, g1, g2), start=1):
        slab = jnp.where(lane == k, val, slab)
    slab_ref[0] = slab


def _mixout(xall, conv_in, g_in, o_f, o_b, modall, cw, cnw, hnw, w_out, nw2, w_r, b_r, l):
    bsz, t_all, _ = xall.shape
    nt = t_all // TM
    n64 = t_all // GRID_W
    per = TM // GRID_W
    tile = lambda w, c: pl.BlockSpec((1, TM, w), lambda b, j, c=c: (b, j, c))
    full = lambda shape: pl.BlockSpec(shape, lambda b, j: (0,) * len(shape))
    layer = lambda shape: pl.BlockSpec((1, *shape), lambda b, j: (l,) + (0,) * len(shape), pipeline_mode=pl.Buffered(1))
    return pl.pallas_call(
        _mixout_kernel,
        grid=(bsz, nt),
        in_specs=[tile(D, 0),
                  tile(CW, 0), tile(CW, 1), tile(HW, 0),
                  pl.BlockSpec((1, GRID_W, CW // 2), lambda b, j: (b, jnp.maximum(j * per - 1, 0), 3)),
                  pl.BlockSpec((1, GRID_W, CW // 2), lambda b, j: (b, jnp.minimum(j * per + per, n64 - 1), 3)),
                  tile(HW, 0), tile(HW, 0),
                  pl.BlockSpec((1, 1, N_MOD, D), lambda b, j: (b, jnp.minimum(j, 1), 0, 0)),
                  full((3, CW)), full((1, CW)), full((1, HW)), layer((D, D)), full((1, D)),
                  full((D, LANES)), full((1, LANES))],
        out_specs=[tile(D, 0), tile(D, 0), tile(LANES, 0),
                   pl.BlockSpec((1, 8, LANES), lambda b, j: (b * nt + j, 0, 0)),
                   pl.BlockSpec((1, LANES), lambda b, j: (0, 0))],
        out_shape=[jax.ShapeDtypeStruct((bsz, t_all, D), F32),
                   jax.ShapeDtypeStruct((bsz, t_all, D), BF16),
                   jax.ShapeDtypeStruct((bsz, t_all, LANES), F32),
                   jax.ShapeDtypeStruct((bsz * nt, 8, LANES), F32),
                   jax.ShapeDtypeStruct((1, LANES), F32)],
        scratch_shapes=[pltpu.VMEM((D, D), BF16), pltpu.VMEM((2, D, LANES), BF16)],
        compiler_params=_cparams(2),
        name="mixout",
    )(xall, conv_in, conv_in, g_in, conv_in, conv_in, o_f, o_b, modall, cw, cnw, hnw, w_out, nw2, w_r, b_r)


RUN_BITS = TM.bit_length()
LONG_RUN_BIT = 6
ROW_TILE = (8, LANES)


def _run_copies(tab_ref, local_ref, sorted_ref, sem, *, to_sorted, wait):
    def per_expert(e, carry):
        ln = tab_ref[0, 0, e]
        lo = tab_ref[0, 0, NE + e]
        go = tab_ref[0, 0, 2 * NE + e]
        def copy_bits(bits):
            for bit in bits:
                size = 1 << bit
                done = lax.shift_left(lax.shift_right_logical(ln, bit + 1), bit + 1)

                @pl.when((ln & size) != 0)
                def _():
                    a = local_ref.at[pl.ds(lo + done, size)]
                    b = sorted_ref.at[pl.ds(go + done, size)]
                    cp = pltpu.make_async_copy(a, b, sem) if to_sorted else pltpu.make_async_copy(b, a, sem)
                    if wait:
                        cp.wait()
                    else:
                        cp.start()

        @pl.when(ln >= (1 << LONG_RUN_BIT))
        def _():
            copy_bits(range(RUN_BITS - 1, LONG_RUN_BIT - 1, -1))

        copy_bits(range(LONG_RUN_BIT - 1, -1, -1))
        return carry

    lax.fori_loop(0, NE, per_expert, 0)


def _wait_tile(local_ref, sorted_ref, sem, *, to_sorted):
    whole = sorted_ref.at[pl.ds(0, 2 * TM)]
    (pltpu.make_async_copy(local_ref, whole, sem) if to_sorted else pltpu.make_async_copy(whole, local_ref, sem)).wait()


def _local_positions(slab):
    col = lax.broadcasted_iota(jnp.int32, (TM, 2 * TM), 1).astype(F32)
    return col == slab[:, 2:3], col == slab[:, 3:4]


def _dispatch_kernel(tab_ref, ptab_ref, h2_ref, slab_ref, xs_ref, buf, zbuf, sem):
    @pl.when(pl.program_id(0) == 0)
    def _():
        zbuf[...] = jnp.zeros_like(zbuf)
        _run_copies(ptab_ref, zbuf, xs_ref, sem, to_sorted=True, wait=False)
        _run_copies(ptab_ref, zbuf, xs_ref, sem, to_sorted=True, wait=True)

        def zero_block(i, carry):
            cp = pltpu.make_async_copy(zbuf, xs_ref.at[pl.ds(i * MOE_P, MOE_P)], sem)
            cp.start()
            cp.wait()
            return carry

        lax.fori_loop(ptab_ref[0, 0, 3 * NE], xs_ref.shape[0] // MOE_P, zero_block, 0)

    i = pl.program_id(0)
    slot = lax.rem(i, 2)
    p1, p2 = _local_positions(slab_ref[...])
    perm_t = jnp.where(p1 | p2, 1.0, 0.0).astype(BF16)
    buf[slot] = _dot_tn(perm_t, h2_ref[...]).reshape(2 * TM, *ROW_TILE)

    @pl.when(i > 0)
    def _():
        _wait_tile(buf.at[1 - slot], xs_ref, sem, to_sorted=True)

    _run_copies(tab_ref, buf.at[slot], xs_ref, sem, to_sorted=True, wait=False)

    @pl.when(i == pl.num_programs(0) - 1)
    def _():
        _wait_tile(buf.at[slot], xs_ref, sem, to_sorted=True)


def _dispatch(tab, pad_tab, h2, slab, n_rows):
    n_tiles = tab.shape[0]
    return pl.pallas_call(
        _dispatch_kernel,
        grid=(n_tiles,),
        in_specs=[pl.BlockSpec((1, 1, LANES), lambda i: (i, 0, 0), memory_space=pltpu.SMEM),
                  pl.BlockSpec((1, 1, LANES), lambda i: (0, 0, 0), memory_space=pltpu.SMEM),
                  pl.BlockSpec((TM, D), lambda i: (i, 0)),
                  pl.BlockSpec((TM, LANES), lambda i: (i, 0))],
        out_specs=pl.BlockSpec(memory_space=pl.ANY),
        out_shape=jax.ShapeDtypeStruct((n_rows, *ROW_TILE), F32),
        scratch_shapes=[pltpu.VMEM((2, 2 * TM, *ROW_TILE), F32), pltpu.VMEM((MOE_P, *ROW_TILE), F32),
                        pltpu.SemaphoreType.DMA(())],
        compiler_params=_cparams(1),
        name="dispatch",
    )(tab, pad_tab, h2, slab)


def _expert_kernel(be_ref, nu_ref, xs_ref, wgu_ref, wd_ref, ys_ref, wgu_b, wd_b):
    i = pl.program_id(0)

    @pl.when(jnp.logical_or(i == 0, be_ref[i] != be_ref[jnp.maximum(i - 1, 0)]))
    def _():
        wgu_b[...] = wgu_ref[0, 0].astype(BF16)
        wd_b[...] = wd_ref[0, 0].astype(BF16)

    @pl.when(i < nu_ref[0])
    def _():
        hmid = _dot(xs_ref[...].reshape(MOE_P, D).astype(BF16), wgu_b[...])
        a = hmid[:, :DE]
        act = (a * _sigmoid(a) * hmid[:, DE:]).astype(BF16)
        ys_ref[...] = _dot(act, wd_b[...]).reshape(MOE_P, *ROW_TILE)

    @pl.when(i >= nu_ref[0])
    def _():
        ys_ref[...] = jnp.zeros_like(ys_ref)


def _experts(block_e, n_used, xs, w_gu, w_down, l):
    n_rows = xs.shape[0]
    grid_spec = pltpu.PrefetchScalarGridSpec(
        num_scalar_prefetch=2,
        grid=(n_rows // MOE_P,),
        in_specs=[pl.BlockSpec((MOE_P, *ROW_TILE), lambda i, be, nu: (jnp.minimum(i, nu[0] - 1), 0, 0)),
                  pl.BlockSpec((1, 1, D, 2 * DE), lambda i, be, nu: (l, be[i], 0, 0)),
                  pl.BlockSpec((1, 1, DE, D), lambda i, be, nu: (l, be[i], 0, 0))],
        out_specs=pl.BlockSpec((MOE_P, *ROW_TILE), lambda i, be, nu: (i, 0, 0)),
        scratch_shapes=[pltpu.VMEM((D, 2 * DE), BF16), pltpu.VMEM((DE, D), BF16)],
    )
    return pl.pallas_call(
        _expert_kernel,
        grid_spec=grid_spec,
        out_shape=jax.ShapeDtypeStruct((n_rows, *ROW_TILE), F32),
        compiler_params=_cparams(1),
        name="experts",
    )(block_e, n_used, xs, w_gu, w_down)


def _combine_kernel(tab_ref, tab_next_ref, xm_ref, slab_ref, mod_ref, fnw_ref, ys_ref, o_ref, buf, sem, *, final):
    i = pl.program_id(0) * pl.num_programs(1) + pl.program_id(1)
    n = pl.num_programs(0) * pl.num_programs(1)
    slot = lax.rem(i, 2)

    @pl.when(i == 0)
    def _():
        _run_copies(tab_ref, buf.at[0], ys_ref, sem.at[0], to_sorted=False, wait=False)

    @pl.when(i + 1 < n)
    def _():
        _run_copies(tab_next_ref, buf.at[1 - slot], ys_ref, sem.at[1 - slot], to_sorted=False, wait=False)

    _wait_tile(buf.at[slot], ys_ref, sem.at[slot], to_sorted=False)
    slab = slab_ref[0]
    p1, p2 = _local_positions(slab)
    yb = buf[slot].reshape(2 * TM, D).astype(BF16)
    y1 = _dot(jnp.where(p1, 1.0, 0.0).astype(BF16), yb)
    y2 = _dot(jnp.where(p2, 1.0, 0.0).astype(BF16), yb)
    f = slab[:, 4:5] * y1 + slab[:, 5:6] * y2
    x = xm_ref[0] + mod_ref[0, 0][5:6] * f
    o_ref[0] = _rms(x, fnw_ref[...]) if final else x


def _combine(tab, xmid, slab, modall, fnw, ys, final):
    bsz, t_all, _ = xmid.shape
    nt = t_all // TM
    out_t = t_all - TM if final else t_all
    out_map = (lambda b, j: (b, jnp.maximum(j - 1, 0), 0)) if final else (lambda b, j: (b, j, 0))
    return pl.pallas_call(
        functools.partial(_combine_kernel, final=final),
        grid=(bsz, nt),
        in_specs=[pl.BlockSpec((1, 1, LANES), lambda b, j: (b * nt + j, 0, 0), memory_space=pltpu.SMEM),
                  pl.BlockSpec((1, 1, LANES), lambda b, j: (jnp.minimum(b * nt + j + 1, bsz * nt - 1), 0, 0),
                               memory_space=pltpu.SMEM),
                  pl.BlockSpec((1, TM, D), lambda b, j: (b, j, 0)),
                  pl.BlockSpec((1, TM, LANES), lambda b, j: (b, j, 0)),
                  pl.BlockSpec((1, 1, N_MOD, D), lambda b, j: (b, jnp.minimum(j, 1), 0, 0)),
                  pl.BlockSpec((1, D), lambda b, j: (0, 0)),
                  pl.BlockSpec(memory_space=pl.ANY)],
        out_specs=pl.BlockSpec((1, TM, D), out_map),
        out_shape=jax.ShapeDtypeStruct((bsz, out_t, D), F32),
        scratch_shapes=[pltpu.VMEM((2, 2 * TM, *ROW_TILE), F32), pltpu.SemaphoreType.DMA((2,))],
        compiler_params=_cparams(2),
        name="combine_final" if final else "combine",
    )(tab, tab, xmid, slab, modall, fnw, ys)


def _lower_bounds(hg_lb):
    p = jax.nn.softmax(hg_lb.astype(F32), axis=1)
    cs = jnp.cumsum(p, axis=1)
    return cs - cs[:, :1]


def kernel(x, c, ctx, c_ctx, norm_w, w_ada, b_ada, w_in, conv_w, conv_norm_w, hg_lb, hg_norm_w, w_out, w_rg, b_rg,
           w_re, b_re, w_e_gu, w_e_down, final_norm_w):
    bsz, seq, _ = x.shape
    n_ctx = ctx.shape[1]
    assert n_ctx == TM and seq % TM == 0
    t_all = n_ctx + seq
    n_tok = bsz * t_all
    n_assign = 2 * n_tok
    n_blocks = -(-(n_assign + NE * (MOE_P - 1)) // MOE_P)
    n_rows = n_blocks * MOE_P

    xall = jnp.concatenate([ctx, x], axis=1)
    cc = jnp.zeros((16, D), F32).at[:bsz].set(c).at[bsz].set(c_ctx)
    mod = _ada(cc, w_ada, b_ada).reshape(DEPTH, 16, N_MOD, D)
    lb = _lower_bounds(hg_lb)
    lbp = jnp.stack([jnp.log(lb[0]), jnp.log1p(-lb[0]), jnp.log(lb[1]), jnp.log1p(-lb[1])], axis=1) * LOG2E
    w_r = jnp.concatenate([w_re, w_rg, jnp.zeros((DEPTH, D, LANES - NE - N_GROUPS), F32)], axis=-1)
    b_r = jnp.concatenate([b_re, b_rg, jnp.zeros((DEPTH, LANES - NE - N_GROUPS), F32)], axis=-1)

    out = None
    for l in range(DEPTH):
        final = l == DEPTH - 1
        modall = jnp.stack([jnp.broadcast_to(mod[l, bsz], (bsz, N_MOD, D)), mod[l, :bsz]], axis=1)
        conv_in, g_in, lf, qkv = _inproj(xall, norm_w[l, 0].reshape(1, D), modall, w_in, lbp[l], l)
        o_f, o_b = _gla(lf, qkv, n_ctx // CH)
        xmid, h2, slab, runs, counts = _mixout(
            xall, conv_in, g_in, o_f, o_b, modall, conv_w[l], conv_norm_w[l].reshape(1, CW),
            jnp.tile(hg_norm_w[l], NH).reshape(1, HW), w_out, norm_w[l, 1].reshape(1, D),
            w_r[l], b_r[l].reshape(1, LANES), l)
        cnt = counts[0, :NE].astype(jnp.int32)
        padded = (cnt + MOE_P - 1) // MOE_P * MOE_P
        pad_ends = jnp.cumsum(padded)
        pad_starts = pad_ends - padded
        run_len = runs[:, 0, :NE].astype(jnp.int32)
        run_global = pad_starts[None, :] + runs[:, 1, :NE].astype(jnp.int32)
        run_local = runs[:, 2, :NE].astype(jnp.int32)
        tab = jnp.concatenate([run_len, run_local, run_global, jnp.zeros_like(run_len)], axis=1)[:, None, :]
        block_start = jnp.arange(n_blocks, dtype=jnp.int32) * MOE_P
        block_e = jnp.minimum(jnp.sum((pad_ends[None, :] <= block_start[:, None]).astype(jnp.int32), axis=1), NE - 1)
        n_used = (pad_ends[-1:] // MOE_P).astype(jnp.int32)
        zero = jnp.zeros_like(cnt)
        pad_tab = jnp.concatenate([padded - cnt, zero, pad_starts + cnt, zero + n_used]).reshape(1, 1, LANES)
        xs = _dispatch(tab, pad_tab, h2.reshape(n_tok, D), slab.reshape(n_tok, LANES), n_rows)
        ys = _experts(block_e, n_used, xs, w_e_gu, w_e_down, l)
        res = _combine(tab, xmid, slab, modall, final_norm_w.reshape(1, D), ys, final)
        if final:
            out = res
        else:
            xall = res
    return out
```

```python
import functools

import jax
import jax.numpy as jnp
import numpy as np
from jax import lax
from jax.experimental import pallas as pl
from jax.experimental.pallas import tpu as pltpu

F32 = jnp.float32
BF16 = jnp.bfloat16

D = 1024
DEPTH = 4
GRID_W = 64
CW = 512
HW = 512
NH = 4
HD = HW // NH
PW = 3 * CW + 5 * HW
PPW = PW - CW
N_GROUPS = 4
EPG = 8
NE = N_GROUPS * EPG
DE = 512
N_MOD = 6
EPS = 1e-6
LOG2E = 1.4426950408889634

TM = 256
CH = 128
LEAF = 16
HEAD_GROUP = 4
MOE_P = 256
LANES = 128

VMEM_LIMIT = 56 * 1024 * 1024


def _cparams(n_axes, flags=None):
    return pltpu.CompilerParams(dimension_semantics=("arbitrary",) * n_axes,
                                vmem_limit_bytes=VMEM_LIMIT, flags=flags)


def _dot(a, b):
    return jnp.dot(a, b, preferred_element_type=F32)


def _dot_nt(a, b):
    return lax.dot_general(a, b, (((1,), (1,)), ((), ())), preferred_element_type=F32)


def _dot_tn(a, b):
    return lax.dot_general(a, b, (((0,), (0,)), ((), ())), preferred_element_type=F32)


def _split2(x):
    hi = x.astype(BF16)
    lo = (x - hi.astype(F32)).astype(BF16)
    return hi, lo


def _dot_hp(a, b):
    a_hi, a_lo = _split2(a)
    b_hi, b_lo = _split2(b)
    return _dot(a_hi, b_hi) + _dot(a_hi, b_lo) + _dot(a_lo, b_hi)


def _sigmoid(x):
    return 1.0 / (1.0 + jnp.exp(-x))


def _rms(x, w):
    return x * lax.rsqrt(jnp.mean(x * x, axis=-1, keepdims=True) + EPS) * w


def _ada_kernel(cc_ref, w_ref, b_ref, o_ref):
    s = cc_ref[...]
    s = s * _sigmoid(s)
    o_ref[0] = _dot_hp(s, w_ref[0]) + b_ref[0]


def _ada(cc, w_ada, b_ada):
    tn = 1536
    n = N_MOD * D
    return pl.pallas_call(
        _ada_kernel,
        grid=(DEPTH, n // tn),
        in_specs=[pl.BlockSpec((16, D), lambda l, j: (0, 0)),
                  pl.BlockSpec((1, D, tn), lambda l, j: (l, 0, j)),
                  pl.BlockSpec((1, 1, tn), lambda l, j: (l, 0, j))],
        out_specs=pl.BlockSpec((1, 16, tn), lambda l, j: (l, 0, j)),
        out_shape=jax.ShapeDtypeStruct((DEPTH, 16, n), F32),
        compiler_params=_cparams(2),
        name="ada",
    )(cc, w_ada, b_ada.reshape(DEPTH, 1, n))


def _silu(x):
    h = 0.5 * x
    return h + h * jnp.tanh(h)


def _neg_abs(x):
    bits = lax.bitcast_convert_type(x, jnp.uint32) | jnp.uint32(0x80000000)
    return lax.bitcast_convert_type(bits, F32)


def _forget_gate(z2, log2lb, log2_1mlb):
    ls = jnp.minimum(z2, 0.0) - jnp.log2(1.0 + jnp.exp2(_neg_abs(z2)))
    c = log2_1mlb + ls
    lf2 = jnp.maximum(log2lb, c) + jnp.log2(1.0 + jnp.exp2(_neg_abs(log2lb - c)))
    return lf2, jnp.exp2(c - z2)


def _inproj_kernel(x_ref, nw_ref, mod_ref, w_ref, lb_ref, conv_ref, g_ref, lf_ref, qkv_ref, w_b):
    @pl.when(jnp.logical_and(pl.program_id(0) == 0, pl.program_id(1) == 0))
    def _():
        def cast_rows(r, carry):
            rows = pl.ds(pl.multiple_of(r * 128, 128), 128)
            lo, hi = 3 * CW + HW, 3 * CW + 3 * HW
            w_b[rows, :lo] = w_ref[0, rows, :lo].astype(BF16)
            w_b[rows, lo:hi] = (w_ref[0, rows, lo:hi] * LOG2E).astype(BF16)
            w_b[rows, hi:] = w_ref[0, rows, hi:].astype(BF16)
            return carry

        lax.fori_loop(0, D // 128, cast_rows, 0)

    m = mod_ref[0, 0]
    h = _rms(x_ref[0], nw_ref[...] * (1.0 + m[1:2])) + m[0:1]
    hb = h.astype(BF16)
    blk = {k: _dot(hb, w_b[:, k * 512:(k + 1) * 512]) for k in (4, 5, 3, 1, 2, 0, 6, 7)}
    for d in range(2):
        lf2, kk = _forget_gate(blk[4 + d], lb_ref[2 * d:2 * d + 1], lb_ref[2 * d + 1:2 * d + 2])
        lf_ref[0, :, d * HW:(d + 1) * HW] = lf2
        qkv_ref[0, :, (1 + d) * HW:(2 + d) * HW] = kk.astype(BF16)
    qkv_ref[0, :, 0:HW] = _silu(blk[3]).astype(BF16)
    conv_ref[0, :, CW:2 * CW] = blk[1] * blk[2]
    conv_ref[0, :, 0:CW] = blk[0]
    qkv_ref[0, :, 3 * HW:4 * HW] = blk[6].astype(BF16)
    g_ref[0] = blk[7]


def _inproj(xall, nw, modall, w_in, lbp, l):
    bsz, t_all, _ = xall.shape
    tile = lambda w: pl.BlockSpec((1, TM, w), lambda b, j: (b, j, 0))
    return pl.pallas_call(
        _inproj_kernel,
        grid=(bsz, t_all // TM),
        in_specs=[pl.BlockSpec((1, TM, D), lambda b, j: (b, j, 0)),
                  pl.BlockSpec((1, D), lambda b, j: (0, 0)),
                  pl.BlockSpec((1, 1, N_MOD, D), lambda b, j: (b, jnp.minimum(j, 1), 0, 0)),
                  pl.BlockSpec((1, D, PW), lambda b, j: (l, 0, 0), pipeline_mode=pl.Buffered(1)),
                  pl.BlockSpec((4, HW), lambda b, j: (0, 0))],
        out_specs=[tile(2 * CW), tile(HW), tile(2 * HW), tile(4 * HW)],
        out_shape=[jax.ShapeDtypeStruct((bsz, t_all, 2 * CW), F32),
                   jax.ShapeDtypeStruct((bsz, t_all, HW), F32),
                   jax.ShapeDtypeStruct((bsz, t_all, 2 * HW), F32),
                   jax.ShapeDtypeStruct((bsz, t_all, 4 * HW), BF16)],
        scratch_shapes=[pltpu.VMEM((D, PW), BF16)],
        compiler_params=_cparams(2),
        name="inproj",
    )(xall, nw, modall, w_in, lbp)


def _gla_tables():
    t = np.arange(CH)[:, None]
    s = np.arange(CH)[None, :]
    x = t ^ s
    lvl = np.full((CH, CH), -1, np.int32)
    n, k = CH, 0
    while n > LEAF:
        lvl[(x < n) & (x >= n // 2)] = k
        n, k = n // 2, k + 1
    lvl[x < LEAF] = k
    fwd = np.where(s <= t, lvl, -1)
    return (np.stack([s <= t, s >= t]).astype(np.float32), np.stack([fwd, fwd.T]).astype(np.int32))


def _gla_factors(lf2, qh, kh, tri, reverse):
    hi = lf2.astype(BF16)
    lo = (lf2 - hi.astype(F32)).astype(BF16)
    b = _dot(tri, hi) + _dot(tri, lo)

    def block_ref(n, idx):
        r = b.reshape(CH // n, n, HW)[:, idx:idx + 1, :]
        return jnp.broadcast_to(r, (CH // n, n, HW)).reshape(CH, HW)

    pow2 = lambda e: jnp.exp2(e.astype(BF16))
    pieces = []
    n = CH
    while n > LEAF:
        h = n // 2
        a = pow2(_neg_abs(b - block_ref(n, h if reverse else h - 1)))
        pieces.append((qh * a, kh * a))
        n = h
    dl = b - block_ref(LEAF, LEAF // 2)
    pieces.append((qh * pow2(dl), kh * pow2(-dl)))

    edge = 0 if reverse else CH - 1
    b_edge = b[edge:edge + 1, :]
    q_in = qh * pow2(b)
    k_out = kh * pow2(b_edge - b)
    d_chunk = jnp.exp2(b_edge)
    return pieces, q_in, k_out, d_chunk


def _gla_heads(factors, v_refs, lvls, s_refs, o_refs):
    dirs = range(len(factors))
    for h0 in range(0, NH, HEAD_GROUP):
        heads = range(h0, h0 + HEAD_GROUP)
        sl = lambda hd: slice(hd * HD, (hd + 1) * HD)
        dots = {(d, hd): [_dot_nt(qa[:, sl(hd)], ka[:, sl(hd)]) for qa, ka in factors[d][0]]
                for hd in heads for d in dirs}
        scs = {}
        for hd in heads:
            for d in dirs:
                sc = jnp.zeros((CH, CH), F32)
                for k, dk in enumerate(dots[d, hd]):
                    sc = jnp.where(lvls[d] == k, dk, sc)
                scs[d, hd] = sc.astype(BF16)
        for hd in heads:
            for d in dirs:
                _, q_in, k_out, d_chunk = factors[d]
                vb = v_refs[d][0, :, sl(hd)]
                st = s_refs[d][hd]
                o_refs[d][0, :, sl(hd)] = _dot(scs[d, hd], vb) + _dot_nt(q_in[:, sl(hd)], st.astype(BF16))
                s_refs[d][hd] = st * d_chunk[:, sl(hd)] + _dot_tn(vb, k_out[:, sl(hd)])


def _gla_kernel(lff_ref, qf_ref, kf_ref, vf_ref, lfb_ref, qb_ref, kb_ref, vb_ref, tri_ref, lvl_ref, of_ref, ob_ref,
                sf_ref, sb_ref):
    @pl.when(pl.program_id(1) == 0)
    def _():
        sf_ref[...] = jnp.zeros_like(sf_ref)
        sb_ref[...] = jnp.zeros_like(sb_ref)

    factors = (_gla_factors(lff_ref[0], qf_ref[0], kf_ref[0], tri_ref[0], False),
               _gla_factors(lfb_ref[0], qb_ref[0], kb_ref[0], tri_ref[1], True))
    _gla_heads(factors, (vf_ref, vb_ref), (lvl_ref[0], lvl_ref[1]), (sf_ref, sb_ref), (of_ref, ob_ref))


def _gla(lf, qkv, n_ctx_chunks):
    bsz, t_all, _ = lf.shape
    nc = t_all // CH

    def cb(j):
        return jnp.where(j < n_ctx_chunks, n_ctx_chunks - 1 - j, nc - 1 - (j - n_ctx_chunks))

    blk = (1, CH, HW)
    fwd = lambda c: pl.BlockSpec(blk, lambda b, j: (b, j, c))
    bwd = lambda c: pl.BlockSpec(blk, lambda b, j: (b, cb(j), c))
    tri, lvl = _gla_tables()
    return pl.pallas_call(
        _gla_kernel,
        grid=(bsz, nc),
        in_specs=[fwd(0), fwd(0), fwd(1), fwd(3),
                  bwd(1), bwd(0), bwd(2), bwd(3),
                  pl.BlockSpec((2, CH, CH), lambda b, j: (0, 0, 0)),
                  pl.BlockSpec((2, CH, CH), lambda b, j: (0, 0, 0))],
        out_specs=[fwd(0), bwd(0)],
        out_shape=[jax.ShapeDtypeStruct((bsz, t_all, HW), F32)] * 2,
        scratch_shapes=[pltpu.VMEM((NH, HD, HD), F32), pltpu.VMEM((NH, HD, HD), F32)],
        compiler_params=_cparams(2),
        name="gla",
    )(lf, qkv, qkv, qkv, lf, qkv, qkv, qkv, jnp.asarray(tri, BF16), jnp.asarray(lvl))


def _shift_rows(u, k):
    return pltpu.roll(u, k % u.shape[0], axis=0)


def _mixout_kernel(x_ref, bg_ref, u_ref, g_ref, up_ref, un_ref, of_ref, ob_ref, mod_ref, cw_ref, cnw_ref,
                   hnw_ref, wo_ref, nw2_ref, wr_ref, br_ref, xm_ref, h2_ref, slab_ref, runs_ref, cnt_ref, wo_b, wr_b):
    j = pl.program_id(1)
    nt = pl.num_programs(1)
    first = jnp.logical_and(pl.program_id(0) == 0, j == 0)

    @pl.when(first)
    def _():
        cnt_ref[...] = jnp.zeros_like(cnt_ref)
        wo_b[...] = wo_ref[0].astype(BF16)
        wr_b[0], wr_b[1] = _split2(wr_ref[...])

    is_ctx = j == 0
    m = mod_ref[0, 0]
    u = u_ref[0]
    cw = cw_ref[...]
    t = lax.broadcasted_iota(jnp.int32, (TM, 1), 0)
    col_in_row = t & (GRID_W - 1)
    keep_l = jnp.where(is_ctx, jnp.where(t == 0, 0.0, 1.0), jnp.where(col_in_row == 0, 0.0, 1.0))
    keep_r = jnp.where(is_ctx, jnp.where(t == TM - 1, 0.0, 1.0), jnp.where(col_in_row == GRID_W - 1, 0.0, 1.0))
    hc = CW // 2

    def seq_taps(a, w):
        left = jnp.where(keep_l > 0.5, _shift_rows(a, 1), 0.0)
        right = jnp.where(keep_r > 0.5, _shift_rows(a, -1), 0.0)
        return w[0:1] * left + w[1:2] * a + w[2:3] * right

    def col_taps(a, w):
        up = jnp.concatenate([jnp.where(j == 1, 0.0, up_ref[0]), a[:TM - GRID_W]], axis=0)
        dn = jnp.concatenate([a[GRID_W:], jnp.where(j == nt - 1, 0.0, un_ref[0])], axis=0)
        return w[0:1] * up + w[1:2] * a + w[2:3] * dn

    y_second = lax.cond(is_ctx, seq_taps, col_taps, u[:, hc:], cw[:, hc:])
    conv = jnp.concatenate([seq_taps(u[:, :hc], cw[:, :hc]), y_second], axis=-1)
    y_conv = _rms(bg_ref[0] * conv, cnw_ref[...])

    o = of_ref[0] + ob_ref[0]
    g = g_ref[0]
    hnw = hnw_ref[...]
    recs = []
    for hd in range(NH):
        hs = slice(hd * HD, (hd + 1) * HD)
        recs.append(_rms(o[:, hs], hnw[:, hs]))
    y_rec = jnp.concatenate(recs, axis=-1) * (g * _sigmoid(g))

    y = jnp.concatenate([y_conv, y_rec], axis=-1).astype(BF16)
    xm = x_ref[0] + m[2:3] * _dot(y, wo_b[...])
    xm_ref[0] = xm
    h2 = _rms(xm, nw2_ref[...] * (1.0 + m[4:5])) + m[3:4]
    h2_ref[0] = h2.astype(BF16)

    h_hi, h_lo = _split2(h2)
    logit = _dot(h_hi, wr_b[0]) + _dot(h_hi, wr_b[1]) + _dot(h_lo, wr_b[0]) + br_ref[...]
    lane = lax.broadcasted_iota(jnp.int32, (TM, LANES), 1).astype(F32)
    ninf = -jnp.inf
    big = 1e9
    gl = jnp.where((lane >= NE) & (lane < NE + N_GROUPS), logit, ninf)
    gmax = jnp.max(gl, axis=-1, keepdims=True)
    gsel = jnp.min(jnp.where(gl == gmax, lane, big), axis=-1, keepdims=True) - NE
    pg = 1.0 / jnp.sum(jnp.exp(gl - gmax), axis=-1, keepdims=True)
    el = jnp.where((lane >= gsel * EPG) & (lane < gsel * EPG + EPG), logit, ninf)
    e1 = jnp.max(el, axis=-1, keepdims=True)
    i1 = jnp.min(jnp.where(el == e1, lane, big), axis=-1, keepdims=True)
    el2 = jnp.where(lane == i1, ninf, el)
    e2 = jnp.max(el2, axis=-1, keepdims=True)
    i2 = jnp.min(jnp.where(el2 == e2, lane, big), axis=-1, keepdims=True)
    r = jnp.exp(e2 - e1)
    g1 = pg / (1.0 + r)
    g2 = pg * r / (1.0 + r)
    oh1 = jnp.where(lane == i1, 1.0, 0.0)
    oh2 = jnp.where(lane == i2, 1.0, 0.0)
    cnt = oh1 + oh2
    rr = lax.broadcasted_iota(jnp.int32, (TM, TM), 0)
    cc = lax.broadcasted_iota(jnp.int32, (TM, TM), 1)
    before = jnp.where(cc < rr, 1.0, 0.0).astype(BF16)
    prior = _dot(before, cnt.astype(BF16))
    c_tile = jnp.sum(cnt, axis=0, keepdims=True)
    li = lax.broadcasted_iota(jnp.int32, (LANES, LANES), 0)
    lj = lax.broadcasted_iota(jnp.int32, (LANES, LANES), 1)
    lower_e = jnp.where(li < lj, 1.0, 0.0).astype(BF16)
    c_hi, c_lo = _split2(jnp.broadcast_to(c_tile, (8, LANES)))
    off = (_dot(c_hi, lower_e) + _dot(c_lo, lower_e))[0:1]
    pos = prior + off
    lpos1 = jnp.sum(oh1 * pos, axis=-1, keepdims=True)
    lpos2 = jnp.sum(oh2 * pos, axis=-1, keepdims=True)
    sub = lax.broadcasted_iota(jnp.int32, (8, LANES), 0)
    runs_ref[0] = jnp.where(sub == 0, c_tile, jnp.where(sub == 1, cnt_ref[...], jnp.where(sub == 2, off, 0.0)))
    cnt_ref[...] += c_tile
    slab = jnp.where(lane == 0, i1, 0.0)
    for k, val in enumerate((i2, lpos1, lpos2, g1, g2), start=1):
        slab = jnp.where(lane == k, val, slab)
    slab_ref[0] = slab


def _mixout(xall, conv_in, g_in, o_f, o_b, modall, cw, cnw, hnw, w_out, nw2, w_r, b_r, l):
    bsz, t_all, _ = xall.shape
    nt = t_all // TM
    n64 = t_all // GRID_W
    per = TM // GRID_W
    tile = lambda w, c: pl.BlockSpec((1, TM, w), lambda b, j, c=c: (b, j, c))
    full = lambda shape: pl.BlockSpec(shape, lambda b, j: (0,) * len(shape))
    layer = lambda shape: pl.BlockSpec((1, *shape), lambda b, j: (l,) + (0,) * len(shape), pipeline_mode=pl.Buffered(1))
    return pl.pallas_call(
        _mixout_kernel,
        grid=(bsz, nt),
        in_specs=[tile(D, 0),
                  tile(CW, 0), tile(CW, 1), tile(HW, 0),
                  pl.BlockSpec((1, GRID_W, CW // 2), lambda b, j: (b, jnp.maximum(j * per - 1, 0), 3)),
                  pl.BlockSpec((1, GRID_W, CW // 2), lambda b, j: (b, jnp.minimum(j * per + per, n64 - 1), 3)),
                  tile(HW, 0), tile(HW, 0),
                  pl.BlockSpec((1, 1, N_MOD, D), lambda b, j: (b, jnp.minimum(j, 1), 0, 0)),
                  full((3, CW)), full((1, CW)), full((1, HW)), layer((D, D)), full((1, D)),
                  full((D, LANES)), full((1, LANES))],
        out_specs=[tile(D, 0), tile(D, 0), tile(LANES, 0),
                   pl.BlockSpec((1, 8, LANES), lambda b, j: (b * nt + j, 0, 0)),
                   pl.BlockSpec((1, LANES), lambda b, j: (0, 0))],
        out_shape=[jax.ShapeDtypeStruct((bsz, t_all, D), F32),
                   jax.ShapeDtypeStruct((bsz, t_all, D), BF16),
                   jax.ShapeDtypeStruct((bsz, t_all, LANES), F32),
                   jax.ShapeDtypeStruct((bsz * nt, 8, LANES), F32),
                   jax.ShapeDtypeStruct((1, LANES), F32)],
        scratch_shapes=[pltpu.VMEM((D, D), BF16), pltpu.VMEM((2, D, LANES), BF16)],
        compiler_params=_cparams(2),
        name="mixout",
    )(xall, conv_in, conv_in, g_in, conv_in, conv_in, o_f, o_b, modall, cw, cnw, hnw, w_out, nw2, w_r, b_r)


RUN_BITS = TM.bit_length()
LONG_RUN_BIT = 6
ROW_TILE = (8, LANES)


def _run_copies(tab_ref, local_ref, sorted_ref, sem, *, to_sorted, wait):
    def per_expert(e, carry):
        ln = tab_ref[0, 0, e]
        lo = tab_ref[0, 0, NE + e]
        go = tab_ref[0, 0, 2 * NE + e]
        def copy_bits(bits):
            for bit in bits:
                size = 1 << bit
                done = ln & ~(2 * size - 1)

                @pl.when((ln & size) != 0)
                def _():
                    a = local_ref.at[pl.ds(lo + done, size)]
                    b = sorted_ref.at[pl.ds(go + done, size)]
                    cp = pltpu.make_async_copy(a, b, sem) if to_sorted else pltpu.make_async_copy(b, a, sem)
                    if wait:
                        cp.wait()
                    else:
                        cp.start()

        @pl.when(ln >= (1 << LONG_RUN_BIT))
        def _():
            copy_bits(range(RUN_BITS - 1, LONG_RUN_BIT - 1, -1))

        copy_bits(range(LONG_RUN_BIT - 1, -1, -1))
        return carry

    lax.fori_loop(0, NE, per_expert, 0)


def _wait_tile(local_ref, sorted_ref, sem, *, to_sorted):
    whole = sorted_ref.at[pl.ds(0, 2 * TM)]
    (pltpu.make_async_copy(local_ref, whole, sem) if to_sorted else pltpu.make_async_copy(whole, local_ref, sem)).wait()


def _local_positions(slab):
    col = lax.broadcasted_iota(jnp.int32, (TM, 2 * TM), 1).astype(F32)
    return col == slab[:, 2:3], col == slab[:, 3:4]


def _dispatch_kernel(tab_ref, ptab_ref, h2_ref, slab_ref, xs_ref, buf, zbuf, sem):
    @pl.when(pl.program_id(0) == 0)
    def _():
        zbuf[...] = jnp.zeros_like(zbuf)
        _run_copies(ptab_ref, zbuf, xs_ref, sem, to_sorted=True, wait=False)
        _run_copies(ptab_ref, zbuf, xs_ref, sem, to_sorted=True, wait=True)

        def zero_block(i, carry):
            cp = pltpu.make_async_copy(zbuf, xs_ref.at[pl.ds(i * MOE_P, MOE_P)], sem)
            cp.start()
            cp.wait()
            return carry

        lax.fori_loop(ptab_ref[0, 0, 3 * NE], xs_ref.shape[0] // MOE_P, zero_block, 0)

    i = pl.program_id(0)
    slot = lax.rem(i, 2)
    p1, p2 = _local_positions(slab_ref[...])
    perm_t = jnp.where(p1 | p2, 1.0, 0.0).astype(BF16)
    buf[slot] = _dot_tn(perm_t, h2_ref[...]).reshape(2 * TM, *ROW_TILE)

    @pl.when(i > 0)
    def _():
        _wait_tile(buf.at[1 - slot], xs_ref, sem, to_sorted=True)

    _run_copies(tab_ref, buf.at[slot], xs_ref, sem, to_sorted=True, wait=False)

    @pl.when(i == pl.num_programs(0) - 1)
    def _():
        _wait_tile(buf.at[slot], xs_ref, sem, to_sorted=True)


def _dispatch(tab, pad_tab, h2, slab, n_rows):
    n_tiles = tab.shape[0]
    return pl.pallas_call(
        _dispatch_kernel,
        grid=(n_tiles,),
        in_specs=[pl.BlockSpec((1, 1, LANES), lambda i: (i, 0, 0), memory_space=pltpu.SMEM),
                  pl.BlockSpec((1, 1, LANES), lambda i: (0, 0, 0), memory_space=pltpu.SMEM),
                  pl.BlockSpec((TM, D), lambda i: (i, 0)),
                  pl.BlockSpec((TM, LANES), lambda i: (i, 0))],
        out_specs=pl.BlockSpec(memory_space=pl.ANY),
        out_shape=jax.ShapeDtypeStruct((n_rows, *ROW_TILE), F32),
        scratch_shapes=[pltpu.VMEM((2, 2 * TM, *ROW_TILE), F32), pltpu.VMEM((MOE_P, *ROW_TILE), F32),
                        pltpu.SemaphoreType.DMA(())],
        compiler_params=_cparams(1),
        name="dispatch",
    )(tab, pad_tab, h2, slab)


def _expert_kernel(be_ref, nx_ref, nu_ref, xs_ref, wgu_hbm, wd_hbm, ys_ref, wgu_f, wd_f, wgu_b, wd_b, slot_ref, sem,
                   *, layer):
    i = pl.program_id(0)
    used = i < nu_ref[0]
    e = be_ref[i]

    def weight_copies(expert, slot):
        return (pltpu.make_async_copy(wgu_hbm.at[layer, expert], wgu_f.at[slot], sem.at[slot]),
                pltpu.make_async_copy(wd_hbm.at[layer, expert], wd_f.at[slot], sem.at[slot]))

    @pl.when(i == 0)
    def _():
        slot_ref[0] = 0
        for cp in weight_copies(e, 0):
            cp.start()

    @pl.when(jnp.logical_and(used, jnp.logical_or(i == 0, e != be_ref[jnp.maximum(i - 1, 0)])))
    def _():
        slot = jnp.where(i == 0, 0, 1 - slot_ref[0])
        slot_ref[0] = slot
        for cp in weight_copies(e, slot):
            cp.wait()
        wgu_b[...] = wgu_f[slot].astype(BF16)
        wd_b[...] = wd_f[slot].astype(BF16)

        @pl.when(nx_ref[i] >= 0)
        def _():
            for cp in weight_copies(nx_ref[i], 1 - slot):
                cp.start()

    @pl.when(used)
    def _():
        hmid = _dot(xs_ref[...].reshape(MOE_P, D).astype(BF16), wgu_b[...])
        a = hmid[:, :DE]
        act = (a * _sigmoid(a) * hmid[:, DE:]).astype(BF16)
        ys_ref[...] = _dot(act, wd_b[...]).reshape(MOE_P, *ROW_TILE)

    @pl.when(jnp.logical_not(used))
    def _():
        ys_ref[...] = jnp.zeros_like(ys_ref)


def _experts(block_e, next_e, n_used, xs, w_gu, w_down, l):
    n_rows = xs.shape[0]
    grid_spec = pltpu.PrefetchScalarGridSpec(
        num_scalar_prefetch=3,
        grid=(n_rows // MOE_P,),
        in_specs=[pl.BlockSpec((MOE_P, *ROW_TILE), lambda i, be, nx, nu: (jnp.minimum(i, nu[0] - 1), 0, 0)),
                  pl.BlockSpec(memory_space=pl.ANY),
                  pl.BlockSpec(memory_space=pl.ANY)],
        out_specs=pl.BlockSpec((MOE_P, *ROW_TILE), lambda i, be, nx, nu: (i, 0, 0)),
        scratch_shapes=[pltpu.VMEM((2, D, 2 * DE), F32), pltpu.VMEM((2, DE, D), F32),
                        pltpu.VMEM((D, 2 * DE), BF16), pltpu.VMEM((DE, D), BF16),
                        pltpu.SMEM((1,), jnp.int32), pltpu.SemaphoreType.DMA((2,))],
    )
    return pl.pallas_call(
        functools.partial(_expert_kernel, layer=l),
        grid_spec=grid_spec,
        out_shape=jax.ShapeDtypeStruct((n_rows, *ROW_TILE), F32),
        compiler_params=_cparams(1),
        name="experts",
    )(block_e, next_e, n_used, xs, w_gu, w_down)


def _combine_kernel(tab_ref, tab_next_ref, xm_ref, slab_ref, mod_ref, fnw_ref, ys_ref, o_ref, buf, sem, *, final):
    i = pl.program_id(0) * pl.num_programs(1) + pl.program_id(1)
    n = pl.num_programs(0) * pl.num_programs(1)
    slot = lax.rem(i, 2)

    @pl.when(i == 0)
    def _():
        _run_copies(tab_ref, buf.at[0], ys_ref, sem.at[0], to_sorted=False, wait=False)

    @pl.when(i + 1 < n)
    def _():
        _run_copies(tab_next_ref, buf.at[1 - slot], ys_ref, sem.at[1 - slot], to_sorted=False, wait=False)

    _wait_tile(buf.at[slot], ys_ref, sem.at[slot], to_sorted=False)
    slab = slab_ref[0]
    p1, p2 = _local_positions(slab)
    yb = buf[slot].reshape(2 * TM, D).astype(BF16)
    y1 = _dot(jnp.where(p1, 1.0, 0.0).astype(BF16), yb)
    y2 = _dot(jnp.where(p2, 1.0, 0.0).astype(BF16), yb)
    f = slab[:, 4:5] * y1 + slab[:, 5:6] * y2
    x = xm_ref[0] + mod_ref[0, 0][5:6] * f
    o_ref[0] = _rms(x, fnw_ref[...]) if final else x


def _combine(tab, xmid, slab, modall, fnw, ys, final):
    bsz, t_all, _ = xmid.shape
    nt = t_all // TM
    out_t = t_all - TM if final else t_all
    out_map = (lambda b, j: (b, jnp.maximum(j - 1, 0), 0)) if final else (lambda b, j: (b, j, 0))
    return pl.pallas_call(
        functools.partial(_combine_kernel, final=final),
        grid=(bsz, nt),
        in_specs=[pl.BlockSpec((1, 1, LANES), lambda b, j: (b * nt + j, 0, 0), memory_space=pltpu.SMEM),
                  pl.BlockSpec((1, 1, LANES), lambda b, j: (jnp.minimum(b * nt + j + 1, bsz * nt - 1), 0, 0),
                               memory_space=pltpu.SMEM),
                  pl.BlockSpec((1, TM, D), lambda b, j: (b, j, 0)),
                  pl.BlockSpec((1, TM, LANES), lambda b, j: (b, j, 0)),
                  pl.BlockSpec((1, 1, N_MOD, D), lambda b, j: (b, jnp.minimum(j, 1), 0, 0)),
                  pl.BlockSpec((1, D), lambda b, j: (0, 0)),
                  pl.BlockSpec(memory_space=pl.ANY)],
        out_specs=pl.BlockSpec((1, TM, D), out_map),
        out_shape=jax.ShapeDtypeStruct((bsz, out_t, D), F32),
        scratch_shapes=[pltpu.VMEM((2, 2 * TM, *ROW_TILE), F32), pltpu.SemaphoreType.DMA((2,))],
        compiler_params=_cparams(2),
        name="combine_final" if final else "combine",
    )(tab, tab, xmid, slab, modall, fnw, ys)


def _lower_bounds(hg_lb):
    p = jax.nn.softmax(hg_lb.astype(F32), axis=1)
    cs = jnp.cumsum(p, axis=1)
    return cs - cs[:, :1]


def kernel(x, c, ctx, c_ctx, norm_w, w_ada, b_ada, w_in, conv_w, conv_norm_w, hg_lb, hg_norm_w, w_out, w_rg, b_rg,
           w_re, b_re, w_e_gu, w_e_down, final_norm_w):
    bsz, seq, _ = x.shape
    n_ctx = ctx.shape[1]
    assert n_ctx == TM and seq % TM == 0
    t_all = n_ctx + seq
    n_tok = bsz * t_all
    n_assign = 2 * n_tok
    n_blocks = -(-(n_assign + NE * (MOE_P - 1)) // MOE_P)
    n_rows = n_blocks * MOE_P

    xall = jnp.concatenate([ctx, x], axis=1)
    cc = jnp.zeros((16, D), F32).at[:bsz].set(c).at[bsz].set(c_ctx)
    mod = _ada(cc, w_ada, b_ada).reshape(DEPTH, 16, N_MOD, D)
    lb = _lower_bounds(hg_lb)
    lbp = jnp.stack([jnp.log(lb[0]), jnp.log1p(-lb[0]), jnp.log(lb[1]), jnp.log1p(-lb[1])], axis=1) * LOG2E
    w_r = jnp.concatenate([w_re, w_rg, jnp.zeros((DEPTH, D, LANES - NE - N_GROUPS), F32)], axis=-1)
    b_r = jnp.concatenate([b_re, b_rg, jnp.zeros((DEPTH, LANES - NE - N_GROUPS), F32)], axis=-1)

    out = None
    for l in range(DEPTH):
        final = l == DEPTH - 1
        modall = jnp.stack([jnp.broadcast_to(mod[l, bsz], (bsz, N_MOD, D)), mod[l, :bsz]], axis=1)
        conv_in, g_in, lf, qkv = _inproj(xall, norm_w[l, 0].reshape(1, D), modall, w_in, lbp[l], l)
        o_f, o_b = _gla(lf, qkv, n_ctx // CH)
        xmid, h2, slab, runs, counts = _mixout(
            xall, conv_in, g_in, o_f, o_b, modall, conv_w[l], conv_norm_w[l].reshape(1, CW),
            jnp.tile(hg_norm_w[l], NH).reshape(1, HW), w_out, norm_w[l, 1].reshape(1, D),
            w_r[l], b_r[l].reshape(1, LANES), l)
        cnt = counts[0, :NE].astype(jnp.int32)
        padded = (cnt + MOE_P - 1) // MOE_P * MOE_P
        pad_ends = jnp.cumsum(padded)
        pad_starts = pad_ends - padded
        run_len = runs[:, 0, :NE].astype(jnp.int32)
        run_global = pad_starts[None, :] + runs[:, 1, :NE].astype(jnp.int32)
        run_local = runs[:, 2, :NE].astype(jnp.int32)
        tab = jnp.concatenate([run_len, run_local, run_global, jnp.zeros_like(run_len)], axis=1)[:, None, :]
        block_start = jnp.arange(n_blocks, dtype=jnp.int32) * MOE_P
        block_e = jnp.minimum(jnp.sum((pad_ends[None, :] <= block_start[:, None]).astype(jnp.int32), axis=1), NE - 1)
        n_used = (pad_ends[-1:] // MOE_P).astype(jnp.int32)
        zero = jnp.zeros_like(cnt)
        pad_tab = jnp.concatenate([padded - cnt, zero, pad_starts + cnt, zero + n_used]).reshape(1, 1, LANES)
        xs = _dispatch(tab, pad_tab, h2.reshape(n_tok, D), slab.reshape(n_tok, LANES), n_rows)
        owners = jnp.where(cnt > 0, jnp.arange(NE, dtype=jnp.int32), NE)
        later = jnp.concatenate([lax.cummin(owners, reverse=True)[1:], jnp.full((1,), NE, jnp.int32)])
        next_e = jnp.where(later < NE, later, -1)[block_e]
        ys = _experts(block_e, next_e, n_used, xs, w_e_gu, w_e_down, l)
        res = _combine(tab, xmid, slab, modall, final_norm_w.reshape(1, D), ys, final)
        if final:
            out = res
        else:
            xall = res
    return out
```

```python
import functools

import jax
import jax.numpy as jnp
import numpy as np
from jax import lax
from jax.experimental import pallas as pl
from jax.experimental.pallas import tpu as pltpu

F32 = jnp.float32
BF16 = jnp.bfloat16

D = 1024
DEPTH = 4
GRID_W = 64
CW = 512
HW = 512
NH = 4
HD = HW // NH
PW = 3 * CW + 5 * HW
PPW = PW - CW
N_GROUPS = 4
EPG = 8
NE = N_GROUPS * EPG
DE = 512
N_MOD = 6
EPS = 1e-6
LOG2E = 1.4426950408889634

TM = 256
CH = 128
LEAF = 16
HEAD_GROUP = 4
MOE_P = 256
EXPERT_CHUNK = 256
LANES = 128

VMEM_LIMIT = 56 * 1024 * 1024


def _cparams(n_axes, flags=None):
    return pltpu.CompilerParams(dimension_semantics=("arbitrary",) * n_axes,
                                vmem_limit_bytes=VMEM_LIMIT, flags=flags)


def _dot(a, b):
    return jnp.dot(a, b, preferred_element_type=F32)


def _dot_nt(a, b):
    return lax.dot_general(a, b, (((1,), (1,)), ((), ())), preferred_element_type=F32)


def _dot_tn(a, b):
    return lax.dot_general(a, b, (((0,), (0,)), ((), ())), preferred_element_type=F32)


def _split2(x):
    hi = x.astype(BF16)
    lo = (x - hi.astype(F32)).astype(BF16)
    return hi, lo


def _dot_hp(a, b):
    a_hi, a_lo = _split2(a)
    b_hi, b_lo = _split2(b)
    return _dot(a_hi, b_hi) + _dot(a_hi, b_lo) + _dot(a_lo, b_hi)


def _sigmoid(x):
    return 1.0 / (1.0 + jnp.exp(-x))


def _rms(x, w):
    return x * lax.rsqrt(jnp.mean(x * x, axis=-1, keepdims=True) + EPS) * w


def _ada_kernel(cc_ref, w_ref, b_ref, o_ref):
    s = cc_ref[...]
    s = s * _sigmoid(s)
    o_ref[0] = _dot_hp(s, w_ref[0]) + b_ref[0]


def _ada(cc, w_ada, b_ada):
    tn = 1536
    n = N_MOD * D
    return pl.pallas_call(
        _ada_kernel,
        grid=(DEPTH, n // tn),
        in_specs=[pl.BlockSpec((16, D), lambda l, j: (0, 0)),
                  pl.BlockSpec((1, D, tn), lambda l, j: (l, 0, j)),
                  pl.BlockSpec((1, 1, tn), lambda l, j: (l, 0, j))],
        out_specs=pl.BlockSpec((1, 16, tn), lambda l, j: (l, 0, j)),
        out_shape=jax.ShapeDtypeStruct((DEPTH, 16, n), F32),
        compiler_params=_cparams(2),
        name="ada",
    )(cc, w_ada, b_ada.reshape(DEPTH, 1, n))


def _silu(x):
    h = 0.5 * x
    return h + h * jnp.tanh(h)


def _neg_abs(x):
    bits = lax.bitcast_convert_type(x, jnp.uint32) | jnp.uint32(0x80000000)
    return lax.bitcast_convert_type(bits, F32)


def _forget_gate(z2, log2lb, log2_1mlb):
    ls = jnp.minimum(z2, 0.0) - jnp.log2(1.0 + jnp.exp2(_neg_abs(z2)))
    c = log2_1mlb + ls
    lf2 = jnp.maximum(log2lb, c) + jnp.log2(1.0 + jnp.exp2(_neg_abs(log2lb - c)))
    return lf2, jnp.exp2(c - z2)


def _inproj_kernel(x_ref, nw_ref, mod_ref, w_ref, lb_ref, conv_ref, g_ref, lf_ref, qkv_ref, w_b):
    @pl.when(jnp.logical_and(pl.program_id(0) == 0, pl.program_id(1) == 0))
    def _():
        def cast_rows(r, carry):
            rows = pl.ds(pl.multiple_of(r * 128, 128), 128)
            lo, hi = 3 * CW + HW, 3 * CW + 3 * HW
            w_b[rows, :lo] = w_ref[0, rows, :lo].astype(BF16)
            w_b[rows, lo:hi] = (w_ref[0, rows, lo:hi] * LOG2E).astype(BF16)
            w_b[rows, hi:] = w_ref[0, rows, hi:].astype(BF16)
            return carry

        lax.fori_loop(0, D // 128, cast_rows, 0)

    m = mod_ref[0, 0]
    h = _rms(x_ref[0], nw_ref[...] * (1.0 + m[1:2])) + m[0:1]
    hb = h.astype(BF16)
    blk = {k: _dot(hb, w_b[:, k * 512:(k + 1) * 512]) for k in (4, 5, 3, 1, 2, 0, 6, 7)}
    for d in range(2):
        lf2, kk = _forget_gate(blk[4 + d], lb_ref[2 * d:2 * d + 1], lb_ref[2 * d + 1:2 * d + 2])
        lf_ref[0, :, d * HW:(d + 1) * HW] = lf2
        qkv_ref[0, :, (1 + d) * HW:(2 + d) * HW] = kk.astype(BF16)
    qkv_ref[0, :, 0:HW] = _silu(blk[3]).astype(BF16)
    conv_ref[0, :, CW:2 * CW] = blk[1] * blk[2]
    conv_ref[0, :, 0:CW] = blk[0]
    qkv_ref[0, :, 3 * HW:4 * HW] = blk[6].astype(BF16)
    g_ref[0] = blk[7]


def _inproj(xall, nw, modall, w_in, lbp, l):
    bsz, t_all, _ = xall.shape
    tile = lambda w: pl.BlockSpec((1, TM, w), lambda b, j: (b, j, 0))
    return pl.pallas_call(
        _inproj_kernel,
        grid=(bsz, t_all // TM),
        in_specs=[pl.BlockSpec((1, TM, D), lambda b, j: (b, j, 0)),
                  pl.BlockSpec((1, D), lambda b, j: (0, 0)),
                  pl.BlockSpec((1, 1, N_MOD, D), lambda b, j: (b, jnp.minimum(j, 1), 0, 0)),
                  pl.BlockSpec((1, D, PW), lambda b, j: (l, 0, 0), pipeline_mode=pl.Buffered(1)),
                  pl.BlockSpec((4, HW), lambda b, j: (0, 0))],
        out_specs=[tile(2 * CW), tile(HW), tile(2 * HW), tile(4 * HW)],
        out_shape=[jax.ShapeDtypeStruct((bsz, t_all, 2 * CW), F32),
                   jax.ShapeDtypeStruct((bsz, t_all, HW), F32),
                   jax.ShapeDtypeStruct((bsz, t_all, 2 * HW), F32),
                   jax.ShapeDtypeStruct((bsz, t_all, 4 * HW), BF16)],
        scratch_shapes=[pltpu.VMEM((D, PW), BF16)],
        compiler_params=_cparams(2),
        name="inproj",
    )(xall, nw, modall, w_in, lbp)


def _gla_tables():
    t = np.arange(CH)[:, None]
    s = np.arange(CH)[None, :]
    x = t ^ s
    lvl = np.full((CH, CH), -1, np.int32)
    n, k = CH, 0
    while n > LEAF:
        lvl[(x < n) & (x >= n // 2)] = k
        n, k = n // 2, k + 1
    lvl[x < LEAF] = k
    fwd = np.where(s <= t, lvl, -1)
    return (np.stack([s <= t, s >= t]).astype(np.float32), np.stack([fwd, fwd.T]).astype(np.int32))


def _gla_factors(lf2, qh, kh, tri, reverse):
    hi = lf2.astype(BF16)
    lo = (lf2 - hi.astype(F32)).astype(BF16)
    b = _dot(tri, hi) + _dot(tri, lo)

    def block_ref(n, idx):
        r = b.reshape(CH // n, n, HW)[:, idx:idx + 1, :]
        return jnp.broadcast_to(r, (CH // n, n, HW)).reshape(CH, HW)

    pow2 = lambda e: jnp.exp2(e.astype(BF16))
    pieces = []
    n = CH
    while n > LEAF:
        h = n // 2
        a = pow2(_neg_abs(b - block_ref(n, h if reverse else h - 1)))
        pieces.append((qh * a, kh * a))
        n = h
    dl = b - block_ref(LEAF, LEAF // 2)
    pieces.append((qh * pow2(dl), kh * pow2(-dl)))

    edge = 0 if reverse else CH - 1
    b_edge = b[edge:edge + 1, :]
    q_in = qh * pow2(b)
    k_out = kh * pow2(b_edge - b)
    d_chunk = jnp.exp2(b_edge)
    return pieces, q_in, k_out, d_chunk


def _gla_heads(factors, v_refs, lvls, s_refs, o_refs):
    dirs = range(len(factors))
    for h0 in range(0, NH, HEAD_GROUP):
        heads = range(h0, h0 + HEAD_GROUP)
        sl = lambda hd: slice(hd * HD, (hd + 1) * HD)
        dots = {(d, hd): [_dot_nt(qa[:, sl(hd)], ka[:, sl(hd)]) for qa, ka in factors[d][0]]
                for hd in heads for d in dirs}
        scs = {}
        for hd in heads:
            for d in dirs:
                sc = jnp.zeros((CH, CH), F32)
                for k, dk in enumerate(dots[d, hd]):
                    sc = jnp.where(lvls[d] == k, dk, sc)
                scs[d, hd] = sc.astype(BF16)
        for hd in heads:
            for d in dirs:
                _, q_in, k_out, d_chunk = factors[d]
                vb = v_refs[d][0, :, sl(hd)]
                st = s_refs[d][hd]
                o_refs[d][0, :, sl(hd)] = _dot(scs[d, hd], vb) + _dot_nt(q_in[:, sl(hd)], st.astype(BF16))
                s_refs[d][hd] = st * d_chunk[:, sl(hd)] + _dot_tn(vb, k_out[:, sl(hd)])


def _gla_kernel(lff_ref, qf_ref, kf_ref, vf_ref, lfb_ref, qb_ref, kb_ref, vb_ref, tri_ref, lvl_ref, of_ref, ob_ref,
                sf_ref, sb_ref):
    @pl.when(pl.program_id(1) == 0)
    def _():
        sf_ref[...] = jnp.zeros_like(sf_ref)
        sb_ref[...] = jnp.zeros_like(sb_ref)

    factors = (_gla_factors(lff_ref[0], qf_ref[0], kf_ref[0], tri_ref[0], False),
               _gla_factors(lfb_ref[0], qb_ref[0], kb_ref[0], tri_ref[1], True))
    _gla_heads(factors, (vf_ref, vb_ref), (lvl_ref[0], lvl_ref[1]), (sf_ref, sb_ref), (of_ref, ob_ref))


def _gla(lf, qkv, n_ctx_chunks):
    bsz, t_all, _ = lf.shape
    nc = t_all // CH

    def cb(j):
        return jnp.where(j < n_ctx_chunks, n_ctx_chunks - 1 - j, nc - 1 - (j - n_ctx_chunks))

    blk = (1, CH, HW)
    fwd = lambda c: pl.BlockSpec(blk, lambda b, j: (b, j, c))
    bwd = lambda c: pl.BlockSpec(blk, lambda b, j: (b, cb(j), c))
    tri, lvl = _gla_tables()
    return pl.pallas_call(
        _gla_kernel,
        grid=(bsz, nc),
        in_specs=[fwd(0), fwd(0), fwd(1), fwd(3),
                  bwd(1), bwd(0), bwd(2), bwd(3),
                  pl.BlockSpec((2, CH, CH), lambda b, j: (0, 0, 0)),
                  pl.BlockSpec((2, CH, CH), lambda b, j: (0, 0, 0))],
        out_specs=[fwd(0), bwd(0)],
        out_shape=[jax.ShapeDtypeStruct((bsz, t_all, HW), F32)] * 2,
        scratch_shapes=[pltpu.VMEM((NH, HD, HD), F32), pltpu.VMEM((NH, HD, HD), F32)],
        compiler_params=_cparams(2),
        name="gla",
    )(lf, qkv, qkv, qkv, lf, qkv, qkv, qkv, jnp.asarray(tri, BF16), jnp.asarray(lvl))


def _shift_rows(u, k):
    return pltpu.roll(u, k % u.shape[0], axis=0)


def _mixout_kernel(x_ref, bg_ref, u_ref, g_ref, up_ref, un_ref, of_ref, ob_ref, mod_ref, cw_ref, cnw_ref,
                   hnw_ref, wo_ref, nw2_ref, wr_ref, br_ref, xm_ref, h2_ref, slab_ref, runs_ref, cnt_ref, wo_b, wr_b):
    j = pl.program_id(1)
    nt = pl.num_programs(1)
    first = jnp.logical_and(pl.program_id(0) == 0, j == 0)

    @pl.when(first)
    def _():
        cnt_ref[...] = jnp.zeros_like(cnt_ref)
        wo_b[...] = wo_ref[0].astype(BF16)
        wr_b[0], wr_b[1] = _split2(wr_ref[...])

    is_ctx = j == 0
    m = mod_ref[0, 0]
    u = u_ref[0]
    cw = cw_ref[...]
    t = lax.broadcasted_iota(jnp.int32, (TM, 1), 0)
    col_in_row = t & (GRID_W - 1)
    keep_l = jnp.where(is_ctx, jnp.where(t == 0, 0.0, 1.0), jnp.where(col_in_row == 0, 0.0, 1.0))
    keep_r = jnp.where(is_ctx, jnp.where(t == TM - 1, 0.0, 1.0), jnp.where(col_in_row == GRID_W - 1, 0.0, 1.0))
    hc = CW // 2

    def seq_taps(a, w):
        left = jnp.where(keep_l > 0.5, _shift_rows(a, 1), 0.0)
        right = jnp.where(keep_r > 0.5, _shift_rows(a, -1), 0.0)
        return w[0:1] * left + w[1:2] * a + w[2:3] * right

    def col_taps(a, w):
        up = jnp.concatenate([jnp.where(j == 1, 0.0, up_ref[0]), a[:TM - GRID_W]], axis=0)
        dn = jnp.concatenate([a[GRID_W:], jnp.where(j == nt - 1, 0.0, un_ref[0])], axis=0)
        return w[0:1] * up + w[1:2] * a + w[2:3] * dn

    y_second = lax.cond(is_ctx, seq_taps, col_taps, u[:, hc:], cw[:, hc:])
    conv = jnp.concatenate([seq_taps(u[:, :hc], cw[:, :hc]), y_second], axis=-1)
    y_conv = _rms(bg_ref[0] * conv, cnw_ref[...])

    o = of_ref[0] + ob_ref[0]
    g = g_ref[0]
    hnw = hnw_ref[...]
    recs = []
    for hd in range(NH):
        hs = slice(hd * HD, (hd + 1) * HD)
        recs.append(_rms(o[:, hs], hnw[:, hs]))
    y_rec = jnp.concatenate(recs, axis=-1) * (g * _sigmoid(g))

    y = jnp.concatenate([y_conv, y_rec], axis=-1).astype(BF16)
    xm = x_ref[0] + m[2:3] * _dot(y, wo_b[...])
    xm_ref[0] = xm
    h2 = _rms(xm, nw2_ref[...] * (1.0 + m[4:5])) + m[3:4]
    h2_ref[0] = h2.astype(BF16)

    h_hi, h_lo = _split2(h2)
    lt = (_dot(h_hi, wr_b[0]) + _dot(h_hi, wr_b[1]) + _dot(h_lo, wr_b[0]) + br_ref[...]).T
    ninf = -jnp.inf
    big = 1e9
    over = lambda fn, a: fn(a, axis=0, keepdims=True)
    grow = lax.broadcasted_iota(jnp.int32, (8, TM), 0).astype(F32)
    gl = jnp.where(grow < N_GROUPS, lt[NE:NE + 8], ninf)
    gmax = over(jnp.max, gl)
    gsel = over(jnp.min, jnp.where(gl == gmax, grow, big))
    pg = 1.0 / over(jnp.sum, jnp.exp(gl - gmax))
    erow_i = lax.broadcasted_iota(jnp.int32, (NE, TM), 0)
    erow = erow_i.astype(F32)
    el = jnp.where((erow_i // EPG).astype(F32) == gsel, lt[:NE], ninf)
    e1 = over(jnp.max, el)
    i1 = over(jnp.min, jnp.where(el == e1, erow, big))
    el2 = jnp.where(erow == i1, ninf, el)
    e2 = over(jnp.max, el2)
    i2 = over(jnp.min, jnp.where(el2 == e2, erow, big))
    r = jnp.exp(e2 - e1)
    g1 = pg / (1.0 + r)
    g2 = pg * r / (1.0 + r)
    oh1 = jnp.where(erow == i1, 1.0, 0.0)
    oh2 = jnp.where(erow == i2, 1.0, 0.0)
    cnt = oh1 + oh2
    cnt_b = cnt.astype(BF16)
    ts = lax.broadcasted_iota(jnp.int32, (TM, TM), 0)
    tt = lax.broadcasted_iota(jnp.int32, (TM, TM), 1)
    prior = _dot(cnt_b, jnp.where(ts < tt, 1.0, 0.0).astype(BF16))
    c_col = jnp.sum(cnt, axis=1, keepdims=True)
    ei = lax.broadcasted_iota(jnp.int32, (NE, NE), 0)
    ej = lax.broadcasted_iota(jnp.int32, (NE, NE), 1)
    cc_hi, cc_lo = _split2(jnp.broadcast_to(c_col, (NE, LANES)))
    fewer = jnp.where(ej < ei, 1.0, 0.0).astype(BF16)
    off_col = (_dot(fewer, cc_hi) + _dot(fewer, cc_lo))[:, 0:1]
    pos = prior + off_col
    lpos1 = over(jnp.sum, oh1 * pos)
    lpos2 = over(jnp.sum, oh2 * pos)
    cnt_pad = jnp.concatenate([cnt_b, jnp.zeros((LANES - NE, TM), BF16)], axis=0)
    c_tile = _dot_nt(jnp.ones((8, TM), BF16), cnt_pad)[0:1]
    li = lax.broadcasted_iota(jnp.int32, (LANES, LANES), 0)
    lj = lax.broadcasted_iota(jnp.int32, (LANES, LANES), 1)
    lower_e = jnp.where(li < lj, 1.0, 0.0).astype(BF16)
    c_hi, c_lo = _split2(jnp.broadcast_to(c_tile, (8, LANES)))
    off = (_dot(c_hi, lower_e) + _dot(c_lo, lower_e))[0:1]
    sub = lax.broadcasted_iota(jnp.int32, (8, LANES), 0)
    runs_ref[0] = jnp.where(sub == 0, c_tile, jnp.where(sub == 1, cnt_ref[...], jnp.where(sub == 2, off, 0.0)))
    cnt_ref[...] += c_tile
    frow = lax.broadcasted_iota(jnp.int32, (LANES, TM), 0)
    fields = jnp.zeros((LANES, TM), F32)
    for k, val in enumerate((i1, i2, lpos1, lpos2, g1, g2)):
        fields = jnp.where(frow == k, val, fields)
    slab_ref[0] = fields.T


def _mixout(xall, conv_in, g_in, o_f, o_b, modall, cw, cnw, hnw, w_out, nw2, w_r, b_r, l):
    bsz, t_all, _ = xall.shape
    nt = t_all // TM
    n64 = t_all // GRID_W
    per = TM // GRID_W
    tile = lambda w, c: pl.BlockSpec((1, TM, w), lambda b, j, c=c: (b, j, c))
    full = lambda shape: pl.BlockSpec(shape, lambda b, j: (0,) * len(shape))
    layer = lambda shape: pl.BlockSpec((1, *shape), lambda b, j: (l,) + (0,) * len(shape), pipeline_mode=pl.Buffered(1))
    return pl.pallas_call(
        _mixout_kernel,
        grid=(bsz, nt),
        in_specs=[tile(D, 0),
                  tile(CW, 0), tile(CW, 1), tile(HW, 0),
                  pl.BlockSpec((1, GRID_W, CW // 2), lambda b, j: (b, jnp.maximum(j * per - 1, 0), 3)),
                  pl.BlockSpec((1, GRID_W, CW // 2), lambda b, j: (b, jnp.minimum(j * per + per, n64 - 1), 3)),
                  tile(HW, 0), tile(HW, 0),
                  pl.BlockSpec((1, 1, N_MOD, D), lambda b, j: (b, jnp.minimum(j, 1), 0, 0)),
                  full((3, CW)), full((1, CW)), full((1, HW)), layer((D, D)), full((1, D)),
                  full((D, LANES)), full((1, LANES))],
        out_specs=[tile(D, 0), tile(D, 0), tile(LANES, 0),
                   pl.BlockSpec((1, 8, LANES), lambda b, j: (b * nt + j, 0, 0)),
                   pl.BlockSpec((1, LANES), lambda b, j: (0, 0))],
        out_shape=[jax.ShapeDtypeStruct((bsz, t_all, D), F32),
                   jax.ShapeDtypeStruct((bsz, t_all, D), BF16),
                   jax.ShapeDtypeStruct((bsz, t_all, LANES), F32),
                   jax.ShapeDtypeStruct((bsz * nt, 8, LANES), F32),
                   jax.ShapeDtypeStruct((1, LANES), F32)],
        scratch_shapes=[pltpu.VMEM((D, D), BF16), pltpu.VMEM((2, D, LANES), BF16)],
        compiler_params=_cparams(2),
        name="mixout",
    )(xall, conv_in, conv_in, g_in, conv_in, conv_in, o_f, o_b, modall, cw, cnw, hnw, w_out, nw2, w_r, b_r)


RUN_BITS = TM.bit_length()
LONG_RUN_BIT = 5
ROW_TILE = (8, LANES)


def _run_copies(tab_ref, local_ref, sorted_ref, sem, *, to_sorted, wait):
    def per_expert(e, carry):
        ln = tab_ref[0, 0, e]
        lo = tab_ref[0, 0, NE + e]
        go = tab_ref[0, 0, 2 * NE + e]
        def copy_bits(bits):
            for bit in bits:
                size = 1 << bit
                done = ln & ~(2 * size - 1)

                @pl.when((ln & size) != 0)
                def _():
                    a = local_ref.at[pl.ds(lo + done, size)]
                    b = sorted_ref.at[pl.ds(go + done, size)]
                    cp = pltpu.make_async_copy(a, b, sem) if to_sorted else pltpu.make_async_copy(b, a, sem)
                    if wait:
                        cp.wait()
                    else:
                        cp.start()

        @pl.when(ln >= (1 << LONG_RUN_BIT))
        def _():
            copy_bits(range(RUN_BITS - 1, LONG_RUN_BIT - 1, -1))

        copy_bits(range(LONG_RUN_BIT - 1, -1, -1))
        return carry

    lax.fori_loop(0, NE, per_expert, 0)


def _wait_tile(local_ref, sorted_ref, sem, *, to_sorted):
    whole = sorted_ref.at[pl.ds(0, 2 * TM)]
    (pltpu.make_async_copy(local_ref, whole, sem) if to_sorted else pltpu.make_async_copy(whole, local_ref, sem)).wait()


def _local_positions(slab):
    col = lax.broadcasted_iota(jnp.int32, (TM, 2 * TM), 1).astype(F32)
    return col == slab[:, 2:3], col == slab[:, 3:4]


def _dispatch_kernel(tab_ref, ptab_ref, h2_ref, slab_ref, xs_ref, buf, zbuf, sem):
    @pl.when(pl.program_id(0) == 0)
    def _():
        zbuf[...] = jnp.zeros_like(zbuf)
        _run_copies(ptab_ref, zbuf, xs_ref, sem, to_sorted=True, wait=False)
        _run_copies(ptab_ref, zbuf, xs_ref, sem, to_sorted=True, wait=True)

        def zero_block(i, carry):
            cp = pltpu.make_async_copy(zbuf, xs_ref.at[pl.ds(i * MOE_P, MOE_P)], sem)
            cp.start()
            cp.wait()
            return carry

        lax.fori_loop(ptab_ref[0, 0, 3 * NE], xs_ref.shape[0] // MOE_P, zero_block, 0)

    i = pl.program_id(0)
    slot = lax.rem(i, 2)
    p1, p2 = _local_positions(slab_ref[...])
    perm_t = jnp.where(p1 | p2, 1.0, 0.0).astype(BF16)
    buf[slot] = _dot_tn(perm_t, h2_ref[...]).reshape(2 * TM, *ROW_TILE)

    @pl.when(i > 0)
    def _():
        _wait_tile(buf.at[1 - slot], xs_ref, sem, to_sorted=True)

    _run_copies(tab_ref, buf.at[slot], xs_ref, sem, to_sorted=True, wait=False)

    @pl.when(i == pl.num_programs(0) - 1)
    def _():
        _wait_tile(buf.at[slot], xs_ref, sem, to_sorted=True)


def _dispatch(tab, pad_tab, h2, slab, n_rows):
    n_tiles = tab.shape[0]
    return pl.pallas_call(
        _dispatch_kernel,
        grid=(n_tiles,),
        in_specs=[pl.BlockSpec((1, 1, LANES), lambda i: (i, 0, 0), memory_space=pltpu.SMEM),
                  pl.BlockSpec((1, 1, LANES), lambda i: (0, 0, 0), memory_space=pltpu.SMEM),
                  pl.BlockSpec((TM, D), lambda i: (i, 0)),
                  pl.BlockSpec((TM, LANES), lambda i: (i, 0))],
        out_specs=pl.BlockSpec(memory_space=pl.ANY),
        out_shape=jax.ShapeDtypeStruct((n_rows, *ROW_TILE), F32),
        scratch_shapes=[pltpu.VMEM((2, 2 * TM, *ROW_TILE), F32), pltpu.VMEM((MOE_P, *ROW_TILE), F32),
                        pltpu.SemaphoreType.DMA(())],
        compiler_params=_cparams(1),
        name="dispatch",
    )(tab, pad_tab, h2, slab)


def _expert_kernel(be_ref, nx_ref, nu_ref, xs_ref, wgu_hbm, wd_hbm, ys_ref, wgu_f, wd_f, wgu_b, wd_b, slot_ref, sem,
                   *, layer):
    i = pl.program_id(0)
    used = i < nu_ref[0]
    e = be_ref[i]

    def weight_copies(expert, slot):
        return (pltpu.make_async_copy(wgu_hbm.at[layer, expert], wgu_f.at[slot], sem.at[slot]),
                pltpu.make_async_copy(wd_hbm.at[layer, expert], wd_f.at[slot], sem.at[slot]))

    @pl.when(i == 0)
    def _():
        slot_ref[0] = 0
        for cp in weight_copies(e, 0):
            cp.start()

    @pl.when(jnp.logical_and(used, jnp.logical_or(i == 0, e != be_ref[jnp.maximum(i - 1, 0)])))
    def _():
        slot = jnp.where(i == 0, 0, 1 - slot_ref[0])
        slot_ref[0] = slot
        for cp in weight_copies(e, slot):
            cp.wait()
        wgu_b[...] = wgu_f[slot].astype(BF16)
        wd_b[...] = wd_f[slot].astype(BF16)

        @pl.when(nx_ref[i] >= 0)
        def _():
            for cp in weight_copies(nx_ref[i], 1 - slot):
                cp.start()

    @pl.when(used)
    def _():
        x = xs_ref[...].reshape(MOE_P, D).astype(BF16)
        acts = []
        for c in range(0, DE, EXPERT_CHUNK):
            a = _dot(x, wgu_b[:, c:c + EXPERT_CHUNK])
            u = _dot(x, wgu_b[:, DE + c:DE + c + EXPERT_CHUNK])
            acts.append((_silu(a) * u).astype(BF16))
        y = _dot(jnp.concatenate(acts, axis=1), wd_b[...])
        ys_ref[...] = y.reshape(MOE_P, *ROW_TILE)

    @pl.when(jnp.logical_not(used))
    def _():
        ys_ref[...] = jnp.zeros_like(ys_ref)


def _experts(block_e, next_e, n_used, xs, w_gu, w_down, l):
    n_rows = xs.shape[0]
    grid_spec = pltpu.PrefetchScalarGridSpec(
        num_scalar_prefetch=3,
        grid=(n_rows // MOE_P,),
        in_specs=[pl.BlockSpec((MOE_P, *ROW_TILE), lambda i, be, nx, nu: (jnp.minimum(i, nu[0] - 1), 0, 0)),
                  pl.BlockSpec(memory_space=pl.ANY),
                  pl.BlockSpec(memory_space=pl.ANY)],
        out_specs=pl.BlockSpec((MOE_P, *ROW_TILE), lambda i, be, nx, nu: (i, 0, 0)),
        scratch_shapes=[pltpu.VMEM((2, D, 2 * DE), F32), pltpu.VMEM((2, DE, D), F32),
                        pltpu.VMEM((D, 2 * DE), BF16), pltpu.VMEM((DE, D), BF16),
                        pltpu.SMEM((1,), jnp.int32), pltpu.SemaphoreType.DMA((2,))],
    )
    return pl.pallas_call(
        functools.partial(_expert_kernel, layer=l),
        grid_spec=grid_spec,
        out_shape=jax.ShapeDtypeStruct((n_rows, *ROW_TILE), F32),
        compiler_params=_cparams(1),
        name="experts",
    )(block_e, next_e, n_used, xs, w_gu, w_down)


def _combine_kernel(tab_ref, tab_next_ref, xm_ref, slab_ref, mod_ref, fnw_ref, ys_ref, o_ref, buf, sem, *, final):
    i = pl.program_id(0) * pl.num_programs(1) + pl.program_id(1)
    n = pl.num_programs(0) * pl.num_programs(1)
    slot = lax.rem(i, 2)

    @pl.when(i == 0)
    def _():
        _run_copies(tab_ref, buf.at[0], ys_ref, sem.at[0], to_sorted=False, wait=False)

    @pl.when(i + 1 < n)
    def _():
        _run_copies(tab_next_ref, buf.at[1 - slot], ys_ref, sem.at[1 - slot], to_sorted=False, wait=False)

    _wait_tile(buf.at[slot], ys_ref, sem.at[slot], to_sorted=False)
    slab = slab_ref[0]
    p1, p2 = _local_positions(slab)
    yb = buf[slot].reshape(2 * TM, D).astype(BF16)
    y1 = _dot(jnp.where(p1, 1.0, 0.0).astype(BF16), yb)
    y2 = _dot(jnp.where(p2, 1.0, 0.0).astype(BF16), yb)
    f = slab[:, 4:5] * y1 + slab[:, 5:6] * y2
    x = xm_ref[0] + mod_ref[0, 0][5:6] * f
    o_ref[0] = _rms(x, fnw_ref[...]) if final else x


def _combine(tab, xmid, slab, modall, fnw, ys, final):
    bsz, t_all, _ = xmid.shape
    nt = t_all // TM
    out_t = t_all - TM if final else t_all
    out_map = (lambda b, j: (b, jnp.maximum(j - 1, 0), 0)) if final else (lambda b, j: (b, j, 0))
    return pl.pallas_call(
        functools.partial(_combine_kernel, final=final),
        grid=(bsz, nt),
        in_specs=[pl.BlockSpec((1, 1, LANES), lambda b, j: (b * nt + j, 0, 0), memory_space=pltpu.SMEM),
                  pl.BlockSpec((1, 1, LANES), lambda b, j: (jnp.minimum(b * nt + j + 1, bsz * nt - 1), 0, 0),
                               memory_space=pltpu.SMEM),
                  pl.BlockSpec((1, TM, D), lambda b, j: (b, j, 0)),
                  pl.BlockSpec((1, TM, LANES), lambda b, j: (b, j, 0)),
                  pl.BlockSpec((1, 1, N_MOD, D), lambda b, j: (b, jnp.minimum(j, 1), 0, 0)),
                  pl.BlockSpec((1, D), lambda b, j: (0, 0)),
                  pl.BlockSpec(memory_space=pl.ANY)],
        out_specs=pl.BlockSpec((1, TM, D), out_map),
        out_shape=jax.ShapeDtypeStruct((bsz, out_t, D), F32),
        scratch_shapes=[pltpu.VMEM((2, 2 * TM, *ROW_TILE), F32), pltpu.SemaphoreType.DMA((2,))],
        compiler_params=_cparams(2),
        name="combine_final" if final else "combine",
    )(tab, tab, xmid, slab, modall, fnw, ys)


def _lower_bounds(hg_lb):
    p = jax.nn.softmax(hg_lb.astype(F32), axis=1)
    cs = jnp.cumsum(p, axis=1)
    return cs - cs[:, :1]


def kernel(x, c, ctx, c_ctx, norm_w, w_ada, b_ada, w_in, conv_w, conv_norm_w, hg_lb, hg_norm_w, w_out, w_rg, b_rg,
           w_re, b_re, w_e_gu, w_e_down, final_norm_w):
    bsz, seq, _ = x.shape
    n_ctx = ctx.shape[1]
    assert n_ctx == TM and seq % TM == 0
    t_all = n_ctx + seq
    n_tok = bsz * t_all
    n_assign = 2 * n_tok
    n_blocks = -(-(n_assign + NE * (MOE_P - 1)) // MOE_P)
    n_rows = n_blocks * MOE_P

    xall = jnp.concatenate([ctx, x], axis=1)
    cc = jnp.zeros((16, D), F32).at[:bsz].set(c).at[bsz].set(c_ctx)
    mod = _ada(cc, w_ada, b_ada).reshape(DEPTH, 16, N_MOD, D)
    lb = _lower_bounds(hg_lb)
    lbp = jnp.stack([jnp.log(lb[0]), jnp.log1p(-lb[0]), jnp.log(lb[1]), jnp.log1p(-lb[1])], axis=1) * LOG2E
    w_r = jnp.concatenate([w_re, w_rg, jnp.zeros((DEPTH, D, LANES - NE - N_GROUPS), F32)], axis=-1)
    b_r = jnp.concatenate([b_re, b_rg, jnp.zeros((DEPTH, LANES - NE - N_GROUPS), F32)], axis=-1)

    out = None
    for l in range(DEPTH):
        final = l == DEPTH - 1
        modall = jnp.stack([jnp.broadcast_to(mod[l, bsz], (bsz, N_MOD, D)), mod[l, :bsz]], axis=1)
        conv_in, g_in, lf, qkv = _inproj(xall, norm_w[l, 0].reshape(1, D), modall, w_in, lbp[l], l)
        o_f, o_b = _gla(lf, qkv, n_ctx // CH)
        xmid, h2, slab, runs, counts = _mixout(
            xall, conv_in, g_in, o_f, o_b, modall, conv_w[l], conv_norm_w[l].reshape(1, CW),
            jnp.tile(hg_norm_w[l], NH).reshape(1, HW), w_out, norm_w[l, 1].reshape(1, D),
            w_r[l], b_r[l].reshape(1, LANES), l)
        cnt = counts[0, :NE].astype(jnp.int32)
        padded = (cnt + MOE_P - 1) // MOE_P * MOE_P
        pad_ends = jnp.cumsum(padded)
        pad_starts = pad_ends - padded
        run_len = runs[:, 0, :NE].astype(jnp.int32)
        run_global = pad_starts[None, :] + runs[:, 1, :NE].astype(jnp.int32)
        run_local = runs[:, 2, :NE].astype(jnp.int32)
        tab = jnp.concatenate([run_len, run_local, run_global, jnp.zeros_like(run_len)], axis=1)[:, None, :]
        block_start = jnp.arange(n_blocks, dtype=jnp.int32) * MOE_P
        block_e = jnp.minimum(jnp.sum((pad_ends[None, :] <= block_start[:, None]).astype(jnp.int32), axis=1), NE - 1)
        n_used = (pad_ends[-1:] // MOE_P).astype(jnp.int32)
        zero = jnp.zeros_like(cnt)
        pad_tab = jnp.concatenate([padded - cnt, zero, pad_starts + cnt, zero + n_used]).reshape(1, 1, LANES)
        xs = _dispatch(tab, pad_tab, h2.reshape(n_tok, D), slab.reshape(n_tok, LANES), n_rows)
        ids = jnp.arange(NE, dtype=jnp.int32)
        later = jnp.min(jnp.where((ids[None, :] > block_e[:, None]) & (cnt[None, :] > 0), ids[None, :], NE), axis=1)
        next_e = jnp.where(later < NE, later, -1)
        ys = _experts(block_e, next_e, n_used, xs, w_e_gu, w_e_down, l)
        res = _combine(tab, xmid, slab, modall, final_norm_w.reshape(1, D), ys, final)
        if final:
            out = res
        else:
            xall = res
    return out
```

```python
import functools

import jax
import jax.numpy as jnp
import numpy as np
from jax import lax
from jax.experimental import pallas as pl
from jax.experimental.pallas import tpu as pltpu

F32 = jnp.float32
BF16 = jnp.bfloat16

D = 1024
DEPTH = 4
GRID_W = 64
CW = 512
HW = 512
NH = 4
HD = HW // NH
PW = 3 * CW + 5 * HW
PPW = PW - CW
N_GROUPS = 4
EPG = 8
NE = N_GROUPS * EPG
DE = 512
N_MOD = 6
EPS = 1e-6
LOG2E = 1.4426950408889634

TM = 256
CH = 128
LEAF = 16
HEAD_GROUP = 4
MOE_P = 256
EXPERT_CHUNK = 256
LANES = 128

VMEM_LIMIT = 56 * 1024 * 1024


def _cparams(n_axes, flags=None):
    return pltpu.CompilerParams(dimension_semantics=("arbitrary",) * n_axes,
                                vmem_limit_bytes=VMEM_LIMIT, flags=flags)


def _dot(a, b):
    return jnp.dot(a, b, preferred_element_type=F32)


def _dot_nt(a, b):
    return lax.dot_general(a, b, (((1,), (1,)), ((), ())), preferred_element_type=F32)


def _dot_tn(a, b):
    return lax.dot_general(a, b, (((0,), (0,)), ((), ())), preferred_element_type=F32)


def _split2(x):
    hi = x.astype(BF16)
    lo = (x - hi.astype(F32)).astype(BF16)
    return hi, lo


def _dot_hp(a, b):
    a_hi, a_lo = _split2(a)
    b_hi, b_lo = _split2(b)
    return _dot(a_hi, b_hi) + _dot(a_hi, b_lo) + _dot(a_lo, b_hi)


def _sigmoid(x):
    return 1.0 / (1.0 + jnp.exp(-x))


def _rms(x, w):
    return x * lax.rsqrt(jnp.mean(x * x, axis=-1, keepdims=True) + EPS) * w


def _ada_kernel(cc_ref, w_ref, b_ref, o_ref):
    s = cc_ref[...]
    s = s * _sigmoid(s)
    o_ref[0] = _dot_hp(s, w_ref[0]) + b_ref[0]


def _ada(cc, w_ada, b_ada):
    tn = 1536
    n = N_MOD * D
    return pl.pallas_call(
        _ada_kernel,
        grid=(DEPTH, n // tn),
        in_specs=[pl.BlockSpec((16, D), lambda l, j: (0, 0)),
                  pl.BlockSpec((1, D, tn), lambda l, j: (l, 0, j)),
                  pl.BlockSpec((1, 1, tn), lambda l, j: (l, 0, j))],
        out_specs=pl.BlockSpec((1, 16, tn), lambda l, j: (l, 0, j)),
        out_shape=jax.ShapeDtypeStruct((DEPTH, 16, n), F32),
        compiler_params=_cparams(2),
        name="ada",
    )(cc, w_ada, b_ada.reshape(DEPTH, 1, n))


def _silu(x):
    h = 0.5 * x
    return h + h * jnp.tanh(h)


def _neg_abs(x):
    bits = lax.bitcast_convert_type(x, jnp.uint32) | jnp.uint32(0x80000000)
    return lax.bitcast_convert_type(bits, F32)


def _forget_gate(z2, log2lb, log2_1mlb):
    ls = jnp.minimum(z2, 0.0) - jnp.log2(1.0 + jnp.exp2(_neg_abs(z2)))
    c = log2_1mlb + ls
    lf2 = jnp.maximum(log2lb, c) + jnp.log2(1.0 + jnp.exp2(_neg_abs(log2lb - c)))
    return lf2, jnp.exp2(c - z2)


def _inproj_kernel(x_ref, nw_ref, mod_ref, w_ref, lb_ref, conv_ref, g_ref, lf_ref, qkv_ref, w_b):
    @pl.when(jnp.logical_and(pl.program_id(0) == 0, pl.program_id(1) == 0))
    def _():
        def cast_rows(r, carry):
            rows = pl.ds(pl.multiple_of(r * 128, 128), 128)
            lo, hi = 3 * CW + HW, 3 * CW + 3 * HW
            w_b[rows, :lo] = w_ref[0, rows, :lo].astype(BF16)
            w_b[rows, lo:hi] = (w_ref[0, rows, lo:hi] * LOG2E).astype(BF16)
            w_b[rows, hi:] = w_ref[0, rows, hi:].astype(BF16)
            return carry

        lax.fori_loop(0, D // 128, cast_rows, 0)

    m = mod_ref[0, 0]
    h = _rms(x_ref[0], nw_ref[...] * (1.0 + m[1:2])) + m[0:1]
    hb = h.astype(BF16)
    blk = {k: _dot(hb, w_b[:, k * 512:(k + 1) * 512]) for k in (4, 5, 3, 1, 2, 0, 6, 7)}
    for d in range(2):
        lf2, kk = _forget_gate(blk[4 + d], lb_ref[2 * d:2 * d + 1], lb_ref[2 * d + 1:2 * d + 2])
        lf_ref[0, :, d * HW:(d + 1) * HW] = lf2
        qkv_ref[0, :, (1 + d) * HW:(2 + d) * HW] = kk.astype(BF16)
    qkv_ref[0, :, 0:HW] = _silu(blk[3]).astype(BF16)
    conv_ref[0, :, CW:2 * CW] = blk[1] * blk[2]
    conv_ref[0, :, 0:CW] = blk[0]
    qkv_ref[0, :, 3 * HW:4 * HW] = blk[6].astype(BF16)
    g_ref[0] = blk[7]


def _inproj(xall, nw, modall, w_in, lbp, l):
    bsz, t_all, _ = xall.shape
    tile = lambda w: pl.BlockSpec((1, TM, w), lambda b, j: (b, j, 0))
    return pl.pallas_call(
        _inproj_kernel,
        grid=(bsz, t_all // TM),
        in_specs=[pl.BlockSpec((1, TM, D), lambda b, j: (b, j, 0)),
                  pl.BlockSpec((1, D), lambda b, j: (0, 0)),
                  pl.BlockSpec((1, 1, N_MOD, D), lambda b, j: (b, jnp.minimum(j, 1), 0, 0)),
                  pl.BlockSpec((1, D, PW), lambda b, j: (l, 0, 0), pipeline_mode=pl.Buffered(1)),
                  pl.BlockSpec((4, HW), lambda b, j: (0, 0))],
        out_specs=[tile(2 * CW), tile(HW), tile(2 * HW), tile(4 * HW)],
        out_shape=[jax.ShapeDtypeStruct((bsz, t_all, 2 * CW), F32),
                   jax.ShapeDtypeStruct((bsz, t_all, HW), F32),
                   jax.ShapeDtypeStruct((bsz, t_all, 2 * HW), F32),
                   jax.ShapeDtypeStruct((bsz, t_all, 4 * HW), BF16)],
        scratch_shapes=[pltpu.VMEM((D, PW), BF16)],
        compiler_params=_cparams(2),
        name="inproj",
    )(xall, nw, modall, w_in, lbp)


def _gla_tables():
    t = np.arange(CH)[:, None]
    s = np.arange(CH)[None, :]
    x = t ^ s
    lvl = np.full((CH, CH), -1, np.int32)
    n, k = CH, 0
    while n > LEAF:
        lvl[(x < n) & (x >= n // 2)] = k
        n, k = n // 2, k + 1
    lvl[x < LEAF] = k
    fwd = np.where(s <= t, lvl, -1)
    return (np.stack([s <= t, s >= t]).astype(np.float32), np.stack([fwd, fwd.T]).astype(np.int32))


def _gla_factors(lf2, qh, kh, tri, reverse):
    hi = lf2.astype(BF16)
    lo = (lf2 - hi.astype(F32)).astype(BF16)
    b = _dot(tri, hi) + _dot(tri, lo)

    def block_ref(n, idx):
        r = b.reshape(CH // n, n, HW)[:, idx:idx + 1, :]
        return jnp.broadcast_to(r, (CH // n, n, HW)).reshape(CH, HW)

    pow2 = lambda e: jnp.exp2(e.astype(BF16))
    pieces = []
    n = CH
    while n > LEAF:
        h = n // 2
        a = pow2(_neg_abs(b - block_ref(n, h if reverse else h - 1)))
        pieces.append((qh * a, kh * a))
        n = h
    dl = b - block_ref(LEAF, LEAF // 2)
    pieces.append((qh * pow2(dl), kh * pow2(-dl)))

    edge = 0 if reverse else CH - 1
    b_edge = b[edge:edge + 1, :]
    q_in = qh * pow2(b)
    k_out = kh * pow2(b_edge - b)
    d_chunk = jnp.exp2(b_edge)
    return pieces, q_in, k_out, d_chunk


def _gla_heads(factors, v_refs, lvls, s_refs, o_refs):
    dirs = range(len(factors))
    owned = [[lvls[d] == k for k in range(len(factors[d][0]))] for d in dirs]
    for h0 in range(0, NH, HEAD_GROUP):
        heads = range(h0, h0 + HEAD_GROUP)
        sl = lambda hd: slice(hd * HD, (hd + 1) * HD)
        dots = {(d, hd): [_dot_nt(qa[:, sl(hd)], ka[:, sl(hd)]) for qa, ka in factors[d][0]]
                for hd in heads for d in dirs}
        scs = {}
        for hd in heads:
            for d in dirs:
                sc = jnp.zeros((CH, CH), F32)
                for k, dk in enumerate(dots[d, hd]):
                    sc = jnp.where(owned[d][k], dk, sc)
                scs[d, hd] = sc.astype(BF16)
        for hd in heads:
            for d in dirs:
                _, q_in, k_out, d_chunk = factors[d]
                vb = v_refs[d][0, :, sl(hd)]
                st = s_refs[d][hd]
                o_refs[d][0, :, sl(hd)] = _dot(scs[d, hd], vb) + _dot_nt(q_in[:, sl(hd)], st.astype(BF16))
                s_refs[d][hd] = st * d_chunk[:, sl(hd)] + _dot_tn(vb, k_out[:, sl(hd)])


def _gla_kernel(lff_ref, qf_ref, kf_ref, vf_ref, lfb_ref, qb_ref, kb_ref, vb_ref, tri_ref, lvl_ref, of_ref, ob_ref,
                sf_ref, sb_ref):
    @pl.when(pl.program_id(1) == 0)
    def _():
        sf_ref[...] = jnp.zeros_like(sf_ref)
        sb_ref[...] = jnp.zeros_like(sb_ref)

    factors = (_gla_factors(lff_ref[0], qf_ref[0], kf_ref[0], tri_ref[0], False),
               _gla_factors(lfb_ref[0], qb_ref[0], kb_ref[0], tri_ref[1], True))
    _gla_heads(factors, (vf_ref, vb_ref), (lvl_ref[0], lvl_ref[1]), (sf_ref, sb_ref), (of_ref, ob_ref))


def _gla(lf, qkv, n_ctx_chunks):
    bsz, t_all, _ = lf.shape
    nc = t_all // CH

    def cb(j):
        return jnp.where(j < n_ctx_chunks, n_ctx_chunks - 1 - j, nc - 1 - (j - n_ctx_chunks))

    blk = (1, CH, HW)
    fwd = lambda c: pl.BlockSpec(blk, lambda b, j: (b, j, c))
    bwd = lambda c: pl.BlockSpec(blk, lambda b, j: (b, cb(j), c))
    tri, lvl = _gla_tables()
    return pl.pallas_call(
        _gla_kernel,
        grid=(bsz, nc),
        in_specs=[fwd(0), fwd(0), fwd(1), fwd(3),
                  bwd(1), bwd(0), bwd(2), bwd(3),
                  pl.BlockSpec((2, CH, CH), lambda b, j: (0, 0, 0)),
                  pl.BlockSpec((2, CH, CH), lambda b, j: (0, 0, 0))],
        out_specs=[fwd(0), bwd(0)],
        out_shape=[jax.ShapeDtypeStruct((bsz, t_all, HW), F32)] * 2,
        scratch_shapes=[pltpu.VMEM((NH, HD, HD), F32), pltpu.VMEM((NH, HD, HD), F32)],
        compiler_params=_cparams(2),
        name="gla",
    )(lf, qkv, qkv, qkv, lf, qkv, qkv, qkv, jnp.asarray(tri, BF16), jnp.asarray(lvl))


def _shift_rows(u, k):
    return pltpu.roll(u, k % u.shape[0], axis=0)


def _mixout_kernel(x_ref, bg_ref, u_ref, g_ref, up_ref, un_ref, of_ref, ob_ref, mod_ref, cw_ref, cnw_ref,
                   hnw_ref, wo_ref, nw2_ref, wr_ref, br_ref, xm_ref, h2_ref, slab_ref, runs_ref, cnt_ref, wo_b, wr_b,
                   *, j0):
    j = pl.program_id(1) + j0
    nt = pl.num_programs(1) + j0
    first = jnp.logical_and(pl.program_id(0) == 0, pl.program_id(1) == 0)

    @pl.when(first)
    def _():
        cnt_ref[...] = jnp.zeros_like(cnt_ref)
        wo_b[...] = wo_ref[0].astype(BF16)
        wr_b[0], wr_b[1] = _split2(wr_ref[...])

    is_ctx = j == 0
    m = mod_ref[0, 0]
    u = u_ref[0]
    cw = cw_ref[...]
    t = lax.broadcasted_iota(jnp.int32, (TM, 1), 0)
    col_in_row = t & (GRID_W - 1)
    keep_l = jnp.where(is_ctx, jnp.where(t == 0, 0.0, 1.0), jnp.where(col_in_row == 0, 0.0, 1.0))
    keep_r = jnp.where(is_ctx, jnp.where(t == TM - 1, 0.0, 1.0), jnp.where(col_in_row == GRID_W - 1, 0.0, 1.0))
    hc = CW // 2

    def seq_taps(a, w):
        left = jnp.where(keep_l > 0.5, _shift_rows(a, 1), 0.0)
        right = jnp.where(keep_r > 0.5, _shift_rows(a, -1), 0.0)
        return w[0:1] * left + w[1:2] * a + w[2:3] * right

    def col_taps(a, w):
        up = jnp.concatenate([jnp.where(j == 1, 0.0, up_ref[0]), a[:TM - GRID_W]], axis=0)
        dn = jnp.concatenate([a[GRID_W:], jnp.where(j == nt - 1, 0.0, un_ref[0])], axis=0)
        return w[0:1] * up + w[1:2] * a + w[2:3] * dn

    y_second = lax.cond(is_ctx, seq_taps, col_taps, u[:, hc:], cw[:, hc:])
    conv = jnp.concatenate([seq_taps(u[:, :hc], cw[:, :hc]), y_second], axis=-1)
    y_conv = _rms(bg_ref[0] * conv, cnw_ref[...])

    o = of_ref[0] + ob_ref[0]
    g = g_ref[0]
    hnw = hnw_ref[...]
    recs = []
    for hd in range(NH):
        hs = slice(hd * HD, (hd + 1) * HD)
        recs.append(_rms(o[:, hs], hnw[:, hs]))
    y_rec = jnp.concatenate(recs, axis=-1) * (g * _sigmoid(g))

    y = jnp.concatenate([y_conv, y_rec], axis=-1).astype(BF16)
    xm = x_ref[0] + m[2:3] * _dot(y, wo_b[...])
    xm_ref[0] = xm
    h2 = _rms(xm, nw2_ref[...] * (1.0 + m[4:5])) + m[3:4]
    h2_ref[0] = h2.astype(BF16)

    h_hi, h_lo = _split2(h2)
    lt = (_dot(h_hi, wr_b[0]) + _dot(h_hi, wr_b[1]) + _dot(h_lo, wr_b[0]) + br_ref[...]).T
    ninf = -jnp.inf
    big = 1e9
    over = lambda fn, a: fn(a, axis=0, keepdims=True)
    grow = lax.broadcasted_iota(jnp.int32, (8, TM), 0).astype(F32)
    gl = jnp.where(grow < N_GROUPS, lt[NE:NE + 8], ninf)
    gmax = over(jnp.max, gl)
    gsel = over(jnp.min, jnp.where(gl == gmax, grow, big))
    pg = 1.0 / over(jnp.sum, jnp.exp(gl - gmax))
    erow_i = lax.broadcasted_iota(jnp.int32, (NE, TM), 0)
    erow = erow_i.astype(F32)
    el = jnp.where((erow_i // EPG).astype(F32) == gsel, lt[:NE], ninf)
    e1 = over(jnp.max, el)
    i1 = over(jnp.min, jnp.where(el == e1, erow, big))
    el2 = jnp.where(erow == i1, ninf, el)
    e2 = over(jnp.max, el2)
    i2 = over(jnp.min, jnp.where(el2 == e2, erow, big))
    r = jnp.exp(e2 - e1)
    g1 = pg / (1.0 + r)
    g2 = pg * r / (1.0 + r)
    oh1 = jnp.where(erow == i1, 1.0, 0.0)
    oh2 = jnp.where(erow == i2, 1.0, 0.0)
    cnt = oh1 + oh2
    cnt_b = cnt.astype(BF16)
    ts = lax.broadcasted_iota(jnp.int32, (TM, TM), 0)
    tt = lax.broadcasted_iota(jnp.int32, (TM, TM), 1)
    prior = _dot(cnt_b, jnp.where(ts < tt, 1.0, 0.0).astype(BF16))
    c_col = jnp.sum(cnt, axis=1, keepdims=True)
    ei = lax.broadcasted_iota(jnp.int32, (NE, NE), 0)
    ej = lax.broadcasted_iota(jnp.int32, (NE, NE), 1)
    cc_hi, cc_lo = _split2(jnp.broadcast_to(c_col, (NE, LANES)))
    fewer = jnp.where(ej < ei, 1.0, 0.0).astype(BF16)
    off_col = (_dot(fewer, cc_hi) + _dot(fewer, cc_lo))[:, 0:1]
    pos = prior + off_col
    lpos1 = over(jnp.sum, oh1 * pos)
    lpos2 = over(jnp.sum, oh2 * pos)
    cnt_pad = jnp.concatenate([cnt_b, jnp.zeros((LANES - NE, TM), BF16)], axis=0)
    c_tile = _dot_nt(jnp.ones((8, TM), BF16), cnt_pad)[0:1]
    li = lax.broadcasted_iota(jnp.int32, (LANES, LANES), 0)
    lj = lax.broadcasted_iota(jnp.int32, (LANES, LANES), 1)
    lower_e = jnp.where(li < lj, 1.0, 0.0).astype(BF16)
    c_hi, c_lo = _split2(jnp.broadcast_to(c_tile, (8, LANES)))
    off = (_dot(c_hi, lower_e) + _dot(c_lo, lower_e))[0:1]
    sub = lax.broadcasted_iota(jnp.int32, (8, LANES), 0)
    runs_ref[0] = jnp.where(sub == 0, c_tile, jnp.where(sub == 1, cnt_ref[...], jnp.where(sub == 2, off, 0.0)))
    cnt_ref[...] += c_tile
    frow = lax.broadcasted_iota(jnp.int32, (LANES, TM), 0)
    fields = jnp.zeros((LANES, TM), F32)
    for k, val in enumerate((i1, i2, lpos1, lpos2, g1, g2)):
        fields = jnp.where(frow == k, val, fields)
    slab_ref[0] = fields.T


def _mixout(xall, conv_in, g_in, o_f, o_b, modall, cw, cnw, hnw, w_out, nw2, w_r, b_r, l, j0):
    bsz, t_all, _ = xall.shape
    nt = t_all // TM - j0
    t_out = nt * TM
    n64 = t_all // GRID_W
    per = TM // GRID_W
    tile = lambda w, c: pl.BlockSpec((1, TM, w), lambda b, j, c=c: (b, j + j0, c))
    out_tile = lambda w: pl.BlockSpec((1, TM, w), lambda b, j: (b, j, 0))
    full = lambda shape: pl.BlockSpec(shape, lambda b, j: (0,) * len(shape))
    layer = lambda shape: pl.BlockSpec((1, *shape), lambda b, j: (l,) + (0,) * len(shape), pipeline_mode=pl.Buffered(1))
    return pl.pallas_call(
        functools.partial(_mixout_kernel, j0=j0),
        grid=(bsz, nt),
        in_specs=[tile(D, 0),
                  tile(CW, 0), tile(CW, 1), tile(HW, 0),
                  pl.BlockSpec((1, GRID_W, CW // 2), lambda b, j: (b, jnp.maximum((j + j0) * per - 1, 0), 3)),
                  pl.BlockSpec((1, GRID_W, CW // 2), lambda b, j: (b, jnp.minimum((j + j0) * per + per, n64 - 1), 3)),
                  tile(HW, 0), tile(HW, 0),
                  pl.BlockSpec((1, 1, N_MOD, D), lambda b, j: (b, jnp.minimum(j + j0, 1), 0, 0)),
                  full((3, CW)), full((1, CW)), full((1, HW)), layer((D, D)), full((1, D)),
                  full((D, LANES)), full((1, LANES))],
        out_specs=[out_tile(D), out_tile(D), out_tile(LANES),
                   pl.BlockSpec((1, 8, LANES), lambda b, j: (b * nt + j, 0, 0)),
                   pl.BlockSpec((1, LANES), lambda b, j: (0, 0))],
        out_shape=[jax.ShapeDtypeStruct((bsz, t_out, D), F32),
                   jax.ShapeDtypeStruct((bsz, t_out, D), BF16),
                   jax.ShapeDtypeStruct((bsz, t_out, LANES), F32),
                   jax.ShapeDtypeStruct((bsz * nt, 8, LANES), F32),
                   jax.ShapeDtypeStruct((1, LANES), F32)],
        scratch_shapes=[pltpu.VMEM((D, D), BF16), pltpu.VMEM((2, D, LANES), BF16)],
        compiler_params=_cparams(2),
        name="mixout",
    )(xall, conv_in, conv_in, g_in, conv_in, conv_in, o_f, o_b, modall, cw, cnw, hnw, w_out, nw2, w_r, b_r)


RUN_BITS = TM.bit_length()
LONG_RUN_BIT = 5
ROW_TILE = (8, LANES)


def _run_copies(tab_ref, local_ref, sorted_ref, sem, *, to_sorted, wait):
    def per_expert(e, carry):
        ln = tab_ref[0, 0, e]
        lo = tab_ref[0, 0, NE + e]
        go = tab_ref[0, 0, 2 * NE + e]
        def copy_bits(bits):
            for bit in bits:
                size = 1 << bit
                done = ln & ~(2 * size - 1)

                @pl.when((ln & size) != 0)
                def _():
                    a = local_ref.at[pl.ds(lo + done, size)]
                    b = sorted_ref.at[pl.ds(go + done, size)]
                    cp = pltpu.make_async_copy(a, b, sem) if to_sorted else pltpu.make_async_copy(b, a, sem)
                    if wait:
                        cp.wait()
                    else:
                        cp.start(priority=bit % 2)

        @pl.when(ln >= (1 << LONG_RUN_BIT))
        def _():
            copy_bits(range(RUN_BITS - 1, LONG_RUN_BIT - 1, -1))

        copy_bits(range(LONG_RUN_BIT - 1, -1, -1))
        return carry

    lax.fori_loop(0, NE, per_expert, 0)


def _wait_tile(local_ref, sorted_ref, sem, *, to_sorted):
    whole = sorted_ref.at[pl.ds(0, 2 * TM)]
    (pltpu.make_async_copy(local_ref, whole, sem) if to_sorted else pltpu.make_async_copy(whole, local_ref, sem)).wait()


def _local_positions(slab):
    col = lax.broadcasted_iota(jnp.int32, (TM, 2 * TM), 1).astype(F32)
    return col == slab[:, 2:3], col == slab[:, 3:4]


def _dispatch_kernel(tab_ref, ptab_ref, h2_ref, slab_ref, xs_ref, buf, zbuf, sem):
    @pl.when(pl.program_id(0) == 0)
    def _():
        zbuf[...] = jnp.zeros_like(zbuf)
        _run_copies(ptab_ref, zbuf, xs_ref, sem, to_sorted=True, wait=False)
        _run_copies(ptab_ref, zbuf, xs_ref, sem, to_sorted=True, wait=True)

        def zero_block(i, carry):
            cp = pltpu.make_async_copy(zbuf, xs_ref.at[pl.ds(i * MOE_P, MOE_P)], sem)
            cp.start()
            cp.wait()
            return carry

        lax.fori_loop(ptab_ref[0, 0, 3 * NE], xs_ref.shape[0] // MOE_P, zero_block, 0)

    i = pl.program_id(0)
    slot = lax.rem(i, 2)
    p1, p2 = _local_positions(slab_ref[...])
    perm_t = jnp.where(p1 | p2, 1.0, 0.0).astype(BF16)
    buf[slot] = _dot_tn(perm_t, h2_ref[...]).reshape(2 * TM, *ROW_TILE)

    @pl.when(i > 0)
    def _():
        _wait_tile(buf.at[1 - slot], xs_ref, sem, to_sorted=True)

    _run_copies(tab_ref, buf.at[slot], xs_ref, sem, to_sorted=True, wait=False)

    @pl.when(i == pl.num_programs(0) - 1)
    def _():
        _wait_tile(buf.at[slot], xs_ref, sem, to_sorted=True)


def _dispatch(tab, pad_tab, h2, slab, n_rows):
    n_tiles = tab.shape[0]
    return pl.pallas_call(
        _dispatch_kernel,
        grid=(n_tiles,),
        in_specs=[pl.BlockSpec((1, 1, LANES), lambda i: (i, 0, 0), memory_space=pltpu.SMEM),
                  pl.BlockSpec((1, 1, LANES), lambda i: (0, 0, 0), memory_space=pltpu.SMEM),
                  pl.BlockSpec((TM, D), lambda i: (i, 0)),
                  pl.BlockSpec((TM, LANES), lambda i: (i, 0))],
        out_specs=pl.BlockSpec(memory_space=pl.ANY),
        out_shape=jax.ShapeDtypeStruct((n_rows, *ROW_TILE), F32),
        scratch_shapes=[pltpu.VMEM((2, 2 * TM, *ROW_TILE), F32), pltpu.VMEM((MOE_P, *ROW_TILE), F32),
                        pltpu.SemaphoreType.DMA(())],
        compiler_params=_cparams(1),
        name="dispatch",
    )(tab, pad_tab, h2, slab)


def _expert_kernel(be_ref, nx_ref, nu_ref, xs_ref, wgu_hbm, wd_hbm, ys_ref, wgu_f, wd_f, wgu_b, wd_b, slot_ref, sem,
                   *, layer):
    i = pl.program_id(0)
    used = i < nu_ref[0]
    e = be_ref[i]

    def weight_copies(expert, slot):
        return (pltpu.make_async_copy(wgu_hbm.at[layer, expert], wgu_f.at[slot], sem.at[slot]),
                pltpu.make_async_copy(wd_hbm.at[layer, expert], wd_f.at[slot], sem.at[slot]))

    @pl.when(i == 0)
    def _():
        slot_ref[0] = 0
        for cp in weight_copies(e, 0):
            cp.start()

    @pl.when(jnp.logical_and(used, jnp.logical_or(i == 0, e != be_ref[jnp.maximum(i - 1, 0)])))
    def _():
        slot = jnp.where(i == 0, 0, 1 - slot_ref[0])
        slot_ref[0] = slot
        for cp in weight_copies(e, slot):
            cp.wait()
        wgu_b[...] = wgu_f[slot].astype(BF16)
        wd_b[...] = wd_f[slot].astype(BF16)

        @pl.when(nx_ref[i] >= 0)
        def _():
            for cp in weight_copies(nx_ref[i], 1 - slot):
                cp.start()

    @pl.when(used)
    def _():
        x = xs_ref[...].reshape(MOE_P, D).astype(BF16)
        acts = []
        for c in range(0, DE, EXPERT_CHUNK):
            a = _dot(x, wgu_b[:, c:c + EXPERT_CHUNK])
            u = _dot(x, wgu_b[:, DE + c:DE + c + EXPERT_CHUNK])
            acts.append((_silu(a) * u).astype(BF16))
        y = _dot(jnp.concatenate(acts, axis=1), wd_b[...])
        ys_ref[...] = y.reshape(MOE_P, *ROW_TILE)

    @pl.when(jnp.logical_not(used))
    def _():
        ys_ref[...] = jnp.zeros_like(ys_ref)


def _experts(block_e, next_e, n_used, xs, w_gu, w_down, l):
    n_rows = xs.shape[0]
    grid_spec = pltpu.PrefetchScalarGridSpec(
        num_scalar_prefetch=3,
        grid=(n_rows // MOE_P,),
        in_specs=[pl.BlockSpec((MOE_P, *ROW_TILE), lambda i, be, nx, nu: (jnp.minimum(i, nu[0] - 1), 0, 0)),
                  pl.BlockSpec(memory_space=pl.ANY),
                  pl.BlockSpec(memory_space=pl.ANY)],
        out_specs=pl.BlockSpec((MOE_P, *ROW_TILE), lambda i, be, nx, nu: (i, 0, 0)),
        scratch_shapes=[pltpu.VMEM((2, D, 2 * DE), F32), pltpu.VMEM((2, DE, D), F32),
                        pltpu.VMEM((D, 2 * DE), BF16), pltpu.VMEM((DE, D), BF16),
                        pltpu.SMEM((1,), jnp.int32), pltpu.SemaphoreType.DMA((2,))],
    )
    return pl.pallas_call(
        functools.partial(_expert_kernel, layer=l),
        grid_spec=grid_spec,
        out_shape=jax.ShapeDtypeStruct((n_rows, *ROW_TILE), F32),
        compiler_params=_cparams(1),
        name="experts",
    )(block_e, next_e, n_used, xs, w_gu, w_down)


def _combine_kernel(tab_ref, tab_next_ref, xm_ref, slab_ref, mod_ref, fnw_ref, ys_ref, o_ref, buf, sem, *, final):
    i = pl.program_id(0) * pl.num_programs(1) + pl.program_id(1)
    n = pl.num_programs(0) * pl.num_programs(1)
    slot = lax.rem(i, 2)

    @pl.when(i == 0)
    def _():
        _run_copies(tab_ref, buf.at[0], ys_ref, sem.at[0], to_sorted=False, wait=False)

    @pl.when(i + 1 < n)
    def _():
        _run_copies(tab_next_ref, buf.at[1 - slot], ys_ref, sem.at[1 - slot], to_sorted=False, wait=False)

    _wait_tile(buf.at[slot], ys_ref, sem.at[slot], to_sorted=False)
    slab = slab_ref[0]
    p1, p2 = _local_positions(slab)
    yb = buf[slot].reshape(2 * TM, D).astype(BF16)
    y1 = _dot(jnp.where(p1, 1.0, 0.0).astype(BF16), yb)
    y2 = _dot(jnp.where(p2, 1.0, 0.0).astype(BF16), yb)
    f = slab[:, 4:5] * y1 + slab[:, 5:6] * y2
    x = xm_ref[0] + mod_ref[0, 0][5:6] * f
    o_ref[0] = _rms(x, fnw_ref[...]) if final else x


def _combine(tab, xmid, slab, modall, fnw, ys, final, j0):
    bsz, t_out, _ = xmid.shape
    nt = t_out // TM
    return pl.pallas_call(
        functools.partial(_combine_kernel, final=final),
        grid=(bsz, nt),
        in_specs=[pl.BlockSpec((1, 1, LANES), lambda b, j: (b * nt + j, 0, 0), memory_space=pltpu.SMEM),
                  pl.BlockSpec((1, 1, LANES), lambda b, j: (jnp.minimum(b * nt + j + 1, bsz * nt - 1), 0, 0),
                               memory_space=pltpu.SMEM),
                  pl.BlockSpec((1, TM, D), lambda b, j: (b, j, 0)),
                  pl.BlockSpec((1, TM, LANES), lambda b, j: (b, j, 0)),
                  pl.BlockSpec((1, 1, N_MOD, D), lambda b, j: (b, jnp.minimum(j + j0, 1), 0, 0)),
                  pl.BlockSpec((1, D), lambda b, j: (0, 0)),
                  pl.BlockSpec(memory_space=pl.ANY)],
        out_specs=pl.BlockSpec((1, TM, D), lambda b, j: (b, j, 0)),
        out_shape=jax.ShapeDtypeStruct((bsz, t_out, D), F32),
        scratch_shapes=[pltpu.VMEM((2, 2 * TM, *ROW_TILE), F32), pltpu.SemaphoreType.DMA((2,))],
        compiler_params=_cparams(2),
        name="combine_final" if final else "combine",
    )(tab, tab, xmid, slab, modall, fnw, ys)


def _lower_bounds(hg_lb):
    p = jax.nn.softmax(hg_lb.astype(F32), axis=1)
    cs = jnp.cumsum(p, axis=1)
    return cs - cs[:, :1]


def kernel(x, c, ctx, c_ctx, norm_w, w_ada, b_ada, w_in, conv_w, conv_norm_w, hg_lb, hg_norm_w, w_out, w_rg, b_rg,
           w_re, b_re, w_e_gu, w_e_down, final_norm_w):
    bsz, seq, _ = x.shape
    n_ctx = ctx.shape[1]
    assert n_ctx == TM and seq % TM == 0

    xall = jnp.concatenate([ctx, x], axis=1)
    cc = jnp.zeros((16, D), F32).at[:bsz].set(c).at[bsz].set(c_ctx)
    mod = _ada(cc, w_ada, b_ada).reshape(DEPTH, 16, N_MOD, D)
    lb = _lower_bounds(hg_lb)
    lbp = jnp.stack([jnp.log(lb[0]), jnp.log1p(-lb[0]), jnp.log(lb[1]), jnp.log1p(-lb[1])], axis=1) * LOG2E
    w_r = jnp.concatenate([w_re, w_rg, jnp.zeros((DEPTH, D, LANES - NE - N_GROUPS), F32)], axis=-1)
    b_r = jnp.concatenate([b_re, b_rg, jnp.zeros((DEPTH, LANES - NE - N_GROUPS), F32)], axis=-1)

    out = None
    for l in range(DEPTH):
        final = l == DEPTH - 1
        j0 = 1 if final else 0
        modall = jnp.stack([jnp.broadcast_to(mod[l, bsz], (bsz, N_MOD, D)), mod[l, :bsz]], axis=1)
        conv_in, g_in, lf, qkv = _inproj(xall, norm_w[l, 0].reshape(1, D), modall, w_in, lbp[l], l)
        o_f, o_b = _gla(lf, qkv, n_ctx // CH)
        xmid, h2, slab, runs, counts = _mixout(
            xall, conv_in, g_in, o_f, o_b, modall, conv_w[l], conv_norm_w[l].reshape(1, CW),
            jnp.tile(hg_norm_w[l], NH).reshape(1, HW), w_out, norm_w[l, 1].reshape(1, D),
            w_r[l], b_r[l].reshape(1, LANES), l, j0)
        n_tok = xmid.shape[0] * xmid.shape[1]
        n_blocks = -(-(2 * n_tok + NE * (MOE_P - 1)) // MOE_P)
        n_rows = n_blocks * MOE_P
        cnt = counts[0, :NE].astype(jnp.int32)
        padded = (cnt + MOE_P - 1) // MOE_P * MOE_P
        pad_ends = jnp.cumsum(padded)
        pad_starts = pad_ends - padded
        run_len = runs[:, 0, :NE].astype(jnp.int32)
        run_global = pad_starts[None, :] + runs[:, 1, :NE].astype(jnp.int32)
        run_local = runs[:, 2, :NE].astype(jnp.int32)
        tab = jnp.concatenate([run_len, run_local, run_global, jnp.zeros_like(run_len)], axis=1)[:, None, :]
        block_start = jnp.arange(n_blocks, dtype=jnp.int32) * MOE_P
        block_e = jnp.minimum(jnp.sum((pad_ends[None, :] <= block_start[:, None]).astype(jnp.int32), axis=1), NE - 1)
        n_used = (pad_ends[-1:] // MOE_P).astype(jnp.int32)
        zero = jnp.zeros_like(cnt)
        pad_tab = jnp.concatenate([padded - cnt, zero, pad_starts + cnt, zero + n_used]).reshape(1, 1, LANES)
        xs = _dispatch(tab, pad_tab, h2.reshape(n_tok, D), slab.reshape(n_tok, LANES), n_rows)
        ids = jnp.arange(NE, dtype=jnp.int32)
        later = jnp.min(jnp.where((ids[None, :] > block_e[:, None]) & (cnt[None, :] > 0), ids[None, :], NE), axis=1)
        next_e = jnp.where(later < NE, later, -1)
        ys = _experts(block_e, next_e, n_used, xs, w_e_gu, w_e_down, l)
        res = _combine(tab, xmid, slab, modall, final_norm_w.reshape(1, D), ys, final, j0)
        if final:
            out = res
        else:
            xall = res
    return out
```

```python
import functools

import jax
import jax.numpy as jnp
import numpy as np
from jax import lax
from jax.experimental import pallas as pl
from jax.experimental.pallas import tpu as pltpu

F32 = jnp.float32
BF16 = jnp.bfloat16

D = 1024
DEPTH = 4
GRID_W = 64
CW = 512
HW = 512
NH = 4
HD = HW // NH
PW = 3 * CW + 5 * HW
PPW = PW - CW
N_GROUPS = 4
EPG = 8
NE = N_GROUPS * EPG
DE = 512
N_MOD = 6
EPS = 1e-6
LOG2E = 1.4426950408889634

TM = 256
CH = 128
LEAF = 16
HEAD_GROUP = 4
GLA_SUB = 2
MOE_P = 256
EXPERT_CHUNK = 256
LANES = 128

VMEM_LIMIT = 56 * 1024 * 1024


def _cparams(n_axes, flags=None):
    return pltpu.CompilerParams(dimension_semantics=("arbitrary",) * n_axes,
                                vmem_limit_bytes=VMEM_LIMIT, flags=flags)


def _dot(a, b):
    return jnp.dot(a, b, preferred_element_type=F32)


def _dot_nt(a, b):
    return lax.dot_general(a, b, (((1,), (1,)), ((), ())), preferred_element_type=F32)


def _dot_tn(a, b):
    return lax.dot_general(a, b, (((0,), (0,)), ((), ())), preferred_element_type=F32)


def _split2(x):
    hi = x.astype(BF16)
    lo = (x - hi.astype(F32)).astype(BF16)
    return hi, lo


def _dot_hp(a, b):
    a_hi, a_lo = _split2(a)
    b_hi, b_lo = _split2(b)
    return _dot(a_hi, b_hi) + _dot(a_hi, b_lo) + _dot(a_lo, b_hi)


def _sigmoid(x):
    return 1.0 / (1.0 + jnp.exp(-x))


def _rms(x, w):
    return x * lax.rsqrt(jnp.mean(x * x, axis=-1, keepdims=True) + EPS) * w


def _ada_kernel(cc_ref, w_ref, b_ref, o_ref):
    s = cc_ref[...]
    s = s * _sigmoid(s)
    o_ref[0] = _dot_hp(s, w_ref[0]) + b_ref[0]


def _ada(cc, w_ada, b_ada):
    tn = 1536
    n = N_MOD * D
    return pl.pallas_call(
        _ada_kernel,
        grid=(DEPTH, n // tn),
        in_specs=[pl.BlockSpec((16, D), lambda l, j: (0, 0)),
                  pl.BlockSpec((1, D, tn), lambda l, j: (l, 0, j)),
                  pl.BlockSpec((1, 1, tn), lambda l, j: (l, 0, j))],
        out_specs=pl.BlockSpec((1, 16, tn), lambda l, j: (l, 0, j)),
        out_shape=jax.ShapeDtypeStruct((DEPTH, 16, n), F32),
        compiler_params=_cparams(2),
        name="ada",
    )(cc, w_ada, b_ada.reshape(DEPTH, 1, n))


def _silu(x):
    h = 0.5 * x
    return h + h * jnp.tanh(h)


def _neg_abs(x):
    bits = lax.bitcast_convert_type(x, jnp.uint32) | jnp.uint32(0x80000000)
    return lax.bitcast_convert_type(bits, F32)


def _forget_gate(z2, log2lb, log2_1mlb):
    ls = jnp.minimum(z2, 0.0) - jnp.log2(1.0 + jnp.exp2(_neg_abs(z2)))
    c = log2_1mlb + ls
    lf2 = jnp.maximum(log2lb, c) + jnp.log2(1.0 + jnp.exp2(_neg_abs(log2lb - c)))
    return lf2, jnp.exp2(c - z2)


def _inproj_kernel(x_ref, nw_ref, mod_ref, w_ref, lb_ref, conv_ref, g_ref, lf_ref, qkv_ref, w_b):
    @pl.when(jnp.logical_and(pl.program_id(0) == 0, pl.program_id(1) == 0))
    def _():
        def cast_rows(r, carry):
            rows = pl.ds(pl.multiple_of(r * 128, 128), 128)
            lo, hi = 3 * CW + HW, 3 * CW + 3 * HW
            w_b[rows, :lo] = w_ref[0, rows, :lo].astype(BF16)
            w_b[rows, lo:hi] = (w_ref[0, rows, lo:hi] * LOG2E).astype(BF16)
            w_b[rows, hi:] = w_ref[0, rows, hi:].astype(BF16)
            return carry

        lax.fori_loop(0, D // 128, cast_rows, 0)

    m = mod_ref[0, 0]
    h = _rms(x_ref[0], nw_ref[...] * (1.0 + m[1:2])) + m[0:1]
    hb = h.astype(BF16)
    blk = {k: _dot(hb, w_b[:, k * 512:(k + 1) * 512]) for k in (4, 5, 3, 1, 2, 0, 6, 7)}
    for d in range(2):
        lf2, kk = _forget_gate(blk[4 + d], lb_ref[2 * d:2 * d + 1], lb_ref[2 * d + 1:2 * d + 2])
        lf_ref[0, :, d * HW:(d + 1) * HW] = lf2
        qkv_ref[0, :, (1 + d) * HW:(2 + d) * HW] = kk.astype(BF16)
    qkv_ref[0, :, 0:HW] = _silu(blk[3]).astype(BF16)
    conv_ref[0, :, CW:2 * CW] = blk[1] * blk[2]
    conv_ref[0, :, 0:CW] = blk[0]
    qkv_ref[0, :, 3 * HW:4 * HW] = blk[6].astype(BF16)
    g_ref[0] = blk[7]


def _inproj(xall, nw, modall, w_in, lbp, l):
    bsz, t_all, _ = xall.shape
    tile = lambda w: pl.BlockSpec((1, TM, w), lambda b, j: (b, j, 0))
    return pl.pallas_call(
        _inproj_kernel,
        grid=(bsz, t_all // TM),
        in_specs=[pl.BlockSpec((1, TM, D), lambda b, j: (b, j, 0)),
                  pl.BlockSpec((1, D), lambda b, j: (0, 0)),
                  pl.BlockSpec((1, 1, N_MOD, D), lambda b, j: (b, jnp.minimum(j, 1), 0, 0)),
                  pl.BlockSpec((1, D, PW), lambda b, j: (l, 0, 0), pipeline_mode=pl.Buffered(1)),
                  pl.BlockSpec((4, HW), lambda b, j: (0, 0))],
        out_specs=[tile(2 * CW), tile(HW), tile(2 * HW), tile(4 * HW)],
        out_shape=[jax.ShapeDtypeStruct((bsz, t_all, 2 * CW), F32),
                   jax.ShapeDtypeStruct((bsz, t_all, HW), F32),
                   jax.ShapeDtypeStruct((bsz, t_all, 2 * HW), F32),
                   jax.ShapeDtypeStruct((bsz, t_all, 4 * HW), BF16)],
        scratch_shapes=[pltpu.VMEM((D, PW), BF16)],
        compiler_params=_cparams(2),
        name="inproj",
    )(xall, nw, modall, w_in, lbp)


def _gla_tables():
    t = np.arange(CH)[:, None]
    s = np.arange(CH)[None, :]
    x = t ^ s
    lvl = np.full((CH, CH), -1, np.int32)
    n, k = CH, 0
    while n > LEAF:
        lvl[(x < n) & (x >= n // 2)] = k
        n, k = n // 2, k + 1
    lvl[x < LEAF] = k
    fwd = np.where(s <= t, lvl, -1)
    return (np.stack([s <= t, s >= t]).astype(np.float32), np.stack([fwd, fwd.T]).astype(np.int32))


def _gla_factors(lf2, qh, kh, tri, reverse):
    hi = lf2.astype(BF16)
    lo = (lf2 - hi.astype(F32)).astype(BF16)
    b = _dot(tri, hi) + _dot(tri, lo)

    def block_ref(n, idx):
        r = b.reshape(CH // n, n, HW)[:, idx:idx + 1, :]
        return jnp.broadcast_to(r, (CH // n, n, HW)).reshape(CH, HW)

    pow2 = lambda e: jnp.exp2(e.astype(BF16))
    pieces = []
    n = CH
    while n > LEAF:
        h = n // 2
        a = pow2(_neg_abs(b - block_ref(n, h if reverse else h - 1)))
        pieces.append((qh * a, kh * a))
        n = h
    dl = b - block_ref(LEAF, LEAF // 2)
    pieces.append((qh * pow2(dl), kh * pow2(-dl)))

    edge = 0 if reverse else CH - 1
    b_edge = b[edge:edge + 1, :]
    q_in = qh * pow2(b)
    k_out = kh * pow2(b_edge - b)
    d_chunk = jnp.exp2(b_edge)
    return pieces, q_in, k_out, d_chunk


def _gla_heads(factors, v_refs, rows, lvls, s_refs, o_refs):
    dirs = range(len(factors))
    owned = [[lvls[d] == k for k in range(len(factors[d][0]))] for d in dirs]
    for h0 in range(0, NH, HEAD_GROUP):
        heads = range(h0, h0 + HEAD_GROUP)
        sl = lambda hd: slice(hd * HD, (hd + 1) * HD)
        dots = {(d, hd): [_dot_nt(qa[:, sl(hd)], ka[:, sl(hd)]) for qa, ka in factors[d][0]]
                for hd in heads for d in dirs}
        scs = {}
        for hd in heads:
            for d in dirs:
                sc = jnp.zeros((CH, CH), F32)
                for k, dk in enumerate(dots[d, hd]):
                    sc = jnp.where(owned[d][k], dk, sc)
                scs[d, hd] = sc.astype(BF16)
        for hd in heads:
            for d in dirs:
                _, q_in, k_out, d_chunk = factors[d]
                vb = v_refs[d][0, rows[d], sl(hd)]
                st = s_refs[d][hd]
                o_refs[d][0, rows[d], sl(hd)] = _dot(scs[d, hd], vb) + _dot_nt(q_in[:, sl(hd)], st.astype(BF16))
                s_refs[d][hd] = st * d_chunk[:, sl(hd)] + _dot_tn(vb, k_out[:, sl(hd)])


def _gla_kernel(lff_ref, qf_ref, kf_ref, vf_ref, lfb_ref, qb_ref, kb_ref, vb_ref, tri_ref, lvl_ref, of_ref, ob_ref,
                sf_ref, sb_ref):
    @pl.when(pl.program_id(1) == 0)
    def _():
        sf_ref[...] = jnp.zeros_like(sf_ref)
        sb_ref[...] = jnp.zeros_like(sb_ref)

    for c in range(GLA_SUB):
        rows = (slice(c * CH, (c + 1) * CH), slice((GLA_SUB - 1 - c) * CH, (GLA_SUB - c) * CH))
        factors = (_gla_factors(lff_ref[0, rows[0]], qf_ref[0, rows[0]], kf_ref[0, rows[0]], tri_ref[0], False),
                   _gla_factors(lfb_ref[0, rows[1]], qb_ref[0, rows[1]], kb_ref[0, rows[1]], tri_ref[1], True))
        _gla_heads(factors, (vf_ref, vb_ref), rows, (lvl_ref[0], lvl_ref[1]), (sf_ref, sb_ref), (of_ref, ob_ref))


def _gla(lf, qkv, n_ctx):
    bsz, t_all, _ = lf.shape
    span = GLA_SUB * CH
    assert n_ctx % span == 0 and t_all % span == 0
    nc = t_all // span
    n_ctx_blocks = n_ctx // span

    def cb(j):
        return jnp.where(j < n_ctx_blocks, n_ctx_blocks - 1 - j, nc - 1 - (j - n_ctx_blocks))

    blk = (1, span, HW)
    fwd = lambda c: pl.BlockSpec(blk, lambda b, j: (b, j, c))
    bwd = lambda c: pl.BlockSpec(blk, lambda b, j: (b, cb(j), c))
    tri, lvl = _gla_tables()
    return pl.pallas_call(
        _gla_kernel,
        grid=(bsz, nc),
        in_specs=[fwd(0), fwd(0), fwd(1), fwd(3),
                  bwd(1), bwd(0), bwd(2), bwd(3),
                  pl.BlockSpec((2, CH, CH), lambda b, j: (0, 0, 0)),
                  pl.BlockSpec((2, CH, CH), lambda b, j: (0, 0, 0))],
        out_specs=[fwd(0), bwd(0)],
        out_shape=[jax.ShapeDtypeStruct((bsz, t_all, HW), F32)] * 2,
        scratch_shapes=[pltpu.VMEM((NH, HD, HD), F32), pltpu.VMEM((NH, HD, HD), F32)],
        compiler_params=_cparams(2),
        name="gla",
    )(lf, qkv, qkv, qkv, lf, qkv, qkv, qkv, jnp.asarray(tri, BF16), jnp.asarray(lvl))


def _shift_rows(u, k):
    return pltpu.roll(u, k % u.shape[0], axis=0)


def _mixout_kernel(x_ref, bg_ref, u_ref, g_ref, up_ref, un_ref, of_ref, ob_ref, mod_ref, cw_ref, cnw_ref,
                   hnw_ref, wo_ref, nw2_ref, wr_ref, br_ref, xm_ref, h2_ref, slab_ref, runs_ref, cnt_ref, wo_b, wr_b,
                   *, j0):
    j = pl.program_id(1) + j0
    nt = pl.num_programs(1) + j0
    first = jnp.logical_and(pl.program_id(0) == 0, pl.program_id(1) == 0)

    @pl.when(first)
    def _():
        cnt_ref[...] = jnp.zeros_like(cnt_ref)
        wo_b[...] = wo_ref[0].astype(BF16)
        wr_b[0], wr_b[1] = _split2(wr_ref[...])

    is_ctx = j == 0
    m = mod_ref[0, 0]
    u = u_ref[0]
    cw = cw_ref[...]
    t = lax.broadcasted_iota(jnp.int32, (TM, 1), 0)
    col_in_row = t & (GRID_W - 1)
    keep_l = jnp.where(is_ctx, jnp.where(t == 0, 0.0, 1.0), jnp.where(col_in_row == 0, 0.0, 1.0))
    keep_r = jnp.where(is_ctx, jnp.where(t == TM - 1, 0.0, 1.0), jnp.where(col_in_row == GRID_W - 1, 0.0, 1.0))
    hc = CW // 2

    def seq_taps(a, w):
        left = jnp.where(keep_l > 0.5, _shift_rows(a, 1), 0.0)
        right = jnp.where(keep_r > 0.5, _shift_rows(a, -1), 0.0)
        return w[0:1] * left + w[1:2] * a + w[2:3] * right

    def col_taps(a, w):
        up = jnp.concatenate([jnp.where(j == 1, 0.0, up_ref[0]), a[:TM - GRID_W]], axis=0)
        dn = jnp.concatenate([a[GRID_W:], jnp.where(j == nt - 1, 0.0, un_ref[0])], axis=0)
        return w[0:1] * up + w[1:2] * a + w[2:3] * dn

    y_second = lax.cond(is_ctx, seq_taps, col_taps, u[:, hc:], cw[:, hc:])
    conv = jnp.concatenate([seq_taps(u[:, :hc], cw[:, :hc]), y_second], axis=-1)
    y_conv = _rms(bg_ref[0] * conv, cnw_ref[...])

    o = of_ref[0] + ob_ref[0]
    g = g_ref[0]
    hnw = hnw_ref[...]
    recs = []
    for hd in range(NH):
        hs = slice(hd * HD, (hd + 1) * HD)
        recs.append(_rms(o[:, hs], hnw[:, hs]))
    y_rec = jnp.concatenate(recs, axis=-1) * (g * _sigmoid(g))

    y = jnp.concatenate([y_conv, y_rec], axis=-1).astype(BF16)
    xm = x_ref[0] + m[2:3] * _dot(y, wo_b[...])
    xm_ref[0] = xm
    h2 = _rms(xm, nw2_ref[...] * (1.0 + m[4:5])) + m[3:4]
    h2_ref[0] = h2.astype(BF16)

    h_hi, h_lo = _split2(h2)
    lt = (_dot(h_hi, wr_b[0]) + _dot(h_hi, wr_b[1]) + _dot(h_lo, wr_b[0]) + br_ref[...]).T
    ninf = -jnp.inf
    big = 1e9
    over = lambda fn, a: fn(a, axis=0, keepdims=True)
    grow = lax.broadcasted_iota(jnp.int32, (8, TM), 0).astype(F32)
    gl = jnp.where(grow < N_GROUPS, lt[NE:NE + 8], ninf)
    gmax = over(jnp.max, gl)
    gsel = over(jnp.min, jnp.where(gl == gmax, grow, big))
    pg = 1.0 / over(jnp.sum, jnp.exp(gl - gmax))
    erow_i = lax.broadcasted_iota(jnp.int32, (NE, TM), 0)
    erow = erow_i.astype(F32)
    el = jnp.where((erow_i // EPG).astype(F32) == gsel, lt[:NE], ninf)
    e1 = over(jnp.max, el)
    i1 = over(jnp.min, jnp.where(el == e1, erow, big))
    el2 = jnp.where(erow == i1, ninf, el)
    e2 = over(jnp.max, el2)
    i2 = over(jnp.min, jnp.where(el2 == e2, erow, big))
    r = jnp.exp(e2 - e1)
    g1 = pg / (1.0 + r)
    g2 = pg * r / (1.0 + r)
    oh1 = jnp.where(erow == i1, 1.0, 0.0)
    oh2 = jnp.where(erow == i2, 1.0, 0.0)
    cnt = oh1 + oh2
    cnt_b = cnt.astype(BF16)
    ts = lax.broadcasted_iota(jnp.int32, (TM, TM), 0)
    tt = lax.broadcasted_iota(jnp.int32, (TM, TM), 1)
    prior = _dot(cnt_b, jnp.where(ts < tt, 1.0, 0.0).astype(BF16))
    c_col = jnp.sum(cnt, axis=1, keepdims=True)
    ei = lax.broadcasted_iota(jnp.int32, (NE, NE), 0)
    ej = lax.broadcasted_iota(jnp.int32, (NE, NE), 1)
    cc_hi, cc_lo = _split2(jnp.broadcast_to(c_col, (NE, LANES)))
    fewer = jnp.where(ej < ei, 1.0, 0.0).astype(BF16)
    off_col = (_dot(fewer, cc_hi) + _dot(fewer, cc_lo))[:, 0:1]
    pos = prior + off_col
    lpos1 = over(jnp.sum, oh1 * pos)
    lpos2 = over(jnp.sum, oh2 * pos)
    cnt_pad = jnp.concatenate([cnt_b, jnp.zeros((LANES - NE, TM), BF16)], axis=0)
    c_tile = _dot_nt(jnp.ones((8, TM), BF16), cnt_pad)[0:1]
    li = lax.broadcasted_iota(jnp.int32, (LANES, LANES), 0)
    lj = lax.broadcasted_iota(jnp.int32, (LANES, LANES), 1)
    lower_e = jnp.where(li < lj, 1.0, 0.0).astype(BF16)
    c_hi, c_lo = _split2(jnp.broadcast_to(c_tile, (8, LANES)))
    off = (_dot(c_hi, lower_e) + _dot(c_lo, lower_e))[0:1]
    sub = lax.broadcasted_iota(jnp.int32, (8, LANES), 0)
    runs_ref[0] = jnp.where(sub == 0, c_tile, jnp.where(sub == 1, cnt_ref[...], jnp.where(sub == 2, off, 0.0)))
    cnt_ref[...] += c_tile
    frow = lax.broadcasted_iota(jnp.int32, (LANES, TM), 0)
    fields = jnp.zeros((LANES, TM), F32)
    for k, val in enumerate((i1, i2, lpos1, lpos2, g1, g2)):
        fields = jnp.where(frow == k, val, fields)
    slab_ref[0] = fields.T


def _mixout(xall, conv_in, g_in, o_f, o_b, modall, cw, cnw, hnw, w_out, nw2, w_r, b_r, l, j0):
    bsz, t_all, _ = xall.shape
    nt = t_all // TM - j0
    t_out = nt * TM
    n64 = t_all // GRID_W
    per = TM // GRID_W
    tile = lambda w, c: pl.BlockSpec((1, TM, w), lambda b, j, c=c: (b, j + j0, c))
    out_tile = lambda w: pl.BlockSpec((1, TM, w), lambda b, j: (b, j, 0))
    full = lambda shape: pl.BlockSpec(shape, lambda b, j: (0,) * len(shape))
    layer = lambda shape: pl.BlockSpec((1, *shape), lambda b, j: (l,) + (0,) * len(shape), pipeline_mode=pl.Buffered(1))
    return pl.pallas_call(
        functools.partial(_mixout_kernel, j0=j0),
        grid=(bsz, nt),
        in_specs=[tile(D, 0),
                  tile(CW, 0), tile(CW, 1), tile(HW, 0),
                  pl.BlockSpec((1, GRID_W, CW // 2), lambda b, j: (b, jnp.maximum((j + j0) * per - 1, 0), 3)),
                  pl.BlockSpec((1, GRID_W, CW // 2), lambda b, j: (b, jnp.minimum((j + j0) * per + per, n64 - 1), 3)),
                  tile(HW, 0), tile(HW, 0),
                  pl.BlockSpec((1, 1, N_MOD, D), lambda b, j: (b, jnp.minimum(j + j0, 1), 0, 0)),
                  full((3, CW)), full((1, CW)), full((1, HW)), layer((D, D)), full((1, D)),
                  full((D, LANES)), full((1, LANES))],
        out_specs=[out_tile(D), out_tile(D), out_tile(LANES),
                   pl.BlockSpec((1, 8, LANES), lambda b, j: (b * nt + j, 0, 0)),
                   pl.BlockSpec((1, LANES), lambda b, j: (0, 0))],
        out_shape=[jax.ShapeDtypeStruct((bsz, t_out, D), F32),
                   jax.ShapeDtypeStruct((bsz, t_out, D), BF16),
                   jax.ShapeDtypeStruct((bsz, t_out, LANES), F32),
                   jax.ShapeDtypeStruct((bsz * nt, 8, LANES), F32),
                   jax.ShapeDtypeStruct((1, LANES), F32)],
        scratch_shapes=[pltpu.VMEM((D, D), BF16), pltpu.VMEM((2, D, LANES), BF16)],
        compiler_params=_cparams(2),
        name="mixout",
    )(xall, conv_in, conv_in, g_in, conv_in, conv_in, o_f, o_b, modall, cw, cnw, hnw, w_out, nw2, w_r, b_r)


RUN_BITS = TM.bit_length()
LONG_RUN_BIT = 5
ROW_TILE = (8, LANES)


def _run_copies(tab_ref, local_ref, sorted_ref, sem, *, to_sorted, wait):
    def per_expert(e, carry):
        ln = tab_ref[0, 0, e]
        lo = tab_ref[0, 0, NE + e]
        go = tab_ref[0, 0, 2 * NE + e]
        def copy_bits(bits):
            for bit in bits:
                size = 1 << bit
                done = ln & ~(2 * size - 1)

                @pl.when((ln & size) != 0)
                def _():
                    a = local_ref.at[pl.ds(lo + done, size)]
                    b = sorted_ref.at[pl.ds(go + done, size)]
                    cp = pltpu.make_async_copy(a, b, sem) if to_sorted else pltpu.make_async_copy(b, a, sem)
                    if wait:
                        cp.wait()
                    else:
                        cp.start(priority=bit % 2)

        @pl.when(ln >= (1 << LONG_RUN_BIT))
        def _():
            copy_bits(range(RUN_BITS - 1, LONG_RUN_BIT - 1, -1))

        copy_bits(range(LONG_RUN_BIT - 1, -1, -1))
        return carry

    lax.fori_loop(0, NE, per_expert, 0)


def _wait_tile(local_ref, sorted_ref, sem, *, to_sorted):
    whole = sorted_ref.at[pl.ds(0, 2 * TM)]
    (pltpu.make_async_copy(local_ref, whole, sem) if to_sorted else pltpu.make_async_copy(whole, local_ref, sem)).wait()


def _local_positions(slab):
    col = lax.broadcasted_iota(jnp.int32, (TM, 2 * TM), 1).astype(F32)
    return col == slab[:, 2:3], col == slab[:, 3:4]


def _dispatch_kernel(tab_ref, ptab_ref, h2_ref, slab_ref, xs_ref, buf, zbuf, sem):
    @pl.when(pl.program_id(0) == 0)
    def _():
        zbuf[...] = jnp.zeros_like(zbuf)
        _run_copies(ptab_ref, zbuf, xs_ref, sem, to_sorted=True, wait=False)
        _run_copies(ptab_ref, zbuf, xs_ref, sem, to_sorted=True, wait=True)

        def zero_block(i, carry):
            cp = pltpu.make_async_copy(zbuf, xs_ref.at[pl.ds(i * MOE_P, MOE_P)], sem)
            cp.start()
            cp.wait()
            return carry

        lax.fori_loop(ptab_ref[0, 0, 3 * NE], xs_ref.shape[0] // MOE_P, zero_block, 0)

    i = pl.program_id(0)
    slot = lax.rem(i, 2)
    p1, p2 = _local_positions(slab_ref[...])
    perm_t = jnp.where(p1 | p2, 1.0, 0.0).astype(BF16)
    buf[slot] = _dot_tn(perm_t, h2_ref[...]).reshape(2 * TM, *ROW_TILE)

    @pl.when(i > 0)
    def _():
        _wait_tile(buf.at[1 - slot], xs_ref, sem, to_sorted=True)

    _run_copies(tab_ref, buf.at[slot], xs_ref, sem, to_sorted=True, wait=False)

    @pl.when(i == pl.num_programs(0) - 1)
    def _():
        _wait_tile(buf.at[slot], xs_ref, sem, to_sorted=True)


def _dispatch(tab, pad_tab, h2, slab, n_rows):
    n_tiles = tab.shape[0]
    return pl.pallas_call(
        _dispatch_kernel,
        grid=(n_tiles,),
        in_specs=[pl.BlockSpec((1, 1, LANES), lambda i: (i, 0, 0), memory_space=pltpu.SMEM),
                  pl.BlockSpec((1, 1, LANES), lambda i: (0, 0, 0), memory_space=pltpu.SMEM),
                  pl.BlockSpec((TM, D), lambda i: (i, 0)),
                  pl.BlockSpec((TM, LANES), lambda i: (i, 0))],
        out_specs=pl.BlockSpec(memory_space=pl.ANY),
        out_shape=jax.ShapeDtypeStruct((n_rows, *ROW_TILE), F32),
        scratch_shapes=[pltpu.VMEM((2, 2 * TM, *ROW_TILE), F32), pltpu.VMEM((MOE_P, *ROW_TILE), F32),
                        pltpu.SemaphoreType.DMA(())],
        compiler_params=_cparams(1),
        name="dispatch",
    )(tab, pad_tab, h2, slab)


def _expert_kernel(be_ref, nx_ref, nu_ref, xs_ref, wgu_hbm, wd_hbm, ys_ref, wgu_f, wd_f, wgu_b, wd_b, slot_ref, sem,
                   *, layer):
    i = pl.program_id(0)
    used = i < nu_ref[0]
    e = be_ref[i]

    def weight_copies(expert, slot):
        return (pltpu.make_async_copy(wgu_hbm.at[layer, expert], wgu_f.at[slot], sem.at[slot]),
                pltpu.make_async_copy(wd_hbm.at[layer, expert], wd_f.at[slot], sem.at[slot]))

    @pl.when(i == 0)
    def _():
        slot_ref[0] = 0
        for cp in weight_copies(e, 0):
            cp.start()

    @pl.when(jnp.logical_and(used, jnp.logical_or(i == 0, e != be_ref[jnp.maximum(i - 1, 0)])))
    def _():
        slot = jnp.where(i == 0, 0, 1 - slot_ref[0])
        slot_ref[0] = slot
        for cp in weight_copies(e, slot):
            cp.wait()
        wgu_b[...] = wgu_f[slot].astype(BF16)
        wd_b[...] = wd_f[slot].astype(BF16)

        @pl.when(nx_ref[i] >= 0)
        def _():
            for cp in weight_copies(nx_ref[i], 1 - slot):
                cp.start()

    @pl.when(used)
    def _():
        x = xs_ref[...].reshape(MOE_P, D).astype(BF16)
        acts = []
        for c in range(0, DE, EXPERT_CHUNK):
            a = _dot(x, wgu_b[:, c:c + EXPERT_CHUNK])
            u = _dot(x, wgu_b[:, DE + c:DE + c + EXPERT_CHUNK])
            acts.append((_silu(a) * u).astype(BF16))
        y = _dot(jnp.concatenate(acts, axis=1), wd_b[...])
        ys_ref[...] = y.reshape(MOE_P, *ROW_TILE)

    @pl.when(jnp.logical_not(used))
    def _():
        ys_ref[...] = jnp.zeros_like(ys_ref)


def _experts(block_e, next_e, n_used, xs, w_gu, w_down, l):
    n_rows = xs.shape[0]
    grid_spec = pltpu.PrefetchScalarGridSpec(
        num_scalar_prefetch=3,
        grid=(n_rows // MOE_P,),
        in_specs=[pl.BlockSpec((MOE_P, *ROW_TILE), lambda i, be, nx, nu: (jnp.minimum(i, nu[0] - 1), 0, 0)),
                  pl.BlockSpec(memory_space=pl.ANY),
                  pl.BlockSpec(memory_space=pl.ANY)],
        out_specs=pl.BlockSpec((MOE_P, *ROW_TILE), lambda i, be, nx, nu: (i, 0, 0)),
        scratch_shapes=[pltpu.VMEM((2, D, 2 * DE), F32), pltpu.VMEM((2, DE, D), F32),
                        pltpu.VMEM((D, 2 * DE), BF16), pltpu.VMEM((DE, D), BF16),
                        pltpu.SMEM((1,), jnp.int32), pltpu.SemaphoreType.DMA((2,))],
    )
    return pl.pallas_call(
        functools.partial(_expert_kernel, layer=l),
        grid_spec=grid_spec,
        out_shape=jax.ShapeDtypeStruct((n_rows, *ROW_TILE), F32),
        compiler_params=_cparams(1),
        name="experts",
    )(block_e, next_e, n_used, xs, w_gu, w_down)


def _combine_kernel(tab_ref, tab_next_ref, xm_ref, slab_ref, mod_ref, fnw_ref, ys_ref, o_ref, buf, sem, *, final):
    i = pl.program_id(0) * pl.num_programs(1) + pl.program_id(1)
    n = pl.num_programs(0) * pl.num_programs(1)
    slot = lax.rem(i, 2)

    @pl.when(i == 0)
    def _():
        _run_copies(tab_ref, buf.at[0], ys_ref, sem.at[0], to_sorted=False, wait=False)

    @pl.when(i + 1 < n)
    def _():
        _run_copies(tab_next_ref, buf.at[1 - slot], ys_ref, sem.at[1 - slot], to_sorted=False, wait=False)

    _wait_tile(buf.at[slot], ys_ref, sem.at[slot], to_sorted=False)
    slab = slab_ref[0]
    p1, p2 = _local_positions(slab)
    yb = buf[slot].reshape(2 * TM, D).astype(BF16)
    y1 = _dot(jnp.where(p1, 1.0, 0.0).astype(BF16), yb)
    y2 = _dot(jnp.where(p2, 1.0, 0.0).astype(BF16), yb)
    f = slab[:, 4:5] * y1 + slab[:, 5:6] * y2
    x = xm_ref[0] + mod_ref[0, 0][5:6] * f
    o_ref[0] = _rms(x, fnw_ref[...]) if final else x


def _combine(tab, xmid, slab, modall, fnw, ys, final, j0):
    bsz, t_out, _ = xmid.shape
    nt = t_out // TM
    return pl.pallas_call(
        functools.partial(_combine_kernel, final=final),
        grid=(bsz, nt),
        in_specs=[pl.BlockSpec((1, 1, LANES), lambda b, j: (b * nt + j, 0, 0), memory_space=pltpu.SMEM),
                  pl.BlockSpec((1, 1, LANES), lambda b, j: (jnp.minimum(b * nt + j + 1, bsz * nt - 1), 0, 0),
                               memory_space=pltpu.SMEM),
                  pl.BlockSpec((1, TM, D), lambda b, j: (b, j, 0)),
                  pl.BlockSpec((1, TM, LANES), lambda b, j: (b, j, 0)),
                  pl.BlockSpec((1, 1, N_MOD, D), lambda b, j: (b, jnp.minimum(j + j0, 1), 0, 0)),
                  pl.BlockSpec((1, D), lambda b, j: (0, 0)),
                  pl.BlockSpec(memory_space=pl.ANY)],
        out_specs=pl.BlockSpec((1, TM, D), lambda b, j: (b, j, 0)),
        out_shape=jax.ShapeDtypeStruct((bsz, t_out, D), F32),
        scratch_shapes=[pltpu.VMEM((2, 2 * TM, *ROW_TILE), F32), pltpu.SemaphoreType.DMA((2,))],
        compiler_params=_cparams(2),
        name="combine_final" if final else "combine",
    )(tab, tab, xmid, slab, modall, fnw, ys)


def _lower_bounds(hg_lb):
    p = jax.nn.softmax(hg_lb.astype(F32), axis=1)
    cs = jnp.cumsum(p, axis=1)
    return cs - cs[:, :1]


def kernel(x, c, ctx, c_ctx, norm_w, w_ada, b_ada, w_in, conv_w, conv_norm_w, hg_lb, hg_norm_w, w_out, w_rg, b_rg,
           w_re, b_re, w_e_gu, w_e_down, final_norm_w):
    bsz, seq, _ = x.shape
    n_ctx = ctx.shape[1]
    assert n_ctx == TM and seq % TM == 0

    xall = jnp.concatenate([ctx, x], axis=1)
    cc = jnp.zeros((16, D), F32).at[:bsz].set(c).at[bsz].set(c_ctx)
    mod = _ada(cc, w_ada, b_ada).reshape(DEPTH, 16, N_MOD, D)
    lb = _lower_bounds(hg_lb)
    lbp = jnp.stack([jnp.log(lb[0]), jnp.log1p(-lb[0]), jnp.log(lb[1]), jnp.log1p(-lb[1])], axis=1) * LOG2E
    w_r = jnp.concatenate([w_re, w_rg, jnp.zeros((DEPTH, D, LANES - NE - N_GROUPS), F32)], axis=-1)
    b_r = jnp.concatenate([b_re, b_rg, jnp.zeros((DEPTH, LANES - NE - N_GROUPS), F32)], axis=-1)

    out = None
    for l in range(DEPTH):
        final = l == DEPTH - 1
        j0 = 1 if final else 0
        modall = jnp.stack([jnp.broadcast_to(mod[l, bsz], (bsz, N_MOD, D)), mod[l, :bsz]], axis=1)
        conv_in, g_in, lf, qkv = _inproj(xall, norm_w[l, 0].reshape(1, D), modall, w_in, lbp[l], l)
        o_f, o_b = _gla(lf, qkv, n_ctx)
        xmid, h2, slab, runs, counts = _mixout(
            xall, conv_in, g_in, o_f, o_b, modall, conv_w[l], conv_norm_w[l].reshape(1, CW),
            jnp.tile(hg_norm_w[l], NH).reshape(1, HW), w_out, norm_w[l, 1].reshape(1, D),
            w_r[l], b_r[l].reshape(1, LANES), l, j0)
        n_tok = xmid.shape[0] * xmid.shape[1]
        n_blocks = -(-(2 * n_tok + NE * (MOE_P - 1)) // MOE_P)
        n_rows = n_blocks * MOE_P
        cnt = counts[0, :NE].astype(jnp.int32)
        padded = (cnt + MOE_P - 1) // MOE_P * MOE_P
        pad_ends = jnp.cumsum(padded)
        pad_starts = pad_ends - padded
        run_len = runs[:, 0, :NE].astype(jnp.int32)
        run_global = pad_starts[None, :] + runs[:, 1, :NE].astype(jnp.int32)
        run_local = runs[:, 2, :NE].astype(jnp.int32)
        tab = jnp.concatenate([run_len, run_local, run_global, jnp.zeros_like(run_len)], axis=1)[:, None, :]
        block_start = jnp.arange(n_blocks, dtype=jnp.int32) * MOE_P
        block_e = jnp.minimum(jnp.sum((pad_ends[None, :] <= block_start[:, None]).astype(jnp.int32), axis=1), NE - 1)
        n_used = (pad_ends[-1:] // MOE_P).astype(jnp.int32)
        zero = jnp.zeros_like(cnt)
        pad_tab = jnp.concatenate([padded - cnt, zero, pad_starts + cnt, zero + n_used]).reshape(1, 1, LANES)
        xs = _dispatch(tab, pad_tab, h2.reshape(n_tok, D), slab.reshape(n_tok, LANES), n_rows)
        ids = jnp.arange(NE, dtype=jnp.int32)
        later = jnp.min(jnp.where((ids[None, :] > block_e[:, None]) & (cnt[None, :] > 0), ids[None, :], NE), axis=1)
        next_e = jnp.where(later < NE, later, -1)
        ys = _experts(block_e, next_e, n_used, xs, w_e_gu, w_e_down, l)
        res = _combine(tab, xmid, slab, modall, final_norm_w.reshape(1, D), ys, final, j0)
        if final:
            out = res
        else:
            xall = res
    return out
```

```python
import functools

import jax
import jax.numpy as jnp
import numpy as np
from jax import lax
from jax.experimental import pallas as pl
from jax.experimental.pallas import tpu as pltpu

F32 = jnp.float32
BF16 = jnp.bfloat16

D = 1024
DEPTH = 4
GRID_W = 64
CW = 512
HW = 512
NH = 4
HD = HW // NH
PW = 3 * CW + 5 * HW
PPW = PW - CW
N_GROUPS = 4
EPG = 8
NE = N_GROUPS * EPG
DE = 512
N_MOD = 6
EPS = 1e-6
LOG2E = 1.4426950408889634

TM = 256
CH = 128
LEAF = 16
HEAD_GROUP = 4
GLA_SUB = 2
MOE_P = 256
EXPERT_CHUNK = 256
LANES = 128

VMEM_LIMIT = 56 * 1024 * 1024


def _cparams(n_axes, flags=None):
    return pltpu.CompilerParams(dimension_semantics=("arbitrary",) * n_axes,
                                vmem_limit_bytes=VMEM_LIMIT, flags=flags)


def _dot(a, b):
    return jnp.dot(a, b, preferred_element_type=F32)


def _dot_nt(a, b):
    return lax.dot_general(a, b, (((1,), (1,)), ((), ())), preferred_element_type=F32)


def _dot_tn(a, b):
    return lax.dot_general(a, b, (((0,), (0,)), ((), ())), preferred_element_type=F32)


def _split2(x):
    hi = x.astype(BF16)
    lo = (x - hi.astype(F32)).astype(BF16)
    return hi, lo


def _dot_hp(a, b):
    a_hi, a_lo = _split2(a)
    b_hi, b_lo = _split2(b)
    return _dot(a_hi, b_hi) + _dot(a_hi, b_lo) + _dot(a_lo, b_hi)


def _sigmoid(x):
    return 1.0 / (1.0 + jnp.exp(-x))


def _rms(x, w):
    return x * lax.rsqrt(jnp.mean(x * x, axis=-1, keepdims=True) + EPS) * w


def _ada_kernel(cc_ref, w_ref, b_ref, o_ref):
    s = cc_ref[...]
    s = s * _sigmoid(s)
    o_ref[0] = _dot_hp(s, w_ref[0]) + b_ref[0]


def _ada(cc, w_ada, b_ada):
    tn = 1536
    n = N_MOD * D
    return pl.pallas_call(
        _ada_kernel,
        grid=(DEPTH, n // tn),
        in_specs=[pl.BlockSpec((16, D), lambda l, j: (0, 0)),
                  pl.BlockSpec((1, D, tn), lambda l, j: (l, 0, j)),
                  pl.BlockSpec((1, 1, tn), lambda l, j: (l, 0, j))],
        out_specs=pl.BlockSpec((1, 16, tn), lambda l, j: (l, 0, j)),
        out_shape=jax.ShapeDtypeStruct((DEPTH, 16, n), F32),
        compiler_params=_cparams(2),
        name="ada",
    )(cc, w_ada, b_ada.reshape(DEPTH, 1, n))


def _silu(x):
    h = 0.5 * x
    return h + h * jnp.tanh(h)


def _neg_abs(x):
    bits = lax.bitcast_convert_type(x, jnp.uint32) | jnp.uint32(0x80000000)
    return lax.bitcast_convert_type(bits, F32)


def _forget_gate(z2, log2lb, log2_1mlb):
    ls = jnp.minimum(z2, 0.0) - jnp.log2(1.0 + jnp.exp2(_neg_abs(z2)))
    c = log2_1mlb + ls
    lf2 = jnp.maximum(log2lb, c) + jnp.log2(1.0 + jnp.exp2(_neg_abs(log2lb - c)))
    return lf2, jnp.exp2(c - z2)


def _inproj_kernel(x_ref, nw_ref, mod_ref, w_ref, lb_ref, conv_ref, g_ref, lf_ref, qkv_ref, w_b):
    @pl.when(jnp.logical_and(pl.program_id(0) == 0, pl.program_id(1) == 0))
    def _():
        def cast_rows(r, carry):
            rows = pl.ds(pl.multiple_of(r * 128, 128), 128)
            lo, hi = 3 * CW + HW, 3 * CW + 3 * HW
            w_b[rows, :lo] = w_ref[0, rows, :lo].astype(BF16)
            w_b[rows, lo:hi] = (w_ref[0, rows, lo:hi] * LOG2E).astype(BF16)
            w_b[rows, hi:] = w_ref[0, rows, hi:].astype(BF16)
            return carry

        lax.fori_loop(0, D // 128, cast_rows, 0)

    m = mod_ref[0, 0]
    h = _rms(x_ref[0], nw_ref[...] * (1.0 + m[1:2])) + m[0:1]
    hb = h.astype(BF16)
    blk = {k: _dot(hb, w_b[:, k * 512:(k + 1) * 512]) for k in (4, 5, 3, 1, 2, 0, 6, 7)}
    for d in range(2):
        lf2, kk = _forget_gate(blk[4 + d], lb_ref[2 * d:2 * d + 1], lb_ref[2 * d + 1:2 * d + 2])
        lf_ref[0, :, d * HW:(d + 1) * HW] = lf2
        qkv_ref[0, :, (1 + d) * HW:(2 + d) * HW] = kk.astype(BF16)
    qkv_ref[0, :, 0:HW] = _silu(blk[3]).astype(BF16)
    conv_ref[0, :, CW:2 * CW] = blk[1] * blk[2]
    conv_ref[0, :, 0:CW] = blk[0]
    qkv_ref[0, :, 3 * HW:4 * HW] = blk[6].astype(BF16)
    g_ref[0] = blk[7]


def _inproj(xall, nw, modall, w_in, lbp, l):
    bsz, t_all, _ = xall.shape
    tile = lambda w: pl.BlockSpec((1, TM, w), lambda b, j: (b, j, 0))
    return pl.pallas_call(
        _inproj_kernel,
        grid=(bsz, t_all // TM),
        in_specs=[pl.BlockSpec((1, TM, D), lambda b, j: (b, j, 0)),
                  pl.BlockSpec((1, D), lambda b, j: (0, 0)),
                  pl.BlockSpec((1, 1, N_MOD, D), lambda b, j: (b, jnp.minimum(j, 1), 0, 0)),
                  pl.BlockSpec((1, D, PW), lambda b, j: (l, 0, 0), pipeline_mode=pl.Buffered(1)),
                  pl.BlockSpec((4, HW), lambda b, j: (0, 0))],
        out_specs=[tile(2 * CW), tile(HW), tile(2 * HW), tile(4 * HW)],
        out_shape=[jax.ShapeDtypeStruct((bsz, t_all, 2 * CW), F32),
                   jax.ShapeDtypeStruct((bsz, t_all, HW), F32),
                   jax.ShapeDtypeStruct((bsz, t_all, 2 * HW), F32),
                   jax.ShapeDtypeStruct((bsz, t_all, 4 * HW), BF16)],
        scratch_shapes=[pltpu.VMEM((D, PW), BF16)],
        compiler_params=_cparams(2),
        name="inproj",
    )(xall, nw, modall, w_in, lbp)


def _gla_tables():
    t = np.arange(CH)[:, None]
    s = np.arange(CH)[None, :]
    x = t ^ s
    lvl = np.full((CH, CH), -1, np.int32)
    n, k = CH, 0
    while n > LEAF:
        lvl[(x < n) & (x >= n // 2)] = k
        n, k = n // 2, k + 1
    lvl[x < LEAF] = k
    fwd = np.where(s <= t, lvl, -1)
    return (np.stack([s <= t, s >= t]).astype(np.float32), np.stack([fwd, fwd.T]).astype(np.int32))


def _gla_factors(lf2, qh, kh, tri, reverse):
    hi = lf2.astype(BF16)
    lo = (lf2 - hi.astype(F32)).astype(BF16)
    b = _dot(tri, hi) + _dot(tri, lo)

    def block_ref(n, idx):
        r = b.reshape(CH // n, n, HW)[:, idx:idx + 1, :]
        return jnp.broadcast_to(r, (CH // n, n, HW)).reshape(CH, HW)

    pow2 = lambda e: jnp.exp2(e.astype(BF16))
    pieces = []
    n = CH
    while n > LEAF:
        h = n // 2
        a = pow2(_neg_abs(b - block_ref(n, h if reverse else h - 1)))
        pieces.append((qh * a, kh * a))
        n = h
    dl = b - block_ref(LEAF, LEAF // 2)
    pieces.append((qh * pow2(dl), kh * pow2(-dl)))

    edge = 0 if reverse else CH - 1
    b_edge = b[edge:edge + 1, :]
    q_in = qh * pow2(b)
    k_out = kh * pow2(b_edge - b)
    d_chunk = jnp.exp2(b_edge)
    return pieces, q_in, k_out, d_chunk


def _gla_heads(factors, v_refs, rows, lvls, s_refs, o_refs):
    dirs = range(len(factors))
    owned = [[lvls[d] == k for k in range(len(factors[d][0]))] for d in dirs]
    for h0 in range(0, NH, HEAD_GROUP):
        heads = range(h0, h0 + HEAD_GROUP)
        sl = lambda hd: slice(hd * HD, (hd + 1) * HD)
        dots = {(d, hd): [_dot_nt(qa[:, sl(hd)], ka[:, sl(hd)]) for qa, ka in factors[d][0]]
                for hd in heads for d in dirs}
        scs = {}
        for hd in heads:
            for d in dirs:
                sc = jnp.zeros((CH, CH), F32)
                for k, dk in enumerate(dots[d, hd]):
                    sc = jnp.where(owned[d][k], dk, sc)
                scs[d, hd] = sc.astype(BF16)
        for hd in heads:
            for d in dirs:
                _, q_in, k_out, d_chunk = factors[d]
                vb = v_refs[d][0, rows[d], sl(hd)]
                st = s_refs[d][hd]
                o_refs[d][0, rows[d], sl(hd)] = _dot(scs[d, hd], vb) + _dot_nt(q_in[:, sl(hd)], st.astype(BF16))
                s_refs[d][hd] = st * d_chunk[:, sl(hd)] + _dot_tn(vb, k_out[:, sl(hd)])


def _gla_kernel(lff_ref, qf_ref, kf_ref, vf_ref, lfb_ref, qb_ref, kb_ref, vb_ref, tri_ref, lvl_ref, of_ref, ob_ref,
                sf_ref, sb_ref):
    @pl.when(pl.program_id(1) == 0)
    def _():
        sf_ref[...] = jnp.zeros_like(sf_ref)
        sb_ref[...] = jnp.zeros_like(sb_ref)

    for c in range(GLA_SUB):
        rows = (slice(c * CH, (c + 1) * CH), slice((GLA_SUB - 1 - c) * CH, (GLA_SUB - c) * CH))
        factors = (_gla_factors(lff_ref[0, rows[0]], qf_ref[0, rows[0]], kf_ref[0, rows[0]], tri_ref[0], False),
                   _gla_factors(lfb_ref[0, rows[1]], qb_ref[0, rows[1]], kb_ref[0, rows[1]], tri_ref[1], True))
        _gla_heads(factors, (vf_ref, vb_ref), rows, (lvl_ref[0], lvl_ref[1]), (sf_ref, sb_ref), (of_ref, ob_ref))


def _gla(lf, qkv, n_ctx):
    bsz, t_all, _ = lf.shape
    span = GLA_SUB * CH
    assert n_ctx % span == 0 and t_all % span == 0
    nc = t_all // span
    n_ctx_blocks = n_ctx // span

    def cb(j):
        return jnp.where(j < n_ctx_blocks, n_ctx_blocks - 1 - j, nc - 1 - (j - n_ctx_blocks))

    blk = (1, span, HW)
    fwd = lambda c: pl.BlockSpec(blk, lambda b, j: (b, j, c))
    bwd = lambda c: pl.BlockSpec(blk, lambda b, j: (b, cb(j), c))
    tri, lvl = _gla_tables()
    return pl.pallas_call(
        _gla_kernel,
        grid=(bsz, nc),
        in_specs=[fwd(0), fwd(0), fwd(1), fwd(3),
                  bwd(1), bwd(0), bwd(2), bwd(3),
                  pl.BlockSpec((2, CH, CH), lambda b, j: (0, 0, 0)),
                  pl.BlockSpec((2, CH, CH), lambda b, j: (0, 0, 0))],
        out_specs=[fwd(0), bwd(0)],
        out_shape=[jax.ShapeDtypeStruct((bsz, t_all, HW), F32)] * 2,
        scratch_shapes=[pltpu.VMEM((NH, HD, HD), F32), pltpu.VMEM((NH, HD, HD), F32)],
        compiler_params=_cparams(2),
        name="gla",
    )(lf, qkv, qkv, qkv, lf, qkv, qkv, qkv, jnp.asarray(tri, BF16), jnp.asarray(lvl))


def _shift_rows(u, k):
    return pltpu.roll(u, k % u.shape[0], axis=0)


def _mixout_kernel(x_ref, bg_ref, u_ref, g_ref, up_ref, un_ref, of_ref, ob_ref, mod_ref, cw_ref, cnw_ref,
                   hnw_ref, wo_ref, nw2_ref, wr_ref, br_ref, xm_ref, h2_ref, slab_ref, runs_ref, cnt_ref, wo_b, wr_b, lg,
                   *, j0, nt):
    i = pl.program_id(0)
    slot = lax.rem(i, 2)
    j = lax.rem(jnp.minimum(i, pl.num_programs(0) - 2), nt - j0) + j0

    @pl.when(i == 0)
    def _():
        cnt_ref[...] = jnp.zeros_like(cnt_ref)
        lg[1] = jnp.zeros_like(lg[1])
        wo_b[...] = wo_ref[0].astype(BF16)
        wr_b[0], wr_b[1] = _split2(wr_ref[...])


    is_ctx = j == 0
    m = mod_ref[0, 0]
    u = u_ref[0]
    cw = cw_ref[...]
    t = lax.broadcasted_iota(jnp.int32, (TM, 1), 0)
    col_in_row = t & (GRID_W - 1)
    keep_l = jnp.where(is_ctx, jnp.where(t == 0, 0.0, 1.0), jnp.where(col_in_row == 0, 0.0, 1.0))
    keep_r = jnp.where(is_ctx, jnp.where(t == TM - 1, 0.0, 1.0), jnp.where(col_in_row == GRID_W - 1, 0.0, 1.0))
    hc = CW // 2

    def seq_taps(a, w):
        left = jnp.where(keep_l > 0.5, _shift_rows(a, 1), 0.0)
        right = jnp.where(keep_r > 0.5, _shift_rows(a, -1), 0.0)
        return w[0:1] * left + w[1:2] * a + w[2:3] * right

    def col_taps(a, w):
        up = jnp.concatenate([jnp.where(j == 1, 0.0, up_ref[0]), a[:TM - GRID_W]], axis=0)
        dn = jnp.concatenate([a[GRID_W:], jnp.where(j == nt - 1, 0.0, un_ref[0])], axis=0)
        return w[0:1] * up + w[1:2] * a + w[2:3] * dn

    y_second = lax.cond(is_ctx, seq_taps, col_taps, u[:, hc:], cw[:, hc:])
    conv = jnp.concatenate([seq_taps(u[:, :hc], cw[:, :hc]), y_second], axis=-1)
    routed = _route(lg[1 - slot].T, jnp.where(i > 0, 1.0, 0.0), cnt_ref[...])
    y_conv = _rms(bg_ref[0] * conv, cnw_ref[...])

    o = of_ref[0] + ob_ref[0]
    g = g_ref[0]
    hnw = hnw_ref[...]
    recs = []
    for hd in range(NH):
        hs = slice(hd * HD, (hd + 1) * HD)
        recs.append(_rms(o[:, hs], hnw[:, hs]))
    y_rec = jnp.concatenate(recs, axis=-1) * (g * _sigmoid(g))

    y = jnp.concatenate([y_conv, y_rec], axis=-1).astype(BF16)
    xm = x_ref[0] + m[2:3] * _dot(y, wo_b[...])
    xm_ref[0] = xm
    h2 = _rms(xm, nw2_ref[...] * (1.0 + m[4:5])) + m[3:4]
    h2_ref[0] = h2.astype(BF16)

    h_hi, h_lo = _split2(h2)
    lg[slot] = _dot(h_hi, wr_b[0]) + _dot(h_hi, wr_b[1]) + _dot(h_lo, wr_b[0]) + br_ref[...]
    slab_ref[0], runs_ref[0], cnt_ref[...] = routed


def _route(lt, live, counted):
    ninf = -jnp.inf
    big = 1e9
    over = lambda fn, a: fn(a, axis=0, keepdims=True)
    grow = lax.broadcasted_iota(jnp.int32, (8, TM), 0).astype(F32)
    gl = jnp.where(grow < N_GROUPS, lt[NE:NE + 8], ninf)
    gmax = over(jnp.max, gl)
    gsel = over(jnp.min, jnp.where(gl == gmax, grow, big))
    pg = 1.0 / over(jnp.sum, jnp.exp(gl - gmax))
    erow_i = lax.broadcasted_iota(jnp.int32, (NE, TM), 0)
    erow = erow_i.astype(F32)
    el = jnp.where((erow_i // EPG).astype(F32) == gsel, lt[:NE], ninf)
    e1 = over(jnp.max, el)
    i1 = over(jnp.min, jnp.where(el == e1, erow, big))
    el2 = jnp.where(erow == i1, ninf, el)
    e2 = over(jnp.max, el2)
    i2 = over(jnp.min, jnp.where(el2 == e2, erow, big))
    r = jnp.exp(e2 - e1)
    g1 = pg / (1.0 + r)
    g2 = pg * r / (1.0 + r)
    oh1 = jnp.where(erow == i1, 1.0, 0.0)
    oh2 = jnp.where(erow == i2, 1.0, 0.0)
    cnt = oh1 + oh2
    cnt_b = cnt.astype(BF16)
    ts = lax.broadcasted_iota(jnp.int32, (TM, TM), 0)
    tt = lax.broadcasted_iota(jnp.int32, (TM, TM), 1)
    prior = _dot(cnt_b, jnp.where(ts < tt, 1.0, 0.0).astype(BF16))
    c_col = jnp.sum(cnt, axis=1, keepdims=True)
    ei = lax.broadcasted_iota(jnp.int32, (NE, NE), 0)
    ej = lax.broadcasted_iota(jnp.int32, (NE, NE), 1)
    cc_hi, cc_lo = _split2(jnp.broadcast_to(c_col, (NE, LANES)))
    fewer = jnp.where(ej < ei, 1.0, 0.0).astype(BF16)
    off_col = (_dot(fewer, cc_hi) + _dot(fewer, cc_lo))[:, 0:1]
    pos = prior + off_col
    lpos1 = over(jnp.sum, oh1 * pos)
    lpos2 = over(jnp.sum, oh2 * pos)
    cnt_pad = jnp.concatenate([cnt_b, jnp.zeros((LANES - NE, TM), BF16)], axis=0)
    c_tile = _dot_nt(jnp.ones((8, TM), BF16), cnt_pad)[0:1]
    c_tile = c_tile * live
    li = lax.broadcasted_iota(jnp.int32, (LANES, LANES), 0)
    lj = lax.broadcasted_iota(jnp.int32, (LANES, LANES), 1)
    lower_e = jnp.where(li < lj, 1.0, 0.0).astype(BF16)
    c_hi, c_lo = _split2(jnp.broadcast_to(c_tile, (8, LANES)))
    off = (_dot(c_hi, lower_e) + _dot(c_lo, lower_e))[0:1]
    sub = lax.broadcasted_iota(jnp.int32, (8, LANES), 0)
    runs = jnp.where(sub == 0, c_tile, jnp.where(sub == 1, counted, jnp.where(sub == 2, off, 0.0)))
    frow = lax.broadcasted_iota(jnp.int32, (LANES, TM), 0)
    fields = jnp.zeros((LANES, TM), F32)
    for k, val in enumerate((i1, i2, lpos1, lpos2, g1, g2)):
        fields = jnp.where(frow == k, val, fields)
    return fields.T, runs, counted + c_tile


def _mixout(xall, conv_in, g_in, o_f, o_b, modall, cw, cnw, hnw, w_out, nw2, w_r, b_r, l, j0):
    bsz, t_all, _ = xall.shape
    nt = t_all // TM - j0
    n = bsz * nt
    t_out = nt * TM
    n64 = t_all // GRID_W
    per = TM // GRID_W
    cur = lambda i: jnp.minimum(i, n - 1)
    bj = lambda i: (cur(i) // nt, cur(i) % nt + j0)
    old = lambda i: jnp.maximum(i - 1, 0)
    tile = lambda w, c: pl.BlockSpec((1, TM, w), lambda i, c=c: (*bj(i), c))
    out_tile = lambda w, at: pl.BlockSpec((1, TM, w), lambda i: (at(i) // nt, at(i) % nt, 0))
    full = lambda shape: pl.BlockSpec(shape, lambda i: (0,) * len(shape))
    layer = lambda shape: pl.BlockSpec((1, *shape), lambda i: (l,) + (0,) * len(shape), pipeline_mode=pl.Buffered(1))
    return pl.pallas_call(
        functools.partial(_mixout_kernel, j0=j0, nt=nt + j0),
        grid=(n + 1,),
        in_specs=[tile(D, 0),
                  tile(CW, 0), tile(CW, 1), tile(HW, 0),
                  pl.BlockSpec((1, GRID_W, CW // 2), lambda i: (bj(i)[0], jnp.maximum(bj(i)[1] * per - 1, 0), 3)),
                  pl.BlockSpec((1, GRID_W, CW // 2),
                               lambda i: (bj(i)[0], jnp.minimum(bj(i)[1] * per + per, n64 - 1), 3)),
                  tile(HW, 0), tile(HW, 0),
                  pl.BlockSpec((1, 1, N_MOD, D), lambda i: (bj(i)[0], jnp.minimum(bj(i)[1], 1), 0, 0)),
                  full((3, CW)), full((1, CW)), full((1, HW)), layer((D, D)), full((1, D)),
                  full((D, LANES)), full((1, LANES))],
        out_specs=[out_tile(D, cur), out_tile(D, cur), out_tile(LANES, old),
                   pl.BlockSpec((1, 8, LANES), lambda i: (old(i), 0, 0)),
                   pl.BlockSpec((1, LANES), lambda i: (0, 0))],
        out_shape=[jax.ShapeDtypeStruct((bsz, t_out, D), F32),
                   jax.ShapeDtypeStruct((bsz, t_out, D), BF16),
                   jax.ShapeDtypeStruct((bsz, t_out, LANES), F32),
                   jax.ShapeDtypeStruct((bsz * nt, 8, LANES), F32),
                   jax.ShapeDtypeStruct((1, LANES), F32)],
        scratch_shapes=[pltpu.VMEM((D, D), BF16), pltpu.VMEM((2, D, LANES), BF16), pltpu.VMEM((2, TM, LANES), F32)],
        compiler_params=_cparams(1),
        name="mixout",
    )(xall, conv_in, conv_in, g_in, conv_in, conv_in, o_f, o_b, modall, cw, cnw, hnw, w_out, nw2, w_r, b_r)


RUN_BITS = TM.bit_length()
LONG_RUN_BIT = 5
ROW_TILE = (8, LANES)


def _run_copies(tab_ref, local_ref, sorted_ref, sem, *, to_sorted, wait):
    def per_expert(e, carry):
        ln = tab_ref[0, 0, e]
        lo = tab_ref[0, 0, NE + e]
        go = tab_ref[0, 0, 2 * NE + e]
        def copy_bits(bits):
            for bit in bits:
                size = 1 << bit
                done = ln & ~(2 * size - 1)

                @pl.when((ln & size) != 0)
                def _():
                    a = local_ref.at[pl.ds(lo + done, size)]
                    b = sorted_ref.at[pl.ds(go + done, size)]
                    cp = pltpu.make_async_copy(a, b, sem) if to_sorted else pltpu.make_async_copy(b, a, sem)
                    if wait:
                        cp.wait()
                    else:
                        cp.start(priority=bit % 2)

        @pl.when(ln >= (1 << LONG_RUN_BIT))
        def _():
            copy_bits(range(RUN_BITS - 1, LONG_RUN_BIT - 1, -1))

        copy_bits(range(LONG_RUN_BIT - 1, -1, -1))
        return carry

    lax.fori_loop(0, NE, per_expert, 0)


def _wait_tile(local_ref, sorted_ref, sem, *, to_sorted):
    whole = sorted_ref.at[pl.ds(0, 2 * TM)]
    (pltpu.make_async_copy(local_ref, whole, sem) if to_sorted else pltpu.make_async_copy(whole, local_ref, sem)).wait()


def _local_positions(slab):
    col = lax.broadcasted_iota(jnp.int32, (TM, 2 * TM), 1).astype(F32)
    return col == slab[:, 2:3], col == slab[:, 3:4]


def _dispatch_kernel(tab_ref, ptab_ref, h2_ref, slab_ref, xs_ref, buf, zbuf, sem):
    @pl.when(pl.program_id(0) == 0)
    def _():
        zbuf[...] = jnp.zeros_like(zbuf)
        _run_copies(ptab_ref, zbuf, xs_ref, sem, to_sorted=True, wait=False)
        _run_copies(ptab_ref, zbuf, xs_ref, sem, to_sorted=True, wait=True)

        def zero_block(i, carry):
            cp = pltpu.make_async_copy(zbuf, xs_ref.at[pl.ds(i * MOE_P, MOE_P)], sem)
            cp.start()
            cp.wait()
            return carry

        lax.fori_loop(ptab_ref[0, 0, 3 * NE], xs_ref.shape[0] // MOE_P, zero_block, 0)

    i = pl.program_id(0)
    slot = lax.rem(i, 2)
    p1, p2 = _local_positions(slab_ref[...])
    perm_t = jnp.where(p1 | p2, 1.0, 0.0).astype(BF16)
    buf[slot] = _dot_tn(perm_t, h2_ref[...]).reshape(2 * TM, *ROW_TILE)

    @pl.when(i > 0)
    def _():
        _wait_tile(buf.at[1 - slot], xs_ref, sem, to_sorted=True)

    _run_copies(tab_ref, buf.at[slot], xs_ref, sem, to_sorted=True, wait=False)

    @pl.when(i == pl.num_programs(0) - 1)
    def _():
        _wait_tile(buf.at[slot], xs_ref, sem, to_sorted=True)


def _dispatch(tab, pad_tab, h2, slab, n_rows):
    n_tiles = tab.shape[0]
    return pl.pallas_call(
        _dispatch_kernel,
        grid=(n_tiles,),
        in_specs=[pl.BlockSpec((1, 1, LANES), lambda i: (i, 0, 0), memory_space=pltpu.SMEM),
                  pl.BlockSpec((1, 1, LANES), lambda i: (0, 0, 0), memory_space=pltpu.SMEM),
                  pl.BlockSpec((TM, D), lambda i: (i, 0)),
                  pl.BlockSpec((TM, LANES), lambda i: (i, 0))],
        out_specs=pl.BlockSpec(memory_space=pl.ANY),
        out_shape=jax.ShapeDtypeStruct((n_rows, *ROW_TILE), F32),
        scratch_shapes=[pltpu.VMEM((2, 2 * TM, *ROW_TILE), F32), pltpu.VMEM((MOE_P, *ROW_TILE), F32),
                        pltpu.SemaphoreType.DMA(())],
        compiler_params=_cparams(1),
        name="dispatch",
    )(tab, pad_tab, h2, slab)


def _expert_kernel(be_ref, nx_ref, nu_ref, xs_ref, wgu_hbm, wd_hbm, ys_ref, wgu_f, wd_f, wgu_b, wd_b, slot_ref, sem,
                   *, layer):
    i = pl.program_id(0)
    used = i < nu_ref[0]
    e = be_ref[i]

    def weight_copies(expert, slot):
        return (pltpu.make_async_copy(wgu_hbm.at[layer, expert], wgu_f.at[slot], sem.at[slot]),
                pltpu.make_async_copy(wd_hbm.at[layer, expert], wd_f.at[slot], sem.at[slot]))

    @pl.when(i == 0)
    def _():
        slot_ref[0] = 0
        for cp in weight_copies(e, 0):
            cp.start()

    @pl.when(jnp.logical_and(used, jnp.logical_or(i == 0, e != be_ref[jnp.maximum(i - 1, 0)])))
    def _():
        slot = jnp.where(i == 0, 0, 1 - slot_ref[0])
        slot_ref[0] = slot
        for cp in weight_copies(e, slot):
            cp.wait()
        wgu_b[...] = wgu_f[slot].astype(BF16)
        wd_b[...] = wd_f[slot].astype(BF16)

        @pl.when(nx_ref[i] >= 0)
        def _():
            for cp in weight_copies(nx_ref[i], 1 - slot):
                cp.start()

    @pl.when(used)
    def _():
        x = xs_ref[...].reshape(MOE_P, D).astype(BF16)
        acts = []
        for c in range(0, DE, EXPERT_CHUNK):
            a = _dot(x, wgu_b[:, c:c + EXPERT_CHUNK])
            u = _dot(x, wgu_b[:, DE + c:DE + c + EXPERT_CHUNK])
            acts.append((_silu(a) * u).astype(BF16))
        y = _dot(jnp.concatenate(acts, axis=1), wd_b[...])
        ys_ref[...] = y.reshape(MOE_P, *ROW_TILE)

    @pl.when(jnp.logical_not(used))
    def _():
        ys_ref[...] = jnp.zeros_like(ys_ref)


def _experts(block_e, next_e, n_used, xs, w_gu, w_down, l):
    n_rows = xs.shape[0]
    grid_spec = pltpu.PrefetchScalarGridSpec(
        num_scalar_prefetch=3,
        grid=(n_rows // MOE_P,),
        in_specs=[pl.BlockSpec((MOE_P, *ROW_TILE), lambda i, be, nx, nu: (jnp.minimum(i, nu[0] - 1), 0, 0)),
                  pl.BlockSpec(memory_space=pl.ANY),
                  pl.BlockSpec(memory_space=pl.ANY)],
        out_specs=pl.BlockSpec((MOE_P, *ROW_TILE), lambda i, be, nx, nu: (i, 0, 0)),
        scratch_shapes=[pltpu.VMEM((2, D, 2 * DE), F32), pltpu.VMEM((2, DE, D), F32),
                        pltpu.VMEM((D, 2 * DE), BF16), pltpu.VMEM((DE, D), BF16),
                        pltpu.SMEM((1,), jnp.int32), pltpu.SemaphoreType.DMA((2,))],
    )
    return pl.pallas_call(
        functools.partial(_expert_kernel, layer=l),
        grid_spec=grid_spec,
        out_shape=jax.ShapeDtypeStruct((n_rows, *ROW_TILE), F32),
        compiler_params=_cparams(1),
        name="experts",
    )(block_e, next_e, n_used, xs, w_gu, w_down)


def _combine_kernel(tab_ref, tab_next_ref, xm_ref, slab_ref, mod_ref, fnw_ref, ys_ref, o_ref, buf, sem, *, final):
    i = pl.program_id(0) * pl.num_programs(1) + pl.program_id(1)
    n = pl.num_programs(0) * pl.num_programs(1)
    slot = lax.rem(i, 2)

    @pl.when(i == 0)
    def _():
        _run_copies(tab_ref, buf.at[0], ys_ref, sem.at[0], to_sorted=False, wait=False)

    @pl.when(i + 1 < n)
    def _():
        _run_copies(tab_next_ref, buf.at[1 - slot], ys_ref, sem.at[1 - slot], to_sorted=False, wait=False)

    _wait_tile(buf.at[slot], ys_ref, sem.at[slot], to_sorted=False)
    slab = slab_ref[0]
    p1, p2 = _local_positions(slab)
    yb = buf[slot].reshape(2 * TM, D).astype(BF16)
    y1 = _dot(jnp.where(p1, 1.0, 0.0).astype(BF16), yb)
    y2 = _dot(jnp.where(p2, 1.0, 0.0).astype(BF16), yb)
    f = slab[:, 4:5] * y1 + slab[:, 5:6] * y2
    x = xm_ref[0] + mod_ref[0, 0][5:6] * f
    o_ref[0] = _rms(x, fnw_ref[...]) if final else x


def _combine(tab, xmid, slab, modall, fnw, ys, final, j0):
    bsz, t_out, _ = xmid.shape
    nt = t_out // TM
    return pl.pallas_call(
        functools.partial(_combine_kernel, final=final),
        grid=(bsz, nt),
        in_specs=[pl.BlockSpec((1, 1, LANES), lambda b, j: (b * nt + j, 0, 0), memory_space=pltpu.SMEM),
                  pl.BlockSpec((1, 1, LANES), lambda b, j: (jnp.minimum(b * nt + j + 1, bsz * nt - 1), 0, 0),
                               memory_space=pltpu.SMEM),
                  pl.BlockSpec((1, TM, D), lambda b, j: (b, j, 0)),
                  pl.BlockSpec((1, TM, LANES), lambda b, j: (b, j, 0)),
                  pl.BlockSpec((1, 1, N_MOD, D), lambda b, j: (b, jnp.minimum(j + j0, 1), 0, 0)),
                  pl.BlockSpec((1, D), lambda b, j: (0, 0)),
                  pl.BlockSpec(memory_space=pl.ANY)],
        out_specs=pl.BlockSpec((1, TM, D), lambda b, j: (b, j, 0)),
        out_shape=jax.ShapeDtypeStruct((bsz, t_out, D), F32),
        scratch_shapes=[pltpu.VMEM((2, 2 * TM, *ROW_TILE), F32), pltpu.SemaphoreType.DMA((2,))],
        compiler_params=_cparams(2),
        name="combine_final" if final else "combine",
    )(tab, tab, xmid, slab, modall, fnw, ys)


def _lower_bounds(hg_lb):
    p = jax.nn.softmax(hg_lb.astype(F32), axis=1)
    cs = jnp.cumsum(p, axis=1)
    return cs - cs[:, :1]


def kernel(x, c, ctx, c_ctx, norm_w, w_ada, b_ada, w_in, conv_w, conv_norm_w, hg_lb, hg_norm_w, w_out, w_rg, b_rg,
           w_re, b_re, w_e_gu, w_e_down, final_norm_w):
    bsz, seq, _ = x.shape
    n_ctx = ctx.shape[1]
    assert n_ctx == TM and seq % TM == 0

    xall = jnp.concatenate([ctx, x], axis=1)
    cc = jnp.zeros((16, D), F32).at[:bsz].set(c).at[bsz].set(c_ctx)
    mod = _ada(cc, w_ada, b_ada).reshape(DEPTH, 16, N_MOD, D)
    lb = _lower_bounds(hg_lb)
    lbp = jnp.stack([jnp.log(lb[0]), jnp.log1p(-lb[0]), jnp.log(lb[1]), jnp.log1p(-lb[1])], axis=1) * LOG2E
    w_r = jnp.concatenate([w_re, w_rg, jnp.zeros((DEPTH, D, LANES - NE - N_GROUPS), F32)], axis=-1)
    b_r = jnp.concatenate([b_re, b_rg, jnp.zeros((DEPTH, LANES - NE - N_GROUPS), F32)], axis=-1)

    out = None
    for l in range(DEPTH):
        final = l == DEPTH - 1
        j0 = 1 if final else 0
        modall = jnp.stack([jnp.broadcast_to(mod[l, bsz], (bsz, N_MOD, D)), mod[l, :bsz]], axis=1)
        conv_in, g_in, lf, qkv = _inproj(xall, norm_w[l, 0].reshape(1, D), modall, w_in, lbp[l], l)
        o_f, o_b = _gla(lf, qkv, n_ctx)
        xmid, h2, slab, runs, counts = _mixout(
            xall, conv_in, g_in, o_f, o_b, modall, conv_w[l], conv_norm_w[l].reshape(1, CW),
            jnp.tile(hg_norm_w[l], NH).reshape(1, HW), w_out, norm_w[l, 1].reshape(1, D),
            w_r[l], b_r[l].reshape(1, LANES), l, j0)
        n_tok = xmid.shape[0] * xmid.shape[1]
        n_blocks = -(-(2 * n_tok + NE * (MOE_P - 1)) // MOE_P)
        n_rows = n_blocks * MOE_P
        cnt = counts[0, :NE].astype(jnp.int32)
        padded = (cnt + MOE_P - 1) // MOE_P * MOE_P
        pad_ends = jnp.cumsum(padded)
        pad_starts = pad_ends - padded
        run_len = runs[:, 0, :NE].astype(jnp.int32)
        run_global = pad_starts[None, :] + runs[:, 1, :NE].astype(jnp.int32)
        run_local = runs[:, 2, :NE].astype(jnp.int32)
        tab = jnp.concatenate([run_len, run_local, run_global, jnp.zeros_like(run_len)], axis=1)[:, None, :]
        block_start = jnp.arange(n_blocks, dtype=jnp.int32) * MOE_P
        block_e = jnp.minimum(jnp.sum((pad_ends[None, :] <= block_start[:, None]).astype(jnp.int32), axis=1), NE - 1)
        n_used = (pad_ends[-1:] // MOE_P).astype(jnp.int32)
        zero = jnp.zeros_like(cnt)
        pad_tab = jnp.concatenate([padded - cnt, zero, pad_starts + cnt, zero + n_used]).reshape(1, 1, LANES)
        xs = _dispatch(tab, pad_tab, h2.reshape(n_tok, D), slab.reshape(n_tok, LANES), n_rows)
        ids = jnp.arange(NE, dtype=jnp.int32)
        later = jnp.min(jnp.where((ids[None, :] > block_e[:, None]) & (cnt[None, :] > 0), ids[None, :], NE), axis=1)
        next_e = jnp.where(later < NE, later, -1)
        ys = _experts(block_e, next_e, n_used, xs, w_e_gu, w_e_down, l)
        res = _combine(tab, xmid, slab, modall, final_norm_w.reshape(1, D), ys, final, j0)
        if final:
            out = res
        else:
            xall = res
    return out
```

```python
import functools

import jax
import jax.numpy as jnp
import numpy as np
from jax import lax
from jax.experimental import pallas as pl
from jax.experimental.pallas import tpu as pltpu

F32 = jnp.float32
BF16 = jnp.bfloat16

D = 1024
DEPTH = 4
GRID_W = 64
CW = 512
HW = 512
NH = 4
HD = HW // NH
PW = 3 * CW + 5 * HW
PPW = PW - CW
N_GROUPS = 4
EPG = 8
NE = N_GROUPS * EPG
DE = 512
N_MOD = 6
EPS = 1e-6
LOG2E = 1.4426950408889634

TM = 256
CH = 128
LEAF = 16
HEAD_GROUP = 4
GLA_SUB = 2
MOE_P = 256
EXPERT_CHUNK = 256
LANES = 128

VMEM_LIMIT = 56 * 1024 * 1024


def _cparams(n_axes, flags=None):
    return pltpu.CompilerParams(dimension_semantics=("arbitrary",) * n_axes,
                                vmem_limit_bytes=VMEM_LIMIT, flags=flags)


def _dot(a, b):
    return jnp.dot(a, b, preferred_element_type=F32)


def _dot_nt(a, b):
    return lax.dot_general(a, b, (((1,), (1,)), ((), ())), preferred_element_type=F32)


def _dot_tn(a, b):
    return lax.dot_general(a, b, (((0,), (0,)), ((), ())), preferred_element_type=F32)


def _split2(x):
    hi = x.astype(BF16)
    lo = (x - hi.astype(F32)).astype(BF16)
    return hi, lo


def _dot_hp(a, b):
    a_hi, a_lo = _split2(a)
    b_hi, b_lo = _split2(b)
    return _dot(a_hi, b_hi) + _dot(a_hi, b_lo) + _dot(a_lo, b_hi)


def _sigmoid(x):
    return 1.0 / (1.0 + jnp.exp(-x))


def _rms(x, w):
    return x * lax.rsqrt(jnp.mean(x * x, axis=-1, keepdims=True) + EPS) * w


def _ada_kernel(cc_ref, w_ref, b_ref, o_ref):
    s = cc_ref[...]
    s = s * _sigmoid(s)
    o_ref[0] = _dot_hp(s, w_ref[0]) + b_ref[0]


def _ada(cc, w_ada, b_ada):
    tn = 1536
    n = N_MOD * D
    return pl.pallas_call(
        _ada_kernel,
        grid=(DEPTH, n // tn),
        in_specs=[pl.BlockSpec((16, D), lambda l, j: (0, 0)),
                  pl.BlockSpec((1, D, tn), lambda l, j: (l, 0, j)),
                  pl.BlockSpec((1, 1, tn), lambda l, j: (l, 0, j))],
        out_specs=pl.BlockSpec((1, 16, tn), lambda l, j: (l, 0, j)),
        out_shape=jax.ShapeDtypeStruct((DEPTH, 16, n), F32),
        compiler_params=_cparams(2),
        name="ada",
    )(cc, w_ada, b_ada.reshape(DEPTH, 1, n))


def _silu(x):
    h = 0.5 * x
    return h + h * jnp.tanh(h)


def _neg_abs(x):
    bits = lax.bitcast_convert_type(x, jnp.uint32) | jnp.uint32(0x80000000)
    return lax.bitcast_convert_type(bits, F32)


def _forget_gate(z2, log2lb, log2_1mlb):
    ls = jnp.minimum(z2, 0.0) - jnp.log2(1.0 + jnp.exp2(_neg_abs(z2)))
    c = log2_1mlb + ls
    lf2 = jnp.maximum(log2lb, c) + jnp.log2(1.0 + jnp.exp2(_neg_abs(log2lb - c)))
    return lf2, jnp.exp2(c - z2)


def _inproj_kernel(x_ref, nw_ref, mod_ref, w_ref, lb_ref, conv_ref, g_ref, lf_ref, qkv_ref, w_b):
    @pl.when(jnp.logical_and(pl.program_id(0) == 0, pl.program_id(1) == 0))
    def _():
        def cast_rows(r, carry):
            rows = pl.ds(pl.multiple_of(r * 128, 128), 128)
            lo, hi = 3 * CW + HW, 3 * CW + 3 * HW
            w_b[rows, :lo] = w_ref[0, rows, :lo].astype(BF16)
            w_b[rows, lo:hi] = (w_ref[0, rows, lo:hi] * LOG2E).astype(BF16)
            w_b[rows, hi:] = w_ref[0, rows, hi:].astype(BF16)
            return carry

        lax.fori_loop(0, D // 128, cast_rows, 0)

    m = mod_ref[0, 0]
    h = _rms(x_ref[0], nw_ref[...] * (1.0 + m[1:2])) + m[0:1]
    hb = h.astype(BF16)
    blk = {k: _dot(hb, w_b[:, k * 512:(k + 1) * 512]) for k in (4, 5, 3, 1, 2, 0, 6, 7)}
    for d in range(2):
        lf2, kk = _forget_gate(blk[4 + d], lb_ref[2 * d:2 * d + 1], lb_ref[2 * d + 1:2 * d + 2])
        lf_ref[0, :, d * HW:(d + 1) * HW] = lf2
        qkv_ref[0, :, (1 + d) * HW:(2 + d) * HW] = kk.astype(BF16)
    qkv_ref[0, :, 0:HW] = _silu(blk[3]).astype(BF16)
    conv_ref[0, :, CW:2 * CW] = blk[1] * blk[2]
    conv_ref[0, :, 0:CW] = blk[0]
    qkv_ref[0, :, 3 * HW:4 * HW] = blk[6].astype(BF16)
    g_ref[0] = blk[7]


def _inproj(xall, nw, modall, w_in, lbp, l):
    bsz, t_all, _ = xall.shape
    tile = lambda w: pl.BlockSpec((1, TM, w), lambda b, j: (b, j, 0))
    return pl.pallas_call(
        _inproj_kernel,
        grid=(bsz, t_all // TM),
        in_specs=[pl.BlockSpec((1, TM, D), lambda b, j: (b, j, 0)),
                  pl.BlockSpec((1, D), lambda b, j: (0, 0)),
                  pl.BlockSpec((1, 1, N_MOD, D), lambda b, j: (b, jnp.minimum(j, 1), 0, 0)),
                  pl.BlockSpec((1, D, PW), lambda b, j: (l, 0, 0), pipeline_mode=pl.Buffered(1)),
                  pl.BlockSpec((4, HW), lambda b, j: (0, 0))],
        out_specs=[tile(2 * CW), tile(HW), tile(2 * HW), tile(4 * HW)],
        out_shape=[jax.ShapeDtypeStruct((bsz, t_all, 2 * CW), F32),
                   jax.ShapeDtypeStruct((bsz, t_all, HW), F32),
                   jax.ShapeDtypeStruct((bsz, t_all, 2 * HW), F32),
                   jax.ShapeDtypeStruct((bsz, t_all, 4 * HW), BF16)],
        scratch_shapes=[pltpu.VMEM((D, PW), BF16)],
        compiler_params=_cparams(2),
        name="inproj",
    )(xall, nw, modall, w_in, lbp)


def _gla_tables():
    t = np.arange(CH)[:, None]
    s = np.arange(CH)[None, :]
    x = t ^ s
    lvl = np.full((CH, CH), -1, np.int32)
    n, k = CH, 0
    while n > LEAF:
        lvl[(x < n) & (x >= n // 2)] = k
        n, k = n // 2, k + 1
    lvl[x < LEAF] = k
    fwd = np.where(s <= t, lvl, -1)
    return (np.stack([s <= t, s >= t]).astype(np.float32), np.stack([fwd, fwd.T]).astype(np.int32))


def _gla_factors(lf2, qh, kh, tri, reverse):
    hi = lf2.astype(BF16)
    lo = (lf2 - hi.astype(F32)).astype(BF16)
    b = _dot(tri, hi) + _dot(tri, lo)

    def block_ref(n, idx):
        r = b.reshape(CH // n, n, HW)[:, idx:idx + 1, :]
        return jnp.broadcast_to(r, (CH // n, n, HW)).reshape(CH, HW)

    pow2 = lambda e: jnp.exp2(e.astype(BF16))
    pieces = []
    n = CH
    while n > LEAF:
        h = n // 2
        a = pow2(_neg_abs(b - block_ref(n, h if reverse else h - 1)))
        pieces.append((qh * a, kh * a))
        n = h
    dl = b - block_ref(LEAF, LEAF // 2)
    pieces.append((qh * pow2(dl), kh * pow2(-dl)))

    edge = 0 if reverse else CH - 1
    b_edge = b[edge:edge + 1, :]
    q_in = qh * pow2(b)
    k_out = kh * pow2(b_edge - b)
    d_chunk = jnp.exp2(b_edge)
    return pieces, q_in, k_out, d_chunk


def _gla_heads(factors, v_refs, rows, lvls, s_refs, o_refs):
    dirs = range(len(factors))
    owned = [[lvls[d] == k for k in range(len(factors[d][0]))] for d in dirs]
    for h0 in range(0, NH, HEAD_GROUP):
        heads = range(h0, h0 + HEAD_GROUP)
        sl = lambda hd: slice(hd * HD, (hd + 1) * HD)
        dots = {(d, hd): [_dot_nt(qa[:, sl(hd)], ka[:, sl(hd)]) for qa, ka in factors[d][0]]
                for hd in heads for d in dirs}
        scs = {}
        for hd in heads:
            for d in dirs:
                sc = jnp.zeros((CH, CH), F32)
                for k, dk in enumerate(dots[d, hd]):
                    sc = jnp.where(owned[d][k], dk, sc)
                scs[d, hd] = sc.astype(BF16)
        for hd in heads:
            for d in dirs:
                _, q_in, k_out, d_chunk = factors[d]
                vb = v_refs[d][0, rows[d], sl(hd)]
                st = s_refs[d][hd]
                o_refs[d][0, rows[d], sl(hd)] = _dot(scs[d, hd], vb) + _dot_nt(q_in[:, sl(hd)], st.astype(BF16))
                s_refs[d][hd] = st * d_chunk[:, sl(hd)] + _dot_tn(vb, k_out[:, sl(hd)])


def _gla_kernel(lff_ref, qf_ref, kf_ref, vf_ref, lfb_ref, qb_ref, kb_ref, vb_ref, tri_ref, lvl_ref, of_ref, ob_ref,
                sf_ref, sb_ref):
    @pl.when(pl.program_id(1) == 0)
    def _():
        sf_ref[...] = jnp.zeros_like(sf_ref)
        sb_ref[...] = jnp.zeros_like(sb_ref)

    for c in range(GLA_SUB):
        rows = (slice(c * CH, (c + 1) * CH), slice((GLA_SUB - 1 - c) * CH, (GLA_SUB - c) * CH))
        factors = (_gla_factors(lff_ref[0, rows[0]], qf_ref[0, rows[0]], kf_ref[0, rows[0]], tri_ref[0], False),
                   _gla_factors(lfb_ref[0, rows[1]], qb_ref[0, rows[1]], kb_ref[0, rows[1]], tri_ref[1], True))
        _gla_heads(factors, (vf_ref, vb_ref), rows, (lvl_ref[0], lvl_ref[1]), (sf_ref, sb_ref), (of_ref, ob_ref))


def _gla(lf, qkv, n_ctx):
    bsz, t_all, _ = lf.shape
    span = GLA_SUB * CH
    assert n_ctx % span == 0 and t_all % span == 0
    nc = t_all // span
    n_ctx_blocks = n_ctx // span

    def cb(j):
        return jnp.where(j < n_ctx_blocks, n_ctx_blocks - 1 - j, nc - 1 - (j - n_ctx_blocks))

    blk = (1, span, HW)
    fwd = lambda c: pl.BlockSpec(blk, lambda b, j: (b, j, c))
    bwd = lambda c: pl.BlockSpec(blk, lambda b, j: (b, cb(j), c))
    tri, lvl = _gla_tables()
    return pl.pallas_call(
        _gla_kernel,
        grid=(bsz, nc),
        in_specs=[fwd(0), fwd(0), fwd(1), fwd(3),
                  bwd(1), bwd(0), bwd(2), bwd(3),
                  pl.BlockSpec((2, CH, CH), lambda b, j: (0, 0, 0)),
                  pl.BlockSpec((2, CH, CH), lambda b, j: (0, 0, 0))],
        out_specs=[fwd(0), bwd(0)],
        out_shape=[jax.ShapeDtypeStruct((bsz, t_all, HW), F32)] * 2,
        scratch_shapes=[pltpu.VMEM((NH, HD, HD), F32), pltpu.VMEM((NH, HD, HD), F32)],
        compiler_params=_cparams(2),
        name="gla",
    )(lf, qkv, qkv, qkv, lf, qkv, qkv, qkv, jnp.asarray(tri, BF16), jnp.asarray(lvl))


def _shift_rows(u, k):
    return pltpu.roll(u, k % u.shape[0], axis=0)


def _mixout_kernel(x_ref, bg_ref, u_ref, g_ref, up_ref, un_ref, of_ref, ob_ref, mod_ref, cw_ref, cnw_ref,
                   hnw_ref, wo_ref, nw2_ref, wr_ref, br_ref, xm_ref, h2_ref, slab_ref, runs_ref, cnt_ref, wo_b, wr_b, lg,
                   *, j0, nt):
    i = pl.program_id(0)
    slot = lax.rem(i, 2)
    j = lax.rem(jnp.minimum(i, pl.num_programs(0) - 2), nt - j0) + j0

    @pl.when(i == 0)
    def _():
        cnt_ref[...] = jnp.zeros_like(cnt_ref)
        lg[1] = jnp.zeros_like(lg[1])
        wo_b[...] = wo_ref[0].astype(BF16)
        wr_b[0], wr_b[1] = _split2(wr_ref[...])


    is_ctx = j == 0
    m = mod_ref[0, 0]
    u = u_ref[0]
    cw = cw_ref[...]
    t = lax.broadcasted_iota(jnp.int32, (TM, 1), 0)
    col_in_row = t & (GRID_W - 1)
    keep_l = jnp.where(is_ctx, jnp.where(t == 0, 0.0, 1.0), jnp.where(col_in_row == 0, 0.0, 1.0))
    keep_r = jnp.where(is_ctx, jnp.where(t == TM - 1, 0.0, 1.0), jnp.where(col_in_row == GRID_W - 1, 0.0, 1.0))
    hc = CW // 2

    def seq_taps(a, w):
        left = jnp.where(keep_l > 0.5, _shift_rows(a, 1), 0.0)
        right = jnp.where(keep_r > 0.5, _shift_rows(a, -1), 0.0)
        return w[0:1] * left + w[1:2] * a + w[2:3] * right

    def col_taps(a, w):
        up = jnp.concatenate([jnp.where(j == 1, 0.0, up_ref[0]), a[:TM - GRID_W]], axis=0)
        dn = jnp.concatenate([a[GRID_W:], jnp.where(j == nt - 1, 0.0, un_ref[0])], axis=0)
        return w[0:1] * up + w[1:2] * a + w[2:3] * dn

    y_second = lax.cond(is_ctx, seq_taps, col_taps, u[:, hc:], cw[:, hc:])
    conv = jnp.concatenate([seq_taps(u[:, :hc], cw[:, :hc]), y_second], axis=-1)
    routed = _route(lg[1 - slot].T, jnp.where(i > 0, 1.0, 0.0), cnt_ref[...])
    y_conv = _rms(bg_ref[0] * conv, cnw_ref[...])

    o = of_ref[0] + ob_ref[0]
    g = g_ref[0]
    hnw = hnw_ref[...]
    recs = []
    for hd in range(NH):
        hs = slice(hd * HD, (hd + 1) * HD)
        recs.append(_rms(o[:, hs], hnw[:, hs]))
    y_rec = jnp.concatenate(recs, axis=-1) * (g * _sigmoid(g))

    y = jnp.concatenate([y_conv, y_rec], axis=-1).astype(BF16)
    xm = x_ref[0] + m[2:3] * _dot(y, wo_b[...])
    xm_ref[0] = xm
    h2 = _rms(xm, nw2_ref[...] * (1.0 + m[4:5])) + m[3:4]
    h2_ref[0] = h2.astype(BF16)

    h_hi, h_lo = _split2(h2)
    lg[slot] = _dot(h_hi, wr_b[0]) + _dot(h_hi, wr_b[1]) + _dot(h_lo, wr_b[0]) + br_ref[...]
    slab_ref[0], runs_ref[0], cnt_ref[...] = routed


def _route(lt, live, counted):
    ninf = -jnp.inf
    big = 1e9
    over = lambda fn, a: fn(a, axis=0, keepdims=True)
    grow = lax.broadcasted_iota(jnp.int32, (8, TM), 0).astype(F32)
    gl = jnp.where(grow < N_GROUPS, lt[NE:NE + 8], ninf)
    gmax = over(jnp.max, gl)
    gsel = over(jnp.min, jnp.where(gl == gmax, grow, big))
    pg = 1.0 / over(jnp.sum, jnp.exp(gl - gmax))
    erow_i = lax.broadcasted_iota(jnp.int32, (NE, TM), 0)
    erow = erow_i.astype(F32)
    el = jnp.where((erow_i // EPG).astype(F32) == gsel, lt[:NE], ninf)
    e1 = over(jnp.max, el)
    i1 = over(jnp.min, jnp.where(el == e1, erow, big))
    el2 = jnp.where(erow == i1, ninf, el)
    e2 = over(jnp.max, el2)
    i2 = over(jnp.min, jnp.where(el2 == e2, erow, big))
    r = jnp.exp(e2 - e1)
    g1 = pg / (1.0 + r)
    g2 = pg * r / (1.0 + r)
    oh1 = jnp.where(erow == i1, 1.0, 0.0)
    oh2 = jnp.where(erow == i2, 1.0, 0.0)
    cnt = oh1 + oh2
    cnt_b = cnt.astype(BF16)
    ts = lax.broadcasted_iota(jnp.int32, (TM, TM), 0)
    tt = lax.broadcasted_iota(jnp.int32, (TM, TM), 1)
    prior = _dot(cnt_b, jnp.where(ts < tt, 1.0, 0.0).astype(BF16))
    c_col = jnp.sum(cnt, axis=1, keepdims=True)
    ei = lax.broadcasted_iota(jnp.int32, (NE, NE), 0)
    ej = lax.broadcasted_iota(jnp.int32, (NE, NE), 1)
    cc_hi, cc_lo = _split2(jnp.broadcast_to(c_col, (NE, LANES)))
    fewer = jnp.where(ej < ei, 1.0, 0.0).astype(BF16)
    off_col = (_dot(fewer, cc_hi) + _dot(fewer, cc_lo))[:, 0:1]
    pos = prior + off_col
    lpos1 = over(jnp.sum, oh1 * pos)
    lpos2 = over(jnp.sum, oh2 * pos)
    cnt_pad = jnp.concatenate([cnt_b, jnp.zeros((LANES - NE, TM), BF16)], axis=0)
    c_tile = _dot_nt(jnp.ones((8, TM), BF16), cnt_pad)[0:1]
    c_tile = c_tile * live
    li = lax.broadcasted_iota(jnp.int32, (LANES, LANES), 0)
    lj = lax.broadcasted_iota(jnp.int32, (LANES, LANES), 1)
    lower_e = jnp.where(li < lj, 1.0, 0.0).astype(BF16)
    c_hi, c_lo = _split2(jnp.broadcast_to(c_tile, (8, LANES)))
    off = (_dot(c_hi, lower_e) + _dot(c_lo, lower_e))[0:1]
    sub = lax.broadcasted_iota(jnp.int32, (8, LANES), 0)
    runs = jnp.where(sub == 0, c_tile, jnp.where(sub == 1, counted, jnp.where(sub == 2, off, 0.0)))
    frow = lax.broadcasted_iota(jnp.int32, (LANES, TM), 0)
    fields = jnp.zeros((LANES, TM), F32)
    for k, val in enumerate((i1, i2, lpos1, lpos2, g1, g2)):
        fields = jnp.where(frow == k, val, fields)
    return fields.T, runs, counted + c_tile


def _mixout(xall, conv_in, g_in, o_f, o_b, modall, cw, cnw, hnw, w_out, nw2, w_r, b_r, l, j0):
    bsz, t_all, _ = xall.shape
    nt = t_all // TM - j0
    n = bsz * nt
    t_out = nt * TM
    n64 = t_all // GRID_W
    per = TM // GRID_W
    cur = lambda i: jnp.minimum(i, n - 1)
    bj = lambda i: (cur(i) // nt, cur(i) % nt + j0)
    old = lambda i: jnp.maximum(i - 1, 0)
    tile = lambda w, c: pl.BlockSpec((1, TM, w), lambda i, c=c: (*bj(i), c))
    out_tile = lambda w, at: pl.BlockSpec((1, TM, w), lambda i: (at(i) // nt, at(i) % nt, 0))
    full = lambda shape: pl.BlockSpec(shape, lambda i: (0,) * len(shape))
    layer = lambda shape: pl.BlockSpec((1, *shape), lambda i: (l,) + (0,) * len(shape), pipeline_mode=pl.Buffered(1))
    return pl.pallas_call(
        functools.partial(_mixout_kernel, j0=j0, nt=nt + j0),
        grid=(n + 1,),
        in_specs=[tile(D, 0),
                  tile(CW, 0), tile(CW, 1), tile(HW, 0),
                  pl.BlockSpec((1, GRID_W, CW // 2), lambda i: (bj(i)[0], jnp.maximum(bj(i)[1] * per - 1, 0), 3)),
                  pl.BlockSpec((1, GRID_W, CW // 2),
                               lambda i: (bj(i)[0], jnp.minimum(bj(i)[1] * per + per, n64 - 1), 3)),
                  tile(HW, 0), tile(HW, 0),
                  pl.BlockSpec((1, 1, N_MOD, D), lambda i: (bj(i)[0], jnp.minimum(bj(i)[1], 1), 0, 0)),
                  full((3, CW)), full((1, CW)), full((1, HW)), layer((D, D)), full((1, D)),
                  full((D, LANES)), full((1, LANES))],
        out_specs=[out_tile(D, cur), out_tile(D, cur), out_tile(LANES, old),
                   pl.BlockSpec((1, 8, LANES), lambda i: (old(i), 0, 0)),
                   pl.BlockSpec((1, LANES), lambda i: (0, 0))],
        out_shape=[jax.ShapeDtypeStruct((bsz, t_out, D), F32),
                   jax.ShapeDtypeStruct((bsz, t_out, D), BF16),
                   jax.ShapeDtypeStruct((bsz, t_out, LANES), F32),
                   jax.ShapeDtypeStruct((bsz * nt, 8, LANES), F32),
                   jax.ShapeDtypeStruct((1, LANES), F32)],
        scratch_shapes=[pltpu.VMEM((D, D), BF16), pltpu.VMEM((2, D, LANES), BF16), pltpu.VMEM((2, TM, LANES), F32)],
        compiler_params=_cparams(1),
        name="mixout",
    )(xall, conv_in, conv_in, g_in, conv_in, conv_in, o_f, o_b, modall, cw, cnw, hnw, w_out, nw2, w_r, b_r)


RUN_BITS = TM.bit_length()
LONG_RUN_BIT = 5
LONG_BITS = tuple(range(RUN_BITS - 1, LONG_RUN_BIT - 1, -1))
SHORT_BITS = tuple(range(LONG_RUN_BIT - 1, -1, -1))
COMBINE_CHUNKS = 4
ROW_TILE = (8, LANES)


def _run_copies(tab_ref, local_ref, sorted_ref, sem, *, to_sorted, wait, long_only=False):
    def per_expert(e, carry):
        @pl.when(tab_ref[0, 0, e] >= (1 << LONG_RUN_BIT))
        def _():
            _copy_run_bits(tab_ref, e, LONG_BITS, local_ref, sorted_ref, sem, to_sorted=to_sorted, wait=wait)

        if not long_only:
            _copy_run_bits(tab_ref, e, SHORT_BITS, local_ref, sorted_ref, sem, to_sorted=to_sorted, wait=wait)
        return carry

    lax.fori_loop(0, NE, per_expert, 0)


def _copy_run_bits(tab_ref, e, bits, local_ref, sorted_ref, sem, *, to_sorted, wait=False, enable=None):
    ln = tab_ref[0, 0, e]
    lo = tab_ref[0, 0, NE + e]
    go = tab_ref[0, 0, 2 * NE + e]
    for bit in bits:
        size = 1 << bit
        done = ln & ~(2 * size - 1)
        pred = (ln & size) != 0

        @pl.when(pred if enable is None else jnp.logical_and(pred, enable))
        def _():
            a = local_ref.at[pl.ds(lo + done, size)]
            b = sorted_ref.at[pl.ds(go + done, size)]
            cp = pltpu.make_async_copy(a, b, sem) if to_sorted else pltpu.make_async_copy(b, a, sem)
            if wait:
                cp.wait()
            else:
                cp.start(priority=bit % 2)


def _wait_tile(local_ref, sorted_ref, sem, *, to_sorted):
    whole = sorted_ref.at[pl.ds(0, 2 * TM)]
    (pltpu.make_async_copy(local_ref, whole, sem) if to_sorted else pltpu.make_async_copy(whole, local_ref, sem)).wait()


def _local_positions(slab):
    col = lax.broadcasted_iota(jnp.int32, (TM, 2 * TM), 1).astype(F32)
    return col == slab[:, 2:3], col == slab[:, 3:4]


def _dispatch_kernel(tab_ref, ptab_ref, h2_ref, slab_ref, xs_ref, buf, zbuf, sem):
    @pl.when(pl.program_id(0) == 0)
    def _():
        zbuf[...] = jnp.zeros_like(zbuf)
        _run_copies(ptab_ref, zbuf, xs_ref, sem, to_sorted=True, wait=False)
        _run_copies(ptab_ref, zbuf, xs_ref, sem, to_sorted=True, wait=True)

        def zero_block(i, carry):
            cp = pltpu.make_async_copy(zbuf, xs_ref.at[pl.ds(i * MOE_P, MOE_P)], sem)
            cp.start()
            cp.wait()
            return carry

        lax.fori_loop(ptab_ref[0, 0, 3 * NE], xs_ref.shape[0] // MOE_P, zero_block, 0)

    i = pl.program_id(0)
    slot = lax.rem(i, 2)
    p1, p2 = _local_positions(slab_ref[...])
    perm_t = jnp.where(p1 | p2, 1.0, 0.0).astype(BF16)
    buf[slot] = _dot_tn(perm_t, h2_ref[...]).reshape(2 * TM, *ROW_TILE)

    @pl.when(i > 0)
    def _():
        _wait_tile(buf.at[1 - slot], xs_ref, sem, to_sorted=True)

    _run_copies(tab_ref, buf.at[slot], xs_ref, sem, to_sorted=True, wait=False)

    @pl.when(i == pl.num_programs(0) - 1)
    def _():
        _wait_tile(buf.at[slot], xs_ref, sem, to_sorted=True)


def _dispatch(tab, pad_tab, h2, slab, n_rows):
    n_tiles = tab.shape[0]
    return pl.pallas_call(
        _dispatch_kernel,
        grid=(n_tiles,),
        in_specs=[pl.BlockSpec((1, 1, LANES), lambda i: (i, 0, 0), memory_space=pltpu.SMEM),
                  pl.BlockSpec((1, 1, LANES), lambda i: (0, 0, 0), memory_space=pltpu.SMEM),
                  pl.BlockSpec((TM, D), lambda i: (i, 0)),
                  pl.BlockSpec((TM, LANES), lambda i: (i, 0))],
        out_specs=pl.BlockSpec(memory_space=pl.ANY),
        out_shape=jax.ShapeDtypeStruct((n_rows, *ROW_TILE), F32),
        scratch_shapes=[pltpu.VMEM((2, 2 * TM, *ROW_TILE), F32), pltpu.VMEM((MOE_P, *ROW_TILE), F32),
                        pltpu.SemaphoreType.DMA(())],
        compiler_params=_cparams(1),
        name="dispatch",
    )(tab, pad_tab, h2, slab)


def _expert_kernel(be_ref, nx_ref, nu_ref, xs_ref, wgu_hbm, wd_hbm, ys_ref, wgu_f, wd_f, wgu_b, wd_b, slot_ref, sem,
                   *, layer):
    i = pl.program_id(0)
    used = i < nu_ref[0]
    e = be_ref[i]

    def weight_copies(expert, slot):
        return (pltpu.make_async_copy(wgu_hbm.at[layer, expert], wgu_f.at[slot], sem.at[slot]),
                pltpu.make_async_copy(wd_hbm.at[layer, expert], wd_f.at[slot], sem.at[slot]))

    @pl.when(i == 0)
    def _():
        slot_ref[0] = 0
        for cp in weight_copies(e, 0):
            cp.start()

    @pl.when(jnp.logical_and(used, jnp.logical_or(i == 0, e != be_ref[jnp.maximum(i - 1, 0)])))
    def _():
        slot = jnp.where(i == 0, 0, 1 - slot_ref[0])
        slot_ref[0] = slot
        for cp in weight_copies(e, slot):
            cp.wait()
        wgu_b[...] = wgu_f[slot].astype(BF16)
        wd_b[...] = wd_f[slot].astype(BF16)

        @pl.when(nx_ref[i] >= 0)
        def _():
            for cp in weight_copies(nx_ref[i], 1 - slot):
                cp.start()

    @pl.when(used)
    def _():
        x = xs_ref[...].reshape(MOE_P, D).astype(BF16)
        acts = []
        for c in range(0, DE, EXPERT_CHUNK):
            a = _dot(x, wgu_b[:, c:c + EXPERT_CHUNK])
            u = _dot(x, wgu_b[:, DE + c:DE + c + EXPERT_CHUNK])
            acts.append((_silu(a) * u).astype(BF16))
        y = _dot(jnp.concatenate(acts, axis=1), wd_b[...])
        ys_ref[...] = y.reshape(MOE_P, *ROW_TILE)

    @pl.when(jnp.logical_not(used))
    def _():
        ys_ref[...] = jnp.zeros_like(ys_ref)


def _experts(block_e, next_e, n_used, xs, w_gu, w_down, l):
    n_rows = xs.shape[0]
    grid_spec = pltpu.PrefetchScalarGridSpec(
        num_scalar_prefetch=3,
        grid=(n_rows // MOE_P,),
        in_specs=[pl.BlockSpec((MOE_P, *ROW_TILE), lambda i, be, nx, nu: (jnp.minimum(i, nu[0] - 1), 0, 0)),
                  pl.BlockSpec(memory_space=pl.ANY),
                  pl.BlockSpec(memory_space=pl.ANY)],
        out_specs=pl.BlockSpec((MOE_P, *ROW_TILE), lambda i, be, nx, nu: (i, 0, 0)),
        scratch_shapes=[pltpu.VMEM((2, D, 2 * DE), F32), pltpu.VMEM((2, DE, D), F32),
                        pltpu.VMEM((D, 2 * DE), BF16), pltpu.VMEM((DE, D), BF16),
                        pltpu.SMEM((1,), jnp.int32), pltpu.SemaphoreType.DMA((2,))],
    )
    return pl.pallas_call(
        functools.partial(_expert_kernel, layer=l),
        grid_spec=grid_spec,
        out_shape=jax.ShapeDtypeStruct((n_rows, *ROW_TILE), F32),
        compiler_params=_cparams(1),
        name="experts",
    )(block_e, next_e, n_used, xs, w_gu, w_down)


def _combine_kernel(tab_ref, tab_next_ref, xm_ref, slab_ref, mod_ref, fnw_ref, ys_ref, o_ref, buf, sem, *, final):
    i = pl.program_id(0) * pl.num_programs(1) + pl.program_id(1)
    n = pl.num_programs(0) * pl.num_programs(1)
    slot = lax.rem(i, 2)

    @pl.when(i == 0)
    def _():
        _run_copies(tab_ref, buf.at[0], ys_ref, sem.at[0], to_sorted=False, wait=False)

    has_next = i + 1 < n

    @pl.when(has_next)
    def _():
        _run_copies(tab_next_ref, buf.at[1 - slot], ys_ref, sem.at[1 - slot], to_sorted=False, wait=False,
                    long_only=True)

    _wait_tile(buf.at[slot], ys_ref, sem.at[slot], to_sorted=False)
    slab = slab_ref[0]
    p1, p2 = _local_positions(slab)
    p1 = jnp.where(p1, 1.0, 0.0).astype(BF16)
    p2 = jnp.where(p2, 1.0, 0.0).astype(BF16)
    yb = buf[slot].reshape(2 * TM, D).astype(BF16)
    gate = mod_ref[0, 0][5:6]
    chunks = []
    for c in range(COMBINE_CHUNKS):
        cols = slice(c * (D // COMBINE_CHUNKS), (c + 1) * (D // COMBINE_CHUNKS))
        f = slab[:, 4:5] * _dot(p1, yb[:, cols]) + slab[:, 5:6] * _dot(p2, yb[:, cols])
        x = xm_ref[0, :, cols] + gate[:, cols] * f
        if final:
            chunks.append(x)
        else:
            o_ref[0, :, cols] = x
        for e in range(c * (NE // COMBINE_CHUNKS), (c + 1) * (NE // COMBINE_CHUNKS)):
            _copy_run_bits(tab_next_ref, e, SHORT_BITS, buf.at[1 - slot], ys_ref, sem.at[1 - slot], to_sorted=False,
                           enable=has_next)
    if final:
        o_ref[0] = _rms(jnp.concatenate(chunks, axis=1), fnw_ref[...])


def _combine(tab, xmid, slab, modall, fnw, ys, final, j0):
    bsz, t_out, _ = xmid.shape
    nt = t_out // TM
    return pl.pallas_call(
        functools.partial(_combine_kernel, final=final),
        grid=(bsz, nt),
        in_specs=[pl.BlockSpec((1, 1, LANES), lambda b, j: (b * nt + j, 0, 0), memory_space=pltpu.SMEM),
                  pl.BlockSpec((1, 1, LANES), lambda b, j: (jnp.minimum(b * nt + j + 1, bsz * nt - 1), 0, 0),
                               memory_space=pltpu.SMEM),
                  pl.BlockSpec((1, TM, D), lambda b, j: (b, j, 0)),
                  pl.BlockSpec((1, TM, LANES), lambda b, j: (b, j, 0)),
                  pl.BlockSpec((1, 1, N_MOD, D), lambda b, j: (b, jnp.minimum(j + j0, 1), 0, 0)),
                  pl.BlockSpec((1, D), lambda b, j: (0, 0)),
                  pl.BlockSpec(memory_space=pl.ANY)],
        out_specs=pl.BlockSpec((1, TM, D), lambda b, j: (b, j, 0)),
        out_shape=jax.ShapeDtypeStruct((bsz, t_out, D), F32),
        scratch_shapes=[pltpu.VMEM((2, 2 * TM, *ROW_TILE), F32), pltpu.SemaphoreType.DMA((2,))],
        compiler_params=_cparams(2),
        name="combine_final" if final else "combine",
    )(tab, tab, xmid, slab, modall, fnw, ys)


def _lower_bounds(hg_lb):
    p = jax.nn.softmax(hg_lb.astype(F32), axis=1)
    cs = jnp.cumsum(p, axis=1)
    return cs - cs[:, :1]


def kernel(x, c, ctx, c_ctx, norm_w, w_ada, b_ada, w_in, conv_w, conv_norm_w, hg_lb, hg_norm_w, w_out, w_rg, b_rg,
           w_re, b_re, w_e_gu, w_e_down, final_norm_w):
    bsz, seq, _ = x.shape
    n_ctx = ctx.shape[1]
    assert n_ctx == TM and seq % TM == 0

    xall = jnp.concatenate([ctx, x], axis=1)
    cc = jnp.zeros((16, D), F32).at[:bsz].set(c).at[bsz].set(c_ctx)
    mod = _ada(cc, w_ada, b_ada).reshape(DEPTH, 16, N_MOD, D)
    lb = _lower_bounds(hg_lb)
    lbp = jnp.stack([jnp.log(lb[0]), jnp.log1p(-lb[0]), jnp.log(lb[1]), jnp.log1p(-lb[1])], axis=1) * LOG2E
    w_r = jnp.concatenate([w_re, w_rg, jnp.zeros((DEPTH, D, LANES - NE - N_GROUPS), F32)], axis=-1)
    b_r = jnp.concatenate([b_re, b_rg, jnp.zeros((DEPTH, LANES - NE - N_GROUPS), F32)], axis=-1)

    out = None
    for l in range(DEPTH):
        final = l == DEPTH - 1
        j0 = 1 if final else 0
        modall = jnp.stack([jnp.broadcast_to(mod[l, bsz], (bsz, N_MOD, D)), mod[l, :bsz]], axis=1)
        conv_in, g_in, lf, qkv = _inproj(xall, norm_w[l, 0].reshape(1, D), modall, w_in, lbp[l], l)
        o_f, o_b = _gla(lf, qkv, n_ctx)
        xmid, h2, slab, runs, counts = _mixout(
            xall, conv_in, g_in, o_f, o_b, modall, conv_w[l], conv_norm_w[l].reshape(1, CW),
            jnp.tile(hg_norm_w[l], NH).reshape(1, HW), w_out, norm_w[l, 1].reshape(1, D),
            w_r[l], b_r[l].reshape(1, LANES), l, j0)
        n_tok = xmid.shape[0] * xmid.shape[1]
        n_blocks = -(-(2 * n_tok + NE * (MOE_P - 1)) // MOE_P)
        n_rows = n_blocks * MOE_P
        cnt = counts[0, :NE].astype(jnp.int32)
        padded = (cnt + MOE_P - 1) // MOE_P * MOE_P
        pad_ends = jnp.cumsum(padded)
        pad_starts = pad_ends - padded
        run_len = runs[:, 0, :NE].astype(jnp.int32)
        run_global = pad_starts[None, :] + runs[:, 1, :NE].astype(jnp.int32)
        run_local = runs[:, 2, :NE].astype(jnp.int32)
        tab = jnp.concatenate([run_len, run_local, run_global, jnp.zeros_like(run_len)], axis=1)[:, None, :]
        block_start = jnp.arange(n_blocks, dtype=jnp.int32) * MOE_P
        block_e = jnp.minimum(jnp.sum((pad_ends[None, :] <= block_start[:, None]).astype(jnp.int32), axis=1), NE - 1)
        n_used = (pad_ends[-1:] // MOE_P).astype(jnp.int32)
        zero = jnp.zeros_like(cnt)
        pad_tab = jnp.concatenate([padded - cnt, zero, pad_starts + cnt, zero + n_used]).reshape(1, 1, LANES)
        xs = _dispatch(tab, pad_tab, h2.reshape(n_tok, D), slab.reshape(n_tok, LANES), n_rows)
        ids = jnp.arange(NE, dtype=jnp.int32)
        later = jnp.min(jnp.where((ids[None, :] > block_e[:, None]) & (cnt[None, :] > 0), ids[None, :], NE), axis=1)
        next_e = jnp.where(later < NE, later, -1)
        ys = _experts(block_e, next_e, n_used, xs, w_e_gu, w_e_down, l)
        res = _combine(tab, xmid, slab, modall, final_norm_w.reshape(1, D), ys, final, j0)
        if final:
            out = res
        else:
            xall = res
    return out
```

```python
import functools

import jax
import jax.numpy as jnp
import numpy as np
from jax import lax
from jax.experimental import pallas as pl
from jax.experimental.pallas import tpu as pltpu

F32 = jnp.float32
BF16 = jnp.bfloat16

D = 1024
DEPTH = 4
GRID_W = 64
CW = 512
HW = 512
NH = 4
HD = HW // NH
PW = 3 * CW + 5 * HW
PPW = PW - CW
N_GROUPS = 4
EPG = 8
NE = N_GROUPS * EPG
DE = 512
N_MOD = 6
EPS = 1e-6
LOG2E = 1.4426950408889634

TM = 256
CH = 128
LEAF = 16
HEAD_GROUP = 4
GLA_SUB = 2
MOE_P = 256
EXPERT_CHUNK = 256
LANES = 128

VMEM_LIMIT = 56 * 1024 * 1024


def _cparams(n_axes, flags=None):
    return pltpu.CompilerParams(dimension_semantics=("arbitrary",) * n_axes,
                                vmem_limit_bytes=VMEM_LIMIT, flags=flags)


def _dot(a, b):
    return jnp.dot(a, b, preferred_element_type=F32)


def _dot_nt(a, b):
    return lax.dot_general(a, b, (((1,), (1,)), ((), ())), preferred_element_type=F32)


def _dot_tn(a, b):
    return lax.dot_general(a, b, (((0,), (0,)), ((), ())), preferred_element_type=F32)


def _split2(x):
    hi = x.astype(BF16)
    lo = (x - hi.astype(F32)).astype(BF16)
    return hi, lo


def _dot_hp(a, b):
    a_hi, a_lo = _split2(a)
    b_hi, b_lo = _split2(b)
    return _dot(a_hi, b_hi) + _dot(a_hi, b_lo) + _dot(a_lo, b_hi)


def _sigmoid(x):
    return 1.0 / (1.0 + jnp.exp(-x))


def _rms(x, w):
    return x * lax.rsqrt(jnp.mean(x * x, axis=-1, keepdims=True) + EPS) * w


def _ada_kernel(cc_ref, w_ref, b_ref, o_ref):
    s = cc_ref[...]
    s = s * _sigmoid(s)
    o_ref[0] = _dot_hp(s, w_ref[0]) + b_ref[0]


def _ada(cc, w_ada, b_ada):
    tn = 1536
    n = N_MOD * D
    return pl.pallas_call(
        _ada_kernel,
        grid=(DEPTH, n // tn),
        in_specs=[pl.BlockSpec((16, D), lambda l, j: (0, 0)),
                  pl.BlockSpec((1, D, tn), lambda l, j: (l, 0, j)),
                  pl.BlockSpec((1, 1, tn), lambda l, j: (l, 0, j))],
        out_specs=pl.BlockSpec((1, 16, tn), lambda l, j: (l, 0, j)),
        out_shape=jax.ShapeDtypeStruct((DEPTH, 16, n), F32),
        compiler_params=_cparams(2),
        name="ada",
    )(cc, w_ada, b_ada.reshape(DEPTH, 1, n))


def _silu(x):
    h = 0.5 * x
    return h + h * jnp.tanh(h)


def _neg_abs(x):
    bits = lax.bitcast_convert_type(x, jnp.uint32) | jnp.uint32(0x80000000)
    return lax.bitcast_convert_type(bits, F32)


def _forget_gate(z2, log2lb, log2_1mlb):
    ls = jnp.minimum(z2, 0.0) - jnp.log2(1.0 + jnp.exp2(_neg_abs(z2)))
    c = log2_1mlb + ls
    lf2 = jnp.maximum(log2lb, c) + jnp.log2(1.0 + jnp.exp2(_neg_abs(log2lb - c)))
    return lf2, jnp.exp2(c - z2)


def _token_tile(ctx_ref, lat_ref, j):
    return jnp.where(j == 0, ctx_ref[0], lat_ref[0])


def _token_specs(src, tile_of):
    ctx_like, lat_like, first = src

    def lat_map(*idx):
        b, j = tile_of(*idx)
        return b, jnp.maximum(j - first, 0), 0

    return ([pl.BlockSpec((1, TM, D), lambda *idx: (tile_of(*idx)[0], 0, 0)), pl.BlockSpec((1, TM, D), lat_map)],
            [ctx_like, lat_like])


def _inproj_kernel(ctx_ref, x_ref, nw_ref, mod_ref, w_ref, lb_ref, conv_ref, g_ref, lf_ref, qkv_ref, w_b):
    @pl.when(jnp.logical_and(pl.program_id(0) == 0, pl.program_id(1) == 0))
    def _():
        def cast_rows(r, carry):
            rows = pl.ds(pl.multiple_of(r * 128, 128), 128)
            lo, hi = 3 * CW + HW, 3 * CW + 3 * HW
            w_b[rows, :lo] = w_ref[0, rows, :lo].astype(BF16)
            w_b[rows, lo:hi] = (w_ref[0, rows, lo:hi] * LOG2E).astype(BF16)
            w_b[rows, hi:] = w_ref[0, rows, hi:].astype(BF16)
            return carry

        lax.fori_loop(0, D // 128, cast_rows, 0)

    m = mod_ref[0, 0]
    h = _rms(_token_tile(ctx_ref, x_ref, pl.program_id(1)), nw_ref[...] * (1.0 + m[1:2])) + m[0:1]
    hb = h.astype(BF16)
    blk = {k: _dot(hb, w_b[:, k * 512:(k + 1) * 512]) for k in (4, 5, 3, 1, 2, 0, 6, 7)}
    for d in range(2):
        lf2, kk = _forget_gate(blk[4 + d], lb_ref[2 * d:2 * d + 1], lb_ref[2 * d + 1:2 * d + 2])
        lf_ref[0, :, d * HW:(d + 1) * HW] = lf2
        qkv_ref[0, :, (1 + d) * HW:(2 + d) * HW] = kk.astype(BF16)
    qkv_ref[0, :, 0:HW] = _silu(blk[3]).astype(BF16)
    conv_ref[0, :, CW:2 * CW] = blk[1] * blk[2]
    conv_ref[0, :, 0:CW] = blk[0]
    qkv_ref[0, :, 3 * HW:4 * HW] = blk[6].astype(BF16)
    g_ref[0] = blk[7]


def _inproj(x_src, t_all, nw, modall, w_in, lbp, l):
    bsz = x_src[0].shape[0]
    tile = lambda w: pl.BlockSpec((1, TM, w), lambda b, j: (b, j, 0))
    x_specs, x_args = _token_specs(x_src, lambda b, j: (b, j))
    return pl.pallas_call(
        _inproj_kernel,
        grid=(bsz, t_all // TM),
        in_specs=[*x_specs,
                  pl.BlockSpec((1, D), lambda b, j: (0, 0)),
                  pl.BlockSpec((1, 1, N_MOD, D), lambda b, j: (b, jnp.minimum(j, 1), 0, 0)),
                  pl.BlockSpec((1, D, PW), lambda b, j: (l, 0, 0), pipeline_mode=pl.Buffered(1)),
                  pl.BlockSpec((4, HW), lambda b, j: (0, 0))],
        out_specs=[tile(2 * CW), tile(HW), tile(2 * HW), tile(4 * HW)],
        out_shape=[jax.ShapeDtypeStruct((bsz, t_all, 2 * CW), F32),
                   jax.ShapeDtypeStruct((bsz, t_all, HW), F32),
                   jax.ShapeDtypeStruct((bsz, t_all, 2 * HW), F32),
                   jax.ShapeDtypeStruct((bsz, t_all, 4 * HW), BF16)],
        scratch_shapes=[pltpu.VMEM((D, PW), BF16)],
        compiler_params=_cparams(2),
        name="inproj",
    )(*x_args, nw, modall, w_in, lbp)


def _gla_tables():
    t = np.arange(CH)[:, None]
    s = np.arange(CH)[None, :]
    x = t ^ s
    lvl = np.full((CH, CH), -1, np.int32)
    n, k = CH, 0
    while n > LEAF:
        lvl[(x < n) & (x >= n // 2)] = k
        n, k = n // 2, k + 1
    lvl[x < LEAF] = k
    fwd = np.where(s <= t, lvl, -1)
    return (np.stack([s <= t, s >= t]).astype(np.float32), np.stack([fwd, fwd.T]).astype(np.int32))


def _gla_factors(lf2, qh, kh, tri, reverse):
    hi = lf2.astype(BF16)
    lo = (lf2 - hi.astype(F32)).astype(BF16)
    b = _dot(tri, hi) + _dot(tri, lo)

    def block_ref(n, idx):
        r = b.reshape(CH // n, n, HW)[:, idx:idx + 1, :]
        return jnp.broadcast_to(r, (CH // n, n, HW)).reshape(CH, HW)

    pow2 = lambda e: jnp.exp2(e.astype(BF16))
    pieces = []
    n = CH
    while n > LEAF:
        h = n // 2
        a = pow2(_neg_abs(b - block_ref(n, h if reverse else h - 1)))
        pieces.append((qh * a, kh * a))
        n = h
    dl = b - block_ref(LEAF, LEAF // 2)
    pieces.append((qh * pow2(dl), kh * pow2(-dl)))

    edge = 0 if reverse else CH - 1
    b_edge = b[edge:edge + 1, :]
    q_in = qh * pow2(b)
    k_out = kh * pow2(b_edge - b)
    d_chunk = jnp.exp2(b_edge)
    return pieces, q_in, k_out, d_chunk


def _gla_heads(factors, v_refs, rows, lvls, s_refs, o_refs):
    dirs = range(len(factors))
    owned = [[lvls[d] == k for k in range(len(factors[d][0]))] for d in dirs]
    for h0 in range(0, NH, HEAD_GROUP):
        heads = range(h0, h0 + HEAD_GROUP)
        sl = lambda hd: slice(hd * HD, (hd + 1) * HD)
        dots = {(d, hd): [_dot_nt(qa[:, sl(hd)], ka[:, sl(hd)]) for qa, ka in factors[d][0]]
                for hd in heads for d in dirs}
        scs = {}
        for hd in heads:
            for d in dirs:
                sc = jnp.zeros((CH, CH), F32)
                for k, dk in enumerate(dots[d, hd]):
                    sc = jnp.where(owned[d][k], dk, sc)
                scs[d, hd] = sc.astype(BF16)
        for hd in heads:
            for d in dirs:
                _, q_in, k_out, d_chunk = factors[d]
                vb = v_refs[d][0, rows[d], sl(hd)]
                st = s_refs[d][hd]
                o_refs[d][0, rows[d], sl(hd)] = _dot(scs[d, hd], vb) + _dot_nt(q_in[:, sl(hd)], st.astype(BF16))
                s_refs[d][hd] = st * d_chunk[:, sl(hd)] + _dot_tn(vb, k_out[:, sl(hd)])


def _gla_kernel(lff_ref, qf_ref, kf_ref, vf_ref, lfb_ref, qb_ref, kb_ref, vb_ref, tri_ref, lvl_ref, of_ref, ob_ref,
                sf_ref, sb_ref):
    @pl.when(pl.program_id(1) == 0)
    def _():
        sf_ref[...] = jnp.zeros_like(sf_ref)
        sb_ref[...] = jnp.zeros_like(sb_ref)

    for c in range(GLA_SUB):
        rows = (slice(c * CH, (c + 1) * CH), slice((GLA_SUB - 1 - c) * CH, (GLA_SUB - c) * CH))
        factors = (_gla_factors(lff_ref[0, rows[0]], qf_ref[0, rows[0]], kf_ref[0, rows[0]], tri_ref[0], False),
                   _gla_factors(lfb_ref[0, rows[1]], qb_ref[0, rows[1]], kb_ref[0, rows[1]], tri_ref[1], True))
        _gla_heads(factors, (vf_ref, vb_ref), rows, (lvl_ref[0], lvl_ref[1]), (sf_ref, sb_ref), (of_ref, ob_ref))


def _gla(lf, qkv, n_ctx):
    bsz, t_all, _ = lf.shape
    span = GLA_SUB * CH
    assert n_ctx % span == 0 and t_all % span == 0
    nc = t_all // span
    n_ctx_blocks = n_ctx // span

    def cb(j):
        return jnp.where(j < n_ctx_blocks, n_ctx_blocks - 1 - j, nc - 1 - (j - n_ctx_blocks))

    blk = (1, span, HW)
    fwd = lambda c: pl.BlockSpec(blk, lambda b, j: (b, j, c))
    bwd = lambda c: pl.BlockSpec(blk, lambda b, j: (b, cb(j), c))
    tri, lvl = _gla_tables()
    return pl.pallas_call(
        _gla_kernel,
        grid=(bsz, nc),
        in_specs=[fwd(0), fwd(0), fwd(1), fwd(3),
                  bwd(1), bwd(0), bwd(2), bwd(3),
                  pl.BlockSpec((2, CH, CH), lambda b, j: (0, 0, 0)),
                  pl.BlockSpec((2, CH, CH), lambda b, j: (0, 0, 0))],
        out_specs=[fwd(0), bwd(0)],
        out_shape=[jax.ShapeDtypeStruct((bsz, t_all, HW), F32)] * 2,
        scratch_shapes=[pltpu.VMEM((NH, HD, HD), F32), pltpu.VMEM((NH, HD, HD), F32)],
        compiler_params=_cparams(2),
        name="gla",
    )(lf, qkv, qkv, qkv, lf, qkv, qkv, qkv, jnp.asarray(tri, BF16), jnp.asarray(lvl))


def _shift_rows(u, k):
    return pltpu.roll(u, k % u.shape[0], axis=0)


def _mixout_kernel(ctx_ref, x_ref, cu_ref, g_ref, up_ref, un_ref, of_ref, ob_ref, mod_ref, cw_ref, cnw_ref,
                   hnw_ref, wo_ref, nw2_ref, wr_ref, br_ref, xm_ref, h2_ref, slab_ref, runs_ref, cnt_ref, wo_b, wr_b, lg,
                   *, j0, nt):
    i = pl.program_id(0)
    slot = lax.rem(i, 2)
    j = lax.rem(jnp.minimum(i, pl.num_programs(0) - 2), nt - j0) + j0

    @pl.when(i == 0)
    def _():
        cnt_ref[...] = jnp.zeros_like(cnt_ref)
        lg[1] = jnp.zeros_like(lg[1])
        wo_b[...] = wo_ref[0].astype(BF16)
        wr_b[0], wr_b[1] = _split2(wr_ref[...])


    is_ctx = j == 0
    m = mod_ref[0, 0]
    u = cu_ref[0, :, CW:]
    cw = cw_ref[...]
    t = lax.broadcasted_iota(jnp.int32, (TM, 1), 0)
    col_in_row = t & (GRID_W - 1)
    keep_l = jnp.where(is_ctx, jnp.where(t == 0, 0.0, 1.0), jnp.where(col_in_row == 0, 0.0, 1.0))
    keep_r = jnp.where(is_ctx, jnp.where(t == TM - 1, 0.0, 1.0), jnp.where(col_in_row == GRID_W - 1, 0.0, 1.0))
    hc = CW // 2

    def seq_taps(a, w):
        left = jnp.where(keep_l > 0.5, _shift_rows(a, 1), 0.0)
        right = jnp.where(keep_r > 0.5, _shift_rows(a, -1), 0.0)
        return w[0:1] * left + w[1:2] * a + w[2:3] * right

    def col_taps(a, w):
        up = jnp.concatenate([jnp.where(j == 1, 0.0, up_ref[0]), a[:TM - GRID_W]], axis=0)
        dn = jnp.concatenate([a[GRID_W:], jnp.where(j == nt - 1, 0.0, un_ref[0])], axis=0)
        return w[0:1] * up + w[1:2] * a + w[2:3] * dn

    y_second = lax.cond(is_ctx, seq_taps, col_taps, u[:, hc:], cw[:, hc:])
    conv = jnp.concatenate([seq_taps(u[:, :hc], cw[:, :hc]), y_second], axis=-1)
    routed = _route(lg[1 - slot].T, jnp.where(i > 0, 1.0, 0.0), cnt_ref[...])
    y_conv = _rms(cu_ref[0, :, :CW] * conv, cnw_ref[...])

    o = of_ref[0] + ob_ref[0]
    g = g_ref[0]
    hnw = hnw_ref[...]
    recs = []
    for hd in range(NH):
        hs = slice(hd * HD, (hd + 1) * HD)
        recs.append(_rms(o[:, hs], hnw[:, hs]))
    y_rec = jnp.concatenate(recs, axis=-1) * (g * _sigmoid(g))

    y = jnp.concatenate([y_conv, y_rec], axis=-1).astype(BF16)
    xm = _token_tile(ctx_ref, x_ref, j) + m[2:3] * _dot(y, wo_b[...])
    xm_ref[0] = xm
    h2 = _rms(xm, nw2_ref[...] * (1.0 + m[4:5])) + m[3:4]
    h2_ref[0] = h2.astype(BF16)

    h_hi, h_lo = _split2(h2)
    lg[slot] = _dot(h_hi, wr_b[0]) + _dot(h_hi, wr_b[1]) + _dot(h_lo, wr_b[0]) + br_ref[...]
    slab_ref[0], runs_ref[0], cnt_ref[...] = routed


def _route(lt, live, counted):
    ninf = -jnp.inf
    big = 1e9
    over = lambda fn, a: fn(a, axis=0, keepdims=True)
    grow = lax.broadcasted_iota(jnp.int32, (8, TM), 0).astype(F32)
    gl = jnp.where(grow < N_GROUPS, lt[NE:NE + 8], ninf)
    gmax = over(jnp.max, gl)
    gsel = over(jnp.min, jnp.where(gl == gmax, grow, big))
    pg = 1.0 / over(jnp.sum, jnp.exp(gl - gmax))
    erow_i = lax.broadcasted_iota(jnp.int32, (NE, TM), 0)
    erow = erow_i.astype(F32)
    el = jnp.where((erow_i // EPG).astype(F32) == gsel, lt[:NE], ninf)
    e1 = over(jnp.max, el)
    i1 = over(jnp.min, jnp.where(el == e1, erow, big))
    el2 = jnp.where(erow == i1, ninf, el)
    e2 = over(jnp.max, el2)
    i2 = over(jnp.min, jnp.where(el2 == e2, erow, big))
    r = jnp.exp(e2 - e1)
    g1 = pg / (1.0 + r)
    g2 = pg * r / (1.0 + r)
    oh1 = jnp.where(erow == i1, 1.0, 0.0)
    oh2 = jnp.where(erow == i2, 1.0, 0.0)
    cnt = oh1 + oh2
    cnt_b = cnt.astype(BF16)
    ts = lax.broadcasted_iota(jnp.int32, (TM, TM), 0)
    tt = lax.broadcasted_iota(jnp.int32, (TM, TM), 1)
    prior = _dot(cnt_b, jnp.where(ts < tt, 1.0, 0.0).astype(BF16))
    c_col = jnp.sum(cnt, axis=1, keepdims=True)
    ei = lax.broadcasted_iota(jnp.int32, (NE, NE), 0)
    ej = lax.broadcasted_iota(jnp.int32, (NE, NE), 1)
    cc_hi, cc_lo = _split2(jnp.broadcast_to(c_col, (NE, LANES)))
    fewer = jnp.where(ej < ei, 1.0, 0.0).astype(BF16)
    off_col = (_dot(fewer, cc_hi) + _dot(fewer, cc_lo))[:, 0:1]
    pos = prior + off_col
    lpos1 = over(jnp.sum, oh1 * pos)
    lpos2 = over(jnp.sum, oh2 * pos)
    cnt_pad = jnp.concatenate([cnt_b, jnp.zeros((LANES - NE, TM), BF16)], axis=0)
    c_tile = _dot_nt(jnp.ones((8, TM), BF16), cnt_pad)[0:1]
    c_tile = c_tile * live
    li = lax.broadcasted_iota(jnp.int32, (LANES, LANES), 0)
    lj = lax.broadcasted_iota(jnp.int32, (LANES, LANES), 1)
    lower_e = jnp.where(li < lj, 1.0, 0.0).astype(BF16)
    c_hi, c_lo = _split2(jnp.broadcast_to(c_tile, (8, LANES)))
    off = (_dot(c_hi, lower_e) + _dot(c_lo, lower_e))[0:1]
    sub = lax.broadcasted_iota(jnp.int32, (8, LANES), 0)
    runs = jnp.where(sub == 0, c_tile, jnp.where(sub == 1, counted, jnp.where(sub == 2, off, 0.0)))
    frow = lax.broadcasted_iota(jnp.int32, (LANES, TM), 0)
    fields = jnp.zeros((LANES, TM), F32)
    for k, val in enumerate((i1, i2, lpos1, lpos2, g1, g2)):
        fields = jnp.where(frow == k, val, fields)
    return fields.T, runs, counted + c_tile


def _mixout(x_src, conv_in, g_in, o_f, o_b, modall, cw, cnw, hnw, w_out, nw2, w_r, b_r, l, j0):
    bsz, t_all, _ = conv_in.shape
    nt = t_all // TM - j0
    n = bsz * nt
    t_out = nt * TM
    n64 = t_all // GRID_W
    per = TM // GRID_W
    cur = lambda i: jnp.minimum(i, n - 1)
    bj = lambda i: (cur(i) // nt, cur(i) % nt + j0)
    old = lambda i: jnp.maximum(i - 1, 0)
    tile = lambda w, c: pl.BlockSpec((1, TM, w), lambda i, c=c: (*bj(i), c))
    out_tile = lambda w, at: pl.BlockSpec((1, TM, w), lambda i: (at(i) // nt, at(i) % nt, 0))
    full = lambda shape: pl.BlockSpec(shape, lambda i: (0,) * len(shape))
    layer = lambda shape: pl.BlockSpec((1, *shape), lambda i: (l,) + (0,) * len(shape), pipeline_mode=pl.Buffered(1))
    x_specs, x_args = _token_specs(x_src, bj)
    return pl.pallas_call(
        functools.partial(_mixout_kernel, j0=j0, nt=nt + j0),
        grid=(n + 1,),
        in_specs=[*x_specs,
                  tile(2 * CW, 0), tile(HW, 0),
                  pl.BlockSpec((1, GRID_W, CW // 2), lambda i: (bj(i)[0], jnp.maximum(bj(i)[1] * per - 1, 0), 3)),
                  pl.BlockSpec((1, GRID_W, CW // 2),
                               lambda i: (bj(i)[0], jnp.minimum(bj(i)[1] * per + per, n64 - 1), 3)),
                  tile(HW, 0), tile(HW, 0),
                  pl.BlockSpec((1, 1, N_MOD, D), lambda i: (bj(i)[0], jnp.minimum(bj(i)[1], 1), 0, 0)),
                  full((3, CW)), full((1, CW)), full((1, HW)), layer((D, D)), full((1, D)),
                  full((D, LANES)), full((1, LANES))],
        out_specs=[out_tile(D, cur), out_tile(D, cur), out_tile(LANES, old),
                   pl.BlockSpec((1, 8, LANES), lambda i: (old(i), 0, 0)),
                   pl.BlockSpec((1, LANES), lambda i: (0, 0))],
        out_shape=[jax.ShapeDtypeStruct((bsz, t_out, D), F32),
                   jax.ShapeDtypeStruct((bsz, t_out, D), BF16),
                   jax.ShapeDtypeStruct((bsz, t_out, LANES), F32),
                   jax.ShapeDtypeStruct((bsz * nt, 8, LANES), F32),
                   jax.ShapeDtypeStruct((1, LANES), F32)],
        scratch_shapes=[pltpu.VMEM((D, D), BF16), pltpu.VMEM((2, D, LANES), BF16), pltpu.VMEM((2, TM, LANES), F32)],
        compiler_params=_cparams(1),
        name="mixout",
    )(*x_args, conv_in, g_in, conv_in, conv_in, o_f, o_b, modall, cw, cnw, hnw, w_out, nw2, w_r, b_r)


RUN_BITS = TM.bit_length()
LONG_RUN_BIT = 5
ROW_TILE = (8, LANES)


def _run_copies(tab_ref, local_ref, sorted_ref, sem, *, to_sorted, wait):
    def per_expert(e, carry):
        ln = tab_ref[0, 0, e]
        lo = tab_ref[0, 0, NE + e]
        go = tab_ref[0, 0, 2 * NE + e]
        def copy_bits(bits):
            for bit in bits:
                size = 1 << bit
                done = ln & ~(2 * size - 1)

                @pl.when((ln & size) != 0)
                def _():
                    a = local_ref.at[pl.ds(lo + done, size)]
                    b = sorted_ref.at[pl.ds(go + done, size)]
                    cp = pltpu.make_async_copy(a, b, sem) if to_sorted else pltpu.make_async_copy(b, a, sem)
                    if wait:
                        cp.wait()
                    else:
                        cp.start(priority=bit % 2)

        @pl.when(ln >= (1 << LONG_RUN_BIT))
        def _():
            copy_bits(range(RUN_BITS - 1, LONG_RUN_BIT - 1, -1))

        copy_bits(range(LONG_RUN_BIT - 1, -1, -1))
        return carry

    lax.fori_loop(0, NE, per_expert, 0)


def _wait_tile(local_ref, sorted_ref, sem, *, to_sorted):
    whole = sorted_ref.at[pl.ds(0, 2 * TM)]
    (pltpu.make_async_copy(local_ref, whole, sem) if to_sorted else pltpu.make_async_copy(whole, local_ref, sem)).wait()


def _local_positions(slab):
    col = lax.broadcasted_iota(jnp.int32, (TM, 2 * TM), 1).astype(F32)
    return col == slab[:, 2:3], col == slab[:, 3:4]


def _dispatch_kernel(tab_ref, ptab_ref, h2_ref, slab_ref, xs_ref, buf, zbuf, sem):
    @pl.when(pl.program_id(0) == 0)
    def _():
        zbuf[...] = jnp.zeros_like(zbuf)
        _run_copies(ptab_ref, zbuf, xs_ref, sem, to_sorted=True, wait=False)
        _run_copies(ptab_ref, zbuf, xs_ref, sem, to_sorted=True, wait=True)

        def zero_block(i, carry):
            cp = pltpu.make_async_copy(zbuf, xs_ref.at[pl.ds(i * MOE_P, MOE_P)], sem)
            cp.start()
            cp.wait()
            return carry

        lax.fori_loop(ptab_ref[0, 0, 3 * NE], xs_ref.shape[0] // MOE_P, zero_block, 0)

    i = pl.program_id(0)
    slot = lax.rem(i, 2)
    p1, p2 = _local_positions(slab_ref[...])
    perm_t = jnp.where(p1 | p2, 1.0, 0.0).astype(BF16)
    buf[slot] = _dot_tn(perm_t, h2_ref[...]).reshape(2 * TM, *ROW_TILE)

    @pl.when(i > 0)
    def _():
        _wait_tile(buf.at[1 - slot], xs_ref, sem, to_sorted=True)

    _run_copies(tab_ref, buf.at[slot], xs_ref, sem, to_sorted=True, wait=False)

    @pl.when(i == pl.num_programs(0) - 1)
    def _():
        _wait_tile(buf.at[slot], xs_ref, sem, to_sorted=True)


def _dispatch(tab, pad_tab, h2, slab, n_rows):
    n_tiles = tab.shape[0]
    return pl.pallas_call(
        _dispatch_kernel,
        grid=(n_tiles,),
        in_specs=[pl.BlockSpec((1, 1, LANES), lambda i: (i, 0, 0), memory_space=pltpu.SMEM),
                  pl.BlockSpec((1, 1, LANES), lambda i: (0, 0, 0), memory_space=pltpu.SMEM),
                  pl.BlockSpec((TM, D), lambda i: (i, 0)),
                  pl.BlockSpec((TM, LANES), lambda i: (i, 0))],
        out_specs=pl.BlockSpec(memory_space=pl.ANY),
        out_shape=jax.ShapeDtypeStruct((n_rows, *ROW_TILE), F32),
        scratch_shapes=[pltpu.VMEM((2, 2 * TM, *ROW_TILE), F32), pltpu.VMEM((MOE_P, *ROW_TILE), F32),
                        pltpu.SemaphoreType.DMA(())],
        compiler_params=_cparams(1),
        name="dispatch",
    )(tab, pad_tab, h2, slab)


def _expert_kernel(be_ref, nx_ref, nu_ref, xs_ref, wgu_hbm, wd_hbm, ys_ref, wgu_f, wd_f, wgu_b, wd_b, slot_ref, sem,
                   *, layer):
    i = pl.program_id(0)
    used = i < nu_ref[0]
    e = be_ref[i]

    def weight_copies(expert, slot):
        return (pltpu.make_async_copy(wgu_hbm.at[layer, expert], wgu_f.at[slot], sem.at[slot]),
                pltpu.make_async_copy(wd_hbm.at[layer, expert], wd_f.at[slot], sem.at[slot]))

    @pl.when(i == 0)
    def _():
        slot_ref[0] = 0
        for cp in weight_copies(e, 0):
            cp.start()

    @pl.when(jnp.logical_and(used, jnp.logical_or(i == 0, e != be_ref[jnp.maximum(i - 1, 0)])))
    def _():
        slot = jnp.where(i == 0, 0, 1 - slot_ref[0])
        slot_ref[0] = slot
        for cp in weight_copies(e, slot):
            cp.wait()
        wgu_b[...] = wgu_f[slot].astype(BF16)
        wd_b[...] = wd_f[slot].astype(BF16)

        @pl.when(nx_ref[i] >= 0)
        def _():
            for cp in weight_copies(nx_ref[i], 1 - slot):
                cp.start()

    @pl.when(used)
    def _():
        x = xs_ref[...].reshape(MOE_P, D).astype(BF16)
        acts = []
        for c in range(0, DE, EXPERT_CHUNK):
            a = _dot(x, wgu_b[:, c:c + EXPERT_CHUNK])
            u = _dot(x, wgu_b[:, DE + c:DE + c + EXPERT_CHUNK])
            acts.append((_silu(a) * u).astype(BF16))
        y = _dot(jnp.concatenate(acts, axis=1), wd_b[...])
        ys_ref[...] = y.reshape(MOE_P, *ROW_TILE)

    @pl.when(jnp.logical_not(used))
    def _():
        ys_ref[...] = jnp.zeros_like(ys_ref)


def _experts(block_e, next_e, n_used, xs, w_gu, w_down, l):
    n_rows = xs.shape[0]
    grid_spec = pltpu.PrefetchScalarGridSpec(
        num_scalar_prefetch=3,
        grid=(n_rows // MOE_P,),
        in_specs=[pl.BlockSpec((MOE_P, *ROW_TILE), lambda i, be, nx, nu: (jnp.minimum(i, nu[0] - 1), 0, 0)),
                  pl.BlockSpec(memory_space=pl.ANY),
                  pl.BlockSpec(memory_space=pl.ANY)],
        out_specs=pl.BlockSpec((MOE_P, *ROW_TILE), lambda i, be, nx, nu: (i, 0, 0)),
        scratch_shapes=[pltpu.VMEM((2, D, 2 * DE), F32), pltpu.VMEM((2, DE, D), F32),
                        pltpu.VMEM((D, 2 * DE), BF16), pltpu.VMEM((DE, D), BF16),
                        pltpu.SMEM((1,), jnp.int32), pltpu.SemaphoreType.DMA((2,))],
    )
    return pl.pallas_call(
        functools.partial(_expert_kernel, layer=l),
        grid_spec=grid_spec,
        out_shape=jax.ShapeDtypeStruct((n_rows, *ROW_TILE), F32),
        compiler_params=_cparams(1),
        name="experts",
    )(block_e, next_e, n_used, xs, w_gu, w_down)


def _combine_kernel(tab_ref, tab_next_ref, xm_ref, slab_ref, mod_ref, fnw_ref, ys_ref, o_ref, buf, sem, *, final):
    i = pl.program_id(0) * pl.num_programs(1) + pl.program_id(1)
    n = pl.num_programs(0) * pl.num_programs(1)
    slot = lax.rem(i, 2)

    @pl.when(i == 0)
    def _():
        _run_copies(tab_ref, buf.at[0], ys_ref, sem.at[0], to_sorted=False, wait=False)

    @pl.when(i + 1 < n)
    def _():
        _run_copies(tab_next_ref, buf.at[1 - slot], ys_ref, sem.at[1 - slot], to_sorted=False, wait=False)

    _wait_tile(buf.at[slot], ys_ref, sem.at[slot], to_sorted=False)
    slab = slab_ref[0]
    p1, p2 = _local_positions(slab)
    yb = buf[slot].reshape(2 * TM, D).astype(BF16)
    y1 = _dot(jnp.where(p1, 1.0, 0.0).astype(BF16), yb)
    y2 = _dot(jnp.where(p2, 1.0, 0.0).astype(BF16), yb)
    f = slab[:, 4:5] * y1 + slab[:, 5:6] * y2
    x = xm_ref[0] + mod_ref[0, 0][5:6] * f
    o_ref[0] = _rms(x, fnw_ref[...]) if final else x


def _combine(tab, xmid, slab, modall, fnw, ys, final, j0):
    bsz, t_out, _ = xmid.shape
    nt = t_out // TM
    return pl.pallas_call(
        functools.partial(_combine_kernel, final=final),
        grid=(bsz, nt),
        in_specs=[pl.BlockSpec((1, 1, LANES), lambda b, j: (b * nt + j, 0, 0), memory_space=pltpu.SMEM),
                  pl.BlockSpec((1, 1, LANES), lambda b, j: (jnp.minimum(b * nt + j + 1, bsz * nt - 1), 0, 0),
                               memory_space=pltpu.SMEM),
                  pl.BlockSpec((1, TM, D), lambda b, j: (b, j, 0)),
                  pl.BlockSpec((1, TM, LANES), lambda b, j: (b, j, 0)),
                  pl.BlockSpec((1, 1, N_MOD, D), lambda b, j: (b, jnp.minimum(j + j0, 1), 0, 0)),
                  pl.BlockSpec((1, D), lambda b, j: (0, 0)),
                  pl.BlockSpec(memory_space=pl.ANY)],
        out_specs=pl.BlockSpec((1, TM, D), lambda b, j: (b, j, 0)),
        out_shape=jax.ShapeDtypeStruct((bsz, t_out, D), F32),
        scratch_shapes=[pltpu.VMEM((2, 2 * TM, *ROW_TILE), F32), pltpu.SemaphoreType.DMA((2,))],
        compiler_params=_cparams(2),
        name="combine_final" if final else "combine",
    )(tab, tab, xmid, slab, modall, fnw, ys)


def _lower_bounds(hg_lb):
    p = jax.nn.softmax(hg_lb.astype(F32), axis=1)
    cs = jnp.cumsum(p, axis=1)
    return cs - cs[:, :1]


def kernel(x, c, ctx, c_ctx, norm_w, w_ada, b_ada, w_in, conv_w, conv_norm_w, hg_lb, hg_norm_w, w_out, w_rg, b_rg,
           w_re, b_re, w_e_gu, w_e_down, final_norm_w):
    bsz, seq, _ = x.shape
    n_ctx = ctx.shape[1]
    assert n_ctx == TM and seq % TM == 0

    t_all = n_ctx + seq
    x_src = (ctx, x, 1)
    cc = jnp.zeros((16, D), F32).at[:bsz].set(c).at[bsz].set(c_ctx)
    mod = _ada(cc, w_ada, b_ada).reshape(DEPTH, 16, N_MOD, D)
    lb = _lower_bounds(hg_lb)
    lbp = jnp.stack([jnp.log(lb[0]), jnp.log1p(-lb[0]), jnp.log(lb[1]), jnp.log1p(-lb[1])], axis=1) * LOG2E
    w_r = jnp.concatenate([w_re, w_rg, jnp.zeros((DEPTH, D, LANES - NE - N_GROUPS), F32)], axis=-1)
    b_r = jnp.concatenate([b_re, b_rg, jnp.zeros((DEPTH, LANES - NE - N_GROUPS), F32)], axis=-1)

    out = None
    for l in range(DEPTH):
        final = l == DEPTH - 1
        j0 = 1 if final else 0
        modall = jnp.stack([jnp.broadcast_to(mod[l, bsz], (bsz, N_MOD, D)), mod[l, :bsz]], axis=1)
        conv_in, g_in, lf, qkv = _inproj(x_src, t_all, norm_w[l, 0].reshape(1, D), modall, w_in, lbp[l], l)
        o_f, o_b = _gla(lf, qkv, n_ctx)
        xmid, h2, slab, runs, counts = _mixout(
            x_src, conv_in, g_in, o_f, o_b, modall, conv_w[l], conv_norm_w[l].reshape(1, CW),
            jnp.tile(hg_norm_w[l], NH).reshape(1, HW), w_out, norm_w[l, 1].reshape(1, D),
            w_r[l], b_r[l].reshape(1, LANES), l, j0)
        n_tok = xmid.shape[0] * xmid.shape[1]
        n_blocks = -(-(2 * n_tok + NE * (MOE_P - 1)) // MOE_P)
        n_rows = n_blocks * MOE_P
        cnt = counts[0, :NE].astype(jnp.int32)
        padded = (cnt + MOE_P - 1) // MOE_P * MOE_P
        pad_ends = jnp.cumsum(padded)
        pad_starts = pad_ends - padded
        run_len = runs[:, 0, :NE].astype(jnp.int32)
        run_global = pad_starts[None, :] + runs[:, 1, :NE].astype(jnp.int32)
        run_local = runs[:, 2, :NE].astype(jnp.int32)
        tab = jnp.concatenate([run_len, run_local, run_global, jnp.zeros_like(run_len)], axis=1)[:, None, :]
        block_start = jnp.arange(n_blocks, dtype=jnp.int32) * MOE_P
        block_e = jnp.minimum(jnp.sum((pad_ends[None, :] <= block_start[:, None]).astype(jnp.int32), axis=1), NE - 1)
        n_used = (pad_ends[-1:] // MOE_P).astype(jnp.int32)
        zero = jnp.zeros_like(cnt)
        pad_tab = jnp.concatenate([padded - cnt, zero, pad_starts + cnt, zero + n_used]).reshape(1, 1, LANES)
        xs = _dispatch(tab, pad_tab, h2.reshape(n_tok, D), slab.reshape(n_tok, LANES), n_rows)
        ids = jnp.arange(NE, dtype=jnp.int32)
        later = jnp.min(jnp.where((ids[None, :] > block_e[:, None]) & (cnt[None, :] > 0), ids[None, :], NE), axis=1)
        next_e = jnp.where(later < NE, later, -1)
        ys = _experts(block_e, next_e, n_used, xs, w_e_gu, w_e_down, l)
        res = _combine(tab, xmid, slab, modall, final_norm_w.reshape(1, D), ys, final, j0)
        if final:
            out = res
        else:
            x_src = (res, res, 0)
    return out
```

```python
import functools

import jax
import jax.numpy as jnp
import numpy as np
from jax import lax
from jax.experimental import pallas as pl
from jax.experimental.pallas import tpu as pltpu

F32 = jnp.float32
BF16 = jnp.bfloat16

D = 1024
DEPTH = 4
GRID_W = 64
CW = 512
HW = 512
NH = 4
HD = HW // NH
PW = 3 * CW + 5 * HW
N_GROUPS = 4
EPG = 8
NE = N_GROUPS * EPG
DE = 512
N_MOD = 6
EPS = 1e-6
LOG2E = 1.4426950408889634
COL_BLK = 512
assert CW == COL_BLK and HW == COL_BLK
ADA_ROWS = 16
ADA_TN = 1536

TM = 256
CH = 128
LEAF = 16
HEAD_GROUP = 4
GLA_SUB = 2
MOE_P = 256
EXPERT_CHUNK = 256
LANES = 128

VMEM_LIMIT = 56 * 1024 * 1024


def _cparams(n_axes):
    return pltpu.CompilerParams(dimension_semantics=("arbitrary",) * n_axes, vmem_limit_bytes=VMEM_LIMIT)


def _dot(a, b):
    return jnp.dot(a, b, preferred_element_type=F32)


def _dot_nt(a, b):
    return lax.dot_general(a, b, (((1,), (1,)), ((), ())), preferred_element_type=F32)


def _dot_tn(a, b):
    return lax.dot_general(a, b, (((0,), (0,)), ((), ())), preferred_element_type=F32)


def _split2(x):
    hi = x.astype(BF16)
    lo = (x - hi.astype(F32)).astype(BF16)
    return hi, lo


def _dot_hp(a, b):
    a_hi, a_lo = _split2(a)
    b_hi, b_lo = _split2(b)
    return _dot(a_hi, b_hi) + _dot(a_hi, b_lo) + _dot(a_lo, b_hi)


def _sigmoid(x):
    return 1.0 / (1.0 + jnp.exp(-x))


def _rms(x, w):
    return x * lax.rsqrt(jnp.mean(x * x, axis=-1, keepdims=True) + EPS) * w


def _ada_kernel(cc_ref, w_ref, b_ref, o_ref):
    s = cc_ref[...]
    s = s * _sigmoid(s)
    o_ref[0] = _dot_hp(s, w_ref[0]) + b_ref[0]


def _ada(cc, w_ada, b_ada):
    tn = ADA_TN
    n = N_MOD * D
    return pl.pallas_call(
        _ada_kernel,
        grid=(DEPTH, n // tn),
        in_specs=[pl.BlockSpec((ADA_ROWS, D), lambda l, j: (0, 0)),
                  pl.BlockSpec((1, D, tn), lambda l, j: (l, 0, j)),
                  pl.BlockSpec((1, 1, tn), lambda l, j: (l, 0, j))],
        out_specs=pl.BlockSpec((1, ADA_ROWS, tn), lambda l, j: (l, 0, j)),
        out_shape=jax.ShapeDtypeStruct((DEPTH, ADA_ROWS, n), F32),
        compiler_params=_cparams(2),
        name="ada",
    )(cc, w_ada, b_ada.reshape(DEPTH, 1, n))


def _silu(x):
    h = 0.5 * x
    return h + h * jnp.tanh(h)


def _neg_abs(x):
    bits = lax.bitcast_convert_type(x, jnp.uint32) | jnp.uint32(0x80000000)
    return lax.bitcast_convert_type(bits, F32)


def _forget_gate(z2, log2lb, log2_1mlb):
    ls = jnp.minimum(z2, 0.0) - jnp.log2(1.0 + jnp.exp2(_neg_abs(z2)))
    c = log2_1mlb + ls
    lf2 = jnp.maximum(log2lb, c) + jnp.log2(1.0 + jnp.exp2(_neg_abs(log2lb - c)))
    return lf2, jnp.exp2(c - z2)


def _inproj_kernel(x_ref, nw_ref, mod_ref, w_ref, lb_ref, conv_ref, g_ref, lf_ref, qkv_ref, w_b):
    @pl.when(jnp.logical_and(pl.program_id(0) == 0, pl.program_id(1) == 0))
    def _():
        def cast_rows(r, carry):
            rows = pl.ds(pl.multiple_of(r * LANES, LANES), LANES)
            lo, hi = 3 * CW + HW, 3 * CW + 3 * HW
            w_b[rows, :lo] = w_ref[0, rows, :lo].astype(BF16)
            w_b[rows, lo:hi] = (w_ref[0, rows, lo:hi] * LOG2E).astype(BF16)
            w_b[rows, hi:] = w_ref[0, rows, hi:].astype(BF16)
            return carry

        lax.fori_loop(0, D // LANES, cast_rows, 0)

    m = mod_ref[0, 0]
    h = _rms(x_ref[0], nw_ref[...] * (1.0 + m[1:2])) + m[0:1]
    hb = h.astype(BF16)
    blk = {k: _dot(hb, w_b[:, k * COL_BLK:(k + 1) * COL_BLK]) for k in (4, 5, 3, 1, 2, 0, 6, 7)}
    for d in range(2):
        lf2, kk = _forget_gate(blk[4 + d], lb_ref[2 * d:2 * d + 1], lb_ref[2 * d + 1:2 * d + 2])
        lf_ref[0, :, d * HW:(d + 1) * HW] = lf2
        qkv_ref[0, :, (1 + d) * HW:(2 + d) * HW] = kk.astype(BF16)
    qkv_ref[0, :, 0:HW] = _silu(blk[3]).astype(BF16)
    conv_ref[0, :, CW:2 * CW] = blk[1] * blk[2]
    conv_ref[0, :, 0:CW] = blk[0]
    qkv_ref[0, :, 3 * HW:4 * HW] = blk[6].astype(BF16)
    g_ref[0] = blk[7]


def _inproj(xall, nw, modall, w_in, lbp, l):
    bsz, t_all, _ = xall.shape
    tile = lambda w: pl.BlockSpec((1, TM, w), lambda b, j: (b, j, 0))
    return pl.pallas_call(
        _inproj_kernel,
        grid=(bsz, t_all // TM),
        in_specs=[pl.BlockSpec((1, TM, D), lambda b, j: (b, j, 0)),
                  pl.BlockSpec((1, D), lambda b, j: (0, 0)),
                  pl.BlockSpec((1, 1, N_MOD, D), lambda b, j: (b, jnp.minimum(j, 1), 0, 0)),
                  pl.BlockSpec((1, D, PW), lambda b, j: (l, 0, 0), pipeline_mode=pl.Buffered(1)),
                  pl.BlockSpec((4, HW), lambda b, j: (0, 0))],
        out_specs=[tile(2 * CW), tile(HW), tile(2 * HW), tile(4 * HW)],
        out_shape=[jax.ShapeDtypeStruct((bsz, t_all, 2 * CW), F32),
                   jax.ShapeDtypeStruct((bsz, t_all, HW), F32),
                   jax.ShapeDtypeStruct((bsz, t_all, 2 * HW), F32),
                   jax.ShapeDtypeStruct((bsz, t_all, 4 * HW), BF16)],
        scratch_shapes=[pltpu.VMEM((D, PW), BF16)],
        compiler_params=_cparams(2),
        name="inproj",
    )(xall, nw, modall, w_in, lbp)


def _gla_tables():
    t = np.arange(CH)[:, None]
    s = np.arange(CH)[None, :]
    x = t ^ s
    lvl = np.full((CH, CH), -1, np.int32)
    n, k = CH, 0
    while n > LEAF:
        lvl[(x < n) & (x >= n // 2)] = k
        n, k = n // 2, k + 1
    lvl[x < LEAF] = k
    fwd = np.where(s <= t, lvl, -1)
    return (np.stack([s <= t, s >= t]).astype(np.float32), np.stack([fwd, fwd.T]).astype(np.int32))


def _gla_factors(lf2, qh, kh, tri, reverse):
    hi = lf2.astype(BF16)
    lo = (lf2 - hi.astype(F32)).astype(BF16)
    b = _dot(tri, hi) + _dot(tri, lo)

    def block_ref(n, idx):
        r = b.reshape(CH // n, n, HW)[:, idx:idx + 1, :]
        return jnp.broadcast_to(r, (CH // n, n, HW)).reshape(CH, HW)

    pow2 = lambda e: jnp.exp2(e.astype(BF16))
    pieces = []
    n = CH
    while n > LEAF:
        h = n // 2
        a = pow2(_neg_abs(b - block_ref(n, h if reverse else h - 1)))
        pieces.append((qh * a, kh * a))
        n = h
    dl = b - block_ref(LEAF, LEAF // 2)
    pieces.append((qh * pow2(dl), kh * pow2(-dl)))

    edge = 0 if reverse else CH - 1
    b_edge = b[edge:edge + 1, :]
    q_in = qh * pow2(b)
    k_out = kh * pow2(b_edge - b)
    d_chunk = jnp.exp2(b_edge)
    return pieces, q_in, k_out, d_chunk


def _gla_heads(factors, v_refs, rows, lvls, s_refs, o_refs):
    dirs = range(len(factors))
    owned = [[lvls[d] == k for k in range(len(factors[d][0]))] for d in dirs]
    for h0 in range(0, NH, HEAD_GROUP):
        heads = range(h0, h0 + HEAD_GROUP)
        sl = lambda hd: slice(hd * HD, (hd + 1) * HD)
        dots = {(d, hd): [_dot_nt(qa[:, sl(hd)], ka[:, sl(hd)]) for qa, ka in factors[d][0]]
                for hd in heads for d in dirs}
        scs = {}
        for hd in heads:
            for d in dirs:
                sc = jnp.zeros((CH, CH), F32)
                for k, dk in enumerate(dots[d, hd]):
                    sc = jnp.where(owned[d][k], dk, sc)
                scs[d, hd] = sc.astype(BF16)
        for hd in heads:
            for d in dirs:
                _, q_in, k_out, d_chunk = factors[d]
                vb = v_refs[d][0, rows[d], sl(hd)]
                st = s_refs[d][hd]
                o_refs[d][0, rows[d], sl(hd)] = _dot(scs[d, hd], vb) + _dot_nt(q_in[:, sl(hd)], st.astype(BF16))
                s_refs[d][hd] = st * d_chunk[:, sl(hd)] + _dot_tn(vb, k_out[:, sl(hd)])


def _gla_kernel(lff_ref, qf_ref, kf_ref, vf_ref, lfb_ref, qb_ref, kb_ref, vb_ref, tri_ref, lvl_ref, of_ref, ob_ref,
                sf_ref, sb_ref):
    @pl.when(pl.program_id(1) == 0)
    def _():
        sf_ref[...] = jnp.zeros_like(sf_ref)
        sb_ref[...] = jnp.zeros_like(sb_ref)

    for c in range(GLA_SUB):
        rows = (slice(c * CH, (c + 1) * CH), slice((GLA_SUB - 1 - c) * CH, (GLA_SUB - c) * CH))
        factors = (_gla_factors(lff_ref[0, rows[0]], qf_ref[0, rows[0]], kf_ref[0, rows[0]], tri_ref[0], False),
                   _gla_factors(lfb_ref[0, rows[1]], qb_ref[0, rows[1]], kb_ref[0, rows[1]], tri_ref[1], True))
        _gla_heads(factors, (vf_ref, vb_ref), rows, (lvl_ref[0], lvl_ref[1]), (sf_ref, sb_ref), (of_ref, ob_ref))


def _gla(lf, qkv, n_ctx):
    bsz, t_all, _ = lf.shape
    span = GLA_SUB * CH
    assert n_ctx % span == 0 and t_all % span == 0
    nc = t_all // span
    n_ctx_blocks = n_ctx // span

    def cb(j):
        return jnp.where(j < n_ctx_blocks, n_ctx_blocks - 1 - j, nc - 1 - (j - n_ctx_blocks))

    blk = (1, span, HW)
    fwd = lambda c: pl.BlockSpec(blk, lambda b, j: (b, j, c))
    bwd = lambda c: pl.BlockSpec(blk, lambda b, j: (b, cb(j), c))
    tri, lvl = _gla_tables()
    return pl.pallas_call(
        _gla_kernel,
        grid=(bsz, nc),
        in_specs=[fwd(0), fwd(0), fwd(1), fwd(3),
                  bwd(1), bwd(0), bwd(2), bwd(3),
                  pl.BlockSpec((2, CH, CH), lambda b, j: (0, 0, 0)),
                  pl.BlockSpec((2, CH, CH), lambda b, j: (0, 0, 0))],
        out_specs=[fwd(0), bwd(0)],
        out_shape=[jax.ShapeDtypeStruct((bsz, t_all, HW), F32)] * 2,
        scratch_shapes=[pltpu.VMEM((NH, HD, HD), F32), pltpu.VMEM((NH, HD, HD), F32)],
        compiler_params=_cparams(2),
        name="gla",
    )(lf, qkv, qkv, qkv, lf, qkv, qkv, qkv, jnp.asarray(tri, BF16), jnp.asarray(lvl))


def _shift_rows(u, k):
    return pltpu.roll(u, k % u.shape[0], axis=0)


def _mixout_kernel(x_ref, bg_ref, u_ref, g_ref, up_ref, un_ref, of_ref, ob_ref, mod_ref, cw_ref, cnw_ref,
                   hnw_ref, wo_ref, nw2_ref, wr_ref, br_ref, xm_ref, h2_ref, slab_ref, runs_ref, cnt_ref, wo_b, wr_b, lg,
                   *, j0, nt):
    i = pl.program_id(0)
    slot = lax.rem(i, 2)
    j = lax.rem(jnp.minimum(i, pl.num_programs(0) - 2), nt - j0) + j0

    @pl.when(i == 0)
    def _():
        cnt_ref[...] = jnp.zeros_like(cnt_ref)
        lg[1] = jnp.zeros_like(lg[1])
        wo_b[...] = wo_ref[0].astype(BF16)
        wr_b[0], wr_b[1] = _split2(wr_ref[...])

    is_ctx = j == 0
    m = mod_ref[0, 0]
    u = u_ref[0]
    cw = cw_ref[...]
    t = lax.broadcasted_iota(jnp.int32, (TM, 1), 0)
    col_in_row = t & (GRID_W - 1)
    keep_l = jnp.where(is_ctx, jnp.where(t == 0, 0.0, 1.0), jnp.where(col_in_row == 0, 0.0, 1.0))
    keep_r = jnp.where(is_ctx, jnp.where(t == TM - 1, 0.0, 1.0), jnp.where(col_in_row == GRID_W - 1, 0.0, 1.0))
    hc = CW // 2

    def seq_taps(a, w):
        left = jnp.where(keep_l > 0.5, _shift_rows(a, 1), 0.0)
        right = jnp.where(keep_r > 0.5, _shift_rows(a, -1), 0.0)
        return w[0:1] * left + w[1:2] * a + w[2:3] * right

    def col_taps(a, w):
        up = jnp.concatenate([jnp.where(j == 1, 0.0, up_ref[0]), a[:TM - GRID_W]], axis=0)
        dn = jnp.concatenate([a[GRID_W:], jnp.where(j == nt - 1, 0.0, un_ref[0])], axis=0)
        return w[0:1] * up + w[1:2] * a + w[2:3] * dn

    y_second = lax.cond(is_ctx, seq_taps, col_taps, u[:, hc:], cw[:, hc:])
    conv = jnp.concatenate([seq_taps(u[:, :hc], cw[:, :hc]), y_second], axis=-1)
    routed = _route(lg[1 - slot].T, jnp.where(i > 0, 1.0, 0.0), cnt_ref[...])
    y_conv = _rms(bg_ref[0] * conv, cnw_ref[...])

    o = of_ref[0] + ob_ref[0]
    g = g_ref[0]
    hnw = hnw_ref[...]
    recs = []
    for hd in range(NH):
        hs = slice(hd * HD, (hd + 1) * HD)
        recs.append(_rms(o[:, hs], hnw[:, hs]))
    y_rec = jnp.concatenate(recs, axis=-1) * _silu(g)

    y = jnp.concatenate([y_conv, y_rec], axis=-1).astype(BF16)
    xm = x_ref[0] + m[2:3] * _dot(y, wo_b[...])
    xm_ref[0] = xm
    h2 = _rms(xm, nw2_ref[...] * (1.0 + m[4:5])) + m[3:4]
    h2_ref[0] = h2.astype(BF16)

    h_hi, h_lo = _split2(h2)
    lg[slot] = _dot(h_hi, wr_b[0]) + _dot(h_hi, wr_b[1]) + _dot(h_lo, wr_b[0]) + br_ref[...]
    slab_ref[0], runs_ref[0], cnt_ref[...] = routed


def _route(lt, live, counted):
    ninf = -jnp.inf
    big = 1e9
    over = lambda fn, a: fn(a, axis=0, keepdims=True)
    grow = lax.broadcasted_iota(jnp.int32, (8, TM), 0).astype(F32)
    gl = jnp.where(grow < N_GROUPS, lt[NE:NE + 8], ninf)
    gmax = over(jnp.max, gl)
    gsel = over(jnp.min, jnp.where(gl == gmax, grow, big))
    pg = 1.0 / over(jnp.sum, jnp.exp(gl - gmax))
    erow_i = lax.broadcasted_iota(jnp.int32, (NE, TM), 0)
    erow = erow_i.astype(F32)
    el = jnp.where((erow_i // EPG).astype(F32) == gsel, lt[:NE], ninf)
    e1 = over(jnp.max, el)
    i1 = over(jnp.min, jnp.where(el == e1, erow, big))
    el2 = jnp.where(erow == i1, ninf, el)
    e2 = over(jnp.max, el2)
    i2 = over(jnp.min, jnp.where(el2 == e2, erow, big))
    r = jnp.exp(e2 - e1)
    g1 = pg / (1.0 + r)
    g2 = pg * r / (1.0 + r)
    oh1 = jnp.where(erow == i1, 1.0, 0.0)
    oh2 = jnp.where(erow == i2, 1.0, 0.0)
    cnt = oh1 + oh2
    cnt_b = cnt.astype(BF16)
    ts = lax.broadcasted_iota(jnp.int32, (TM, TM), 0)
    tt = lax.broadcasted_iota(jnp.int32, (TM, TM), 1)
    prior = _dot(cnt_b, jnp.where(ts < tt, 1.0, 0.0).astype(BF16))
    c_col = jnp.sum(cnt, axis=1, keepdims=True)
    ei = lax.broadcasted_iota(jnp.int32, (NE, NE), 0)
    ej = lax.broadcasted_iota(jnp.int32, (NE, NE), 1)
    cc_hi, cc_lo = _split2(jnp.broadcast_to(c_col, (NE, LANES)))
    fewer = jnp.where(ej < ei, 1.0, 0.0).astype(BF16)
    off_col = (_dot(fewer, cc_hi) + _dot(fewer, cc_lo))[:, 0:1]
    pos = prior + off_col
    lpos1 = over(jnp.sum, oh1 * pos)
    lpos2 = over(jnp.sum, oh2 * pos)
    cnt_pad = jnp.concatenate([cnt_b, jnp.zeros((LANES - NE, TM), BF16)], axis=0)
    c_tile = _dot_nt(jnp.ones((8, TM), BF16), cnt_pad)[0:1]
    c_tile = c_tile * live
    li = lax.broadcasted_iota(jnp.int32, (LANES, LANES), 0)
    lj = lax.broadcasted_iota(jnp.int32, (LANES, LANES), 1)
    lower_e = jnp.where(li < lj, 1.0, 0.0).astype(BF16)
    c_hi, c_lo = _split2(jnp.broadcast_to(c_tile, (8, LANES)))
    off = (_dot(c_hi, lower_e) + _dot(c_lo, lower_e))[0:1]
    sub = lax.broadcasted_iota(jnp.int32, (8, LANES), 0)
    runs = jnp.where(sub == 0, c_tile, jnp.where(sub == 1, counted, jnp.where(sub == 2, off, 0.0)))
    frow = lax.broadcasted_iota(jnp.int32, (LANES, TM), 0)
    fields = jnp.zeros((LANES, TM), F32)
    for k, val in enumerate((i1, i2, lpos1, lpos2, g1, g2)):
        fields = jnp.where(frow == k, val, fields)
    return fields.T, runs, counted + c_tile


def _mixout(xall, conv_in, g_in, o_f, o_b, modall, cw, cnw, hnw, w_out, nw2, w_r, b_r, l, j0):
    bsz, t_all, _ = xall.shape
    nt = t_all // TM - j0
    n = bsz * nt
    t_out = nt * TM
    n64 = t_all // GRID_W
    per = TM // GRID_W
    cur = lambda i: jnp.minimum(i, n - 1)
    bj = lambda i: (cur(i) // nt, cur(i) % nt + j0)
    old = lambda i: jnp.maximum(i - 1, 0)
    tile = lambda w, c: pl.BlockSpec((1, TM, w), lambda i, c=c: (*bj(i), c))
    out_tile = lambda w, at: pl.BlockSpec((1, TM, w), lambda i: (at(i) // nt, at(i) % nt, 0))
    full = lambda shape: pl.BlockSpec(shape, lambda i: (0,) * len(shape))
    layer = lambda shape: pl.BlockSpec((1, *shape), lambda i: (l,) + (0,) * len(shape), pipeline_mode=pl.Buffered(1))
    return pl.pallas_call(
        functools.partial(_mixout_kernel, j0=j0, nt=nt + j0),
        grid=(n + 1,),
        in_specs=[tile(D, 0),
                  tile(CW, 0), tile(CW, 1), tile(HW, 0),
                  pl.BlockSpec((1, GRID_W, CW // 2), lambda i: (bj(i)[0], jnp.maximum(bj(i)[1] * per - 1, 0), 3)),
                  pl.BlockSpec((1, GRID_W, CW // 2),
                               lambda i: (bj(i)[0], jnp.minimum(bj(i)[1] * per + per, n64 - 1), 3)),
                  tile(HW, 0), tile(HW, 0),
                  pl.BlockSpec((1, 1, N_MOD, D), lambda i: (bj(i)[0], jnp.minimum(bj(i)[1], 1), 0, 0)),
                  full((3, CW)), full((1, CW)), full((1, HW)), layer((D, D)), full((1, D)),
                  full((D, LANES)), full((1, LANES))],
        out_specs=[out_tile(D, cur), out_tile(D, cur), out_tile(LANES, old),
                   pl.BlockSpec((1, 8, LANES), lambda i: (old(i), 0, 0)),
                   pl.BlockSpec((1, LANES), lambda i: (0, 0))],
        out_shape=[jax.ShapeDtypeStruct((bsz, t_out, D), F32),
                   jax.ShapeDtypeStruct((bsz, t_out, D), BF16),
                   jax.ShapeDtypeStruct((bsz, t_out, LANES), F32),
                   jax.ShapeDtypeStruct((bsz * nt, 8, LANES), F32),
                   jax.ShapeDtypeStruct((1, LANES), F32)],
        scratch_shapes=[pltpu.VMEM((D, D), BF16), pltpu.VMEM((2, D, LANES), BF16), pltpu.VMEM((2, TM, LANES), F32)],
        compiler_params=_cparams(1),
        name="mixout",
    )(xall, conv_in, conv_in, g_in, conv_in, conv_in, o_f, o_b, modall, cw, cnw, hnw, w_out, nw2, w_r, b_r)


RUN_BITS = TM.bit_length()
LONG_RUN_BIT = 5
ROW_TILE = (8, LANES)


def _run_copies(tab_ref, local_ref, sorted_ref, sem, *, to_sorted, wait):
    def per_expert(e, carry):
        ln = tab_ref[0, 0, e]
        lo = tab_ref[0, 0, NE + e]
        go = tab_ref[0, 0, 2 * NE + e]
        def copy_bits(bits):
            for bit in bits:
                size = 1 << bit
                done = ln & ~(2 * size - 1)

                @pl.when((ln & size) != 0)
                def _():
                    a = local_ref.at[pl.ds(lo + done, size)]
                    b = sorted_ref.at[pl.ds(go + done, size)]
                    cp = pltpu.make_async_copy(a, b, sem) if to_sorted else pltpu.make_async_copy(b, a, sem)
                    if wait:
                        cp.wait()
                    else:
                        cp.start(priority=bit % 2)

        @pl.when(ln >= (1 << LONG_RUN_BIT))
        def _():
            copy_bits(range(RUN_BITS - 1, LONG_RUN_BIT - 1, -1))

        copy_bits(range(LONG_RUN_BIT - 1, -1, -1))
        return carry

    lax.fori_loop(0, NE, per_expert, 0)


def _wait_tile(local_ref, sorted_ref, sem, *, to_sorted):
    whole = sorted_ref.at[pl.ds(0, 2 * TM)]
    (pltpu.make_async_copy(local_ref, whole, sem) if to_sorted else pltpu.make_async_copy(whole, local_ref, sem)).wait()


def _local_positions(slab):
    col = lax.broadcasted_iota(jnp.int32, (TM, 2 * TM), 1).astype(F32)
    return col == slab[:, 2:3], col == slab[:, 3:4]


def _dispatch_kernel(tab_ref, ptab_ref, h2_ref, slab_ref, xs_ref, buf, zbuf, sem):
    @pl.when(pl.program_id(0) == 0)
    def _():
        zbuf[...] = jnp.zeros_like(zbuf)
        _run_copies(ptab_ref, zbuf, xs_ref, sem, to_sorted=True, wait=False)
        _run_copies(ptab_ref, zbuf, xs_ref, sem, to_sorted=True, wait=True)

        def zero_block(i, carry):
            cp = pltpu.make_async_copy(zbuf, xs_ref.at[pl.ds(i * MOE_P, MOE_P)], sem)
            cp.start()
            cp.wait()
            return carry

        lax.fori_loop(ptab_ref[0, 0, 3 * NE], xs_ref.shape[0] // MOE_P, zero_block, 0)

    i = pl.program_id(0)
    slot = lax.rem(i, 2)
    p1, p2 = _local_positions(slab_ref[...])
    perm_t = jnp.where(p1 | p2, 1.0, 0.0).astype(BF16)
    buf[slot] = _dot_tn(perm_t, h2_ref[...]).reshape(2 * TM, *ROW_TILE)

    @pl.when(i > 0)
    def _():
        _wait_tile(buf.at[1 - slot], xs_ref, sem, to_sorted=True)

    _run_copies(tab_ref, buf.at[slot], xs_ref, sem, to_sorted=True, wait=False)

    @pl.when(i == pl.num_programs(0) - 1)
    def _():
        _wait_tile(buf.at[slot], xs_ref, sem, to_sorted=True)


def _dispatch(tab, pad_tab, h2, slab, n_rows):
    n_tiles = tab.shape[0]
    return pl.pallas_call(
        _dispatch_kernel,
        grid=(n_tiles,),
        in_specs=[pl.BlockSpec((1, 1, LANES), lambda i: (i, 0, 0), memory_space=pltpu.SMEM),
                  pl.BlockSpec((1, 1, LANES), lambda i: (0, 0, 0), memory_space=pltpu.SMEM),
                  pl.BlockSpec((TM, D), lambda i: (i, 0)),
                  pl.BlockSpec((TM, LANES), lambda i: (i, 0))],
        out_specs=pl.BlockSpec(memory_space=pl.ANY),
        out_shape=jax.ShapeDtypeStruct((n_rows, *ROW_TILE), F32),
        scratch_shapes=[pltpu.VMEM((2, 2 * TM, *ROW_TILE), F32), pltpu.VMEM((MOE_P, *ROW_TILE), F32),
                        pltpu.SemaphoreType.DMA(())],
        compiler_params=_cparams(1),
        name="dispatch",
    )(tab, pad_tab, h2, slab)


def _expert_kernel(be_ref, nx_ref, nu_ref, xs_ref, wgu_hbm, wd_hbm, ys_ref, wgu_f, wd_f, wgu_b, wd_b, slot_ref, sem,
                   *, layer):
    i = pl.program_id(0)
    used = i < nu_ref[0]
    e = be_ref[i]

    def weight_copies(expert, slot):
        return (pltpu.make_async_copy(wgu_hbm.at[layer, expert], wgu_f.at[slot], sem.at[slot]),
                pltpu.make_async_copy(wd_hbm.at[layer, expert], wd_f.at[slot], sem.at[slot]))

    @pl.when(i == 0)
    def _():
        slot_ref[0] = 0
        for cp in weight_copies(e, 0):
            cp.start()

    @pl.when(jnp.logical_and(used, jnp.logical_or(i == 0, e != be_ref[jnp.maximum(i - 1, 0)])))
    def _():
        slot = jnp.where(i == 0, 0, 1 - slot_ref[0])
        slot_ref[0] = slot
        for cp in weight_copies(e, slot):
            cp.wait()
        wgu_b[...] = wgu_f[slot].astype(BF16)
        wd_b[...] = wd_f[slot].astype(BF16)

        @pl.when(nx_ref[i] >= 0)
        def _():
            for cp in weight_copies(nx_ref[i], 1 - slot):
                cp.start()

    @pl.when(used)
    def _():
        x = xs_ref[...].reshape(MOE_P, D).astype(BF16)
        acts = []
        for c in range(0, DE, EXPERT_CHUNK):
            a = _dot(x, wgu_b[:, c:c + EXPERT_CHUNK])
            u = _dot(x, wgu_b[:, DE + c:DE + c + EXPERT_CHUNK])
            acts.append((_silu(a) * u).astype(BF16))
        y = _dot(jnp.concatenate(acts, axis=1), wd_b[...])
        ys_ref[...] = y.reshape(MOE_P, *ROW_TILE)

    @pl.when(jnp.logical_not(used))
    def _():
        ys_ref[...] = jnp.zeros_like(ys_ref)


def _experts(block_e, next_e, n_used, xs, w_gu, w_down, l):
    n_rows = xs.shape[0]
    grid_spec = pltpu.PrefetchScalarGridSpec(
        num_scalar_prefetch=3,
        grid=(n_rows // MOE_P,),
        in_specs=[pl.BlockSpec((MOE_P, *ROW_TILE), lambda i, be, nx, nu: (jnp.minimum(i, nu[0] - 1), 0, 0)),
                  pl.BlockSpec(memory_space=pl.ANY),
                  pl.BlockSpec(memory_space=pl.ANY)],
        out_specs=pl.BlockSpec((MOE_P, *ROW_TILE), lambda i, be, nx, nu: (i, 0, 0)),
        scratch_shapes=[pltpu.VMEM((2, D, 2 * DE), F32), pltpu.VMEM((2, DE, D), F32),
                        pltpu.VMEM((D, 2 * DE), BF16), pltpu.VMEM((DE, D), BF16),
                        pltpu.SMEM((1,), jnp.int32), pltpu.SemaphoreType.DMA((2,))],
    )
    return pl.pallas_call(
        functools.partial(_expert_kernel, layer=l),
        grid_spec=grid_spec,
        out_shape=jax.ShapeDtypeStruct((n_rows, *ROW_TILE), F32),
        compiler_params=_cparams(1),
        name="experts",
    )(block_e, next_e, n_used, xs, w_gu, w_down)


def _combine_kernel(tab_ref, tab_next_ref, xm_ref, slab_ref, mod_ref, fnw_ref, ys_ref, o_ref, buf, sem, *, final):
    i = pl.program_id(0) * pl.num_programs(1) + pl.program_id(1)
    n = pl.num_programs(0) * pl.num_programs(1)
    slot = lax.rem(i, 2)

    @pl.when(i == 0)
    def _():
        _run_copies(tab_ref, buf.at[0], ys_ref, sem.at[0], to_sorted=False, wait=False)

    @pl.when(i + 1 < n)
    def _():
        _run_copies(tab_next_ref, buf.at[1 - slot], ys_ref, sem.at[1 - slot], to_sorted=False, wait=False)

    _wait_tile(buf.at[slot], ys_ref, sem.at[slot], to_sorted=False)
    slab = slab_ref[0]
    p1, p2 = _local_positions(slab)
    yb = buf[slot].reshape(2 * TM, D).astype(BF16)
    y1 = _dot(jnp.where(p1, 1.0, 0.0).astype(BF16), yb)
    y2 = _dot(jnp.where(p2, 1.0, 0.0).astype(BF16), yb)
    f = slab[:, 4:5] * y1 + slab[:, 5:6] * y2
    x = xm_ref[0] + mod_ref[0, 0][5:6] * f
    o_ref[0] = _rms(x, fnw_ref[...]) if final else x


def _combine(tab, xmid, slab, modall, fnw, ys, final, j0):
    bsz, t_out, _ = xmid.shape
    nt = t_out // TM
    return pl.pallas_call(
        functools.partial(_combine_kernel, final=final),
        grid=(bsz, nt),
        in_specs=[pl.BlockSpec((1, 1, LANES), lambda b, j: (b * nt + j, 0, 0), memory_space=pltpu.SMEM),
                  pl.BlockSpec((1, 1, LANES), lambda b, j: (jnp.minimum(b * nt + j + 1, bsz * nt - 1), 0, 0),
                               memory_space=pltpu.SMEM),
                  pl.BlockSpec((1, TM, D), lambda b, j: (b, j, 0)),
                  pl.BlockSpec((1, TM, LANES), lambda b, j: (b, j, 0)),
                  pl.BlockSpec((1, 1, N_MOD, D), lambda b, j: (b, jnp.minimum(j + j0, 1), 0, 0)),
                  pl.BlockSpec((1, D), lambda b, j: (0, 0)),
                  pl.BlockSpec(memory_space=pl.ANY)],
        out_specs=pl.BlockSpec((1, TM, D), lambda b, j: (b, j, 0)),
        out_shape=jax.ShapeDtypeStruct((bsz, t_out, D), F32),
        scratch_shapes=[pltpu.VMEM((2, 2 * TM, *ROW_TILE), F32), pltpu.SemaphoreType.DMA((2,))],
        compiler_params=_cparams(2),
        name="combine_final" if final else "combine",
    )(tab, tab, xmid, slab, modall, fnw, ys)


def _lower_bounds(hg_lb):
    p = jax.nn.softmax(hg_lb.astype(F32), axis=1)
    cs = jnp.cumsum(p, axis=1)
    return cs - cs[:, :1]


def kernel(x, c, ctx, c_ctx, norm_w, w_ada, b_ada, w_in, conv_w, conv_norm_w, hg_lb, hg_norm_w, w_out, w_rg, b_rg,
           w_re, b_re, w_e_gu, w_e_down, final_norm_w):
    bsz, seq, _ = x.shape
    n_ctx = ctx.shape[1]
    assert n_ctx == TM and seq % TM == 0

    xall = jnp.concatenate([ctx, x], axis=1)
    assert bsz < ADA_ROWS
    cc = jnp.zeros((ADA_ROWS, D), F32).at[:bsz].set(c).at[bsz].set(c_ctx)
    mod = _ada(cc, w_ada, b_ada).reshape(DEPTH, ADA_ROWS, N_MOD, D)
    lb = _lower_bounds(hg_lb)
    lbp = jnp.stack([jnp.log(lb[0]), jnp.log1p(-lb[0]), jnp.log(lb[1]), jnp.log1p(-lb[1])], axis=1) * LOG2E
    w_r = jnp.concatenate([w_re, w_rg, jnp.zeros((DEPTH, D, LANES - NE - N_GROUPS), F32)], axis=-1)
    b_r = jnp.concatenate([b_re, b_rg, jnp.zeros((DEPTH, LANES - NE - N_GROUPS), F32)], axis=-1)

    out = None
    for l in range(DEPTH):
        final = l == DEPTH - 1
        j0 = 1 if final else 0
        modall = jnp.stack([jnp.broadcast_to(mod[l, bsz], (bsz, N_MOD, D)), mod[l, :bsz]], axis=1)
        conv_in, g_in, lf, qkv = _inproj(xall, norm_w[l, 0].reshape(1, D), modall, w_in, lbp[l], l)
        o_f, o_b = _gla(lf, qkv, n_ctx)
        xmid, h2, slab, runs, counts = _mixout(
            xall, conv_in, g_in, o_f, o_b, modall, conv_w[l], conv_norm_w[l].reshape(1, CW),
            jnp.tile(hg_norm_w[l], NH).reshape(1, HW), w_out, norm_w[l, 1].reshape(1, D),
            w_r[l], b_r[l].reshape(1, LANES), l, j0)
        n_tok = xmid.shape[0] * xmid.shape[1]
        n_blocks = -(-(2 * n_tok + NE * (MOE_P - 1)) // MOE_P)
        n_rows = n_blocks * MOE_P
        cnt = counts[0, :NE].astype(jnp.int32)
        padded = (cnt + MOE_P - 1) // MOE_P * MOE_P
        pad_ends = jnp.cumsum(padded)
        pad_starts = pad_ends - padded
        run_len = runs[:, 0, :NE].astype(jnp.int32)
        run_global = pad_starts[None, :] + runs[:, 1, :NE].astype(jnp.int32)
        run_local = runs[:, 2, :NE].astype(jnp.int32)
        tab = jnp.concatenate([run_len, run_local, run_global, jnp.zeros_like(run_len)], axis=1)[:, None, :]
        block_start = jnp.arange(n_blocks, dtype=jnp.int32) * MOE_P
        block_e = jnp.minimum(jnp.sum((pad_ends[None, :] <= block_start[:, None]).astype(jnp.int32), axis=1), NE - 1)
        n_used = (pad_ends[-1:] // MOE_P).astype(jnp.int32)
        zero = jnp.zeros_like(cnt)
        pad_tab = jnp.concatenate([padded - cnt, zero, pad_starts + cnt, zero + n_used]).reshape(1, 1, LANES)
        xs = _dispatch(tab, pad_tab, h2.reshape(n_tok, D), slab.reshape(n_tok, LANES), n_rows)
        ids = jnp.arange(NE, dtype=jnp.int32)
        later = jnp.min(jnp.where((ids[None, :] > block_e[:, None]) & (cnt[None, :] > 0), ids[None, :], NE), axis=1)
        next_e = jnp.where(later < NE, later, -1)
        ys = _experts(block_e, next_e, n_used, xs, w_e_gu, w_e_down, l)
        res = _combine(tab, xmid, slab, modall, final_norm_w.reshape(1, D), ys, final, j0)
        if final:
            out = res
        else:
            xall = res
    return out
```

```python
import functools

import jax
import jax.numpy as jnp
import numpy as np
from jax import lax
from jax.experimental import pallas as pl
from jax.experimental.pallas import tpu as pltpu

F32 = jnp.float32
BF16 = jnp.bfloat16

D = 1024
DEPTH = 4
GRID_W = 64
CW = 512
HW = 512
NH = 4
HD = HW // NH
PW = 3 * CW + 5 * HW
N_GROUPS = 4
EPG = 8
NE = N_GROUPS * EPG
DE = 512
N_MOD = 6
EPS = 1e-6
LOG2E = 1.4426950408889634
COL_BLK = 512
assert CW == COL_BLK and HW == COL_BLK
ADA_ROWS = 16
ADA_TN = 1536

TM = 256
CH = 128
LEAF = 16
HEAD_GROUP = 4
GLA_SUB = 2
MOE_P = 256
EXPERT_CHUNK = 256
LANES = 128

VMEM_LIMIT = 56 * 1024 * 1024


def _cparams(n_axes):
    return pltpu.CompilerParams(dimension_semantics=("arbitrary",) * n_axes, vmem_limit_bytes=VMEM_LIMIT)


def _dot(a, b):
    return jnp.dot(a, b, preferred_element_type=F32)


def _dot_nt(a, b):
    return lax.dot_general(a, b, (((1,), (1,)), ((), ())), preferred_element_type=F32)


def _dot_tn(a, b):
    return lax.dot_general(a, b, (((0,), (0,)), ((), ())), preferred_element_type=F32)


def _split2(x):
    hi = x.astype(BF16)
    lo = (x - hi.astype(F32)).astype(BF16)
    return hi, lo


def _dot_hp(a, b):
    a_hi, a_lo = _split2(a)
    b_hi, b_lo = _split2(b)
    return _dot(a_hi, b_hi) + _dot(a_hi, b_lo) + _dot(a_lo, b_hi)


def _sigmoid(x):
    return 1.0 / (1.0 + jnp.exp(-x))


def _rms(x, w):
    return x * lax.rsqrt(jnp.mean(x * x, axis=-1, keepdims=True) + EPS) * w


def _ada_kernel(cc_ref, w_ref, b_ref, o_ref):
    s = cc_ref[...]
    s = s * _sigmoid(s)
    o_ref[0] = _dot_hp(s, w_ref[0]) + b_ref[0]


def _ada(cc, w_ada, b_ada):
    tn = ADA_TN
    n = N_MOD * D
    return pl.pallas_call(
        _ada_kernel,
        grid=(DEPTH, n // tn),
        in_specs=[pl.BlockSpec((ADA_ROWS, D), lambda l, j: (0, 0)),
                  pl.BlockSpec((1, D, tn), lambda l, j: (l, 0, j)),
                  pl.BlockSpec((1, 1, tn), lambda l, j: (l, 0, j))],
        out_specs=pl.BlockSpec((1, ADA_ROWS, tn), lambda l, j: (l, 0, j)),
        out_shape=jax.ShapeDtypeStruct((DEPTH, ADA_ROWS, n), F32),
        compiler_params=_cparams(2),
        name="ada",
    )(cc, w_ada, b_ada.reshape(DEPTH, 1, n))


def _silu(x):
    h = 0.5 * x
    return h + h * jnp.tanh(h)


def _neg_abs(x):
    bits = lax.bitcast_convert_type(x, jnp.uint32) | jnp.uint32(0x80000000)
    return lax.bitcast_convert_type(bits, F32)


def _forget_gate(z2, log2lb, log2_1mlb):
    ls = jnp.minimum(z2, 0.0) - jnp.log2(1.0 + jnp.exp2(_neg_abs(z2)))
    c = log2_1mlb + ls
    lf2 = jnp.maximum(log2lb, c) + jnp.log2(1.0 + jnp.exp2(_neg_abs(log2lb - c)))
    return lf2, jnp.exp2(c - z2)


def _inproj_kernel(x_ref, nw_ref, mod_ref, w_ref, lb_ref, conv_ref, g_ref, lf_ref, qkv_ref, w_b):
    @pl.when(jnp.logical_and(pl.program_id(0) == 0, pl.program_id(1) == 0))
    def _():
        def cast_rows(r, carry):
            rows = pl.ds(pl.multiple_of(r * LANES, LANES), LANES)
            lo, hi = 3 * CW + HW, 3 * CW + 3 * HW
            w_b[rows, :lo] = w_ref[0, rows, :lo].astype(BF16)
            w_b[rows, lo:hi] = (w_ref[0, rows, lo:hi] * LOG2E).astype(BF16)
            w_b[rows, hi:] = w_ref[0, rows, hi:].astype(BF16)
            return carry

        lax.fori_loop(0, D // LANES, cast_rows, 0)

    m = mod_ref[0, 0]
    h = _rms(x_ref[0], nw_ref[...] * (1.0 + m[1:2])) + m[0:1]
    hb = h.astype(BF16)
    blk = {k: _dot(hb, w_b[:, k * COL_BLK:(k + 1) * COL_BLK]) for k in (4, 5, 3, 1, 2, 0, 6, 7)}
    for d in range(2):
        lf2, kk = _forget_gate(blk[4 + d], lb_ref[2 * d:2 * d + 1], lb_ref[2 * d + 1:2 * d + 2])
        lf_ref[0, :, d * HW:(d + 1) * HW] = lf2
        qkv_ref[0, :, (1 + d) * HW:(2 + d) * HW] = kk.astype(BF16)
    qkv_ref[0, :, 0:HW] = _silu(blk[3]).astype(BF16)
    conv_ref[0, :, CW:2 * CW] = blk[1] * blk[2]
    conv_ref[0, :, 0:CW] = blk[0]
    qkv_ref[0, :, 3 * HW:4 * HW] = blk[6].astype(BF16)
    g_ref[0] = blk[7]


def _inproj(xall, nw, modall, w_in, lbp, l):
    bsz, t_all, _ = xall.shape
    tile = lambda w: pl.BlockSpec((1, TM, w), lambda b, j: (b, j, 0))
    return pl.pallas_call(
        _inproj_kernel,
        grid=(bsz, t_all // TM),
        in_specs=[pl.BlockSpec((1, TM, D), lambda b, j: (b, j, 0)),
                  pl.BlockSpec((1, D), lambda b, j: (0, 0)),
                  pl.BlockSpec((1, 1, N_MOD, D), lambda b, j: (b, jnp.minimum(j, 1), 0, 0)),
                  pl.BlockSpec((1, D, PW), lambda b, j: (l, 0, 0), pipeline_mode=pl.Buffered(1)),
                  pl.BlockSpec((4, HW), lambda b, j: (0, 0))],
        out_specs=[tile(2 * CW), tile(HW), tile(2 * HW), tile(4 * HW)],
        out_shape=[jax.ShapeDtypeStruct((bsz, t_all, 2 * CW), F32),
                   jax.ShapeDtypeStruct((bsz, t_all, HW), F32),
                   jax.ShapeDtypeStruct((bsz, t_all, 2 * HW), F32),
                   jax.ShapeDtypeStruct((bsz, t_all, 4 * HW), BF16)],
        scratch_shapes=[pltpu.VMEM((D, PW), BF16)],
        compiler_params=_cparams(2),
        name="inproj",
    )(xall, nw, modall, w_in, lbp)


def _gla_tables():
    t = np.arange(CH)[:, None]
    s = np.arange(CH)[None, :]
    x = t ^ s
    lvl = np.full((CH, CH), -1, np.int32)
    n, k = CH, 0
    while n > LEAF:
        lvl[(x < n) & (x >= n // 2)] = k
        n, k = n // 2, k + 1
    lvl[x < LEAF] = k
    fwd = np.where(s <= t, lvl, -1)
    return (np.stack([s <= t, s >= t]).astype(np.float32), np.stack([fwd, fwd.T]).astype(np.int32))


def _gla_factors(lf2, qh, kh, tri, reverse):
    hi = lf2.astype(BF16)
    lo = (lf2 - hi.astype(F32)).astype(BF16)
    b = _dot(tri, hi) + _dot(tri, lo)

    def block_ref(n, idx):
        r = b.reshape(CH // n, n, HW)[:, idx:idx + 1, :]
        return jnp.broadcast_to(r, (CH // n, n, HW)).reshape(CH, HW)

    qf = qh.astype(F32)
    kf = kh.astype(F32)
    scaled = lambda x, e: (x * jnp.exp2(e)).astype(BF16)
    pieces = []
    n = CH
    while n > LEAF:
        h = n // 2
        a = jnp.exp2(_neg_abs(b - block_ref(n, h if reverse else h - 1)))
        pieces.append(((qf * a).astype(BF16), (kf * a).astype(BF16)))
        n = h
    dl = b - block_ref(LEAF, LEAF // 2)
    pieces.append((scaled(qf, dl), scaled(kf, -dl)))

    edge = 0 if reverse else CH - 1
    b_edge = b[edge:edge + 1, :]
    q_in = scaled(qf, b)
    k_out = scaled(kf, b_edge - b)
    d_chunk = jnp.exp2(b_edge)
    return pieces, q_in, k_out, d_chunk


def _gla_heads(factors, v_refs, rows, lvls, s_refs, o_refs):
    dirs = range(len(factors))
    owned = [[lvls[d] == k for k in range(len(factors[d][0]))] for d in dirs]
    for h0 in range(0, NH, HEAD_GROUP):
        heads = range(h0, h0 + HEAD_GROUP)
        sl = lambda hd: slice(hd * HD, (hd + 1) * HD)
        dots = {(d, hd): [_dot_nt(qa[:, sl(hd)], ka[:, sl(hd)]) for qa, ka in factors[d][0]]
                for hd in heads for d in dirs}
        scs = {}
        for hd in heads:
            for d in dirs:
                sc = jnp.zeros((CH, CH), F32)
                for k, dk in enumerate(dots[d, hd]):
                    sc = jnp.where(owned[d][k], dk, sc)
                scs[d, hd] = sc.astype(BF16)
        for hd in heads:
            for d in dirs:
                _, q_in, k_out, d_chunk = factors[d]
                vb = v_refs[d][0, rows[d], sl(hd)]
                st = s_refs[d][hd]
                o_refs[d][0, rows[d], sl(hd)] = _dot(scs[d, hd], vb) + _dot_nt(q_in[:, sl(hd)], st.astype(BF16))
                s_refs[d][hd] = st * d_chunk[:, sl(hd)] + _dot_tn(vb, k_out[:, sl(hd)])


def _gla_kernel(lff_ref, qf_ref, kf_ref, vf_ref, lfb_ref, qb_ref, kb_ref, vb_ref, tri_ref, lvl_ref, of_ref, ob_ref,
                sf_ref, sb_ref):
    @pl.when(pl.program_id(1) == 0)
    def _():
        sf_ref[...] = jnp.zeros_like(sf_ref)
        sb_ref[...] = jnp.zeros_like(sb_ref)

    for c in range(GLA_SUB):
        rows = (slice(c * CH, (c + 1) * CH), slice((GLA_SUB - 1 - c) * CH, (GLA_SUB - c) * CH))
        factors = (_gla_factors(lff_ref[0, rows[0]], qf_ref[0, rows[0]], kf_ref[0, rows[0]], tri_ref[0], False),
                   _gla_factors(lfb_ref[0, rows[1]], qb_ref[0, rows[1]], kb_ref[0, rows[1]], tri_ref[1], True))
        _gla_heads(factors, (vf_ref, vb_ref), rows, (lvl_ref[0], lvl_ref[1]), (sf_ref, sb_ref), (of_ref, ob_ref))


def _gla(lf, qkv, n_ctx):
    bsz, t_all, _ = lf.shape
    span = GLA_SUB * CH
    assert n_ctx % span == 0 and t_all % span == 0
    nc = t_all // span
    n_ctx_blocks = n_ctx // span

    def cb(j):
        return jnp.where(j < n_ctx_blocks, n_ctx_blocks - 1 - j, nc - 1 - (j - n_ctx_blocks))

    blk = (1, span, HW)
    fwd = lambda c: pl.BlockSpec(blk, lambda b, j: (b, j, c))
    bwd = lambda c: pl.BlockSpec(blk, lambda b, j: (b, cb(j), c))
    tri, lvl = _gla_tables()
    return pl.pallas_call(
        _gla_kernel,
        grid=(bsz, nc),
        in_specs=[fwd(0), fwd(0), fwd(1), fwd(3),
                  bwd(1), bwd(0), bwd(2), bwd(3),
                  pl.BlockSpec((2, CH, CH), lambda b, j: (0, 0, 0)),
                  pl.BlockSpec((2, CH, CH), lambda b, j: (0, 0, 0))],
        out_specs=[fwd(0), bwd(0)],
        out_shape=[jax.ShapeDtypeStruct((bsz, t_all, HW), F32)] * 2,
        scratch_shapes=[pltpu.VMEM((NH, HD, HD), F32), pltpu.VMEM((NH, HD, HD), F32)],
        compiler_params=_cparams(2),
        name="gla",
    )(lf, qkv, qkv, qkv, lf, qkv, qkv, qkv, jnp.asarray(tri, BF16), jnp.asarray(lvl))


def _shift_rows(u, k):
    return pltpu.roll(u, k % u.shape[0], axis=0)


def _mixout_kernel(x_ref, bg_ref, u_ref, g_ref, up_ref, un_ref, of_ref, ob_ref, mod_ref, cw_ref, cnw_ref,
                   hnw_ref, wo_ref, nw2_ref, wr_ref, br_ref, xm_ref, h2_ref, slab_ref, runs_ref, cnt_ref, wo_b, wr_b, lg,
                   *, j0, nt):
    i = pl.program_id(0)
    slot = lax.rem(i, 2)
    j = lax.rem(jnp.minimum(i, pl.num_programs(0) - 2), nt - j0) + j0

    @pl.when(i == 0)
    def _():
        cnt_ref[...] = jnp.zeros_like(cnt_ref)
        lg[1] = jnp.zeros_like(lg[1])
        wo_b[...] = wo_ref[0].astype(BF16)
        wr_b[0], wr_b[1] = _split2(wr_ref[...])

    is_ctx = j == 0
    m = mod_ref[0, 0]
    u = u_ref[0]
    cw = cw_ref[...]
    t = lax.broadcasted_iota(jnp.int32, (TM, 1), 0)
    col_in_row = t & (GRID_W - 1)
    keep_l = jnp.where(is_ctx, jnp.where(t == 0, 0.0, 1.0), jnp.where(col_in_row == 0, 0.0, 1.0))
    keep_r = jnp.where(is_ctx, jnp.where(t == TM - 1, 0.0, 1.0), jnp.where(col_in_row == GRID_W - 1, 0.0, 1.0))
    hc = CW // 2

    def seq_taps(a, w):
        left = jnp.where(keep_l > 0.5, _shift_rows(a, 1), 0.0)
        right = jnp.where(keep_r > 0.5, _shift_rows(a, -1), 0.0)
        return w[0:1] * left + w[1:2] * a + w[2:3] * right

    def col_taps(a, w):
        up = jnp.concatenate([jnp.where(j == 1, 0.0, up_ref[0]), a[:TM - GRID_W]], axis=0)
        dn = jnp.concatenate([a[GRID_W:], jnp.where(j == nt - 1, 0.0, un_ref[0])], axis=0)
        return w[0:1] * up + w[1:2] * a + w[2:3] * dn

    y_second = lax.cond(is_ctx, seq_taps, col_taps, u[:, hc:], cw[:, hc:])
    conv = jnp.concatenate([seq_taps(u[:, :hc], cw[:, :hc]), y_second], axis=-1)
    routed = _route(lg[1 - slot].T, jnp.where(i > 0, 1.0, 0.0), cnt_ref[...])
    y_conv = _rms(bg_ref[0] * conv, cnw_ref[...])

    o = of_ref[0] + ob_ref[0]
    g = g_ref[0]
    hnw = hnw_ref[...]
    recs = []
    for hd in range(NH):
        hs = slice(hd * HD, (hd + 1) * HD)
        recs.append(_rms(o[:, hs], hnw[:, hs]))
    y_rec = jnp.concatenate(recs, axis=-1) * _silu(g)

    y = jnp.concatenate([y_conv, y_rec], axis=-1).astype(BF16)
    xm = x_ref[0] + m[2:3] * _dot(y, wo_b[...])
    xm_ref[0] = xm
    h2 = _rms(xm, nw2_ref[...] * (1.0 + m[4:5])) + m[3:4]
    h2_ref[0] = h2.astype(BF16)

    h_hi, h_lo = _split2(h2)
    lg[slot] = _dot(h_hi, wr_b[0]) + _dot(h_hi, wr_b[1]) + _dot(h_lo, wr_b[0]) + br_ref[...]
    slab_ref[0], runs_ref[0], cnt_ref[...] = routed


def _route(lt, live, counted):
    ninf = -jnp.inf
    big = 1e9
    over = lambda fn, a: fn(a, axis=0, keepdims=True)
    grow = lax.broadcasted_iota(jnp.int32, (8, TM), 0).astype(F32)
    gl = jnp.where(grow < N_GROUPS, lt[NE:NE + 8], ninf)
    gmax = over(jnp.max, gl)
    gsel = over(jnp.min, jnp.where(gl == gmax, grow, big))
    pg = 1.0 / over(jnp.sum, jnp.exp(gl - gmax))
    erow_i = lax.broadcasted_iota(jnp.int32, (NE, TM), 0)
    erow = erow_i.astype(F32)
    el = jnp.where((erow_i // EPG).astype(F32) == gsel, lt[:NE], ninf)
    e1 = over(jnp.max, el)
    i1 = over(jnp.min, jnp.where(el == e1, erow, big))
    el2 = jnp.where(erow == i1, ninf, el)
    e2 = over(jnp.max, el2)
    i2 = over(jnp.min, jnp.where(el2 == e2, erow, big))
    r = jnp.exp(e2 - e1)
    g1 = pg / (1.0 + r)
    g2 = pg * r / (1.0 + r)
    oh1 = jnp.where(erow == i1, 1.0, 0.0)
    oh2 = jnp.where(erow == i2, 1.0, 0.0)
    cnt = oh1 + oh2
    cnt_b = cnt.astype(BF16)
    ts = lax.broadcasted_iota(jnp.int32, (TM, TM), 0)
    tt = lax.broadcasted_iota(jnp.int32, (TM, TM), 1)
    prior = _dot(cnt_b, jnp.where(ts < tt, 1.0, 0.0).astype(BF16))
    c_col = jnp.sum(cnt, axis=1, keepdims=True)
    ei = lax.broadcasted_iota(jnp.int32, (NE, NE), 0)
    ej = lax.broadcasted_iota(jnp.int32, (NE, NE), 1)
    cc_hi, cc_lo = _split2(jnp.broadcast_to(c_col, (NE, LANES)))
    fewer = jnp.where(ej < ei, 1.0, 0.0).astype(BF16)
    off_col = (_dot(fewer, cc_hi) + _dot(fewer, cc_lo))[:, 0:1]
    pos = prior + off_col
    lpos1 = over(jnp.sum, oh1 * pos)
    lpos2 = over(jnp.sum, oh2 * pos)
    cnt_pad = jnp.concatenate([cnt_b, jnp.zeros((LANES - NE, TM), BF16)], axis=0)
    c_tile = _dot_nt(jnp.ones((8, TM), BF16), cnt_pad)[0:1]
    c_tile = c_tile * live
    li = lax.broadcasted_iota(jnp.int32, (LANES, LANES), 0)
    lj = lax.broadcasted_iota(jnp.int32, (LANES, LANES), 1)
    lower_e = jnp.where(li < lj, 1.0, 0.0).astype(BF16)
    c_hi, c_lo = _split2(jnp.broadcast_to(c_tile, (8, LANES)))
    off = (_dot(c_hi, lower_e) + _dot(c_lo, lower_e))[0:1]
    sub = lax.broadcasted_iota(jnp.int32, (8, LANES), 0)
    runs = jnp.where(sub == 0, c_tile, jnp.where(sub == 1, counted, jnp.where(sub == 2, off, 0.0)))
    frow = lax.broadcasted_iota(jnp.int32, (LANES, TM), 0)
    fields = jnp.zeros((LANES, TM), F32)
    for k, val in enumerate((i1, i2, lpos1, lpos2, g1, g2)):
        fields = jnp.where(frow == k, val, fields)
    return fields.T, runs, counted + c_tile


def _mixout(xall, conv_in, g_in, o_f, o_b, modall, cw, cnw, hnw, w_out, nw2, w_r, b_r, l, j0):
    bsz, t_all, _ = xall.shape
    nt = t_all // TM - j0
    n = bsz * nt
    t_out = nt * TM
    n64 = t_all // GRID_W
    per = TM // GRID_W
    cur = lambda i: jnp.minimum(i, n - 1)
    bj = lambda i: (cur(i) // nt, cur(i) % nt + j0)
    old = lambda i: jnp.maximum(i - 1, 0)
    tile = lambda w, c: pl.BlockSpec((1, TM, w), lambda i, c=c: (*bj(i), c))
    out_tile = lambda w, at: pl.BlockSpec((1, TM, w), lambda i: (at(i) // nt, at(i) % nt, 0))
    full = lambda shape: pl.BlockSpec(shape, lambda i: (0,) * len(shape))
    layer = lambda shape: pl.BlockSpec((1, *shape), lambda i: (l,) + (0,) * len(shape), pipeline_mode=pl.Buffered(1))
    return pl.pallas_call(
        functools.partial(_mixout_kernel, j0=j0, nt=nt + j0),
        grid=(n + 1,),
        in_specs=[tile(D, 0),
                  tile(CW, 0), tile(CW, 1), tile(HW, 0),
                  pl.BlockSpec((1, GRID_W, CW // 2), lambda i: (bj(i)[0], jnp.maximum(bj(i)[1] * per - 1, 0), 3)),
                  pl.BlockSpec((1, GRID_W, CW // 2),
                               lambda i: (bj(i)[0], jnp.minimum(bj(i)[1] * per + per, n64 - 1), 3)),
                  tile(HW, 0), tile(HW, 0),
                  pl.BlockSpec((1, 1, N_MOD, D), lambda i: (bj(i)[0], jnp.minimum(bj(i)[1], 1), 0, 0)),
                  full((3, CW)), full((1, CW)), full((1, HW)), layer((D, D)), full((1, D)),
                  full((D, LANES)), full((1, LANES))],
        out_specs=[out_tile(D, cur), out_tile(D, cur), out_tile(LANES, old),
                   pl.BlockSpec((1, 8, LANES), lambda i: (old(i), 0, 0)),
                   pl.BlockSpec((1, LANES), lambda i: (0, 0))],
        out_shape=[jax.ShapeDtypeStruct((bsz, t_out, D), F32),
                   jax.ShapeDtypeStruct((bsz, t_out, D), BF16),
                   jax.ShapeDtypeStruct((bsz, t_out, LANES), F32),
                   jax.ShapeDtypeStruct((bsz * nt, 8, LANES), F32),
                   jax.ShapeDtypeStruct((1, LANES), F32)],
        scratch_shapes=[pltpu.VMEM((D, D), BF16), pltpu.VMEM((2, D, LANES), BF16), pltpu.VMEM((2, TM, LANES), F32)],
        compiler_params=_cparams(1),
        name="mixout",
    )(xall, conv_in, conv_in, g_in, conv_in, conv_in, o_f, o_b, modall, cw, cnw, hnw, w_out, nw2, w_r, b_r)


RUN_BITS = TM.bit_length()
LONG_RUN_BIT = 5
ROW_TILE = (8, LANES)


def _run_copies(tab_ref, local_ref, sorted_ref, sem, *, to_sorted, wait):
    def per_expert(e, carry):
        ln = tab_ref[0, 0, e]
        lo = tab_ref[0, 0, NE + e]
        go = tab_ref[0, 0, 2 * NE + e]
        def copy_bits(bits):
            for bit in bits:
                size = 1 << bit
                done = ln & ~(2 * size - 1)

                @pl.when((ln & size) != 0)
                def _():
                    a = local_ref.at[pl.ds(lo + done, size)]
                    b = sorted_ref.at[pl.ds(go + done, size)]
                    cp = pltpu.make_async_copy(a, b, sem) if to_sorted else pltpu.make_async_copy(b, a, sem)
                    if wait:
                        cp.wait()
                    else:
                        cp.start(priority=bit % 2)

        @pl.when(ln >= (1 << LONG_RUN_BIT))
        def _():
            copy_bits(range(RUN_BITS - 1, LONG_RUN_BIT - 1, -1))

        copy_bits(range(LONG_RUN_BIT - 1, -1, -1))
        return carry

    lax.fori_loop(0, NE, per_expert, 0)


def _wait_tile(local_ref, sorted_ref, sem, *, to_sorted):
    whole = sorted_ref.at[pl.ds(0, 2 * TM)]
    (pltpu.make_async_copy(local_ref, whole, sem) if to_sorted else pltpu.make_async_copy(whole, local_ref, sem)).wait()


def _local_positions(slab):
    col = lax.broadcasted_iota(jnp.int32, (TM, 2 * TM), 1).astype(F32)
    return col == slab[:, 2:3], col == slab[:, 3:4]


def _dispatch_kernel(tab_ref, ptab_ref, h2_ref, slab_ref, xs_ref, buf, zbuf, sem):
    @pl.when(pl.program_id(0) == 0)
    def _():
        zbuf[...] = jnp.zeros_like(zbuf)
        _run_copies(ptab_ref, zbuf, xs_ref, sem, to_sorted=True, wait=False)
        _run_copies(ptab_ref, zbuf, xs_ref, sem, to_sorted=True, wait=True)

        def zero_block(i, carry):
            cp = pltpu.make_async_copy(zbuf, xs_ref.at[pl.ds(i * MOE_P, MOE_P)], sem)
            cp.start()
            cp.wait()
            return carry

        lax.fori_loop(ptab_ref[0, 0, 3 * NE], xs_ref.shape[0] // MOE_P, zero_block, 0)

    i = pl.program_id(0)
    slot = lax.rem(i, 2)
    p1, p2 = _local_positions(slab_ref[...])
    perm_t = jnp.where(p1 | p2, 1.0, 0.0).astype(BF16)
    buf[slot] = _dot_tn(perm_t, h2_ref[...]).reshape(2 * TM, *ROW_TILE)

    @pl.when(i > 0)
    def _():
        _wait_tile(buf.at[1 - slot], xs_ref, sem, to_sorted=True)

    _run_copies(tab_ref, buf.at[slot], xs_ref, sem, to_sorted=True, wait=False)

    @pl.when(i == pl.num_programs(0) - 1)
    def _():
        _wait_tile(buf.at[slot], xs_ref, sem, to_sorted=True)


def _dispatch(tab, pad_tab, h2, slab, n_rows):
    n_tiles = tab.shape[0]
    return pl.pallas_call(
        _dispatch_kernel,
        grid=(n_tiles,),
        in_specs=[pl.BlockSpec((1, 1, LANES), lambda i: (i, 0, 0), memory_space=pltpu.SMEM),
                  pl.BlockSpec((1, 1, LANES), lambda i: (0, 0, 0), memory_space=pltpu.SMEM),
                  pl.BlockSpec((TM, D), lambda i: (i, 0)),
                  pl.BlockSpec((TM, LANES), lambda i: (i, 0))],
        out_specs=pl.BlockSpec(memory_space=pl.ANY),
        out_shape=jax.ShapeDtypeStruct((n_rows, *ROW_TILE), F32),
        scratch_shapes=[pltpu.VMEM((2, 2 * TM, *ROW_TILE), F32), pltpu.VMEM((MOE_P, *ROW_TILE), F32),
                        pltpu.SemaphoreType.DMA(())],
        compiler_params=_cparams(1),
        name="dispatch",
    )(tab, pad_tab, h2, slab)


def _expert_kernel(be_ref, nx_ref, nu_ref, xs_ref, wgu_hbm, wd_hbm, ys_ref, wgu_f, wd_f, wgu_b, wd_b, slot_ref, sem,
                   *, layer):
    i = pl.program_id(0)
    used = i < nu_ref[0]
    e = be_ref[i]

    def weight_copies(expert, slot):
        return (pltpu.make_async_copy(wgu_hbm.at[layer, expert], wgu_f.at[slot], sem.at[slot]),
                pltpu.make_async_copy(wd_hbm.at[layer, expert], wd_f.at[slot], sem.at[slot]))

    @pl.when(i == 0)
    def _():
        slot_ref[0] = 0
        for cp in weight_copies(e, 0):
            cp.start()

    @pl.when(jnp.logical_and(used, jnp.logical_or(i == 0, e != be_ref[jnp.maximum(i - 1, 0)])))
    def _():
        slot = jnp.where(i == 0, 0, 1 - slot_ref[0])
        slot_ref[0] = slot
        for cp in weight_copies(e, slot):
            cp.wait()
        wgu_b[...] = wgu_f[slot].astype(BF16)
        wd_b[...] = wd_f[slot].astype(BF16)

        @pl.when(nx_ref[i] >= 0)
        def _():
            for cp in weight_copies(nx_ref[i], 1 - slot):
                cp.start()

    @pl.when(used)
    def _():
        x = xs_ref[...].reshape(MOE_P, D).astype(BF16)
        acts = []
        for c in range(0, DE, EXPERT_CHUNK):
            a = _dot(x, wgu_b[:, c:c + EXPERT_CHUNK])
            u = _dot(x, wgu_b[:, DE + c:DE + c + EXPERT_CHUNK])
            acts.append((_silu(a) * u).astype(BF16))
        y = _dot(jnp.concatenate(acts, axis=1), wd_b[...])
        ys_ref[...] = y.reshape(MOE_P, *ROW_TILE)

    @pl.when(jnp.logical_not(used))
    def _():
        ys_ref[...] = jnp.zeros_like(ys_ref)


def _experts(block_e, next_e, n_used, xs, w_gu, w_down, l):
    n_rows = xs.shape[0]
    grid_spec = pltpu.PrefetchScalarGridSpec(
        num_scalar_prefetch=3,
        grid=(n_rows // MOE_P,),
        in_specs=[pl.BlockSpec((MOE_P, *ROW_TILE), lambda i, be, nx, nu: (jnp.minimum(i, nu[0] - 1), 0, 0)),
                  pl.BlockSpec(memory_space=pl.ANY),
                  pl.BlockSpec(memory_space=pl.ANY)],
        out_specs=pl.BlockSpec((MOE_P, *ROW_TILE), lambda i, be, nx, nu: (i, 0, 0)),
        scratch_shapes=[pltpu.VMEM((2, D, 2 * DE), F32), pltpu.VMEM((2, DE, D), F32),
                        pltpu.VMEM((D, 2 * DE), BF16), pltpu.VMEM((DE, D), BF16),
                        pltpu.SMEM((1,), jnp.int32), pltpu.SemaphoreType.DMA((2,))],
    )
    return pl.pallas_call(
        functools.partial(_expert_kernel, layer=l),
        grid_spec=grid_spec,
        out_shape=jax.ShapeDtypeStruct((n_rows, *ROW_TILE), F32),
        compiler_params=_cparams(1),
        name="experts",
    )(block_e, next_e, n_used, xs, w_gu, w_down)


def _combine_kernel(tab_ref, tab_next_ref, xm_ref, slab_ref, mod_ref, fnw_ref, ys_ref, o_ref, buf, sem, *, final):
    i = pl.program_id(0) * pl.num_programs(1) + pl.program_id(1)
    n = pl.num_programs(0) * pl.num_programs(1)
    slot = lax.rem(i, 2)

    @pl.when(i == 0)
    def _():
        _run_copies(tab_ref, buf.at[0], ys_ref, sem.at[0], to_sorted=False, wait=False)

    @pl.when(i + 1 < n)
    def _():
        _run_copies(tab_next_ref, buf.at[1 - slot], ys_ref, sem.at[1 - slot], to_sorted=False, wait=False)

    _wait_tile(buf.at[slot], ys_ref, sem.at[slot], to_sorted=False)
    slab = slab_ref[0]
    p1, p2 = _local_positions(slab)
    yb = buf[slot].reshape(2 * TM, D).astype(BF16)
    y1 = _dot(jnp.where(p1, 1.0, 0.0).astype(BF16), yb)
    y2 = _dot(jnp.where(p2, 1.0, 0.0).astype(BF16), yb)
    f = slab[:, 4:5] * y1 + slab[:, 5:6] * y2
    x = xm_ref[0] + mod_ref[0, 0][5:6] * f
    o_ref[0] = _rms(x, fnw_ref[...]) if final else x


def _combine(tab, xmid, slab, modall, fnw, ys, final, j0):
    bsz, t_out, _ = xmid.shape
    nt = t_out // TM
    return pl.pallas_call(
        functools.partial(_combine_kernel, final=final),
        grid=(bsz, nt),
        in_specs=[pl.BlockSpec((1, 1, LANES), lambda b, j: (b * nt + j, 0, 0), memory_space=pltpu.SMEM),
                  pl.BlockSpec((1, 1, LANES), lambda b, j: (jnp.minimum(b * nt + j + 1, bsz * nt - 1), 0, 0),
                               memory_space=pltpu.SMEM),
                  pl.BlockSpec((1, TM, D), lambda b, j: (b, j, 0)),
                  pl.BlockSpec((1, TM, LANES), lambda b, j: (b, j, 0)),
                  pl.BlockSpec((1, 1, N_MOD, D), lambda b, j: (b, jnp.minimum(j + j0, 1), 0, 0)),
                  pl.BlockSpec((1, D), lambda b, j: (0, 0)),
                  pl.BlockSpec(memory_space=pl.ANY)],
        out_specs=pl.BlockSpec((1, TM, D), lambda b, j: (b, j, 0)),
        out_shape=jax.ShapeDtypeStruct((bsz, t_out, D), F32),
        scratch_shapes=[pltpu.VMEM((2, 2 * TM, *ROW_TILE), F32), pltpu.SemaphoreType.DMA((2,))],
        compiler_params=_cparams(2),
        name="combine_final" if final else "combine",
    )(tab, tab, xmid, slab, modall, fnw, ys)


def _lower_bounds(hg_lb):
    p = jax.nn.softmax(hg_lb.astype(F32), axis=1)
    cs = jnp.cumsum(p, axis=1)
    return cs - cs[:, :1]


def kernel(x, c, ctx, c_ctx, norm_w, w_ada, b_ada, w_in, conv_w, conv_norm_w, hg_lb, hg_norm_w, w_out, w_rg, b_rg,
           w_re, b_re, w_e_gu, w_e_down, final_norm_w):
    bsz, seq, _ = x.shape
    n_ctx = ctx.shape[1]
    assert n_ctx == TM and seq % TM == 0

    xall = jnp.concatenate([ctx, x], axis=1)
    assert bsz < ADA_ROWS
    cc = jnp.zeros((ADA_ROWS, D), F32).at[:bsz].set(c).at[bsz].set(c_ctx)
    mod = _ada(cc, w_ada, b_ada).reshape(DEPTH, ADA_ROWS, N_MOD, D)
    lb = _lower_bounds(hg_lb)
    lbp = jnp.stack([jnp.log(lb[0]), jnp.log1p(-lb[0]), jnp.log(lb[1]), jnp.log1p(-lb[1])], axis=1) * LOG2E
    w_r = jnp.concatenate([w_re, w_rg, jnp.zeros((DEPTH, D, LANES - NE - N_GROUPS), F32)], axis=-1)
    b_r = jnp.concatenate([b_re, b_rg, jnp.zeros((DEPTH, LANES - NE - N_GROUPS), F32)], axis=-1)

    out = None
    for l in range(DEPTH):
        final = l == DEPTH - 1
        j0 = 1 if final else 0
        modall = jnp.stack([jnp.broadcast_to(mod[l, bsz], (bsz, N_MOD, D)), mod[l, :bsz]], axis=1)
        conv_in, g_in, lf, qkv = _inproj(xall, norm_w[l, 0].reshape(1, D), modall, w_in, lbp[l], l)
        o_f, o_b = _gla(lf, qkv, n_ctx)
        xmid, h2, slab, runs, counts = _mixout(
            xall, conv_in, g_in, o_f, o_b, modall, conv_w[l], conv_norm_w[l].reshape(1, CW),
            jnp.tile(hg_norm_w[l], NH).reshape(1, HW), w_out, norm_w[l, 1].reshape(1, D),
            w_r[l], b_r[l].reshape(1, LANES), l, j0)
        n_tok = xmid.shape[0] * xmid.shape[1]
        n_blocks = -(-(2 * n_tok + NE * (MOE_P - 1)) // MOE_P)
        n_rows = n_blocks * MOE_P
        cnt = counts[0, :NE].astype(jnp.int32)
        padded = (cnt + MOE_P - 1) // MOE_P * MOE_P
        pad_ends = jnp.cumsum(padded)
        pad_starts = pad_ends - padded
        run_len = runs[:, 0, :NE].astype(jnp.int32)
        run_global = pad_starts[None, :] + runs[:, 1, :NE].astype(jnp.int32)
        run_local = runs[:, 2, :NE].astype(jnp.int32)
        tab = jnp.concatenate([run_len, run_local, run_global, jnp.zeros_like(run_len)], axis=1)[:, None, :]
        block_start = jnp.arange(n_blocks, dtype=jnp.int32) * MOE_P
        block_e = jnp.minimum(jnp.sum((pad_ends[None, :] <= block_start[:, None]).astype(jnp.int32), axis=1), NE - 1)
        n_used = (pad_ends[-1:] // MOE_P).astype(jnp.int32)
        zero = jnp.zeros_like(cnt)
        pad_tab = jnp.concatenate([padded - cnt, zero, pad_starts + cnt, zero + n_used]).reshape(1, 1, LANES)
        xs = _dispatch(tab, pad_tab, h2.reshape(n_tok, D), slab.reshape(n_tok, LANES), n_rows)
        ids = jnp.arange(NE, dtype=jnp.int32)
        later = jnp.min(jnp.where((ids[None, :] > block_e[:, None]) & (cnt[None, :] > 0), ids[None, :], NE), axis=1)
        next_e = jnp.where(later < NE, later, -1)
        ys = _experts(block_e, next_e, n_used, xs, w_e_gu, w_e_down, l)
        res = _combine(tab, xmid, slab, modall, final_norm_w.reshape(1, D), ys, final, j0)
        if final:
            out = res
        else:
            xall = res
    return out
```

```python
import functools

import jax
import jax.numpy as jnp
import numpy as np
from jax import lax
from jax.experimental import pallas as pl
from jax.experimental.pallas import tpu as pltpu

F32 = jnp.float32
BF16 = jnp.bfloat16

D = 1024
DEPTH = 4
GRID_W = 64
CW = 512
HW = 512
NH = 4
HD = HW // NH
PW = 3 * CW + 5 * HW
N_GROUPS = 4
EPG = 8
NE = N_GROUPS * EPG
DE = 512
N_MOD = 6
EPS = 1e-6
LOG2E = 1.4426950408889634
COL_BLK = 512
assert CW == COL_BLK and HW == COL_BLK
ADA_ROWS = 16
ADA_TN = 1536

TM = 256
CH = 128
LEAF = 16
HEAD_GROUP = 4
GLA_SUB = 2
MOE_P = 256
EXPERT_CHUNK = 256
LANES = 128

VMEM_LIMIT = 56 * 1024 * 1024


def _cparams(n_axes):
    return pltpu.CompilerParams(dimension_semantics=("arbitrary",) * n_axes, vmem_limit_bytes=VMEM_LIMIT)


def _dot(a, b):
    return jnp.dot(a, b, preferred_element_type=F32)


def _dot_nt(a, b):
    return lax.dot_general(a, b, (((1,), (1,)), ((), ())), preferred_element_type=F32)


def _dot_tn(a, b):
    return lax.dot_general(a, b, (((0,), (0,)), ((), ())), preferred_element_type=F32)


def _split2(x):
    hi = x.astype(BF16)
    lo = (x - hi.astype(F32)).astype(BF16)
    return hi, lo


def _dot_hp(a, b):
    a_hi, a_lo = _split2(a)
    b_hi, b_lo = _split2(b)
    return _dot(a_hi, b_hi) + _dot(a_hi, b_lo) + _dot(a_lo, b_hi)


def _sigmoid(x):
    return 1.0 / (1.0 + jnp.exp(-x))


def _rms(x, w):
    return x * lax.rsqrt(jnp.mean(x * x, axis=-1, keepdims=True) + EPS) * w


def _ada_kernel(cc_ref, w_ref, b_ref, o_ref):
    s = cc_ref[...]
    s = s * _sigmoid(s)
    o_ref[0] = _dot_hp(s, w_ref[0]) + b_ref[0]


def _ada(cc, w_ada, b_ada):
    tn = ADA_TN
    n = N_MOD * D
    return pl.pallas_call(
        _ada_kernel,
        grid=(DEPTH, n // tn),
        in_specs=[pl.BlockSpec((ADA_ROWS, D), lambda l, j: (0, 0)),
                  pl.BlockSpec((1, D, tn), lambda l, j: (l, 0, j)),
                  pl.BlockSpec((1, 1, tn), lambda l, j: (l, 0, j))],
        out_specs=pl.BlockSpec((1, ADA_ROWS, tn), lambda l, j: (l, 0, j)),
        out_shape=jax.ShapeDtypeStruct((DEPTH, ADA_ROWS, n), F32),
        compiler_params=_cparams(2),
        name="ada",
    )(cc, w_ada, b_ada.reshape(DEPTH, 1, n))


def _silu(x):
    h = 0.5 * x
    return h + h * jnp.tanh(h)


def _neg_abs(x):
    bits = lax.bitcast_convert_type(x, jnp.uint32) | jnp.uint32(0x80000000)
    return lax.bitcast_convert_type(bits, F32)


def _forget_gate(z2, log2lb, log2_1mlb):
    ls = jnp.minimum(z2, 0.0) - jnp.log2(1.0 + jnp.exp2(_neg_abs(z2)))
    c = log2_1mlb + ls
    lf2 = jnp.maximum(log2lb, c) + jnp.log2(1.0 + jnp.exp2(_neg_abs(log2lb - c)))
    return lf2, jnp.exp2(c - z2)


def _cast_w_in(w_ref, w_b):
    def cast_rows(r, carry):
        rows = pl.ds(pl.multiple_of(r * LANES, LANES), LANES)
        lo, hi = 3 * CW + HW, 3 * CW + 3 * HW
        w_b[rows, :lo] = w_ref[0, rows, :lo].astype(BF16)
        w_b[rows, lo:hi] = (w_ref[0, rows, lo:hi] * LOG2E).astype(BF16)
        w_b[rows, hi:] = w_ref[0, rows, hi:].astype(BF16)
        return carry

    lax.fori_loop(0, D // LANES, cast_rows, 0)


def _inproj_kernel(x_ref, nw_ref, mod_ref, w_ref, lb_ref, conv_ref, g_ref, lf_ref, qkv_ref, w_b):
    @pl.when(jnp.logical_and(pl.program_id(0) == 0, pl.program_id(1) == 0))
    def _():
        _cast_w_in(w_ref, w_b)

    _project(x_ref[0], nw_ref, mod_ref, lb_ref, w_b, conv_ref, g_ref, lf_ref, qkv_ref)


def _project(x, nw_ref, mod_ref, lb_ref, w_b, conv_ref, g_ref, lf_ref, qkv_ref):
    m = mod_ref[0, 0]
    h = _rms(x, nw_ref[...] * (1.0 + m[1:2])) + m[0:1]
    hb = h.astype(BF16)
    blk = {k: _dot(hb, w_b[:, k * COL_BLK:(k + 1) * COL_BLK]) for k in (4, 5, 3, 1, 2, 0, 6, 7)}
    for d in range(2):
        lf2, kk = _forget_gate(blk[4 + d], lb_ref[2 * d:2 * d + 1], lb_ref[2 * d + 1:2 * d + 2])
        lf_ref[0, :, d * HW:(d + 1) * HW] = lf2
        qkv_ref[0, :, (1 + d) * HW:(2 + d) * HW] = kk.astype(BF16)
    qkv_ref[0, :, 0:HW] = _silu(blk[3]).astype(BF16)
    conv_ref[0, :, CW:2 * CW] = blk[1] * blk[2]
    conv_ref[0, :, 0:CW] = blk[0]
    qkv_ref[0, :, 3 * HW:4 * HW] = blk[6].astype(BF16)
    g_ref[0] = blk[7]


def _inproj(xall, nw, modall, w_in, lbp, l):
    bsz, t_all, _ = xall.shape
    tile = lambda w: pl.BlockSpec((1, TM, w), lambda b, j: (b, j, 0))
    return pl.pallas_call(
        _inproj_kernel,
        grid=(bsz, t_all // TM),
        in_specs=[pl.BlockSpec((1, TM, D), lambda b, j: (b, j, 0)),
                  pl.BlockSpec((1, D), lambda b, j: (0, 0)),
                  pl.BlockSpec((1, 1, N_MOD, D), lambda b, j: (b, jnp.minimum(j, 1), 0, 0)),
                  pl.BlockSpec((1, D, PW), lambda b, j: (l, 0, 0), pipeline_mode=pl.Buffered(1)),
                  pl.BlockSpec((4, HW), lambda b, j: (0, 0))],
        out_specs=[tile(2 * CW), tile(HW), tile(2 * HW), tile(4 * HW)],
        out_shape=[jax.ShapeDtypeStruct((bsz, t_all, 2 * CW), F32),
                   jax.ShapeDtypeStruct((bsz, t_all, HW), F32),
                   jax.ShapeDtypeStruct((bsz, t_all, 2 * HW), F32),
                   jax.ShapeDtypeStruct((bsz, t_all, 4 * HW), BF16)],
        scratch_shapes=[pltpu.VMEM((D, PW), BF16)],
        compiler_params=_cparams(2),
        name="inproj",
    )(xall, nw, modall, w_in, lbp)


def _gla_tables():
    t = np.arange(CH)[:, None]
    s = np.arange(CH)[None, :]
    x = t ^ s
    lvl = np.full((CH, CH), -1, np.int32)
    n, k = CH, 0
    while n > LEAF:
        lvl[(x < n) & (x >= n // 2)] = k
        n, k = n // 2, k + 1
    lvl[x < LEAF] = k
    fwd = np.where(s <= t, lvl, -1)
    return (np.stack([s <= t, s >= t]).astype(np.float32), np.stack([fwd, fwd.T]).astype(np.int32))


def _gla_factors(lf2, qh, kh, tri, reverse):
    hi = lf2.astype(BF16)
    lo = (lf2 - hi.astype(F32)).astype(BF16)
    b = _dot(tri, hi) + _dot(tri, lo)

    def block_ref(n, idx):
        r = b.reshape(CH // n, n, HW)[:, idx:idx + 1, :]
        return jnp.broadcast_to(r, (CH // n, n, HW)).reshape(CH, HW)

    qf = qh.astype(F32)
    kf = kh.astype(F32)
    scaled = lambda x, e: (x * jnp.exp2(e)).astype(BF16)
    pieces = []
    n = CH
    while n > LEAF:
        h = n // 2
        a = jnp.exp2(_neg_abs(b - block_ref(n, h if reverse else h - 1)))
        pieces.append(((qf * a).astype(BF16), (kf * a).astype(BF16)))
        n = h
    dl = b - block_ref(LEAF, LEAF // 2)
    pieces.append((scaled(qf, dl), scaled(kf, -dl)))

    edge = 0 if reverse else CH - 1
    b_edge = b[edge:edge + 1, :]
    q_in = scaled(qf, b)
    k_out = scaled(kf, b_edge - b)
    d_chunk = jnp.exp2(b_edge)
    return pieces, q_in, k_out, d_chunk


def _gla_heads(factors, v_refs, rows, lvls, s_refs, o_refs):
    dirs = range(len(factors))
    owned = [[lvls[d] == k for k in range(len(factors[d][0]))] for d in dirs]
    for h0 in range(0, NH, HEAD_GROUP):
        heads = range(h0, h0 + HEAD_GROUP)
        sl = lambda hd: slice(hd * HD, (hd + 1) * HD)
        dots = {(d, hd): [_dot_nt(qa[:, sl(hd)], ka[:, sl(hd)]) for qa, ka in factors[d][0]]
                for hd in heads for d in dirs}
        scs = {}
        for hd in heads:
            for d in dirs:
                sc = jnp.zeros((CH, CH), F32)
                for k, dk in enumerate(dots[d, hd]):
                    sc = jnp.where(owned[d][k], dk, sc)
                scs[d, hd] = sc.astype(BF16)
        for hd in heads:
            for d in dirs:
                _, q_in, k_out, d_chunk = factors[d]
                vb = v_refs[d][0, rows[d], sl(hd)]
                st = s_refs[d][hd]
                o_refs[d][0, rows[d], sl(hd)] = _dot(scs[d, hd], vb) + _dot_nt(q_in[:, sl(hd)], st.astype(BF16))
                s_refs[d][hd] = st * d_chunk[:, sl(hd)] + _dot_tn(vb, k_out[:, sl(hd)])


def _gla_kernel(lff_ref, qf_ref, kf_ref, vf_ref, lfb_ref, qb_ref, kb_ref, vb_ref, tri_ref, lvl_ref, of_ref, ob_ref,
                sf_ref, sb_ref):
    @pl.when(pl.program_id(1) == 0)
    def _():
        sf_ref[...] = jnp.zeros_like(sf_ref)
        sb_ref[...] = jnp.zeros_like(sb_ref)

    for c in range(GLA_SUB):
        rows = (slice(c * CH, (c + 1) * CH), slice((GLA_SUB - 1 - c) * CH, (GLA_SUB - c) * CH))
        factors = (_gla_factors(lff_ref[0, rows[0]], qf_ref[0, rows[0]], kf_ref[0, rows[0]], tri_ref[0], False),
                   _gla_factors(lfb_ref[0, rows[1]], qb_ref[0, rows[1]], kb_ref[0, rows[1]], tri_ref[1], True))
        _gla_heads(factors, (vf_ref, vb_ref), rows, (lvl_ref[0], lvl_ref[1]), (sf_ref, sb_ref), (of_ref, ob_ref))


def _gla(lf, qkv, n_ctx):
    bsz, t_all, _ = lf.shape
    span = GLA_SUB * CH
    assert n_ctx % span == 0 and t_all % span == 0
    nc = t_all // span
    n_ctx_blocks = n_ctx // span

    def cb(j):
        return jnp.where(j < n_ctx_blocks, n_ctx_blocks - 1 - j, nc - 1 - (j - n_ctx_blocks))

    blk = (1, span, HW)
    fwd = lambda c: pl.BlockSpec(blk, lambda b, j: (b, j, c))
    bwd = lambda c: pl.BlockSpec(blk, lambda b, j: (b, cb(j), c))
    tri, lvl = _gla_tables()
    return pl.pallas_call(
        _gla_kernel,
        grid=(bsz, nc),
        in_specs=[fwd(0), fwd(0), fwd(1), fwd(3),
                  bwd(1), bwd(0), bwd(2), bwd(3),
                  pl.BlockSpec((2, CH, CH), lambda b, j: (0, 0, 0)),
                  pl.BlockSpec((2, CH, CH), lambda b, j: (0, 0, 0))],
        out_specs=[fwd(0), bwd(0)],
        out_shape=[jax.ShapeDtypeStruct((bsz, t_all, HW), F32)] * 2,
        scratch_shapes=[pltpu.VMEM((NH, HD, HD), F32), pltpu.VMEM((NH, HD, HD), F32)],
        compiler_params=_cparams(2),
        name="gla",
    )(lf, qkv, qkv, qkv, lf, qkv, qkv, qkv, jnp.asarray(tri, BF16), jnp.asarray(lvl))


def _shift_rows(u, k):
    return pltpu.roll(u, k % u.shape[0], axis=0)


def _mixout_kernel(x_ref, bg_ref, u_ref, g_ref, up_ref, un_ref, of_ref, ob_ref, mod_ref, cw_ref, cnw_ref,
                   hnw_ref, wo_ref, nw2_ref, wr_ref, br_ref, xm_ref, h2_ref, slab_ref, runs_ref, cnt_ref, wo_b, wr_b, lg,
                   *, j0, nt):
    i = pl.program_id(0)
    slot = lax.rem(i, 2)
    j = lax.rem(jnp.minimum(i, pl.num_programs(0) - 2), nt - j0) + j0

    @pl.when(i == 0)
    def _():
        cnt_ref[...] = jnp.zeros_like(cnt_ref)
        lg[1] = jnp.zeros_like(lg[1])
        wo_b[...] = wo_ref[0].astype(BF16)
        wr_b[0], wr_b[1] = _split2(wr_ref[...])

    is_ctx = j == 0
    m = mod_ref[0, 0]
    u = u_ref[0]
    cw = cw_ref[...]
    t = lax.broadcasted_iota(jnp.int32, (TM, 1), 0)
    col_in_row = t & (GRID_W - 1)
    keep_l = jnp.where(is_ctx, jnp.where(t == 0, 0.0, 1.0), jnp.where(col_in_row == 0, 0.0, 1.0))
    keep_r = jnp.where(is_ctx, jnp.where(t == TM - 1, 0.0, 1.0), jnp.where(col_in_row == GRID_W - 1, 0.0, 1.0))
    hc = CW // 2

    def seq_taps(a, w):
        left = jnp.where(keep_l > 0.5, _shift_rows(a, 1), 0.0)
        right = jnp.where(keep_r > 0.5, _shift_rows(a, -1), 0.0)
        return w[0:1] * left + w[1:2] * a + w[2:3] * right

    def col_taps(a, w):
        up = jnp.concatenate([jnp.where(j == 1, 0.0, up_ref[0]), a[:TM - GRID_W]], axis=0)
        dn = jnp.concatenate([a[GRID_W:], jnp.where(j == nt - 1, 0.0, un_ref[0])], axis=0)
        return w[0:1] * up + w[1:2] * a + w[2:3] * dn

    y_second = lax.cond(is_ctx, seq_taps, col_taps, u[:, hc:], cw[:, hc:])
    conv = jnp.concatenate([seq_taps(u[:, :hc], cw[:, :hc]), y_second], axis=-1)
    routed = _route(lg[1 - slot].T, jnp.where(i > 0, 1.0, 0.0), cnt_ref[...])
    y_conv = _rms(bg_ref[0] * conv, cnw_ref[...])

    o = of_ref[0] + ob_ref[0]
    g = g_ref[0]
    hnw = hnw_ref[...]
    recs = []
    for hd in range(NH):
        hs = slice(hd * HD, (hd + 1) * HD)
        recs.append(_rms(o[:, hs], hnw[:, hs]))
    y_rec = jnp.concatenate(recs, axis=-1) * _silu(g)

    y = jnp.concatenate([y_conv, y_rec], axis=-1).astype(BF16)
    xm = x_ref[0] + m[2:3] * _dot(y, wo_b[...])
    xm_ref[0] = xm
    h2 = _rms(xm, nw2_ref[...] * (1.0 + m[4:5])) + m[3:4]
    h2_ref[0] = h2.astype(BF16)

    h_hi, h_lo = _split2(h2)
    lg[slot] = _dot(h_hi, wr_b[0]) + _dot(h_hi, wr_b[1]) + _dot(h_lo, wr_b[0]) + br_ref[...]
    slab_ref[0], runs_ref[0], cnt_ref[...] = routed


def _route(lt, live, counted):
    ninf = -jnp.inf
    big = 1e9
    over = lambda fn, a: fn(a, axis=0, keepdims=True)
    grow = lax.broadcasted_iota(jnp.int32, (8, TM), 0).astype(F32)
    gl = jnp.where(grow < N_GROUPS, lt[NE:NE + 8], ninf)
    gmax = over(jnp.max, gl)
    gsel = over(jnp.min, jnp.where(gl == gmax, grow, big))
    pg = 1.0 / over(jnp.sum, jnp.exp(gl - gmax))
    erow_i = lax.broadcasted_iota(jnp.int32, (NE, TM), 0)
    erow = erow_i.astype(F32)
    el = jnp.where((erow_i // EPG).astype(F32) == gsel, lt[:NE], ninf)
    e1 = over(jnp.max, el)
    i1 = over(jnp.min, jnp.where(el == e1, erow, big))
    el2 = jnp.where(erow == i1, ninf, el)
    e2 = over(jnp.max, el2)
    i2 = over(jnp.min, jnp.where(el2 == e2, erow, big))
    r = jnp.exp(e2 - e1)
    g1 = pg / (1.0 + r)
    g2 = pg * r / (1.0 + r)
    oh1 = jnp.where(erow == i1, 1.0, 0.0)
    oh2 = jnp.where(erow == i2, 1.0, 0.0)
    cnt = oh1 + oh2
    cnt_b = cnt.astype(BF16)
    ts = lax.broadcasted_iota(jnp.int32, (TM, TM), 0)
    tt = lax.broadcasted_iota(jnp.int32, (TM, TM), 1)
    prior = _dot(cnt_b, jnp.where(ts < tt, 1.0, 0.0).astype(BF16))
    c_col = jnp.sum(cnt, axis=1, keepdims=True)
    ei = lax.broadcasted_iota(jnp.int32, (NE, NE), 0)
    ej = lax.broadcasted_iota(jnp.int32, (NE, NE), 1)
    cc_hi, cc_lo = _split2(jnp.broadcast_to(c_col, (NE, LANES)))
    fewer = jnp.where(ej < ei, 1.0, 0.0).astype(BF16)
    off_col = (_dot(fewer, cc_hi) + _dot(fewer, cc_lo))[:, 0:1]
    pos = prior + off_col
    lpos1 = over(jnp.sum, oh1 * pos)
    lpos2 = over(jnp.sum, oh2 * pos)
    cnt_pad = jnp.concatenate([cnt_b, jnp.zeros((LANES - NE, TM), BF16)], axis=0)
    c_tile = _dot_nt(jnp.ones((8, TM), BF16), cnt_pad)[0:1]
    c_tile = c_tile * live
    li = lax.broadcasted_iota(jnp.int32, (LANES, LANES), 0)
    lj = lax.broadcasted_iota(jnp.int32, (LANES, LANES), 1)
    lower_e = jnp.where(li < lj, 1.0, 0.0).astype(BF16)
    c_hi, c_lo = _split2(jnp.broadcast_to(c_tile, (8, LANES)))
    off = (_dot(c_hi, lower_e) + _dot(c_lo, lower_e))[0:1]
    sub = lax.broadcasted_iota(jnp.int32, (8, LANES), 0)
    runs = jnp.where(sub == 0, c_tile, jnp.where(sub == 1, counted, jnp.where(sub == 2, off, 0.0)))
    frow = lax.broadcasted_iota(jnp.int32, (LANES, TM), 0)
    fields = jnp.zeros((LANES, TM), F32)
    for k, val in enumerate((i1, i2, lpos1, lpos2, g1, g2)):
        fields = jnp.where(frow == k, val, fields)
    return fields.T, runs, counted + c_tile


def _mixout(xall, conv_in, g_in, o_f, o_b, modall, cw, cnw, hnw, w_out, nw2, w_r, b_r, l, j0):
    bsz, t_all, _ = xall.shape
    nt = t_all // TM - j0
    n = bsz * nt
    t_out = nt * TM
    n64 = t_all // GRID_W
    per = TM // GRID_W
    cur = lambda i: jnp.minimum(i, n - 1)
    bj = lambda i: (cur(i) // nt, cur(i) % nt + j0)
    old = lambda i: jnp.maximum(i - 1, 0)
    tile = lambda w, c: pl.BlockSpec((1, TM, w), lambda i, c=c: (*bj(i), c))
    out_tile = lambda w, at: pl.BlockSpec((1, TM, w), lambda i: (at(i) // nt, at(i) % nt, 0))
    full = lambda shape: pl.BlockSpec(shape, lambda i: (0,) * len(shape))
    layer = lambda shape: pl.BlockSpec((1, *shape), lambda i: (l,) + (0,) * len(shape), pipeline_mode=pl.Buffered(1))
    return pl.pallas_call(
        functools.partial(_mixout_kernel, j0=j0, nt=nt + j0),
        grid=(n + 1,),
        in_specs=[tile(D, 0),
                  tile(CW, 0), tile(CW, 1), tile(HW, 0),
                  pl.BlockSpec((1, GRID_W, CW // 2), lambda i: (bj(i)[0], jnp.maximum(bj(i)[1] * per - 1, 0), 3)),
                  pl.BlockSpec((1, GRID_W, CW // 2),
                               lambda i: (bj(i)[0], jnp.minimum(bj(i)[1] * per + per, n64 - 1), 3)),
                  tile(HW, 0), tile(HW, 0),
                  pl.BlockSpec((1, 1, N_MOD, D), lambda i: (bj(i)[0], jnp.minimum(bj(i)[1], 1), 0, 0)),
                  full((3, CW)), full((1, CW)), full((1, HW)), layer((D, D)), full((1, D)),
                  full((D, LANES)), full((1, LANES))],
        out_specs=[out_tile(D, cur), out_tile(D, cur), out_tile(LANES, old),
                   pl.BlockSpec((1, 8, LANES), lambda i: (old(i), 0, 0)),
                   pl.BlockSpec((1, LANES), lambda i: (0, 0))],
        out_shape=[jax.ShapeDtypeStruct((bsz, t_out, D), F32),
                   jax.ShapeDtypeStruct((bsz, t_out, D), BF16),
                   jax.ShapeDtypeStruct((bsz, t_out, LANES), F32),
                   jax.ShapeDtypeStruct((bsz * nt, 8, LANES), F32),
                   jax.ShapeDtypeStruct((1, LANES), F32)],
        scratch_shapes=[pltpu.VMEM((D, D), BF16), pltpu.VMEM((2, D, LANES), BF16), pltpu.VMEM((2, TM, LANES), F32)],
        compiler_params=_cparams(1),
        name="mixout",
    )(xall, conv_in, conv_in, g_in, conv_in, conv_in, o_f, o_b, modall, cw, cnw, hnw, w_out, nw2, w_r, b_r)


RUN_BITS = TM.bit_length()
LONG_RUN_BIT = 5
ROW_TILE = (8, LANES)


def _run_copies(tab_ref, local_ref, sorted_ref, sem, *, to_sorted, wait):
    def per_expert(e, carry):
        ln = tab_ref[0, 0, e]
        lo = tab_ref[0, 0, NE + e]
        go = tab_ref[0, 0, 2 * NE + e]
        def copy_bits(bits):
            for bit in bits:
                size = 1 << bit
                done = ln & ~(2 * size - 1)

                @pl.when((ln & size) != 0)
                def _():
                    a = local_ref.at[pl.ds(lo + done, size)]
                    b = sorted_ref.at[pl.ds(go + done, size)]
                    cp = pltpu.make_async_copy(a, b, sem) if to_sorted else pltpu.make_async_copy(b, a, sem)
                    if wait:
                        cp.wait()
                    else:
                        cp.start(priority=bit % 2)

        @pl.when(ln >= (1 << LONG_RUN_BIT))
        def _():
            copy_bits(range(RUN_BITS - 1, LONG_RUN_BIT - 1, -1))

        copy_bits(range(LONG_RUN_BIT - 1, -1, -1))
        return carry

    lax.fori_loop(0, NE, per_expert, 0)


def _wait_tile(local_ref, sorted_ref, sem, *, to_sorted):
    whole = sorted_ref.at[pl.ds(0, 2 * TM)]
    (pltpu.make_async_copy(local_ref, whole, sem) if to_sorted else pltpu.make_async_copy(whole, local_ref, sem)).wait()


def _local_positions(slab):
    col = lax.broadcasted_iota(jnp.int32, (TM, 2 * TM), 1).astype(F32)
    return col == slab[:, 2:3], col == slab[:, 3:4]


def _dispatch_kernel(tab_ref, ptab_ref, h2_ref, slab_ref, xs_ref, buf, zbuf, sem):
    @pl.when(pl.program_id(0) == 0)
    def _():
        zbuf[...] = jnp.zeros_like(zbuf)
        _run_copies(ptab_ref, zbuf, xs_ref, sem, to_sorted=True, wait=False)
        _run_copies(ptab_ref, zbuf, xs_ref, sem, to_sorted=True, wait=True)

        def zero_block(i, carry):
            cp = pltpu.make_async_copy(zbuf, xs_ref.at[pl.ds(i * MOE_P, MOE_P)], sem)
            cp.start()
            cp.wait()
            return carry

        lax.fori_loop(ptab_ref[0, 0, 3 * NE], xs_ref.shape[0] // MOE_P, zero_block, 0)

    i = pl.program_id(0)
    slot = lax.rem(i, 2)
    p1, p2 = _local_positions(slab_ref[...])
    perm_t = jnp.where(p1 | p2, 1.0, 0.0).astype(BF16)
    buf[slot] = _dot_tn(perm_t, h2_ref[...]).reshape(2 * TM, *ROW_TILE)

    @pl.when(i > 0)
    def _():
        _wait_tile(buf.at[1 - slot], xs_ref, sem, to_sorted=True)

    _run_copies(tab_ref, buf.at[slot], xs_ref, sem, to_sorted=True, wait=False)

    @pl.when(i == pl.num_programs(0) - 1)
    def _():
        _wait_tile(buf.at[slot], xs_ref, sem, to_sorted=True)


def _dispatch(tab, pad_tab, h2, slab, n_rows):
    n_tiles = tab.shape[0]
    return pl.pallas_call(
        _dispatch_kernel,
        grid=(n_tiles,),
        in_specs=[pl.BlockSpec((1, 1, LANES), lambda i: (i, 0, 0), memory_space=pltpu.SMEM),
                  pl.BlockSpec((1, 1, LANES), lambda i: (0, 0, 0), memory_space=pltpu.SMEM),
                  pl.BlockSpec((TM, D), lambda i: (i, 0)),
                  pl.BlockSpec((TM, LANES), lambda i: (i, 0))],
        out_specs=pl.BlockSpec(memory_space=pl.ANY),
        out_shape=jax.ShapeDtypeStruct((n_rows, *ROW_TILE), F32),
        scratch_shapes=[pltpu.VMEM((2, 2 * TM, *ROW_TILE), F32), pltpu.VMEM((MOE_P, *ROW_TILE), F32),
                        pltpu.SemaphoreType.DMA(())],
        compiler_params=_cparams(1),
        name="dispatch",
    )(tab, pad_tab, h2, slab)


def _expert_kernel(be_ref, nx_ref, nu_ref, xs_ref, wgu_hbm, wd_hbm, ys_ref, wgu_f, wd_f, wgu_b, wd_b, slot_ref, sem,
                   *, layer):
    i = pl.program_id(0)
    used = i < nu_ref[0]
    e = be_ref[i]

    def weight_copies(expert, slot):
        return (pltpu.make_async_copy(wgu_hbm.at[layer, expert], wgu_f.at[slot], sem.at[slot]),
                pltpu.make_async_copy(wd_hbm.at[layer, expert], wd_f.at[slot], sem.at[slot]))

    @pl.when(i == 0)
    def _():
        slot_ref[0] = 0
        for cp in weight_copies(e, 0):
            cp.start()

    @pl.when(jnp.logical_and(used, jnp.logical_or(i == 0, e != be_ref[jnp.maximum(i - 1, 0)])))
    def _():
        slot = jnp.where(i == 0, 0, 1 - slot_ref[0])
        slot_ref[0] = slot
        for cp in weight_copies(e, slot):
            cp.wait()
        wgu_b[...] = wgu_f[slot].astype(BF16)
        wd_b[...] = wd_f[slot].astype(BF16)

        @pl.when(nx_ref[i] >= 0)
        def _():
            for cp in weight_copies(nx_ref[i], 1 - slot):
                cp.start()

    @pl.when(used)
    def _():
        x = xs_ref[...].reshape(MOE_P, D).astype(BF16)
        acts = []
        for c in range(0, DE, EXPERT_CHUNK):
            a = _dot(x, wgu_b[:, c:c + EXPERT_CHUNK])
            u = _dot(x, wgu_b[:, DE + c:DE + c + EXPERT_CHUNK])
            acts.append((_silu(a) * u).astype(BF16))
        y = _dot(jnp.concatenate(acts, axis=1), wd_b[...])
        ys_ref[...] = y.reshape(MOE_P, *ROW_TILE)

    @pl.when(jnp.logical_not(used))
    def _():
        ys_ref[...] = jnp.zeros_like(ys_ref)


def _experts(block_e, next_e, n_used, xs, w_gu, w_down, l):
    n_rows = xs.shape[0]
    grid_spec = pltpu.PrefetchScalarGridSpec(
        num_scalar_prefetch=3,
        grid=(n_rows // MOE_P,),
        in_specs=[pl.BlockSpec((MOE_P, *ROW_TILE), lambda i, be, nx, nu: (jnp.minimum(i, nu[0] - 1), 0, 0)),
                  pl.BlockSpec(memory_space=pl.ANY),
                  pl.BlockSpec(memory_space=pl.ANY)],
        out_specs=pl.BlockSpec((MOE_P, *ROW_TILE), lambda i, be, nx, nu: (i, 0, 0)),
        scratch_shapes=[pltpu.VMEM((2, D, 2 * DE), F32), pltpu.VMEM((2, DE, D), F32),
                        pltpu.VMEM((D, 2 * DE), BF16), pltpu.VMEM((DE, D), BF16),
                        pltpu.SMEM((1,), jnp.int32), pltpu.SemaphoreType.DMA((2,))],
    )
    return pl.pallas_call(
        functools.partial(_expert_kernel, layer=l),
        grid_spec=grid_spec,
        out_shape=jax.ShapeDtypeStruct((n_rows, *ROW_TILE), F32),
        compiler_params=_cparams(1),
        name="experts",
    )(block_e, next_e, n_used, xs, w_gu, w_down)


def _combine_kernel(tab_ref, tab_next_ref, xm_ref, slab_ref, mod_ref, fnw_ref, ys_ref, o_ref, buf, sem, *, final):
    i = pl.program_id(0) * pl.num_programs(1) + pl.program_id(1)
    n = pl.num_programs(0) * pl.num_programs(1)
    slot = lax.rem(i, 2)

    @pl.when(i == 0)
    def _():
        _run_copies(tab_ref, buf.at[0], ys_ref, sem.at[0], to_sorted=False, wait=False)

    @pl.when(i + 1 < n)
    def _():
        _run_copies(tab_next_ref, buf.at[1 - slot], ys_ref, sem.at[1 - slot], to_sorted=False, wait=False)

    _wait_tile(buf.at[slot], ys_ref, sem.at[slot], to_sorted=False)
    slab = slab_ref[0]
    p1, p2 = _local_positions(slab)
    yb = buf[slot].reshape(2 * TM, D).astype(BF16)
    y1 = _dot(jnp.where(p1, 1.0, 0.0).astype(BF16), yb)
    y2 = _dot(jnp.where(p2, 1.0, 0.0).astype(BF16), yb)
    f = slab[:, 4:5] * y1 + slab[:, 5:6] * y2
    x = xm_ref[0] + mod_ref[0, 0][5:6] * f
    o_ref[0] = _rms(x, fnw_ref[...]) if final else x


def _combine(tab, xmid, slab, modall, fnw, ys, final, j0):
    bsz, t_out, _ = xmid.shape
    nt = t_out // TM
    return pl.pallas_call(
        functools.partial(_combine_kernel, final=final),
        grid=(bsz, nt),
        in_specs=[pl.BlockSpec((1, 1, LANES), lambda b, j: (b * nt + j, 0, 0), memory_space=pltpu.SMEM),
                  pl.BlockSpec((1, 1, LANES), lambda b, j: (jnp.minimum(b * nt + j + 1, bsz * nt - 1), 0, 0),
                               memory_space=pltpu.SMEM),
                  pl.BlockSpec((1, TM, D), lambda b, j: (b, j, 0)),
                  pl.BlockSpec((1, TM, LANES), lambda b, j: (b, j, 0)),
                  pl.BlockSpec((1, 1, N_MOD, D), lambda b, j: (b, jnp.minimum(j + j0, 1), 0, 0)),
                  pl.BlockSpec((1, D), lambda b, j: (0, 0)),
                  pl.BlockSpec(memory_space=pl.ANY)],
        out_specs=pl.BlockSpec((1, TM, D), lambda b, j: (b, j, 0)),
        out_shape=jax.ShapeDtypeStruct((bsz, t_out, D), F32),
        scratch_shapes=[pltpu.VMEM((2, 2 * TM, *ROW_TILE), F32), pltpu.SemaphoreType.DMA((2,))],
        compiler_params=_cparams(2),
        name="combine_final" if final else "combine",
    )(tab, tab, xmid, slab, modall, fnw, ys)


def _combine_project_kernel(tab_ref, tab_next_ref, xm_ref, slab_ref, mod_ref, ys_ref, nw_ref, mod2_ref, w_ref, lb_ref,
                            o_ref, conv_ref, g_ref, lf_ref, qkv_ref, buf, sem, w_b):
    i = pl.program_id(0) * pl.num_programs(1) + pl.program_id(1)
    n = pl.num_programs(0) * pl.num_programs(1)
    slot = lax.rem(i, 2)

    @pl.when(i == 0)
    def _():
        _run_copies(tab_ref, buf.at[0], ys_ref, sem.at[0], to_sorted=False, wait=False)
        _cast_w_in(w_ref, w_b)

    @pl.when(i + 1 < n)
    def _():
        _run_copies(tab_next_ref, buf.at[1 - slot], ys_ref, sem.at[1 - slot], to_sorted=False, wait=False)

    _wait_tile(buf.at[slot], ys_ref, sem.at[slot], to_sorted=False)
    slab = slab_ref[0]
    p1, p2 = _local_positions(slab)
    yb = buf[slot].reshape(2 * TM, D).astype(BF16)
    y1 = _dot(jnp.where(p1, 1.0, 0.0).astype(BF16), yb)
    y2 = _dot(jnp.where(p2, 1.0, 0.0).astype(BF16), yb)
    x = xm_ref[0] + mod_ref[0, 0][5:6] * (slab[:, 4:5] * y1 + slab[:, 5:6] * y2)
    o_ref[0] = x
    _project(x, nw_ref, mod2_ref, lb_ref, w_b, conv_ref, g_ref, lf_ref, qkv_ref)


def _combine_project(tab, xmid, slab, modall, ys, nw_next, modall_next, w_in, lbp_next, l_next):
    bsz, t_all, _ = xmid.shape
    nt = t_all // TM
    tile = lambda w: pl.BlockSpec((1, TM, w), lambda b, j: (b, j, 0))
    mod_spec = pl.BlockSpec((1, 1, N_MOD, D), lambda b, j: (b, jnp.minimum(j, 1), 0, 0))
    return pl.pallas_call(
        _combine_project_kernel,
        grid=(bsz, nt),
        in_specs=[pl.BlockSpec((1, 1, LANES), lambda b, j: (b * nt + j, 0, 0), memory_space=pltpu.SMEM),
                  pl.BlockSpec((1, 1, LANES), lambda b, j: (jnp.minimum(b * nt + j + 1, bsz * nt - 1), 0, 0),
                               memory_space=pltpu.SMEM),
                  tile(D), tile(LANES), mod_spec,
                  pl.BlockSpec(memory_space=pl.ANY),
                  pl.BlockSpec((1, D), lambda b, j: (0, 0)),
                  mod_spec,
                  pl.BlockSpec((1, D, PW), lambda b, j: (l_next, 0, 0), pipeline_mode=pl.Buffered(1)),
                  pl.BlockSpec((4, HW), lambda b, j: (0, 0))],
        out_specs=[tile(D), tile(2 * CW), tile(HW), tile(2 * HW), tile(4 * HW)],
        out_shape=[jax.ShapeDtypeStruct((bsz, t_all, D), F32),
                   jax.ShapeDtypeStruct((bsz, t_all, 2 * CW), F32),
                   jax.ShapeDtypeStruct((bsz, t_all, HW), F32),
                   jax.ShapeDtypeStruct((bsz, t_all, 2 * HW), F32),
                   jax.ShapeDtypeStruct((bsz, t_all, 4 * HW), BF16)],
        scratch_shapes=[pltpu.VMEM((2, 2 * TM, *ROW_TILE), F32), pltpu.SemaphoreType.DMA((2,)),
                        pltpu.VMEM((D, PW), BF16)],
        compiler_params=_cparams(2),
        name="combine_project",
    )(tab, tab, xmid, slab, modall, ys, nw_next, modall_next, w_in, lbp_next)


def _lower_bounds(hg_lb):
    p = jax.nn.softmax(hg_lb.astype(F32), axis=1)
    cs = jnp.cumsum(p, axis=1)
    return cs - cs[:, :1]


def kernel(x, c, ctx, c_ctx, norm_w, w_ada, b_ada, w_in, conv_w, conv_norm_w, hg_lb, hg_norm_w, w_out, w_rg, b_rg,
           w_re, b_re, w_e_gu, w_e_down, final_norm_w):
    bsz, seq, _ = x.shape
    n_ctx = ctx.shape[1]
    assert n_ctx == TM and seq % TM == 0

    xall = jnp.concatenate([ctx, x], axis=1)
    assert bsz < ADA_ROWS
    cc = jnp.zeros((ADA_ROWS, D), F32).at[:bsz].set(c).at[bsz].set(c_ctx)
    mod = _ada(cc, w_ada, b_ada).reshape(DEPTH, ADA_ROWS, N_MOD, D)
    lb = _lower_bounds(hg_lb)
    lbp = jnp.stack([jnp.log(lb[0]), jnp.log1p(-lb[0]), jnp.log(lb[1]), jnp.log1p(-lb[1])], axis=1) * LOG2E
    w_r = jnp.concatenate([w_re, w_rg, jnp.zeros((DEPTH, D, LANES - NE - N_GROUPS), F32)], axis=-1)
    b_r = jnp.concatenate([b_re, b_rg, jnp.zeros((DEPTH, LANES - NE - N_GROUPS), F32)], axis=-1)

    layer_mod = lambda l: jnp.stack([jnp.broadcast_to(mod[l, bsz], (bsz, N_MOD, D)), mod[l, :bsz]], axis=1)
    projected = _inproj(xall, norm_w[0, 0].reshape(1, D), layer_mod(0), w_in, lbp[0], 0)
    for l in range(DEPTH):
        final = l == DEPTH - 1
        j0 = 1 if final else 0
        modall = layer_mod(l)
        conv_in, g_in, lf, qkv = projected
        o_f, o_b = _gla(lf, qkv, n_ctx)
        xmid, h2, slab, runs, counts = _mixout(
            xall, conv_in, g_in, o_f, o_b, modall, conv_w[l], conv_norm_w[l].reshape(1, CW),
            jnp.tile(hg_norm_w[l], NH).reshape(1, HW), w_out, norm_w[l, 1].reshape(1, D),
            w_r[l], b_r[l].reshape(1, LANES), l, j0)
        n_tok = xmid.shape[0] * xmid.shape[1]
        n_blocks = -(-(2 * n_tok + NE * (MOE_P - 1)) // MOE_P)
        n_rows = n_blocks * MOE_P
        cnt = counts[0, :NE].astype(jnp.int32)
        padded = (cnt + MOE_P - 1) // MOE_P * MOE_P
        pad_ends = jnp.cumsum(padded)
        pad_starts = pad_ends - padded
        run_len = runs[:, 0, :NE].astype(jnp.int32)
        run_global = pad_starts[None, :] + runs[:, 1, :NE].astype(jnp.int32)
        run_local = runs[:, 2, :NE].astype(jnp.int32)
        tab = jnp.concatenate([run_len, run_local, run_global, jnp.zeros_like(run_len)], axis=1)[:, None, :]
        block_start = jnp.arange(n_blocks, dtype=jnp.int32) * MOE_P
        block_e = jnp.minimum(jnp.sum((pad_ends[None, :] <= block_start[:, None]).astype(jnp.int32), axis=1), NE - 1)
        n_used = (pad_ends[-1:] // MOE_P).astype(jnp.int32)
        zero = jnp.zeros_like(cnt)
        pad_tab = jnp.concatenate([padded - cnt, zero, pad_starts + cnt, zero + n_used]).reshape(1, 1, LANES)
        xs = _dispatch(tab, pad_tab, h2.reshape(n_tok, D), slab.reshape(n_tok, LANES), n_rows)
        ids = jnp.arange(NE, dtype=jnp.int32)
        later = jnp.min(jnp.where((ids[None, :] > block_e[:, None]) & (cnt[None, :] > 0), ids[None, :], NE), axis=1)
        next_e = jnp.where(later < NE, later, -1)
        ys = _experts(block_e, next_e, n_used, xs, w_e_gu, w_e_down, l)
        if final:
            return _combine(tab, xmid, slab, modall, final_norm_w.reshape(1, D), ys, final, j0)
        xall, *projected = _combine_project(tab, xmid, slab, modall, ys, norm_w[l + 1, 0].reshape(1, D), layer_mod(l + 1),
                                            w_in, lbp[l + 1], l + 1)
```

```python
import functools

import jax
import jax.numpy as jnp
import numpy as np
from jax import lax
from jax.experimental import pallas as pl
from jax.experimental.pallas import tpu as pltpu

F32 = jnp.float32
BF16 = jnp.bfloat16

D = 1024
DEPTH = 4
GRID_W = 64
CW = 512
HW = 512
NH = 4
HD = HW // NH
PW = 3 * CW + 5 * HW
N_GROUPS = 4
EPG = 8
NE = N_GROUPS * EPG
DE = 512
N_MOD = 6
EPS = 1e-6
LOG2E = 1.4426950408889634
COL_BLK = 512
assert CW == COL_BLK and HW == COL_BLK
ADA_ROWS = 16
ADA_TN = 1536

TM = 256
CH = 128
LEAF = 16
HEAD_GROUP = 4
GLA_SUB = 2
MOE_P = 256
EXPERT_CHUNK = 256
LANES = 128

VMEM_LIMIT = 56 * 1024 * 1024


def _cparams(n_axes):
    return pltpu.CompilerParams(dimension_semantics=("arbitrary",) * n_axes, vmem_limit_bytes=VMEM_LIMIT)


def _dot(a, b):
    return jnp.dot(a, b, preferred_element_type=F32)


def _dot_nt(a, b):
    return lax.dot_general(a, b, (((1,), (1,)), ((), ())), preferred_element_type=F32)


def _dot_tn(a, b):
    return lax.dot_general(a, b, (((0,), (0,)), ((), ())), preferred_element_type=F32)


def _split2(x):
    hi = x.astype(BF16)
    lo = (x - hi.astype(F32)).astype(BF16)
    return hi, lo


def _dot_hp(a, b):
    a_hi, a_lo = _split2(a)
    b_hi, b_lo = _split2(b)
    return _dot(a_hi, b_hi) + _dot(a_hi, b_lo) + _dot(a_lo, b_hi)


def _sigmoid(x):
    return 1.0 / (1.0 + jnp.exp(-x))


def _rms(x, w):
    return x * lax.rsqrt(jnp.mean(x * x, axis=-1, keepdims=True) + EPS) * w


def _ada_kernel(cc_ref, w_ref, b_ref, o_ref):
    s = cc_ref[...]
    s = s * _sigmoid(s)
    o_ref[0] = _dot_hp(s, w_ref[0]) + b_ref[0]


def _ada(cc, w_ada, b_ada):
    tn = ADA_TN
    n = N_MOD * D
    return pl.pallas_call(
        _ada_kernel,
        grid=(DEPTH, n // tn),
        in_specs=[pl.BlockSpec((ADA_ROWS, D), lambda l, j: (0, 0)),
                  pl.BlockSpec((1, D, tn), lambda l, j: (l, 0, j)),
                  pl.BlockSpec((1, 1, tn), lambda l, j: (l, 0, j))],
        out_specs=pl.BlockSpec((1, ADA_ROWS, tn), lambda l, j: (l, 0, j)),
        out_shape=jax.ShapeDtypeStruct((DEPTH, ADA_ROWS, n), F32),
        compiler_params=_cparams(2),
        name="ada",
    )(cc, w_ada, b_ada.reshape(DEPTH, 1, n))


def _silu(x):
    h = 0.5 * x
    return h + h * jnp.tanh(h)


def _neg_abs(x):
    bits = lax.bitcast_convert_type(x, jnp.uint32) | jnp.uint32(0x80000000)
    return lax.bitcast_convert_type(bits, F32)


def _forget_gate(z2, log2lb, log2_1mlb):
    ls = jnp.minimum(z2, 0.0) - jnp.log2(1.0 + jnp.exp2(_neg_abs(z2)))
    c = log2_1mlb + ls
    lf2 = jnp.maximum(log2lb, c) + jnp.log2(1.0 + jnp.exp2(_neg_abs(log2lb - c)))
    return lf2, jnp.exp2(c - z2)


def _inproj_kernel(x_ref, nw_ref, mod_ref, w_ref, lb_ref, conv_ref, g_ref, lf_ref, qkv_ref, w_b):
    @pl.when(jnp.logical_and(pl.program_id(0) == 0, pl.program_id(1) == 0))
    def _():
        def cast_rows(r, carry):
            rows = pl.ds(pl.multiple_of(r * LANES, LANES), LANES)
            lo, hi = 3 * CW + HW, 3 * CW + 3 * HW
            w_b[rows, :lo] = w_ref[0, rows, :lo].astype(BF16)
            w_b[rows, lo:hi] = (w_ref[0, rows, lo:hi] * LOG2E).astype(BF16)
            w_b[rows, hi:] = w_ref[0, rows, hi:].astype(BF16)
            return carry

        lax.fori_loop(0, D // LANES, cast_rows, 0)

    m = mod_ref[0, 0]
    h = _rms(x_ref[0], nw_ref[...] * (1.0 + m[1:2])) + m[0:1]
    hb = h.astype(BF16)
    blk = {k: _dot(hb, w_b[:, k * COL_BLK:(k + 1) * COL_BLK]) for k in (4, 5, 3, 1, 2, 0, 6, 7)}
    for d in range(2):
        lf2, kk = _forget_gate(blk[4 + d], lb_ref[2 * d:2 * d + 1], lb_ref[2 * d + 1:2 * d + 2])
        lf_ref[0, :, d * HW:(d + 1) * HW] = lf2
        qkv_ref[0, :, (1 + d) * HW:(2 + d) * HW] = kk.astype(BF16)
    qkv_ref[0, :, 0:HW] = _silu(blk[3]).astype(BF16)
    conv_ref[0, :, CW:2 * CW] = blk[1] * blk[2]
    conv_ref[0, :, 0:CW] = blk[0]
    qkv_ref[0, :, 3 * HW:4 * HW] = blk[6].astype(BF16)
    g_ref[0] = blk[7]


def _inproj(xall, nw, modall, w_in, lbp, l):
    bsz, t_all, _ = xall.shape
    tile = lambda w: pl.BlockSpec((1, TM, w), lambda b, j: (b, j, 0))
    return pl.pallas_call(
        _inproj_kernel,
        grid=(bsz, t_all // TM),
        in_specs=[pl.BlockSpec((1, TM, D), lambda b, j: (b, j, 0)),
                  pl.BlockSpec((1, D), lambda b, j: (0, 0)),
                  pl.BlockSpec((1, 1, N_MOD, D), lambda b, j: (b, jnp.minimum(j, 1), 0, 0)),
                  pl.BlockSpec((1, D, PW), lambda b, j: (l, 0, 0), pipeline_mode=pl.Buffered(1)),
                  pl.BlockSpec((4, HW), lambda b, j: (0, 0))],
        out_specs=[tile(2 * CW), tile(HW), tile(2 * HW), tile(4 * HW)],
        out_shape=[jax.ShapeDtypeStruct((bsz, t_all, 2 * CW), F32),
                   jax.ShapeDtypeStruct((bsz, t_all, HW), F32),
                   jax.ShapeDtypeStruct((bsz, t_all, 2 * HW), F32),
                   jax.ShapeDtypeStruct((bsz, t_all, 4 * HW), BF16)],
        scratch_shapes=[pltpu.VMEM((D, PW), BF16)],
        compiler_params=_cparams(2),
        name="inproj",
    )(xall, nw, modall, w_in, lbp)


def _gla_tables():
    t = np.arange(CH)[:, None]
    s = np.arange(CH)[None, :]
    x = t ^ s
    lvl = np.full((CH, CH), -1, np.int32)
    n, k = CH, 0
    while n > LEAF:
        lvl[(x < n) & (x >= n // 2)] = k
        n, k = n // 2, k + 1
    lvl[x < LEAF] = k
    fwd = np.where(s <= t, lvl, -1)
    return (np.stack([s <= t, s >= t]).astype(np.float32), np.stack([fwd, fwd.T]).astype(np.int32))


def _gla_factors(lf2, qh, kh, tri, reverse):
    hi = lf2.astype(BF16)
    lo = (lf2 - hi.astype(F32)).astype(BF16)
    b = _dot(tri, hi) + _dot(tri, lo)

    def block_ref(n, idx):
        r = b.reshape(CH // n, n, HW)[:, idx:idx + 1, :]
        return jnp.broadcast_to(r, (CH // n, n, HW)).reshape(CH, HW)

    qf = qh.astype(F32)
    kf = kh.astype(F32)
    scaled = lambda x, e: (x * jnp.exp2(e)).astype(BF16)
    pieces = []
    n = CH
    while n > LEAF:
        h = n // 2
        a = jnp.exp2(_neg_abs(b - block_ref(n, h if reverse else h - 1)))
        pieces.append(((qf * a).astype(BF16), (kf * a).astype(BF16)))
        n = h
    dl = b - block_ref(LEAF, LEAF // 2)
    pieces.append((scaled(qf, dl), scaled(kf, -dl)))

    edge = 0 if reverse else CH - 1
    b_edge = b[edge:edge + 1, :]
    q_in = scaled(qf, b)
    k_out = scaled(kf, b_edge - b)
    d_chunk = jnp.exp2(b_edge)
    return pieces, q_in, k_out, d_chunk


def _gla_heads(factors, v_refs, rows, lvls, s_refs, o_refs):
    dirs = range(len(factors))
    owned = [[lvls[d] == k for k in range(len(factors[d][0]))] for d in dirs]
    for h0 in range(0, NH, HEAD_GROUP):
        heads = range(h0, h0 + HEAD_GROUP)
        sl = lambda hd: slice(hd * HD, (hd + 1) * HD)
        dots = {(d, hd): [_dot_nt(qa[:, sl(hd)], ka[:, sl(hd)]) for qa, ka in factors[d][0]]
                for hd in heads for d in dirs}
        scs = {}
        for hd in heads:
            for d in dirs:
                sc = jnp.zeros((CH, CH), F32)
                for k, dk in enumerate(dots[d, hd]):
                    sc = jnp.where(owned[d][k], dk, sc)
                scs[d, hd] = sc.astype(BF16)
        for hd in heads:
            for d in dirs:
                _, q_in, k_out, d_chunk = factors[d]
                vb = v_refs[d][0, rows[d], sl(hd)]
                st = s_refs[d][hd]
                o_refs[d][0, rows[d], sl(hd)] = _dot(scs[d, hd], vb) + _dot_nt(q_in[:, sl(hd)], st.astype(BF16))
                s_refs[d][hd] = st * d_chunk[:, sl(hd)] + _dot_tn(vb, k_out[:, sl(hd)])


def _gla_kernel(lff_ref, qf_ref, kf_ref, vf_ref, lfb_ref, qb_ref, kb_ref, vb_ref, tri_ref, lvl_ref, of_ref, ob_ref,
                sf_ref, sb_ref):
    @pl.when(pl.program_id(1) == 0)
    def _():
        sf_ref[...] = jnp.zeros_like(sf_ref)
        sb_ref[...] = jnp.zeros_like(sb_ref)

    for c in range(GLA_SUB):
        rows = (slice(c * CH, (c + 1) * CH), slice((GLA_SUB - 1 - c) * CH, (GLA_SUB - c) * CH))
        factors = (_gla_factors(lff_ref[0, rows[0]], qf_ref[0, rows[0]], kf_ref[0, rows[0]], tri_ref[0], False),
                   _gla_factors(lfb_ref[0, rows[1]], qb_ref[0, rows[1]], kb_ref[0, rows[1]], tri_ref[1], True))
        _gla_heads(factors, (vf_ref, vb_ref), rows, (lvl_ref[0], lvl_ref[1]), (sf_ref, sb_ref), (of_ref, ob_ref))


def _gla(lf, qkv, n_ctx):
    bsz, t_all, _ = lf.shape
    span = GLA_SUB * CH
    assert n_ctx % span == 0 and t_all % span == 0
    nc = t_all // span
    n_ctx_blocks = n_ctx // span

    def cb(j):
        return jnp.where(j < n_ctx_blocks, n_ctx_blocks - 1 - j, nc - 1 - (j - n_ctx_blocks))

    blk = (1, span, HW)
    fwd = lambda c: pl.BlockSpec(blk, lambda b, j: (b, j, c))
    bwd = lambda c: pl.BlockSpec(blk, lambda b, j: (b, cb(j), c))
    tri, lvl = _gla_tables()
    return pl.pallas_call(
        _gla_kernel,
        grid=(bsz, nc),
        in_specs=[fwd(0), fwd(0), fwd(1), fwd(3),
                  bwd(1), bwd(0), bwd(2), bwd(3),
                  pl.BlockSpec((2, CH, CH), lambda b, j: (0, 0, 0)),
                  pl.BlockSpec((2, CH, CH), lambda b, j: (0, 0, 0))],
        out_specs=[fwd(0), bwd(0)],
        out_shape=[jax.ShapeDtypeStruct((bsz, t_all, HW), F32)] * 2,
        scratch_shapes=[pltpu.VMEM((NH, HD, HD), F32), pltpu.VMEM((NH, HD, HD), F32)],
        compiler_params=_cparams(2),
        name="gla",
    )(lf, qkv, qkv, qkv, lf, qkv, qkv, qkv, jnp.asarray(tri, BF16), jnp.asarray(lvl))


def _shift_rows(u, k):
    return pltpu.roll(u, k % u.shape[0], axis=0)


def _mixout_kernel(x_ref, bg_ref, u_ref, g_ref, up_ref, un_ref, of_ref, ob_ref, mod_ref, cw_ref, cnw_ref,
                   hnw_ref, wo_ref, nw2_ref, wr_ref, br_ref, xm_ref, h2_ref, slab_ref, runs_ref, cnt_ref, wo_b, wr_b, lg,
                   *, j0, nt):
    i = pl.program_id(0)
    slot = lax.rem(i, 2)
    j = lax.rem(jnp.minimum(i, pl.num_programs(0) - 2), nt - j0) + j0

    @pl.when(i == 0)
    def _():
        cnt_ref[...] = jnp.zeros_like(cnt_ref)
        lg[1] = jnp.zeros_like(lg[1])
        wo_b[...] = wo_ref[0].astype(BF16)
        wr_b[0], wr_b[1] = _split2(wr_ref[...])

    is_ctx = j == 0
    m = mod_ref[0, 0]
    u = u_ref[0]
    cw = cw_ref[...]
    t = lax.broadcasted_iota(jnp.int32, (TM, 1), 0)
    col_in_row = t & (GRID_W - 1)
    keep_l = jnp.where(is_ctx, jnp.where(t == 0, 0.0, 1.0), jnp.where(col_in_row == 0, 0.0, 1.0))
    keep_r = jnp.where(is_ctx, jnp.where(t == TM - 1, 0.0, 1.0), jnp.where(col_in_row == GRID_W - 1, 0.0, 1.0))
    hc = CW // 2

    def seq_taps(a, w):
        left = jnp.where(keep_l > 0.5, _shift_rows(a, 1), 0.0)
        right = jnp.where(keep_r > 0.5, _shift_rows(a, -1), 0.0)
        return w[0:1] * left + w[1:2] * a + w[2:3] * right

    def col_taps(a, w):
        up = jnp.concatenate([jnp.where(j == 1, 0.0, up_ref[0]), a[:TM - GRID_W]], axis=0)
        dn = jnp.concatenate([a[GRID_W:], jnp.where(j == nt - 1, 0.0, un_ref[0])], axis=0)
        return w[0:1] * up + w[1:2] * a + w[2:3] * dn

    y_second = lax.cond(is_ctx, seq_taps, col_taps, u[:, hc:], cw[:, hc:])
    conv = jnp.concatenate([seq_taps(u[:, :hc], cw[:, :hc]), y_second], axis=-1)
    routed = _route(lg[1 - slot].T, jnp.where(i > 0, 1.0, 0.0), cnt_ref[...])
    y_conv = _rms(bg_ref[0] * conv, cnw_ref[...])

    o = of_ref[0] + ob_ref[0]
    g = g_ref[0]
    hnw = hnw_ref[...]
    recs = []
    for hd in range(NH):
        hs = slice(hd * HD, (hd + 1) * HD)
        recs.append(_rms(o[:, hs], hnw[:, hs]))
    y_rec = jnp.concatenate(recs, axis=-1) * _silu(g)

    y = jnp.concatenate([y_conv, y_rec], axis=-1).astype(BF16)
    xm = x_ref[0] + m[2:3] * _dot(y, wo_b[...])
    xm_ref[0] = xm
    h2 = _rms(xm, nw2_ref[...] * (1.0 + m[4:5])) + m[3:4]
    h2_ref[0] = h2.astype(BF16)

    h_hi, h_lo = _split2(h2)
    lg[slot] = _dot(h_hi, wr_b[0]) + _dot(h_hi, wr_b[1]) + _dot(h_lo, wr_b[0]) + br_ref[...]
    slab_ref[0], runs_ref[0], cnt_ref[...] = routed


def _route(lt, live, counted):
    ninf = -jnp.inf
    big = 1e9
    over = lambda fn, a: fn(a, axis=0, keepdims=True)
    grow = lax.broadcasted_iota(jnp.int32, (8, TM), 0).astype(F32)
    gl = jnp.where(grow < N_GROUPS, lt[NE:NE + 8], ninf)
    gmax = over(jnp.max, gl)
    gsel = over(jnp.min, jnp.where(gl == gmax, grow, big))
    pg = 1.0 / over(jnp.sum, jnp.exp(gl - gmax))
    erow_i = lax.broadcasted_iota(jnp.int32, (NE, TM), 0)
    erow = erow_i.astype(F32)
    el = jnp.where((erow_i // EPG).astype(F32) == gsel, lt[:NE], ninf)
    e1 = over(jnp.max, el)
    i1 = over(jnp.min, jnp.where(el == e1, erow, big))
    el2 = jnp.where(erow == i1, ninf, el)
    e2 = over(jnp.max, el2)
    i2 = over(jnp.min, jnp.where(el2 == e2, erow, big))
    r = jnp.exp(e2 - e1)
    g1 = pg / (1.0 + r)
    g2 = pg * r / (1.0 + r)
    oh1 = jnp.where(erow == i1, 1.0, 0.0)
    oh2 = jnp.where(erow == i2, 1.0, 0.0)
    cnt = oh1 + oh2
    cnt_b = cnt.astype(BF16)
    ts = lax.broadcasted_iota(jnp.int32, (TM, TM), 0)
    tt = lax.broadcasted_iota(jnp.int32, (TM, TM), 1)
    prior = _dot(cnt_b, jnp.where(ts < tt, 1.0, 0.0).astype(BF16))
    c_col = jnp.sum(cnt, axis=1, keepdims=True)
    ei = lax.broadcasted_iota(jnp.int32, (NE, NE), 0)
    ej = lax.broadcasted_iota(jnp.int32, (NE, NE), 1)
    cc_hi, cc_lo = _split2(jnp.broadcast_to(c_col, (NE, LANES)))
    fewer = jnp.where(ej < ei, 1.0, 0.0).astype(BF16)
    off_col = (_dot(fewer, cc_hi) + _dot(fewer, cc_lo))[:, 0:1]
    pos = prior + off_col
    lpos1 = over(jnp.sum, oh1 * pos)
    lpos2 = over(jnp.sum, oh2 * pos)
    cnt_pad = jnp.concatenate([cnt_b, jnp.zeros((LANES - NE, TM), BF16)], axis=0)
    c_tile = _dot_nt(jnp.ones((8, TM), BF16), cnt_pad)[0:1]
    c_tile = c_tile * live
    li = lax.broadcasted_iota(jnp.int32, (LANES, LANES), 0)
    lj = lax.broadcasted_iota(jnp.int32, (LANES, LANES), 1)
    lower_e = jnp.where(li < lj, 1.0, 0.0).astype(BF16)
    c_hi, c_lo = _split2(jnp.broadcast_to(c_tile, (8, LANES)))
    off = (_dot(c_hi, lower_e) + _dot(c_lo, lower_e))[0:1]
    sub = lax.broadcasted_iota(jnp.int32, (8, LANES), 0)
    runs = jnp.where(sub == 0, c_tile, jnp.where(sub == 1, counted, jnp.where(sub == 2, off, 0.0)))
    frow = lax.broadcasted_iota(jnp.int32, (LANES, TM), 0)
    fields = jnp.zeros((LANES, TM), F32)
    for k, val in enumerate((i1, i2, lpos1, lpos2, g1, g2)):
        fields = jnp.where(frow == k, val, fields)
    return fields.T, runs, counted + c_tile


def _mixout(xall, conv_in, g_in, o_f, o_b, modall, cw, cnw, hnw, w_out, nw2, w_r, b_r, l, j0):
    bsz, t_all, _ = xall.shape
    nt = t_all // TM - j0
    n = bsz * nt
    t_out = nt * TM
    n64 = t_all // GRID_W
    per = TM // GRID_W
    cur = lambda i: jnp.minimum(i, n - 1)
    bj = lambda i: (cur(i) // nt, cur(i) % nt + j0)
    old = lambda i: jnp.maximum(i - 1, 0)
    tile = lambda w, c: pl.BlockSpec((1, TM, w), lambda i, c=c: (*bj(i), c))
    out_tile = lambda w, at: pl.BlockSpec((1, TM, w), lambda i: (at(i) // nt, at(i) % nt, 0))
    full = lambda shape: pl.BlockSpec(shape, lambda i: (0,) * len(shape))
    layer = lambda shape: pl.BlockSpec((1, *shape), lambda i: (l,) + (0,) * len(shape), pipeline_mode=pl.Buffered(1))
    return pl.pallas_call(
        functools.partial(_mixout_kernel, j0=j0, nt=nt + j0),
        grid=(n + 1,),
        in_specs=[tile(D, 0),
                  tile(CW, 0), tile(CW, 1), tile(HW, 0),
                  pl.BlockSpec((1, GRID_W, CW // 2), lambda i: (bj(i)[0], jnp.maximum(bj(i)[1] * per - 1, 0), 3)),
                  pl.BlockSpec((1, GRID_W, CW // 2),
                               lambda i: (bj(i)[0], jnp.minimum(bj(i)[1] * per + per, n64 - 1), 3)),
                  tile(HW, 0), tile(HW, 0),
                  pl.BlockSpec((1, 1, N_MOD, D), lambda i: (bj(i)[0], jnp.minimum(bj(i)[1], 1), 0, 0)),
                  full((3, CW)), full((1, CW)), full((1, HW)), layer((D, D)), full((1, D)),
                  full((D, LANES)), full((1, LANES))],
        out_specs=[out_tile(D, cur), out_tile(D, cur), out_tile(LANES, old),
                   pl.BlockSpec((1, 8, LANES), lambda i: (old(i), 0, 0)),
                   pl.BlockSpec((1, LANES), lambda i: (0, 0))],
        out_shape=[jax.ShapeDtypeStruct((bsz, t_out, D), F32),
                   jax.ShapeDtypeStruct((bsz, t_out, D), BF16),
                   jax.ShapeDtypeStruct((bsz, t_out, LANES), F32),
                   jax.ShapeDtypeStruct((bsz * nt, 8, LANES), F32),
                   jax.ShapeDtypeStruct((1, LANES), F32)],
        scratch_shapes=[pltpu.VMEM((D, D), BF16), pltpu.VMEM((2, D, LANES), BF16), pltpu.VMEM((2, TM, LANES), F32)],
        compiler_params=_cparams(1),
        name="mixout",
    )(xall, conv_in, conv_in, g_in, conv_in, conv_in, o_f, o_b, modall, cw, cnw, hnw, w_out, nw2, w_r, b_r)


RUN_BITS = TM.bit_length()
LONG_RUN_BIT = 5
ROW_TILE = (8, LANES)


def _run_copies(tab_ref, local_ref, sorted_ref, sem, *, to_sorted, wait):
    def per_expert(e, carry):
        ln = tab_ref[0, 0, e]
        lo = tab_ref[0, 0, NE + e]
        go = tab_ref[0, 0, 2 * NE + e]
        def copy_bits(bits):
            for bit in bits:
                size = 1 << bit
                done = ln & ~(2 * size - 1)

                @pl.when((ln & size) != 0)
                def _():
                    a = local_ref.at[pl.ds(lo + done, size)]
                    b = sorted_ref.at[pl.ds(go + done, size)]
                    cp = pltpu.make_async_copy(a, b, sem) if to_sorted else pltpu.make_async_copy(b, a, sem)
                    if wait:
                        cp.wait()
                    else:
                        cp.start(priority=bit % 2)

        @pl.when(ln >= (1 << LONG_RUN_BIT))
        def _():
            copy_bits(range(RUN_BITS - 1, LONG_RUN_BIT - 1, -1))

        copy_bits(range(LONG_RUN_BIT - 1, -1, -1))
        return carry

    lax.fori_loop(0, NE, per_expert, 0)


def _wait_tile(local_ref, sorted_ref, sem, *, to_sorted):
    whole = sorted_ref.at[pl.ds(0, 2 * TM)]
    (pltpu.make_async_copy(local_ref, whole, sem) if to_sorted else pltpu.make_async_copy(whole, local_ref, sem)).wait()


def _local_positions(slab):
    col = lax.broadcasted_iota(jnp.int32, (TM, 2 * TM), 1).astype(F32)
    return col == slab[:, 2:3], col == slab[:, 3:4]


def _dispatch_kernel(tab_ref, ptab_ref, h2_ref, slab_ref, xs_ref, buf, zbuf, sem):
    @pl.when(pl.program_id(0) == 0)
    def _():
        zbuf[...] = jnp.zeros_like(zbuf)
        _run_copies(ptab_ref, zbuf, xs_ref, sem, to_sorted=True, wait=False)
        _run_copies(ptab_ref, zbuf, xs_ref, sem, to_sorted=True, wait=True)

        def zero_block(wait, i, carry):
            cp = pltpu.make_async_copy(zbuf, xs_ref.at[pl.ds(i * MOE_P, MOE_P)], sem)
            if wait:
                cp.wait()
            else:
                cp.start()
            return carry

        for wait in (False, True):
            lax.fori_loop(ptab_ref[0, 0, 3 * NE], xs_ref.shape[0] // MOE_P, functools.partial(zero_block, wait), 0)

    i = pl.program_id(0)
    slot = lax.rem(i, 2)
    p1, p2 = _local_positions(slab_ref[...])
    perm_t = jnp.where(p1 | p2, 1.0, 0.0).astype(BF16)
    buf[slot] = _dot_tn(perm_t, h2_ref[...]).reshape(2 * TM, *ROW_TILE)

    @pl.when(i > 0)
    def _():
        _wait_tile(buf.at[1 - slot], xs_ref, sem, to_sorted=True)

    _run_copies(tab_ref, buf.at[slot], xs_ref, sem, to_sorted=True, wait=False)

    @pl.when(i == pl.num_programs(0) - 1)
    def _():
        _wait_tile(buf.at[slot], xs_ref, sem, to_sorted=True)


def _dispatch(tab, pad_tab, h2, slab, n_rows):
    n_tiles = tab.shape[0]
    return pl.pallas_call(
        _dispatch_kernel,
        grid=(n_tiles,),
        in_specs=[pl.BlockSpec((1, 1, LANES), lambda i: (i, 0, 0), memory_space=pltpu.SMEM),
                  pl.BlockSpec((1, 1, LANES), lambda i: (0, 0, 0), memory_space=pltpu.SMEM),
                  pl.BlockSpec((TM, D), lambda i: (i, 0)),
                  pl.BlockSpec((TM, LANES), lambda i: (i, 0))],
        out_specs=pl.BlockSpec(memory_space=pl.ANY),
        out_shape=jax.ShapeDtypeStruct((n_rows, *ROW_TILE), F32),
        scratch_shapes=[pltpu.VMEM((2, 2 * TM, *ROW_TILE), F32), pltpu.VMEM((MOE_P, *ROW_TILE), F32),
                        pltpu.SemaphoreType.DMA(())],
        compiler_params=_cparams(1),
        name="dispatch",
    )(tab, pad_tab, h2, slab)


def _expert_kernel(be_ref, nx_ref, nu_ref, xs_ref, wgu_hbm, wd_hbm, ys_ref, wgu_f, wd_f, wgu_b, wd_b, slot_ref, sem,
                   *, layer):
    i = pl.program_id(0)
    used = i < nu_ref[0]
    e = be_ref[i]

    def weight_copies(expert, slot):
        return (pltpu.make_async_copy(wgu_hbm.at[layer, expert], wgu_f.at[slot], sem.at[slot]),
                pltpu.make_async_copy(wd_hbm.at[layer, expert], wd_f.at[slot], sem.at[slot]))

    @pl.when(i == 0)
    def _():
        slot_ref[0] = 0
        for cp in weight_copies(e, 0):
            cp.start()

    @pl.when(jnp.logical_and(used, jnp.logical_or(i == 0, e != be_ref[jnp.maximum(i - 1, 0)])))
    def _():
        slot = jnp.where(i == 0, 0, 1 - slot_ref[0])
        slot_ref[0] = slot
        for cp in weight_copies(e, slot):
            cp.wait()
        wgu_b[...] = wgu_f[slot].astype(BF16)
        wd_b[...] = wd_f[slot].astype(BF16)

        @pl.when(nx_ref[i] >= 0)
        def _():
            for cp in weight_copies(nx_ref[i], 1 - slot):
                cp.start()

    @pl.when(used)
    def _():
        x = xs_ref[...].reshape(MOE_P, D).astype(BF16)
        acts = []
        for c in range(0, DE, EXPERT_CHUNK):
            a = _dot(x, wgu_b[:, c:c + EXPERT_CHUNK])
            u = _dot(x, wgu_b[:, DE + c:DE + c + EXPERT_CHUNK])
            acts.append((_silu(a) * u).astype(BF16))
        y = _dot(jnp.concatenate(acts, axis=1), wd_b[...])
        ys_ref[...] = y.reshape(MOE_P, *ROW_TILE)

    @pl.when(jnp.logical_not(used))
    def _():
        ys_ref[...] = jnp.zeros_like(ys_ref)


def _experts(block_e, next_e, n_used, xs, w_gu, w_down, l):
    n_rows = xs.shape[0]
    grid_spec = pltpu.PrefetchScalarGridSpec(
        num_scalar_prefetch=3,
        grid=(n_rows // MOE_P,),
        in_specs=[pl.BlockSpec((MOE_P, *ROW_TILE), lambda i, be, nx, nu: (jnp.minimum(i, nu[0] - 1), 0, 0)),
                  pl.BlockSpec(memory_space=pl.ANY),
                  pl.BlockSpec(memory_space=pl.ANY)],
        out_specs=pl.BlockSpec((MOE_P, *ROW_TILE), lambda i, be, nx, nu: (i, 0, 0)),
        scratch_shapes=[pltpu.VMEM((2, D, 2 * DE), F32), pltpu.VMEM((2, DE, D), F32),
                        pltpu.VMEM((D, 2 * DE), BF16), pltpu.VMEM((DE, D), BF16),
                        pltpu.SMEM((1,), jnp.int32), pltpu.SemaphoreType.DMA((2,))],
    )
    return pl.pallas_call(
        functools.partial(_expert_kernel, layer=l),
        grid_spec=grid_spec,
        out_shape=jax.ShapeDtypeStruct((n_rows, *ROW_TILE), F32),
        compiler_params=_cparams(1),
        name="experts",
    )(block_e, next_e, n_used, xs, w_gu, w_down)


def _combine_kernel(tab_ref, tab_next_ref, xm_ref, slab_ref, mod_ref, fnw_ref, ys_ref, o_ref, buf, sem, *, final):
    i = pl.program_id(0) * pl.num_programs(1) + pl.program_id(1)
    n = pl.num_programs(0) * pl.num_programs(1)
    slot = lax.rem(i, 2)

    @pl.when(i == 0)
    def _():
        _run_copies(tab_ref, buf.at[0], ys_ref, sem.at[0], to_sorted=False, wait=False)

    @pl.when(i + 1 < n)
    def _():
        _run_copies(tab_next_ref, buf.at[1 - slot], ys_ref, sem.at[1 - slot], to_sorted=False, wait=False)

    _wait_tile(buf.at[slot], ys_ref, sem.at[slot], to_sorted=False)
    slab = slab_ref[0]
    p1, p2 = _local_positions(slab)
    yb = buf[slot].reshape(2 * TM, D).astype(BF16)
    y1 = _dot(jnp.where(p1, 1.0, 0.0).astype(BF16), yb)
    y2 = _dot(jnp.where(p2, 1.0, 0.0).astype(BF16), yb)
    f = slab[:, 4:5] * y1 + slab[:, 5:6] * y2
    x = xm_ref[0] + mod_ref[0, 0][5:6] * f
    o_ref[0] = _rms(x, fnw_ref[...]) if final else x


def _combine(tab, xmid, slab, modall, fnw, ys, final, j0):
    bsz, t_out, _ = xmid.shape
    nt = t_out // TM
    return pl.pallas_call(
        functools.partial(_combine_kernel, final=final),
        grid=(bsz, nt),
        in_specs=[pl.BlockSpec((1, 1, LANES), lambda b, j: (b * nt + j, 0, 0), memory_space=pltpu.SMEM),
                  pl.BlockSpec((1, 1, LANES), lambda b, j: (jnp.minimum(b * nt + j + 1, bsz * nt - 1), 0, 0),
                               memory_space=pltpu.SMEM),
                  pl.BlockSpec((1, TM, D), lambda b, j: (b, j, 0)),
                  pl.BlockSpec((1, TM, LANES), lambda b, j: (b, j, 0)),
                  pl.BlockSpec((1, 1, N_MOD, D), lambda b, j: (b, jnp.minimum(j + j0, 1), 0, 0)),
                  pl.BlockSpec((1, D), lambda b, j: (0, 0)),
                  pl.BlockSpec(memory_space=pl.ANY)],
        out_specs=pl.BlockSpec((1, TM, D), lambda b, j: (b, j, 0)),
        out_shape=jax.ShapeDtypeStruct((bsz, t_out, D), F32),
        scratch_shapes=[pltpu.VMEM((2, 2 * TM, *ROW_TILE), F32), pltpu.SemaphoreType.DMA((2,))],
        compiler_params=_cparams(2),
        name="combine_final" if final else "combine",
    )(tab, tab, xmid, slab, modall, fnw, ys)


def _lower_bounds(hg_lb):
    p = jax.nn.softmax(hg_lb.astype(F32), axis=1)
    cs = jnp.cumsum(p, axis=1)
    return cs - cs[:, :1]


def kernel(x, c, ctx, c_ctx, norm_w, w_ada, b_ada, w_in, conv_w, conv_norm_w, hg_lb, hg_norm_w, w_out, w_rg, b_rg,
           w_re, b_re, w_e_gu, w_e_down, final_norm_w):
    bsz, seq, _ = x.shape
    n_ctx = ctx.shape[1]
    assert n_ctx == TM and seq % TM == 0

    xall = jnp.concatenate([ctx, x], axis=1)
    assert bsz < ADA_ROWS
    cc = jnp.zeros((ADA_ROWS, D), F32).at[:bsz].set(c).at[bsz].set(c_ctx)
    mod = _ada(cc, w_ada, b_ada).reshape(DEPTH, ADA_ROWS, N_MOD, D)
    lb = _lower_bounds(hg_lb)
    lbp = jnp.stack([jnp.log(lb[0]), jnp.log1p(-lb[0]), jnp.log(lb[1]), jnp.log1p(-lb[1])], axis=1) * LOG2E
    w_r = jnp.concatenate([w_re, w_rg, jnp.zeros((DEPTH, D, LANES - NE - N_GROUPS), F32)], axis=-1)
    b_r = jnp.concatenate([b_re, b_rg, jnp.zeros((DEPTH, LANES - NE - N_GROUPS), F32)], axis=-1)

    out = None
    for l in range(DEPTH):
        final = l == DEPTH - 1
        j0 = 1 if final else 0
        modall = jnp.stack([jnp.broadcast_to(mod[l, bsz], (bsz, N_MOD, D)), mod[l, :bsz]], axis=1)
        conv_in, g_in, lf, qkv = _inproj(xall, norm_w[l, 0].reshape(1, D), modall, w_in, lbp[l], l)
        o_f, o_b = _gla(lf, qkv, n_ctx)
        xmid, h2, slab, runs, counts = _mixout(
            xall, conv_in, g_in, o_f, o_b, modall, conv_w[l], conv_norm_w[l].reshape(1, CW),
            jnp.tile(hg_norm_w[l], NH).reshape(1, HW), w_out, norm_w[l, 1].reshape(1, D),
            w_r[l], b_r[l].reshape(1, LANES), l, j0)
        n_tok = xmid.shape[0] * xmid.shape[1]
        n_blocks = -(-(2 * n_tok + NE * (MOE_P - 1)) // MOE_P)
        n_rows = n_blocks * MOE_P
        cnt = counts[0, :NE].astype(jnp.int32)
        padded = (cnt + MOE_P - 1) // MOE_P * MOE_P
        pad_ends = jnp.cumsum(padded)
        pad_starts = pad_ends - padded
        run_len = runs[:, 0, :NE].astype(jnp.int32)
        run_global = pad_starts[None, :] + runs[:, 1, :NE].astype(jnp.int32)
        run_local = runs[:, 2, :NE].astype(jnp.int32)
        tab = jnp.concatenate([run_len, run_local, run_global, jnp.zeros_like(run_len)], axis=1)[:, None, :]
        block_start = jnp.arange(n_blocks, dtype=jnp.int32) * MOE_P
        block_e = jnp.minimum(jnp.sum((pad_ends[None, :] <= block_start[:, None]).astype(jnp.int32), axis=1), NE - 1)
        n_used = (pad_ends[-1:] // MOE_P).astype(jnp.int32)
        zero = jnp.zeros_like(cnt)
        pad_tab = jnp.concatenate([padded - cnt, zero, pad_starts + cnt, zero + n_used]).reshape(1, 1, LANES)
        xs = _dispatch(tab, pad_tab, h2.reshape(n_tok, D), slab.reshape(n_tok, LANES), n_rows)
        ids = jnp.arange(NE, dtype=jnp.int32)
        later = jnp.min(jnp.where((ids[None, :] > block_e[:, None]) & (cnt[None, :] > 0), ids[None, :], NE), axis=1)
        next_e = jnp.where(later < NE, later, -1)
        ys = _experts(block_e, next_e, n_used, xs, w_e_gu, w_e_down, l)
        res = _combine(tab, xmid, slab, modall, final_norm_w.reshape(1, D), ys, final, j0)
        if final:
            out = res
        else:
            xall = res
    return out
```

```python
import functools

import jax
import jax.numpy as jnp
import numpy as np
from jax import lax
from jax.experimental import pallas as pl
from jax.experimental.pallas import tpu as pltpu

F32 = jnp.float32
BF16 = jnp.bfloat16

D = 1024
DEPTH = 4
GRID_W = 64
CW = 512
HW = 512
NH = 4
HD = HW // NH
PW = 3 * CW + 5 * HW
N_GROUPS = 4
EPG = 8
NE = N_GROUPS * EPG
DE = 512
N_MOD = 6
EPS = 1e-6
LOG2E = 1.4426950408889634
COL_BLK = 512
assert CW == COL_BLK and HW == COL_BLK
ADA_ROWS = 16
ADA_TN = 1536

TM = 256
CH = 128
LEAF = 8
HEAD_GROUP = 4
GLA_SUB = 2
MOE_P = 256
EXPERT_CHUNK = 256
LANES = 128

VMEM_LIMIT = 56 * 1024 * 1024


def _cparams(n_axes):
    return pltpu.CompilerParams(dimension_semantics=("arbitrary",) * n_axes, vmem_limit_bytes=VMEM_LIMIT)


def _dot(a, b):
    return jnp.dot(a, b, preferred_element_type=F32)


def _dot_nt(a, b):
    return lax.dot_general(a, b, (((1,), (1,)), ((), ())), preferred_element_type=F32)


def _dot_tn(a, b):
    return lax.dot_general(a, b, (((0,), (0,)), ((), ())), preferred_element_type=F32)


def _split2(x):
    hi = x.astype(BF16)
    lo = (x - hi.astype(F32)).astype(BF16)
    return hi, lo


def _dot_hp(a, b):
    a_hi, a_lo = _split2(a)
    b_hi, b_lo = _split2(b)
    return _dot(a_hi, b_hi) + _dot(a_hi, b_lo) + _dot(a_lo, b_hi)


def _sigmoid(x):
    return 1.0 / (1.0 + jnp.exp(-x))


def _rms(x, w):
    return x * lax.rsqrt(jnp.mean(x * x, axis=-1, keepdims=True) + EPS) * w


def _ada_kernel(cc_ref, w_ref, b_ref, o_ref):
    s = cc_ref[...]
    s = s * _sigmoid(s)
    o_ref[0] = _dot_hp(s, w_ref[0]) + b_ref[0]


def _ada(cc, w_ada, b_ada):
    tn = ADA_TN
    n = N_MOD * D
    return pl.pallas_call(
        _ada_kernel,
        grid=(DEPTH, n // tn),
        in_specs=[pl.BlockSpec((ADA_ROWS, D), lambda l, j: (0, 0)),
                  pl.BlockSpec((1, D, tn), lambda l, j: (l, 0, j)),
                  pl.BlockSpec((1, 1, tn), lambda l, j: (l, 0, j))],
        out_specs=pl.BlockSpec((1, ADA_ROWS, tn), lambda l, j: (l, 0, j)),
        out_shape=jax.ShapeDtypeStruct((DEPTH, ADA_ROWS, n), F32),
        compiler_params=_cparams(2),
        name="ada",
    )(cc, w_ada, b_ada.reshape(DEPTH, 1, n))


def _silu(x):
    h = 0.5 * x
    return h + h * jnp.tanh(h)


def _neg_abs(x):
    bits = lax.bitcast_convert_type(x, jnp.uint32) | jnp.uint32(0x80000000)
    return lax.bitcast_convert_type(bits, F32)


def _forget_gate(z2, log2lb, log2_1mlb):
    ls = jnp.minimum(z2, 0.0) - jnp.log2(1.0 + jnp.exp2(_neg_abs(z2)))
    c = log2_1mlb + ls
    lf2 = jnp.maximum(log2lb, c) + jnp.log2(1.0 + jnp.exp2(_neg_abs(log2lb - c)))
    return lf2, jnp.exp2(c - z2)


def _inproj_kernel(x_ref, nw_ref, mod_ref, w_ref, lb_ref, conv_ref, g_ref, lf_ref, qkv_ref, w_b):
    @pl.when(jnp.logical_and(pl.program_id(0) == 0, pl.program_id(1) == 0))
    def _():
        def cast_rows(r, carry):
            rows = pl.ds(pl.multiple_of(r * LANES, LANES), LANES)
            lo, hi = 3 * CW + HW, 3 * CW + 3 * HW
            w_b[rows, :lo] = w_ref[0, rows, :lo].astype(BF16)
            w_b[rows, lo:hi] = (w_ref[0, rows, lo:hi] * LOG2E).astype(BF16)
            w_b[rows, hi:] = w_ref[0, rows, hi:].astype(BF16)
            return carry

        lax.fori_loop(0, D // LANES, cast_rows, 0)

    m = mod_ref[0, 0]
    h = _rms(x_ref[0], nw_ref[...] * (1.0 + m[1:2])) + m[0:1]
    hb = h.astype(BF16)
    blk = {k: _dot(hb, w_b[:, k * COL_BLK:(k + 1) * COL_BLK]) for k in (4, 5, 3, 1, 2, 0, 6, 7)}
    for d in range(2):
        lf2, kk = _forget_gate(blk[4 + d], lb_ref[2 * d:2 * d + 1], lb_ref[2 * d + 1:2 * d + 2])
        lf_ref[0, :, d * HW:(d + 1) * HW] = lf2
        qkv_ref[0, :, (1 + d) * HW:(2 + d) * HW] = kk.astype(BF16)
    qkv_ref[0, :, 0:HW] = _silu(blk[3]).astype(BF16)
    conv_ref[0, :, CW:2 * CW] = blk[1] * blk[2]
    conv_ref[0, :, 0:CW] = blk[0]
    qkv_ref[0, :, 3 * HW:4 * HW] = blk[6].astype(BF16)
    g_ref[0] = blk[7]


def _inproj(xall, nw, modall, w_in, lbp, l):
    bsz, t_all, _ = xall.shape
    tile = lambda w: pl.BlockSpec((1, TM, w), lambda b, j: (b, j, 0))
    return pl.pallas_call(
        _inproj_kernel,
        grid=(bsz, t_all // TM),
        in_specs=[pl.BlockSpec((1, TM, D), lambda b, j: (b, j, 0)),
                  pl.BlockSpec((1, D), lambda b, j: (0, 0)),
                  pl.BlockSpec((1, 1, N_MOD, D), lambda b, j: (b, jnp.minimum(j, 1), 0, 0)),
                  pl.BlockSpec((1, D, PW), lambda b, j: (l, 0, 0), pipeline_mode=pl.Buffered(1)),
                  pl.BlockSpec((4, HW), lambda b, j: (0, 0))],
        out_specs=[tile(2 * CW), tile(HW), tile(2 * HW), tile(4 * HW)],
        out_shape=[jax.ShapeDtypeStruct((bsz, t_all, 2 * CW), F32),
                   jax.ShapeDtypeStruct((bsz, t_all, HW), F32),
                   jax.ShapeDtypeStruct((bsz, t_all, 2 * HW), F32),
                   jax.ShapeDtypeStruct((bsz, t_all, 4 * HW), BF16)],
        scratch_shapes=[pltpu.VMEM((D, PW), BF16)],
        compiler_params=_cparams(2),
        name="inproj",
    )(xall, nw, modall, w_in, lbp)


def _gla_tables():
    t = np.arange(CH)[:, None]
    s = np.arange(CH)[None, :]
    x = t ^ s
    lvl = np.full((CH, CH), -1, np.int32)
    n, k = CH, 0
    while n > LEAF:
        lvl[(x < n) & (x >= n // 2)] = k
        n, k = n // 2, k + 1
    lvl[x < LEAF] = k
    fwd = np.where(s <= t, lvl, -1)
    return (np.stack([s <= t, s >= t]).astype(np.float32), np.stack([fwd, fwd.T]).astype(np.int32))


def _gla_factors(lf2, qh, kh, tri, reverse):
    hi = lf2.astype(BF16)
    lo = (lf2 - hi.astype(F32)).astype(BF16)
    b = _dot(tri, hi) + _dot(tri, lo)

    def block_ref(n, idx):
        r = b.reshape(CH // n, n, HW)[:, idx:idx + 1, :]
        return jnp.broadcast_to(r, (CH // n, n, HW)).reshape(CH, HW)

    qf = qh.astype(F32)
    kf = kh.astype(F32)
    scaled = lambda x, e: (x * jnp.exp2(e)).astype(BF16)
    pieces = []
    n = CH
    while n > LEAF:
        h = n // 2
        a = jnp.exp2(_neg_abs(b - block_ref(n, h if reverse else h - 1)))
        pieces.append(((qf * a).astype(BF16), (kf * a).astype(BF16)))
        n = h
    dl = b - block_ref(LEAF, LEAF // 2)
    pieces.append((scaled(qf, dl), scaled(kf, -dl)))

    edge = 0 if reverse else CH - 1
    b_edge = b[edge:edge + 1, :]
    q_in = scaled(qf, b)
    k_out = scaled(kf, b_edge - b)
    d_chunk = jnp.exp2(b_edge)
    return pieces, q_in, k_out, d_chunk


def _gla_heads(factors, v_refs, rows, lvls, s_refs, o_refs):
    dirs = range(len(factors))
    owned = [[lvls[d] == k for k in range(len(factors[d][0]))] for d in dirs]
    for h0 in range(0, NH, HEAD_GROUP):
        heads = range(h0, h0 + HEAD_GROUP)
        sl = lambda hd: slice(hd * HD, (hd + 1) * HD)
        dots = {(d, hd): [_dot_nt(qa[:, sl(hd)], ka[:, sl(hd)]) for qa, ka in factors[d][0]]
                for hd in heads for d in dirs}
        scs = {}
        for hd in heads:
            for d in dirs:
                sc = jnp.zeros((CH, CH), F32)
                for k, dk in enumerate(dots[d, hd]):
                    sc = jnp.where(owned[d][k], dk, sc)
                scs[d, hd] = sc.astype(BF16)
        for hd in heads:
            for d in dirs:
                _, q_in, k_out, d_chunk = factors[d]
                vb = v_refs[d][0, rows[d], sl(hd)]
                st = s_refs[d][hd]
                o_refs[d][0, rows[d], sl(hd)] = _dot(scs[d, hd], vb) + _dot_nt(q_in[:, sl(hd)], st.astype(BF16))
                s_refs[d][hd] = st * d_chunk[:, sl(hd)] + _dot_tn(vb, k_out[:, sl(hd)])


def _gla_kernel(lff_ref, qf_ref, kf_ref, vf_ref, lfb_ref, qb_ref, kb_ref, vb_ref, tri_ref, lvl_ref, of_ref, ob_ref,
                sf_ref, sb_ref):
    @pl.when(pl.program_id(1) == 0)
    def _():
        sf_ref[...] = jnp.zeros_like(sf_ref)
        sb_ref[...] = jnp.zeros_like(sb_ref)

    for c in range(GLA_SUB):
        rows = (slice(c * CH, (c + 1) * CH), slice((GLA_SUB - 1 - c) * CH, (GLA_SUB - c) * CH))
        factors = (_gla_factors(lff_ref[0, rows[0]], qf_ref[0, rows[0]], kf_ref[0, rows[0]], tri_ref[0], False),
                   _gla_factors(lfb_ref[0, rows[1]], qb_ref[0, rows[1]], kb_ref[0, rows[1]], tri_ref[1], True))
        _gla_heads(factors, (vf_ref, vb_ref), rows, (lvl_ref[0], lvl_ref[1]), (sf_ref, sb_ref), (of_ref, ob_ref))


def _gla(lf, qkv, n_ctx):
    bsz, t_all, _ = lf.shape
    span = GLA_SUB * CH
    assert n_ctx % span == 0 and t_all % span == 0
    nc = t_all // span
    n_ctx_blocks = n_ctx // span

    def cb(j):
        return jnp.where(j < n_ctx_blocks, n_ctx_blocks - 1 - j, nc - 1 - (j - n_ctx_blocks))

    blk = (1, span, HW)
    fwd = lambda c: pl.BlockSpec(blk, lambda b, j: (b, j, c))
    bwd = lambda c: pl.BlockSpec(blk, lambda b, j: (b, cb(j), c))
    tri, lvl = _gla_tables()
    return pl.pallas_call(
        _gla_kernel,
        grid=(bsz, nc),
        in_specs=[fwd(0), fwd(0), fwd(1), fwd(3),
                  bwd(1), bwd(0), bwd(2), bwd(3),
                  pl.BlockSpec((2, CH, CH), lambda b, j: (0, 0, 0)),
                  pl.BlockSpec((2, CH, CH), lambda b, j: (0, 0, 0))],
        out_specs=[fwd(0), bwd(0)],
        out_shape=[jax.ShapeDtypeStruct((bsz, t_all, HW), F32)] * 2,
        scratch_shapes=[pltpu.VMEM((NH, HD, HD), F32), pltpu.VMEM((NH, HD, HD), F32)],
        compiler_params=_cparams(2),
        name="gla",
    )(lf, qkv, qkv, qkv, lf, qkv, qkv, qkv, jnp.asarray(tri, BF16), jnp.asarray(lvl))


def _shift_rows(u, k):
    return pltpu.roll(u, k % u.shape[0], axis=0)


def _mixout_kernel(x_ref, bg_ref, u_ref, g_ref, up_ref, un_ref, of_ref, ob_ref, mod_ref, cw_ref, cnw_ref,
                   hnw_ref, wo_ref, nw2_ref, wr_ref, br_ref, xm_ref, h2_ref, slab_ref, runs_ref, cnt_ref, wo_b, wr_b, lg,
                   *, j0, nt):
    i = pl.program_id(0)
    slot = lax.rem(i, 2)
    j = lax.rem(jnp.minimum(i, pl.num_programs(0) - 2), nt - j0) + j0

    @pl.when(i == 0)
    def _():
        cnt_ref[...] = jnp.zeros_like(cnt_ref)
        lg[1] = jnp.zeros_like(lg[1])
        wo_b[...] = wo_ref[0].astype(BF16)
        wr_b[0], wr_b[1] = _split2(wr_ref[...])

    is_ctx = j == 0
    m = mod_ref[0, 0]
    u = u_ref[0]
    cw = cw_ref[...]
    t = lax.broadcasted_iota(jnp.int32, (TM, 1), 0)
    col_in_row = t & (GRID_W - 1)
    keep_l = jnp.where(is_ctx, jnp.where(t == 0, 0.0, 1.0), jnp.where(col_in_row == 0, 0.0, 1.0))
    keep_r = jnp.where(is_ctx, jnp.where(t == TM - 1, 0.0, 1.0), jnp.where(col_in_row == GRID_W - 1, 0.0, 1.0))
    hc = CW // 2

    def seq_taps(a, w):
        left = jnp.where(keep_l > 0.5, _shift_rows(a, 1), 0.0)
        right = jnp.where(keep_r > 0.5, _shift_rows(a, -1), 0.0)
        return w[0:1] * left + w[1:2] * a + w[2:3] * right

    def col_taps(a, w):
        up = jnp.concatenate([jnp.where(j == 1, 0.0, up_ref[0]), a[:TM - GRID_W]], axis=0)
        dn = jnp.concatenate([a[GRID_W:], jnp.where(j == nt - 1, 0.0, un_ref[0])], axis=0)
        return w[0:1] * up + w[1:2] * a + w[2:3] * dn

    y_second = lax.cond(is_ctx, seq_taps, col_taps, u[:, hc:], cw[:, hc:])
    conv = jnp.concatenate([seq_taps(u[:, :hc], cw[:, :hc]), y_second], axis=-1)
    routed = _route(lg[1 - slot].T, jnp.where(i > 0, 1.0, 0.0), cnt_ref[...])
    y_conv = _rms(bg_ref[0] * conv, cnw_ref[...])

    o = of_ref[0] + ob_ref[0]
    g = g_ref[0]
    hnw = hnw_ref[...]
    recs = []
    for hd in range(NH):
        hs = slice(hd * HD, (hd + 1) * HD)
        recs.append(_rms(o[:, hs], hnw[:, hs]))
    y_rec = jnp.concatenate(recs, axis=-1) * _silu(g)

    y = jnp.concatenate([y_conv, y_rec], axis=-1).astype(BF16)
    xm = x_ref[0] + m[2:3] * _dot(y, wo_b[...])
    xm_ref[0] = xm
    h2 = _rms(xm, nw2_ref[...] * (1.0 + m[4:5])) + m[3:4]
    h2_ref[0] = h2.astype(BF16)

    h_hi, h_lo = _split2(h2)
    lg[slot] = _dot(h_hi, wr_b[0]) + _dot(h_hi, wr_b[1]) + _dot(h_lo, wr_b[0]) + br_ref[...]
    slab_ref[0], runs_ref[0], cnt_ref[...] = routed


def _route(lt, live, counted):
    ninf = -jnp.inf
    big = 1e9
    over = lambda fn, a: fn(a, axis=0, keepdims=True)
    grow = lax.broadcasted_iota(jnp.int32, (8, TM), 0).astype(F32)
    gl = jnp.where(grow < N_GROUPS, lt[NE:NE + 8], ninf)
    gmax = over(jnp.max, gl)
    gsel = over(jnp.min, jnp.where(gl == gmax, grow, big))
    pg = 1.0 / over(jnp.sum, jnp.exp(gl - gmax))
    erow_i = lax.broadcasted_iota(jnp.int32, (NE, TM), 0)
    erow = erow_i.astype(F32)
    el = jnp.where((erow_i // EPG).astype(F32) == gsel, lt[:NE], ninf)
    e1 = over(jnp.max, el)
    i1 = over(jnp.min, jnp.where(el == e1, erow, big))
    el2 = jnp.where(erow == i1, ninf, el)
    e2 = over(jnp.max, el2)
    i2 = over(jnp.min, jnp.where(el2 == e2, erow, big))
    r = jnp.exp(e2 - e1)
    g1 = pg / (1.0 + r)
    g2 = pg * r / (1.0 + r)
    oh1 = jnp.where(erow == i1, 1.0, 0.0)
    oh2 = jnp.where(erow == i2, 1.0, 0.0)
    cnt = oh1 + oh2
    cnt_b = cnt.astype(BF16)
    ts = lax.broadcasted_iota(jnp.int32, (TM, TM), 0)
    tt = lax.broadcasted_iota(jnp.int32, (TM, TM), 1)
    prior = _dot(cnt_b, jnp.where(ts < tt, 1.0, 0.0).astype(BF16))
    c_col = jnp.sum(cnt, axis=1, keepdims=True)
    ei = lax.broadcasted_iota(jnp.int32, (NE, NE), 0)
    ej = lax.broadcasted_iota(jnp.int32, (NE, NE), 1)
    cc_hi, cc_lo = _split2(jnp.broadcast_to(c_col, (NE, LANES)))
    fewer = jnp.where(ej < ei, 1.0, 0.0).astype(BF16)
    off_col = (_dot(fewer, cc_hi) + _dot(fewer, cc_lo))[:, 0:1]
    pos = prior + off_col
    lpos1 = over(jnp.sum, oh1 * pos)
    lpos2 = over(jnp.sum, oh2 * pos)
    cnt_pad = jnp.concatenate([cnt_b, jnp.zeros((LANES - NE, TM), BF16)], axis=0)
    c_tile = _dot_nt(jnp.ones((8, TM), BF16), cnt_pad)[0:1]
    c_tile = c_tile * live
    li = lax.broadcasted_iota(jnp.int32, (LANES, LANES), 0)
    lj = lax.broadcasted_iota(jnp.int32, (LANES, LANES), 1)
    lower_e = jnp.where(li < lj, 1.0, 0.0).astype(BF16)
    c_hi, c_lo = _split2(jnp.broadcast_to(c_tile, (8, LANES)))
    off = (_dot(c_hi, lower_e) + _dot(c_lo, lower_e))[0:1]
    sub = lax.broadcasted_iota(jnp.int32, (8, LANES), 0)
    runs = jnp.where(sub == 0, c_tile, jnp.where(sub == 1, counted, jnp.where(sub == 2, off, 0.0)))
    frow = lax.broadcasted_iota(jnp.int32, (LANES, TM), 0)
    fields = jnp.zeros((LANES, TM), F32)
    for k, val in enumerate((i1, i2, lpos1, lpos2, g1, g2)):
        fields = jnp.where(frow == k, val, fields)
    return fields.T, runs, counted + c_tile


def _mixout(xall, conv_in, g_in, o_f, o_b, modall, cw, cnw, hnw, w_out, nw2, w_r, b_r, l, j0):
    bsz, t_all, _ = xall.shape
    nt = t_all // TM - j0
    n = bsz * nt
    t_out = nt * TM
    n64 = t_all // GRID_W
    per = TM // GRID_W
    cur = lambda i: jnp.minimum(i, n - 1)
    bj = lambda i: (cur(i) // nt, cur(i) % nt + j0)
    old = lambda i: jnp.maximum(i - 1, 0)
    tile = lambda w, c: pl.BlockSpec((1, TM, w), lambda i, c=c: (*bj(i), c))
    out_tile = lambda w, at: pl.BlockSpec((1, TM, w), lambda i: (at(i) // nt, at(i) % nt, 0))
    full = lambda shape: pl.BlockSpec(shape, lambda i: (0,) * len(shape))
    layer = lambda shape: pl.BlockSpec((1, *shape), lambda i: (l,) + (0,) * len(shape), pipeline_mode=pl.Buffered(1))
    return pl.pallas_call(
        functools.partial(_mixout_kernel, j0=j0, nt=nt + j0),
        grid=(n + 1,),
        in_specs=[tile(D, 0),
                  tile(CW, 0), tile(CW, 1), tile(HW, 0),
                  pl.BlockSpec((1, GRID_W, CW // 2), lambda i: (bj(i)[0], jnp.maximum(bj(i)[1] * per - 1, 0), 3)),
                  pl.BlockSpec((1, GRID_W, CW // 2),
                               lambda i: (bj(i)[0], jnp.minimum(bj(i)[1] * per + per, n64 - 1), 3)),
                  tile(HW, 0), tile(HW, 0),
                  pl.BlockSpec((1, 1, N_MOD, D), lambda i: (bj(i)[0], jnp.minimum(bj(i)[1], 1), 0, 0)),
                  full((3, CW)), full((1, CW)), full((1, HW)), layer((D, D)), full((1, D)),
                  full((D, LANES)), full((1, LANES))],
        out_specs=[out_tile(D, cur), out_tile(D, cur), out_tile(LANES, old),
                   pl.BlockSpec((1, 8, LANES), lambda i: (old(i), 0, 0)),
                   pl.BlockSpec((1, LANES), lambda i: (0, 0))],
        out_shape=[jax.ShapeDtypeStruct((bsz, t_out, D), F32),
                   jax.ShapeDtypeStruct((bsz, t_out, D), BF16),
                   jax.ShapeDtypeStruct((bsz, t_out, LANES), F32),
                   jax.ShapeDtypeStruct((bsz * nt, 8, LANES), F32),
                   jax.ShapeDtypeStruct((1, LANES), F32)],
        scratch_shapes=[pltpu.VMEM((D, D), BF16), pltpu.VMEM((2, D, LANES), BF16), pltpu.VMEM((2, TM, LANES), F32)],
        compiler_params=_cparams(1),
        name="mixout",
    )(xall, conv_in, conv_in, g_in, conv_in, conv_in, o_f, o_b, modall, cw, cnw, hnw, w_out, nw2, w_r, b_r)


RUN_BITS = TM.bit_length()
LONG_RUN_BIT = 5
ROW_TILE = (8, LANES)


def _run_copies(tab_ref, local_ref, sorted_ref, sem, *, to_sorted, wait):
    def per_expert(e, carry):
        ln = tab_ref[0, 0, e]
        lo = tab_ref[0, 0, NE + e]
        go = tab_ref[0, 0, 2 * NE + e]
        def copy_bits(bits):
            for bit in bits:
                size = 1 << bit
                done = ln & ~(2 * size - 1)

                @pl.when((ln & size) != 0)
                def _():
                    a = local_ref.at[pl.ds(lo + done, size)]
                    b = sorted_ref.at[pl.ds(go + done, size)]
                    cp = pltpu.make_async_copy(a, b, sem) if to_sorted else pltpu.make_async_copy(b, a, sem)
                    if wait:
                        cp.wait()
                    else:
                        cp.start(priority=bit % 2)

        @pl.when(ln >= (1 << LONG_RUN_BIT))
        def _():
            copy_bits(range(RUN_BITS - 1, LONG_RUN_BIT - 1, -1))

        copy_bits(range(LONG_RUN_BIT - 1, -1, -1))
        return carry

    lax.fori_loop(0, NE, per_expert, 0)


def _wait_tile(local_ref, sorted_ref, sem, *, to_sorted):
    whole = sorted_ref.at[pl.ds(0, 2 * TM)]
    (pltpu.make_async_copy(local_ref, whole, sem) if to_sorted else pltpu.make_async_copy(whole, local_ref, sem)).wait()


def _local_positions(slab):
    col = lax.broadcasted_iota(jnp.int32, (TM, 2 * TM), 1).astype(F32)
    return col == slab[:, 2:3], col == slab[:, 3:4]


def _dispatch_kernel(tab_ref, ptab_ref, h2_ref, slab_ref, xs_ref, buf, zbuf, sem):
    @pl.when(pl.program_id(0) == 0)
    def _():
        zbuf[...] = jnp.zeros_like(zbuf)
        _run_copies(ptab_ref, zbuf, xs_ref, sem, to_sorted=True, wait=False)
        _run_copies(ptab_ref, zbuf, xs_ref, sem, to_sorted=True, wait=True)

        def zero_block(wait, i, carry):
            cp = pltpu.make_async_copy(zbuf, xs_ref.at[pl.ds(i * MOE_P, MOE_P)], sem)
            if wait:
                cp.wait()
            else:
                cp.start()
            return carry

        for wait in (False, True):
            lax.fori_loop(ptab_ref[0, 0, 3 * NE], xs_ref.shape[0] // MOE_P, functools.partial(zero_block, wait), 0)

    i = pl.program_id(0)
    slot = lax.rem(i, 2)
    p1, p2 = _local_positions(slab_ref[...])
    perm_t = jnp.where(p1 | p2, 1.0, 0.0).astype(BF16)
    buf[slot] = _dot_tn(perm_t, h2_ref[...]).reshape(2 * TM, *ROW_TILE)

    @pl.when(i > 0)
    def _():
        _wait_tile(buf.at[1 - slot], xs_ref, sem, to_sorted=True)

    _run_copies(tab_ref, buf.at[slot], xs_ref, sem, to_sorted=True, wait=False)

    @pl.when(i == pl.num_programs(0) - 1)
    def _():
        _wait_tile(buf.at[slot], xs_ref, sem, to_sorted=True)


def _dispatch(tab, pad_tab, h2, slab, n_rows):
    n_tiles = tab.shape[0]
    return pl.pallas_call(
        _dispatch_kernel,
        grid=(n_tiles,),
        in_specs=[pl.BlockSpec((1, 1, LANES), lambda i: (i, 0, 0), memory_space=pltpu.SMEM),
                  pl.BlockSpec((1, 1, LANES), lambda i: (0, 0, 0), memory_space=pltpu.SMEM),
                  pl.BlockSpec((TM, D), lambda i: (i, 0)),
                  pl.BlockSpec((TM, LANES), lambda i: (i, 0))],
        out_specs=pl.BlockSpec(memory_space=pl.ANY),
        out_shape=jax.ShapeDtypeStruct((n_rows, *ROW_TILE), F32),
        scratch_shapes=[pltpu.VMEM((2, 2 * TM, *ROW_TILE), F32), pltpu.VMEM((MOE_P, *ROW_TILE), F32),
                        pltpu.SemaphoreType.DMA(())],
        compiler_params=_cparams(1),
        name="dispatch",
    )(tab, pad_tab, h2, slab)


def _expert_kernel(be_ref, nx_ref, nu_ref, xs_ref, wgu_hbm, wd_hbm, ys_ref, wgu_f, wd_f, wgu_b, wd_b, slot_ref, sem,
                   *, layer):
    i = pl.program_id(0)
    used = i < nu_ref[0]
    e = be_ref[i]

    def weight_copies(expert, slot):
        return (pltpu.make_async_copy(wgu_hbm.at[layer, expert], wgu_f.at[slot], sem.at[slot]),
                pltpu.make_async_copy(wd_hbm.at[layer, expert], wd_f.at[slot], sem.at[slot]))

    @pl.when(i == 0)
    def _():
        slot_ref[0] = 0
        for cp in weight_copies(e, 0):
            cp.start()

    @pl.when(jnp.logical_and(used, jnp.logical_or(i == 0, e != be_ref[jnp.maximum(i - 1, 0)])))
    def _():
        slot = jnp.where(i == 0, 0, 1 - slot_ref[0])
        slot_ref[0] = slot
        for cp in weight_copies(e, slot):
            cp.wait()
        wgu_b[...] = wgu_f[slot].astype(BF16)
        wd_b[...] = wd_f[slot].astype(BF16)

        @pl.when(nx_ref[i] >= 0)
        def _():
            for cp in weight_copies(nx_ref[i], 1 - slot):
                cp.start()

    @pl.when(used)
    def _():
        x = xs_ref[...].reshape(MOE_P, D).astype(BF16)
        acts = []
        for c in range(0, DE, EXPERT_CHUNK):
            a = _dot(x, wgu_b[:, c:c + EXPERT_CHUNK])
            u = _dot(x, wgu_b[:, DE + c:DE + c + EXPERT_CHUNK])
            acts.append((_silu(a) * u).astype(BF16))
        y = _dot(jnp.concatenate(acts, axis=1), wd_b[...])
        ys_ref[...] = y.reshape(MOE_P, *ROW_TILE)

    @pl.when(jnp.logical_not(used))
    def _():
        ys_ref[...] = jnp.zeros_like(ys_ref)


def _experts(block_e, next_e, n_used, xs, w_gu, w_down, l):
    n_rows = xs.shape[0]
    grid_spec = pltpu.PrefetchScalarGridSpec(
        num_scalar_prefetch=3,
        grid=(n_rows // MOE_P,),
        in_specs=[pl.BlockSpec((MOE_P, *ROW_TILE), lambda i, be, nx, nu: (jnp.minimum(i, nu[0] - 1), 0, 0)),
                  pl.BlockSpec(memory_space=pl.ANY),
                  pl.BlockSpec(memory_space=pl.ANY)],
        out_specs=pl.BlockSpec((MOE_P, *ROW_TILE), lambda i, be, nx, nu: (i, 0, 0)),
        scratch_shapes=[pltpu.VMEM((2, D, 2 * DE), F32), pltpu.VMEM((2, DE, D), F32),
                        pltpu.VMEM((D, 2 * DE), BF16), pltpu.VMEM((DE, D), BF16),
                        pltpu.SMEM((1,), jnp.int32), pltpu.SemaphoreType.DMA((2,))],
    )
    return pl.pallas_call(
        functools.partial(_expert_kernel, layer=l),
        grid_spec=grid_spec,
        out_shape=jax.ShapeDtypeStruct((n_rows, *ROW_TILE), F32),
        compiler_params=_cparams(1),
        name="experts",
    )(block_e, next_e, n_used, xs, w_gu, w_down)


def _combine_kernel(tab_ref, tab_next_ref, xm_ref, slab_ref, mod_ref, fnw_ref, ys_ref, o_ref, buf, sem, *, final):
    i = pl.program_id(0) * pl.num_programs(1) + pl.program_id(1)
    n = pl.num_programs(0) * pl.num_programs(1)
    slot = lax.rem(i, 2)

    @pl.when(i == 0)
    def _():
        _run_copies(tab_ref, buf.at[0], ys_ref, sem.at[0], to_sorted=False, wait=False)

    @pl.when(i + 1 < n)
    def _():
        _run_copies(tab_next_ref, buf.at[1 - slot], ys_ref, sem.at[1 - slot], to_sorted=False, wait=False)

    _wait_tile(buf.at[slot], ys_ref, sem.at[slot], to_sorted=False)
    slab = slab_ref[0]
    p1, p2 = _local_positions(slab)
    yb = buf[slot].reshape(2 * TM, D).astype(BF16)
    y1 = _dot(jnp.where(p1, 1.0, 0.0).astype(BF16), yb)
    y2 = _dot(jnp.where(p2, 1.0, 0.0).astype(BF16), yb)
    f = slab[:, 4:5] * y1 + slab[:, 5:6] * y2
    x = xm_ref[0] + mod_ref[0, 0][5:6] * f
    o_ref[0] = _rms(x, fnw_ref[...]) if final else x


def _combine(tab, xmid, slab, modall, fnw, ys, final, j0):
    bsz, t_out, _ = xmid.shape
    nt = t_out // TM
    return pl.pallas_call(
        functools.partial(_combine_kernel, final=final),
        grid=(bsz, nt),
        in_specs=[pl.BlockSpec((1, 1, LANES), lambda b, j: (b * nt + j, 0, 0), memory_space=pltpu.SMEM),
                  pl.BlockSpec((1, 1, LANES), lambda b, j: (jnp.minimum(b * nt + j + 1, bsz * nt - 1), 0, 0),
                               memory_space=pltpu.SMEM),
                  pl.BlockSpec((1, TM, D), lambda b, j: (b, j, 0)),
                  pl.BlockSpec((1, TM, LANES), lambda b, j: (b, j, 0)),
                  pl.BlockSpec((1, 1, N_MOD, D), lambda b, j: (b, jnp.minimum(j + j0, 1), 0, 0)),
                  pl.BlockSpec((1, D), lambda b, j: (0, 0)),
                  pl.BlockSpec(memory_space=pl.ANY)],
        out_specs=pl.BlockSpec((1, TM, D), lambda b, j: (b, j, 0)),
        out_shape=jax.ShapeDtypeStruct((bsz, t_out, D), F32),
        scratch_shapes=[pltpu.VMEM((2, 2 * TM, *ROW_TILE), F32), pltpu.SemaphoreType.DMA((2,))],
        compiler_params=_cparams(2),
        name="combine_final" if final else "combine",
    )(tab, tab, xmid, slab, modall, fnw, ys)


def _lower_bounds(hg_lb):
    p = jax.nn.softmax(hg_lb.astype(F32), axis=1)
    cs = jnp.cumsum(p, axis=1)
    return cs - cs[:, :1]


def kernel(x, c, ctx, c_ctx, norm_w, w_ada, b_ada, w_in, conv_w, conv_norm_w, hg_lb, hg_norm_w, w_out, w_rg, b_rg,
           w_re, b_re, w_e_gu, w_e_down, final_norm_w):
    bsz, seq, _ = x.shape
    n_ctx = ctx.shape[1]
    assert n_ctx == TM and seq % TM == 0

    xall = jnp.concatenate([ctx, x], axis=1)
    assert bsz < ADA_ROWS
    cc = jnp.zeros((ADA_ROWS, D), F32).at[:bsz].set(c).at[bsz].set(c_ctx)
    mod = _ada(cc, w_ada, b_ada).reshape(DEPTH, ADA_ROWS, N_MOD, D)
    lb = _lower_bounds(hg_lb)
    lbp = jnp.stack([jnp.log(lb[0]), jnp.log1p(-lb[0]), jnp.log(lb[1]), jnp.log1p(-lb[1])], axis=1) * LOG2E
    w_r = jnp.concatenate([w_re, w_rg, jnp.zeros((DEPTH, D, LANES - NE - N_GROUPS), F32)], axis=-1)
    b_r = jnp.concatenate([b_re, b_rg, jnp.zeros((DEPTH, LANES - NE - N_GROUPS), F32)], axis=-1)

    out = None
    for l in range(DEPTH):
        final = l == DEPTH - 1
        j0 = 1 if final else 0
        modall = jnp.stack([jnp.broadcast_to(mod[l, bsz], (bsz, N_MOD, D)), mod[l, :bsz]], axis=1)
        conv_in, g_in, lf, qkv = _inproj(xall, norm_w[l, 0].reshape(1, D), modall, w_in, lbp[l], l)
        o_f, o_b = _gla(lf, qkv, n_ctx)
        xmid, h2, slab, runs, counts = _mixout(
            xall, conv_in, g_in, o_f, o_b, modall, conv_w[l], conv_norm_w[l].reshape(1, CW),
            jnp.tile(hg_norm_w[l], NH).reshape(1, HW), w_out, norm_w[l, 1].reshape(1, D),
            w_r[l], b_r[l].reshape(1, LANES), l, j0)
        n_tok = xmid.shape[0] * xmid.shape[1]
        n_blocks = -(-(2 * n_tok + NE * (MOE_P - 1)) // MOE_P)
        n_rows = n_blocks * MOE_P
        cnt = counts[0, :NE].astype(jnp.int32)
        padded = (cnt + MOE_P - 1) // MOE_P * MOE_P
        pad_ends = jnp.cumsum(padded)
        pad_starts = pad_ends - padded
        run_len = runs[:, 0, :NE].astype(jnp.int32)
        run_global = pad_starts[None, :] + runs[:, 1, :NE].astype(jnp.int32)
        run_local = runs[:, 2, :NE].astype(jnp.int32)
        tab = jnp.concatenate([run_len, run_local, run_global, jnp.zeros_like(run_len)], axis=1)[:, None, :]
        block_start = jnp.arange(n_blocks, dtype=jnp.int32) * MOE_P
        block_e = jnp.minimum(jnp.sum((pad_ends[None, :] <= block_start[:, None]).astype(jnp.int32), axis=1), NE - 1)
        n_used = (pad_ends[-1:] // MOE_P).astype(jnp.int32)
        zero = jnp.zeros_like(cnt)
        pad_tab = jnp.concatenate([padded - cnt, zero, pad_starts + cnt, zero + n_used]).reshape(1, 1, LANES)
        xs = _dispatch(tab, pad_tab, h2.reshape(n_tok, D), slab.reshape(n_tok, LANES), n_rows)
        ids = jnp.arange(NE, dtype=jnp.int32)
        later = jnp.min(jnp.where((ids[None, :] > block_e[:, None]) & (cnt[None, :] > 0), ids[None, :], NE), axis=1)
        next_e = jnp.where(later < NE, later, -1)
        ys = _experts(block_e, next_e, n_used, xs, w_e_gu, w_e_down, l)
        res = _combine(tab, xmid, slab, modall, final_norm_w.reshape(1, D), ys, final, j0)
        if final:
            out = res
        else:
            xall = res
    return out
```

```python
import functools

import jax
import jax.numpy as jnp
import numpy as np
from jax import lax
from jax.experimental import pallas as pl
from jax.experimental.pallas import tpu as pltpu

F32 = jnp.float32
BF16 = jnp.bfloat16

D = 1024
DEPTH = 4
GRID_W = 64
CW = 512
HW = 512
NH = 4
HD = HW // NH
PW = 3 * CW + 5 * HW
N_GROUPS = 4
EPG = 8
NE = N_GROUPS * EPG
DE = 512
N_MOD = 6
EPS = 1e-6
LOG2E = 1.4426950408889634
COL_BLK = 512
assert CW == COL_BLK and HW == COL_BLK
ADA_ROWS = 16
ADA_TN = 1536

TM = 256
CH = 128
LEAF = 16
HEAD_GROUP = 4
GLA_SUB = 2
MOE_P = 256
EXPERT_CHUNK = 256
LANES = 128

VMEM_LIMIT = 56 * 1024 * 1024


def _cparams(n_axes):
    return pltpu.CompilerParams(dimension_semantics=("arbitrary",) * n_axes, vmem_limit_bytes=VMEM_LIMIT)


def _dot(a, b):
    return jnp.dot(a, b, preferred_element_type=F32)


def _dot_nt(a, b):
    return lax.dot_general(a, b, (((1,), (1,)), ((), ())), preferred_element_type=F32)


def _dot_tn(a, b):
    return lax.dot_general(a, b, (((0,), (0,)), ((), ())), preferred_element_type=F32)


def _split2(x):
    hi = x.astype(BF16)
    lo = (x - hi.astype(F32)).astype(BF16)
    return hi, lo


def _dot_hp(a, b):
    a_hi, a_lo = _split2(a)
    b_hi, b_lo = _split2(b)
    return _dot(a_hi, b_hi) + _dot(a_hi, b_lo) + _dot(a_lo, b_hi)


def _sigmoid(x):
    return 1.0 / (1.0 + jnp.exp(-x))


def _rms(x, w):
    return x * lax.rsqrt(jnp.mean(x * x, axis=-1, keepdims=True) + EPS) * w


def _ada_kernel(cc_ref, w_ref, b_ref, o_ref):
    s = cc_ref[...]
    s = s * _sigmoid(s)
    o_ref[0] = _dot_hp(s, w_ref[0]) + b_ref[0]


def _ada(cc, w_ada, b_ada):
    tn = ADA_TN
    n = N_MOD * D
    return pl.pallas_call(
        _ada_kernel,
        grid=(DEPTH, n // tn),
        in_specs=[pl.BlockSpec((ADA_ROWS, D), lambda l, j: (0, 0)),
                  pl.BlockSpec((1, D, tn), lambda l, j: (l, 0, j)),
                  pl.BlockSpec((1, 1, tn), lambda l, j: (l, 0, j))],
        out_specs=pl.BlockSpec((1, ADA_ROWS, tn), lambda l, j: (l, 0, j)),
        out_shape=jax.ShapeDtypeStruct((DEPTH, ADA_ROWS, n), F32),
        compiler_params=_cparams(2),
        name="ada",
    )(cc, w_ada, b_ada.reshape(DEPTH, 1, n))


def _silu(x):
    h = 0.5 * x
    return h + h * jnp.tanh(h)


def _neg_abs(x):
    bits = lax.bitcast_convert_type(x, jnp.uint32) | jnp.uint32(0x80000000)
    return lax.bitcast_convert_type(bits, F32)


def _forget_gate(z2, log2lb, log2_1mlb):
    ls = jnp.minimum(z2, 0.0) - jnp.log2(1.0 + jnp.exp2(_neg_abs(z2)))
    c = log2_1mlb + ls
    lf2 = jnp.maximum(log2lb, c) + jnp.log2(1.0 + jnp.exp2(_neg_abs(log2lb - c)))
    return lf2, jnp.exp2(c - z2)


def _inproj_kernel(x_ref, nw_ref, mod_ref, w_ref, lb_ref, conv_ref, g_ref, lf_ref, qkv_ref, w_b):
    @pl.when(jnp.logical_and(pl.program_id(0) == 0, pl.program_id(1) == 0))
    def _():
        def cast_rows(r, carry):
            rows = pl.ds(pl.multiple_of(r * LANES, LANES), LANES)
            lo, hi = 3 * CW + HW, 3 * CW + 3 * HW
            w_b[rows, :lo] = w_ref[0, rows, :lo].astype(BF16)
            w_b[rows, lo:hi] = (w_ref[0, rows, lo:hi] * LOG2E).astype(BF16)
            w_b[rows, hi:] = w_ref[0, rows, hi:].astype(BF16)
            return carry

        lax.fori_loop(0, D // LANES, cast_rows, 0)

    m = mod_ref[0, 0]
    h = _rms(x_ref[0], nw_ref[...] * (1.0 + m[1:2])) + m[0:1]
    hb = h.astype(BF16)
    blk = {k: _dot(hb, w_b[:, k * COL_BLK:(k + 1) * COL_BLK]) for k in (4, 5, 3, 1, 2, 0, 6, 7)}
    for d in range(2):
        lf2, kk = _forget_gate(blk[4 + d], lb_ref[2 * d:2 * d + 1], lb_ref[2 * d + 1:2 * d + 2])
        lf_ref[0, :, d * HW:(d + 1) * HW] = lf2
        qkv_ref[0, :, (1 + d) * HW:(2 + d) * HW] = kk.astype(BF16)
    qkv_ref[0, :, 0:HW] = _silu(blk[3]).astype(BF16)
    conv_ref[0, :, CW:2 * CW] = blk[1] * blk[2]
    conv_ref[0, :, 0:CW] = blk[0]
    qkv_ref[0, :, 3 * HW:4 * HW] = blk[6].astype(BF16)
    g_ref[0] = blk[7]


def _inproj(xall, nw, modall, w_in, lbp, l):
    bsz, t_all, _ = xall.shape
    tile = lambda w: pl.BlockSpec((1, TM, w), lambda b, j: (b, j, 0))
    return pl.pallas_call(
        _inproj_kernel,
        grid=(bsz, t_all // TM),
        in_specs=[pl.BlockSpec((1, TM, D), lambda b, j: (b, j, 0)),
                  pl.BlockSpec((1, D), lambda b, j: (0, 0)),
                  pl.BlockSpec((1, 1, N_MOD, D), lambda b, j: (b, jnp.minimum(j, 1), 0, 0)),
                  pl.BlockSpec((1, D, PW), lambda b, j: (l, 0, 0), pipeline_mode=pl.Buffered(1)),
                  pl.BlockSpec((4, HW), lambda b, j: (0, 0))],
        out_specs=[tile(2 * CW), tile(HW), tile(2 * HW), tile(4 * HW)],
        out_shape=[jax.ShapeDtypeStruct((bsz, t_all, 2 * CW), F32),
                   jax.ShapeDtypeStruct((bsz, t_all, HW), F32),
                   jax.ShapeDtypeStruct((bsz, t_all, 2 * HW), F32),
                   jax.ShapeDtypeStruct((bsz, t_all, 4 * HW), BF16)],
        scratch_shapes=[pltpu.VMEM((D, PW), BF16)],
        compiler_params=_cparams(2),
        name="inproj",
    )(xall, nw, modall, w_in, lbp)


def _gla_tables():
    t = np.arange(CH)[:, None]
    s = np.arange(CH)[None, :]
    x = t ^ s
    lvl = np.full((CH, CH), -1, np.int32)
    n, k = CH, 0
    while n > LEAF:
        lvl[(x < n) & (x >= n // 2)] = k
        n, k = n // 2, k + 1
    lvl[x < LEAF] = k
    fwd = np.where(s <= t, lvl, -1)
    return (np.stack([s <= t, s >= t]).astype(np.float32), np.stack([fwd, fwd.T]).astype(np.int32))


def _gla_factors(lf2, qh, kh, tri, reverse):
    hi = lf2.astype(BF16)
    lo = (lf2 - hi.astype(F32)).astype(BF16)
    b = _dot(tri, hi) + _dot(tri, lo)

    def block_ref(n, idx):
        r = b.reshape(CH // n, n, HW)[:, idx:idx + 1, :]
        return jnp.broadcast_to(r, (CH // n, n, HW)).reshape(CH, HW)

    qf = qh.astype(F32)
    kf = kh.astype(F32)
    scaled = lambda x, e: (x * jnp.exp2(e)).astype(BF16)
    pieces = []
    n = CH
    while n > LEAF:
        h = n // 2
        a = jnp.exp2(_neg_abs(b - block_ref(n, h if reverse else h - 1)))
        pieces.append(((qf * a).astype(BF16), (kf * a).astype(BF16)))
        n = h
    dl = b - block_ref(LEAF, LEAF // 2)
    pieces.append((scaled(qf, dl), scaled(kf, -dl)))

    edge = 0 if reverse else CH - 1
    b_edge = b[edge:edge + 1, :]
    q_in = scaled(qf, b)
    k_out = scaled(kf, b_edge - b)
    d_chunk = jnp.exp2(b_edge)
    return pieces, q_in, k_out, d_chunk


def _gla_heads(factors, v_refs, rows, lvls, s_refs, o_refs):
    dirs = range(len(factors))
    owned = [[lvls[d] == k for k in range(len(factors[d][0]))] for d in dirs]
    for h0 in range(0, NH, HEAD_GROUP):
        heads = range(h0, h0 + HEAD_GROUP)
        sl = lambda hd: slice(hd * HD, (hd + 1) * HD)
        dots = {(d, hd): [_dot_nt(qa[:, sl(hd)], ka[:, sl(hd)]) for qa, ka in factors[d][0]]
                for hd in heads for d in dirs}
        scs = {}
        for hd in heads:
            for d in dirs:
                sc = jnp.zeros((CH, CH), F32)
                for k, dk in enumerate(dots[d, hd]):
                    sc = jnp.where(owned[d][k], dk, sc)
                scs[d, hd] = sc.astype(BF16)
        for hd in heads:
            for d in dirs:
                _, q_in, k_out, d_chunk = factors[d]
                vb = v_refs[d][0, rows[d], sl(hd)]
                st = s_refs[d][hd]
                o_refs[d][0, rows[d], sl(hd)] = _dot(scs[d, hd], vb) + _dot_nt(q_in[:, sl(hd)], st.astype(BF16))
                s_refs[d][hd] = st * d_chunk[:, sl(hd)] + _dot_tn(vb, k_out[:, sl(hd)])


def _gla_kernel(lff_ref, qf_ref, kf_ref, vf_ref, lfb_ref, qb_ref, kb_ref, vb_ref, tri_ref, lvl_ref, of_ref, ob_ref,
                sf_ref, sb_ref):
    @pl.when(pl.program_id(1) == 0)
    def _():
        sf_ref[...] = jnp.zeros_like(sf_ref)
        sb_ref[...] = jnp.zeros_like(sb_ref)

    for c in range(GLA_SUB):
        rows = (slice(c * CH, (c + 1) * CH), slice((GLA_SUB - 1 - c) * CH, (GLA_SUB - c) * CH))
        factors = (_gla_factors(lff_ref[0, rows[0]], qf_ref[0, rows[0]], kf_ref[0, rows[0]], tri_ref[0], False),
                   _gla_factors(lfb_ref[0, rows[1]], qb_ref[0, rows[1]], kb_ref[0, rows[1]], tri_ref[1], True))
        _gla_heads(factors, (vf_ref, vb_ref), rows, (lvl_ref[0], lvl_ref[1]), (sf_ref, sb_ref), (of_ref, ob_ref))


def _gla(lf, qkv, n_ctx):
    bsz, t_all, _ = lf.shape
    span = GLA_SUB * CH
    assert n_ctx % span == 0 and t_all % span == 0
    nc = t_all // span
    n_ctx_blocks = n_ctx // span

    def cb(j):
        return jnp.where(j < n_ctx_blocks, n_ctx_blocks - 1 - j, nc - 1 - (j - n_ctx_blocks))

    blk = (1, span, HW)
    fwd = lambda c: pl.BlockSpec(blk, lambda b, j: (b, j, c))
    bwd = lambda c: pl.BlockSpec(blk, lambda b, j: (b, cb(j), c))
    tri, lvl = _gla_tables()
    return pl.pallas_call(
        _gla_kernel,
        grid=(bsz, nc),
        in_specs=[fwd(0), fwd(0), fwd(1), fwd(3),
                  bwd(1), bwd(0), bwd(2), bwd(3),
                  pl.BlockSpec((2, CH, CH), lambda b, j: (0, 0, 0)),
                  pl.BlockSpec((2, CH, CH), lambda b, j: (0, 0, 0))],
        out_specs=[fwd(0), bwd(0)],
        out_shape=[jax.ShapeDtypeStruct((bsz, t_all, HW), F32)] * 2,
        scratch_shapes=[pltpu.VMEM((NH, HD, HD), F32), pltpu.VMEM((NH, HD, HD), F32)],
        compiler_params=_cparams(2),
        name="gla",
    )(lf, qkv, qkv, qkv, lf, qkv, qkv, qkv, jnp.asarray(tri, BF16), jnp.asarray(lvl))


def _shift_rows(u, k):
    return pltpu.roll(u, k % u.shape[0], axis=0)


def _mixout_kernel(x_ref, bg_ref, u_ref, g_ref, up_ref, un_ref, of_ref, ob_ref, mod_ref, cw_ref, cnw_ref,
                   hnw_ref, wo_ref, nw2_ref, wr_ref, br_ref, xm_ref, h2_ref, slab_ref, runs_ref, cnt_ref, wo_b, wr_b, lg,
                   *, j0, nt):
    i = pl.program_id(0)
    slot = lax.rem(i, 2)
    j = lax.rem(jnp.minimum(i, pl.num_programs(0) - 2), nt - j0) + j0

    @pl.when(i == 0)
    def _():
        cnt_ref[...] = jnp.zeros_like(cnt_ref)
        lg[1] = jnp.zeros_like(lg[1])
        wo_b[...] = wo_ref[0].astype(BF16)
        wr_b[0], wr_b[1] = _split2(wr_ref[...])

    is_ctx = j == 0
    m = mod_ref[0, 0]
    u = u_ref[0]
    cw = cw_ref[...]
    t = lax.broadcasted_iota(jnp.int32, (TM, 1), 0)
    col_in_row = t & (GRID_W - 1)
    keep_l = jnp.where(is_ctx, jnp.where(t == 0, 0.0, 1.0), jnp.where(col_in_row == 0, 0.0, 1.0))
    keep_r = jnp.where(is_ctx, jnp.where(t == TM - 1, 0.0, 1.0), jnp.where(col_in_row == GRID_W - 1, 0.0, 1.0))
    hc = CW // 2

    def seq_taps(a, w):
        left = jnp.where(keep_l > 0.5, _shift_rows(a, 1), 0.0)
        right = jnp.where(keep_r > 0.5, _shift_rows(a, -1), 0.0)
        return w[0:1] * left + w[1:2] * a + w[2:3] * right

    def col_taps(a, w):
        up = jnp.concatenate([jnp.where(j == 1, 0.0, up_ref[0]), a[:TM - GRID_W]], axis=0)
        dn = jnp.concatenate([a[GRID_W:], jnp.where(j == nt - 1, 0.0, un_ref[0])], axis=0)
        return w[0:1] * up + w[1:2] * a + w[2:3] * dn

    y_second = lax.cond(is_ctx, seq_taps, col_taps, u[:, hc:], cw[:, hc:])
    conv = jnp.concatenate([seq_taps(u[:, :hc], cw[:, :hc]), y_second], axis=-1)
    routed = _route(lg[1 - slot].T, jnp.where(i > 0, 1.0, 0.0), cnt_ref[...])
    y_conv = _rms(bg_ref[0] * conv, cnw_ref[...])

    o = of_ref[0] + ob_ref[0]
    g = g_ref[0]
    hnw = hnw_ref[...]
    recs = []
    for hd in range(NH):
        hs = slice(hd * HD, (hd + 1) * HD)
        recs.append(_rms(o[:, hs], hnw[:, hs]))
    y_rec = jnp.concatenate(recs, axis=-1) * _silu(g)

    y = jnp.concatenate([y_conv, y_rec], axis=-1).astype(BF16)
    xm = x_ref[0] + m[2:3] * _dot(y, wo_b[...])
    xm_ref[0] = xm
    h2 = _rms(xm, nw2_ref[...] * (1.0 + m[4:5])) + m[3:4]
    h2_ref[0] = h2.astype(BF16)

    h_hi, h_lo = _split2(h2)
    lg[slot] = _dot(h_hi, wr_b[0]) + _dot(h_hi, wr_b[1]) + _dot(h_lo, wr_b[0]) + br_ref[...]
    slab_ref[0], runs_ref[0], cnt_ref[...] = routed


def _route(lt, live, counted):
    ninf = -jnp.inf
    big = 1e9
    over = lambda fn, a: fn(a, axis=0, keepdims=True)
    grow = lax.broadcasted_iota(jnp.int32, (8, TM), 0).astype(F32)
    gl = jnp.where(grow < N_GROUPS, lt[NE:NE + 8], ninf)
    gmax = over(jnp.max, gl)
    gsel = over(jnp.min, jnp.where(gl == gmax, grow, big))
    pg = 1.0 / over(jnp.sum, jnp.exp(gl - gmax))
    erow_i = lax.broadcasted_iota(jnp.int32, (NE, TM), 0)
    erow = erow_i.astype(F32)
    el = jnp.where((erow_i // EPG).astype(F32) == gsel, lt[:NE], ninf)
    e1 = over(jnp.max, el)
    i1 = over(jnp.min, jnp.where(el == e1, erow, big))
    el2 = jnp.where(erow == i1, ninf, el)
    e2 = over(jnp.max, el2)
    i2 = over(jnp.min, jnp.where(el2 == e2, erow, big))
    r = jnp.exp(e2 - e1)
    g1 = pg / (1.0 + r)
    g2 = pg * r / (1.0 + r)
    oh1 = jnp.where(erow == i1, 1.0, 0.0)
    oh2 = jnp.where(erow == i2, 1.0, 0.0)
    cnt = oh1 + oh2
    cnt_b = cnt.astype(BF16)
    ts = lax.broadcasted_iota(jnp.int32, (TM, TM), 0)
    tt = lax.broadcasted_iota(jnp.int32, (TM, TM), 1)
    prior = _dot(cnt_b, jnp.where(ts < tt, 1.0, 0.0).astype(BF16))
    c_col = jnp.sum(cnt, axis=1, keepdims=True)
    ei = lax.broadcasted_iota(jnp.int32, (NE, NE), 0)
    ej = lax.broadcasted_iota(jnp.int32, (NE, NE), 1)
    cc_hi, cc_lo = _split2(jnp.broadcast_to(c_col, (NE, LANES)))
    fewer = jnp.where(ej < ei, 1.0, 0.0).astype(BF16)
    off_col = (_dot(fewer, cc_hi) + _dot(fewer, cc_lo))[:, 0:1]
    pos = prior + off_col
    lpos1 = over(jnp.sum, oh1 * pos)
    lpos2 = over(jnp.sum, oh2 * pos)
    cnt_pad = jnp.concatenate([cnt_b, jnp.zeros((LANES - NE, TM), BF16)], axis=0)
    c_tile = _dot_nt(jnp.ones((8, TM), BF16), cnt_pad)[0:1]
    c_tile = c_tile * live
    li = lax.broadcasted_iota(jnp.int32, (LANES, LANES), 0)
    lj = lax.broadcasted_iota(jnp.int32, (LANES, LANES), 1)
    lower_e = jnp.where(li < lj, 1.0, 0.0).astype(BF16)
    c_hi, c_lo = _split2(jnp.broadcast_to(c_tile, (8, LANES)))
    off = (_dot(c_hi, lower_e) + _dot(c_lo, lower_e))[0:1]
    sub = lax.broadcasted_iota(jnp.int32, (8, LANES), 0)
    runs = jnp.where(sub == 0, c_tile, jnp.where(sub == 1, counted, jnp.where(sub == 2, off, 0.0)))
    frow = lax.broadcasted_iota(jnp.int32, (LANES, TM), 0)
    fields = jnp.zeros((LANES, TM), F32)
    for k, val in enumerate((i1, i2, lpos1, lpos2, g1, g2)):
        fields = jnp.where(frow == k, val, fields)
    return fields.T, runs, counted + c_tile


def _mixout(xall, conv_in, g_in, o_f, o_b, modall, cw, cnw, hnw, w_out, nw2, w_r, b_r, l, j0):
    bsz, t_all, _ = xall.shape
    nt = t_all // TM - j0
    n = bsz * nt
    t_out = nt * TM
    n64 = t_all // GRID_W
    per = TM // GRID_W
    cur = lambda i: jnp.minimum(i, n - 1)
    bj = lambda i: (cur(i) // nt, cur(i) % nt + j0)
    old = lambda i: jnp.maximum(i - 1, 0)
    tile = lambda w, c: pl.BlockSpec((1, TM, w), lambda i, c=c: (*bj(i), c))
    out_tile = lambda w, at: pl.BlockSpec((1, TM, w), lambda i: (at(i) // nt, at(i) % nt, 0))
    full = lambda shape: pl.BlockSpec(shape, lambda i: (0,) * len(shape))
    layer = lambda shape: pl.BlockSpec((1, *shape), lambda i: (l,) + (0,) * len(shape), pipeline_mode=pl.Buffered(1))
    return pl.pallas_call(
        functools.partial(_mixout_kernel, j0=j0, nt=nt + j0),
        grid=(n + 1,),
        in_specs=[tile(D, 0),
                  tile(CW, 0), tile(CW, 1), tile(HW, 0),
                  pl.BlockSpec((1, GRID_W, CW // 2), lambda i: (bj(i)[0], jnp.maximum(bj(i)[1] * per - 1, 0), 3)),
                  pl.BlockSpec((1, GRID_W, CW // 2),
                               lambda i: (bj(i)[0], jnp.minimum(bj(i)[1] * per + per, n64 - 1), 3)),
                  tile(HW, 0), tile(HW, 0),
                  pl.BlockSpec((1, 1, N_MOD, D), lambda i: (bj(i)[0], jnp.minimum(bj(i)[1], 1), 0, 0)),
                  full((3, CW)), full((1, CW)), full((1, HW)), layer((D, D)), full((1, D)),
                  full((D, LANES)), full((1, LANES))],
        out_specs=[out_tile(D, cur), out_tile(D, cur), out_tile(LANES, old),
                   pl.BlockSpec((1, 8, LANES), lambda i: (old(i), 0, 0)),
                   pl.BlockSpec((1, LANES), lambda i: (0, 0))],
        out_shape=[jax.ShapeDtypeStruct((bsz, t_out, D), F32),
                   jax.ShapeDtypeStruct((bsz, t_out, D), BF16),
                   jax.ShapeDtypeStruct((bsz, t_out, LANES), F32),
                   jax.ShapeDtypeStruct((bsz * nt, 8, LANES), F32),
                   jax.ShapeDtypeStruct((1, LANES), F32)],
        scratch_shapes=[pltpu.VMEM((D, D), BF16), pltpu.VMEM((2, D, LANES), BF16), pltpu.VMEM((2, TM, LANES), F32)],
        compiler_params=_cparams(1),
        name="mixout",
    )(xall, conv_in, conv_in, g_in, conv_in, conv_in, o_f, o_b, modall, cw, cnw, hnw, w_out, nw2, w_r, b_r)


RUN_BITS = TM.bit_length()
LONG_RUN_BIT = 5
COMBINE_CHUNKS = 4
ROW_TILE = (8, LANES)


def _run_copies(tab_ref, local_ref, sorted_ref, sem, *, to_sorted, wait):
    def per_expert(e, carry):
        ln = tab_ref[0, 0, e]
        lo = tab_ref[0, 0, NE + e]
        go = tab_ref[0, 0, 2 * NE + e]
        def copy_bits(bits):
            for bit in bits:
                size = 1 << bit
                done = ln & ~(2 * size - 1)

                @pl.when((ln & size) != 0)
                def _():
                    a = local_ref.at[pl.ds(lo + done, size)]
                    b = sorted_ref.at[pl.ds(go + done, size)]
                    cp = pltpu.make_async_copy(a, b, sem) if to_sorted else pltpu.make_async_copy(b, a, sem)
                    if wait:
                        cp.wait()
                    else:
                        cp.start(priority=bit % 2)

        @pl.when(ln >= (1 << LONG_RUN_BIT))
        def _():
            copy_bits(range(RUN_BITS - 1, LONG_RUN_BIT - 1, -1))

        copy_bits(range(LONG_RUN_BIT - 1, -1, -1))
        return carry

    lax.fori_loop(0, NE, per_expert, 0)


def _wait_tile(local_ref, sorted_ref, sem, *, to_sorted):
    whole = sorted_ref.at[pl.ds(0, 2 * TM)]
    (pltpu.make_async_copy(local_ref, whole, sem) if to_sorted else pltpu.make_async_copy(whole, local_ref, sem)).wait()


def _local_positions(slab):
    col = lax.broadcasted_iota(jnp.int32, (TM, 2 * TM), 1).astype(F32)
    return col == slab[:, 2:3], col == slab[:, 3:4]


def _dispatch_kernel(tab_ref, ptab_ref, h2_ref, slab_ref, xs_ref, buf, zbuf, sem):
    @pl.when(pl.program_id(0) == 0)
    def _():
        zbuf[...] = jnp.zeros_like(zbuf)
        _run_copies(ptab_ref, zbuf, xs_ref, sem, to_sorted=True, wait=False)
        _run_copies(ptab_ref, zbuf, xs_ref, sem, to_sorted=True, wait=True)

        def zero_block(wait, i, carry):
            cp = pltpu.make_async_copy(zbuf, xs_ref.at[pl.ds(i * MOE_P, MOE_P)], sem)
            if wait:
                cp.wait()
            else:
                cp.start()
            return carry

        for wait in (False, True):
            lax.fori_loop(ptab_ref[0, 0, 3 * NE], xs_ref.shape[0] // MOE_P, functools.partial(zero_block, wait), 0)

    i = pl.program_id(0)
    slot = lax.rem(i, 2)
    p1, p2 = _local_positions(slab_ref[...])
    perm_t = jnp.where(p1 | p2, 1.0, 0.0).astype(BF16)
    buf[slot] = _dot_tn(perm_t, h2_ref[...]).reshape(2 * TM, *ROW_TILE)

    @pl.when(i > 0)
    def _():
        _wait_tile(buf.at[1 - slot], xs_ref, sem, to_sorted=True)

    _run_copies(tab_ref, buf.at[slot], xs_ref, sem, to_sorted=True, wait=False)

    @pl.when(i == pl.num_programs(0) - 1)
    def _():
        _wait_tile(buf.at[slot], xs_ref, sem, to_sorted=True)


def _dispatch(tab, pad_tab, h2, slab, n_rows):
    n_tiles = tab.shape[0]
    return pl.pallas_call(
        _dispatch_kernel,
        grid=(n_tiles,),
        in_specs=[pl.BlockSpec((1, 1, LANES), lambda i: (i, 0, 0), memory_space=pltpu.SMEM),
                  pl.BlockSpec((1, 1, LANES), lambda i: (0, 0, 0), memory_space=pltpu.SMEM),
                  pl.BlockSpec((TM, D), lambda i: (i, 0)),
                  pl.BlockSpec((TM, LANES), lambda i: (i, 0))],
        out_specs=pl.BlockSpec(memory_space=pl.ANY),
        out_shape=jax.ShapeDtypeStruct((n_rows, *ROW_TILE), F32),
        scratch_shapes=[pltpu.VMEM((2, 2 * TM, *ROW_TILE), F32), pltpu.VMEM((MOE_P, *ROW_TILE), F32),
                        pltpu.SemaphoreType.DMA(())],
        compiler_params=_cparams(1),
        name="dispatch",
    )(tab, pad_tab, h2, slab)


def _expert_kernel(be_ref, nx_ref, nu_ref, xs_ref, wgu_hbm, wd_hbm, ys_ref, wgu_f, wd_f, wgu_b, wd_b, slot_ref, sem,
                   *, layer):
    i = pl.program_id(0)
    used = i < nu_ref[0]
    e = be_ref[i]

    def weight_copies(expert, slot):
        return (pltpu.make_async_copy(wgu_hbm.at[layer, expert], wgu_f.at[slot], sem.at[slot]),
                pltpu.make_async_copy(wd_hbm.at[layer, expert], wd_f.at[slot], sem.at[slot]))

    @pl.when(i == 0)
    def _():
        slot_ref[0] = 0
        for cp in weight_copies(e, 0):
            cp.start()

    @pl.when(jnp.logical_and(used, jnp.logical_or(i == 0, e != be_ref[jnp.maximum(i - 1, 0)])))
    def _():
        slot = jnp.where(i == 0, 0, 1 - slot_ref[0])
        slot_ref[0] = slot
        for cp in weight_copies(e, slot):
            cp.wait()
        wgu_b[...] = wgu_f[slot].astype(BF16)
        wd_b[...] = wd_f[slot].astype(BF16)

        @pl.when(nx_ref[i] >= 0)
        def _():
            for cp in weight_copies(nx_ref[i], 1 - slot):
                cp.start()

    @pl.when(used)
    def _():
        x = xs_ref[...].reshape(MOE_P, D).astype(BF16)
        acts = []
        for c in range(0, DE, EXPERT_CHUNK):
            a = _dot(x, wgu_b[:, c:c + EXPERT_CHUNK])
            u = _dot(x, wgu_b[:, DE + c:DE + c + EXPERT_CHUNK])
            acts.append((_silu(a) * u).astype(BF16))
        y = _dot(jnp.concatenate(acts, axis=1), wd_b[...])
        ys_ref[...] = y.reshape(MOE_P, *ROW_TILE)

    @pl.when(jnp.logical_not(used))
    def _():
        ys_ref[...] = jnp.zeros_like(ys_ref)


def _experts(block_e, next_e, n_used, xs, w_gu, w_down, l):
    n_rows = xs.shape[0]
    grid_spec = pltpu.PrefetchScalarGridSpec(
        num_scalar_prefetch=3,
        grid=(n_rows // MOE_P,),
        in_specs=[pl.BlockSpec((MOE_P, *ROW_TILE), lambda i, be, nx, nu: (jnp.minimum(i, nu[0] - 1), 0, 0)),
                  pl.BlockSpec(memory_space=pl.ANY),
                  pl.BlockSpec(memory_space=pl.ANY)],
        out_specs=pl.BlockSpec((MOE_P, *ROW_TILE), lambda i, be, nx, nu: (i, 0, 0)),
        scratch_shapes=[pltpu.VMEM((2, D, 2 * DE), F32), pltpu.VMEM((2, DE, D), F32),
                        pltpu.VMEM((D, 2 * DE), BF16), pltpu.VMEM((DE, D), BF16),
                        pltpu.SMEM((1,), jnp.int32), pltpu.SemaphoreType.DMA((2,))],
    )
    return pl.pallas_call(
        functools.partial(_expert_kernel, layer=l),
        grid_spec=grid_spec,
        out_shape=jax.ShapeDtypeStruct((n_rows, *ROW_TILE), F32),
        compiler_params=_cparams(1),
        name="experts",
    )(block_e, next_e, n_used, xs, w_gu, w_down)


def _combine_kernel(tab_ref, tab_next_ref, xm_ref, slab_ref, mod_ref, fnw_ref, ys_ref, o_ref, buf, sem, *, final):
    i = pl.program_id(0) * pl.num_programs(1) + pl.program_id(1)
    n = pl.num_programs(0) * pl.num_programs(1)
    slot = lax.rem(i, 2)

    @pl.when(i == 0)
    def _():
        _run_copies(tab_ref, buf.at[0], ys_ref, sem.at[0], to_sorted=False, wait=False)

    @pl.when(i + 1 < n)
    def _():
        _run_copies(tab_next_ref, buf.at[1 - slot], ys_ref, sem.at[1 - slot], to_sorted=False, wait=False)

    _wait_tile(buf.at[slot], ys_ref, sem.at[slot], to_sorted=False)
    slab = slab_ref[0]
    p1, p2 = _local_positions(slab)
    p1 = jnp.where(p1, 1.0, 0.0).astype(BF16)
    p2 = jnp.where(p2, 1.0, 0.0).astype(BF16)
    gate = mod_ref[0, 0][5:6]
    per = ROW_TILE[0] // COMBINE_CHUNKS
    xs = []
    for c in range(COMBINE_CHUNKS):
        cols = slice(c * per * LANES, (c + 1) * per * LANES)
        yb = buf[slot, :, c * per:(c + 1) * per, :].reshape(2 * TM, per * LANES).astype(BF16)
        f = slab[:, 4:5] * _dot(p1, yb) + slab[:, 5:6] * _dot(p2, yb)
        x = xm_ref[0, :, cols] + gate[:, cols] * f
        if final:
            xs.append(x)
        else:
            o_ref[0, :, cols] = x
    if final:
        o_ref[0] = _rms(jnp.concatenate(xs, axis=1), fnw_ref[...])


def _combine(tab, xmid, slab, modall, fnw, ys, final, j0):
    bsz, t_out, _ = xmid.shape
    nt = t_out // TM
    return pl.pallas_call(
        functools.partial(_combine_kernel, final=final),
        grid=(bsz, nt),
        in_specs=[pl.BlockSpec((1, 1, LANES), lambda b, j: (b * nt + j, 0, 0), memory_space=pltpu.SMEM),
                  pl.BlockSpec((1, 1, LANES), lambda b, j: (jnp.minimum(b * nt + j + 1, bsz * nt - 1), 0, 0),
                               memory_space=pltpu.SMEM),
                  pl.BlockSpec((1, TM, D), lambda b, j: (b, j, 0)),
                  pl.BlockSpec((1, TM, LANES), lambda b, j: (b, j, 0)),
                  pl.BlockSpec((1, 1, N_MOD, D), lambda b, j: (b, jnp.minimum(j + j0, 1), 0, 0)),
                  pl.BlockSpec((1, D), lambda b, j: (0, 0)),
                  pl.BlockSpec(memory_space=pl.ANY)],
        out_specs=pl.BlockSpec((1, TM, D), lambda b, j: (b, j, 0)),
        out_shape=jax.ShapeDtypeStruct((bsz, t_out, D), F32),
        scratch_shapes=[pltpu.VMEM((2, 2 * TM, *ROW_TILE), F32), pltpu.SemaphoreType.DMA((2,))],
        compiler_params=_cparams(2),
        name="combine_final" if final else "combine",
    )(tab, tab, xmid, slab, modall, fnw, ys)


def _lower_bounds(hg_lb):
    p = jax.nn.softmax(hg_lb.astype(F32), axis=1)
    cs = jnp.cumsum(p, axis=1)
    return cs - cs[:, :1]


def kernel(x, c, ctx, c_ctx, norm_w, w_ada, b_ada, w_in, conv_w, conv_norm_w, hg_lb, hg_norm_w, w_out, w_rg, b_rg,
           w_re, b_re, w_e_gu, w_e_down, final_norm_w):
    bsz, seq, _ = x.shape
    n_ctx = ctx.shape[1]
    assert n_ctx == TM and seq % TM == 0

    xall = jnp.concatenate([ctx, x], axis=1)
    assert bsz < ADA_ROWS
    cc = jnp.zeros((ADA_ROWS, D), F32).at[:bsz].set(c).at[bsz].set(c_ctx)
    mod = _ada(cc, w_ada, b_ada).reshape(DEPTH, ADA_ROWS, N_MOD, D)
    lb = _lower_bounds(hg_lb)
    lbp = jnp.stack([jnp.log(lb[0]), jnp.log1p(-lb[0]), jnp.log(lb[1]), jnp.log1p(-lb[1])], axis=1) * LOG2E
    w_r = jnp.concatenate([w_re, w_rg, jnp.zeros((DEPTH, D, LANES - NE - N_GROUPS), F32)], axis=-1)
    b_r = jnp.concatenate([b_re, b_rg, jnp.zeros((DEPTH, LANES - NE - N_GROUPS), F32)], axis=-1)

    out = None
    for l in range(DEPTH):
        final = l == DEPTH - 1
        j0 = 1 if final else 0
        modall = jnp.stack([jnp.broadcast_to(mod[l, bsz], (bsz, N_MOD, D)), mod[l, :bsz]], axis=1)
        conv_in, g_in, lf, qkv = _inproj(xall, norm_w[l, 0].reshape(1, D), modall, w_in, lbp[l], l)
        o_f, o_b = _gla(lf, qkv, n_ctx)
        xmid, h2, slab, runs, counts = _mixout(
            xall, conv_in, g_in, o_f, o_b, modall, conv_w[l], conv_norm_w[l].reshape(1, CW),
            jnp.tile(hg_norm_w[l], NH).reshape(1, HW), w_out, norm_w[l, 1].reshape(1, D),
            w_r[l], b_r[l].reshape(1, LANES), l, j0)
        n_tok = xmid.shape[0] * xmid.shape[1]
        n_blocks = -(-(2 * n_tok + NE * (MOE_P - 1)) // MOE_P)
        n_rows = n_blocks * MOE_P
        cnt = counts[0, :NE].astype(jnp.int32)
        padded = (cnt + MOE_P - 1) // MOE_P * MOE_P
        pad_ends = jnp.cumsum(padded)
        pad_starts = pad_ends - padded
        run_len = runs[:, 0, :NE].astype(jnp.int32)
        run_global = pad_starts[None, :] + runs[:, 1, :NE].astype(jnp.int32)
        run_local = runs[:, 2, :NE].astype(jnp.int32)
        tab = jnp.concatenate([run_len, run_local, run_global, jnp.zeros_like(run_len)], axis=1)[:, None, :]
        block_start = jnp.arange(n_blocks, dtype=jnp.int32) * MOE_P
        block_e = jnp.minimum(jnp.sum((pad_ends[None, :] <= block_start[:, None]).astype(jnp.int32), axis=1), NE - 1)
        n_used = (pad_ends[-1:] // MOE_P).astype(jnp.int32)
        zero = jnp.zeros_like(cnt)
        pad_tab = jnp.concatenate([padded - cnt, zero, pad_starts + cnt, zero + n_used]).reshape(1, 1, LANES)
        xs = _dispatch(tab, pad_tab, h2.reshape(n_tok, D), slab.reshape(n_tok, LANES), n_rows)
        ids = jnp.arange(NE, dtype=jnp.int32)
        later = jnp.min(jnp.where((ids[None, :] > block_e[:, None]) & (cnt[None, :] > 0), ids[None, :], NE), axis=1)
        next_e = jnp.where(later < NE, later, -1)
        ys = _experts(block_e, next_e, n_used, xs, w_e_gu, w_e_down, l)
        res = _combine(tab, xmid, slab, modall, final_norm_w.reshape(1, D), ys, final, j0)
        if final:
            out = res
        else:
            xall = res
    return out
```
